```python
import jax, jax.numpy as jnp
from jax import lax
import numpy as np

D_MODEL = 1024
BATCH = 16
SEQ = 4096
DEPTH = 1
DEC_BATCH = 4
DEC_SEQ = 8192
PAST_LEN = 128

HEAD_DIM = 64
HEADS_PER_GROUP = 4
DILATED_GROUPS = ((128, 1), (512, 4), (2048, 16))
N_ATTN_GROUPS = len(DILATED_GROUPS)
N_HEADS = HEADS_PER_GROUP * N_ATTN_GROUPS
D_ATTN = N_HEADS * HEAD_DIM
ATTN_SCALE = HEAD_DIM ** -0.5
ROPE_DIM = HEAD_DIM // 4
ROPE_THETA = 500000.0
MASK_VALUE = -1e30
D_CONV = D_MODEL
CONV_WIDTH = 3
SPLITS = (D_ATTN, 2 * D_ATTN, 3 * D_ATTN, 3 * D_ATTN + D_CONV, 3 * D_ATTN + 2 * D_CONV,
          3 * D_ATTN + 3 * D_CONV, 3 * D_ATTN + 3 * D_CONV + D_MODEL)
D_IN = 3 * D_ATTN + 3 * D_CONV + 2 * D_MODEL
N_EXPERT_GROUPS = 4
EXPERTS_PER_GROUP = 8
N_EXPERTS = N_EXPERT_GROUPS * EXPERTS_PER_GROUP
TOP_K_INNER = 2
D_EXPERT = 512
EXPERT_BLOCK = 256
LN_EPS = 1e-5
DEEPNORM_ALPHA = (2 * DEPTH) ** 0.25
DEEPNORM_BETA = (8 * DEPTH) ** -0.25

kernel_name = 'dilated_attn_shortconv_hmoe_encoder'


def _layer_norm(x, g, b):
    xf = x.astype(jnp.float32)
    mu = xf.mean(-1, keepdims=True)
    var = jnp.square(xf - mu).mean(-1, keepdims=True)
    y = (xf - mu) * lax.rsqrt(var + LN_EPS) * g.astype(jnp.float32) + b.astype(jnp.float32)
    return y.astype(x.dtype)


def _partial_rope(x, pos):
    half = ROPE_DIM // 2
    inv_freq = ROPE_THETA ** (-jnp.arange(half, dtype=jnp.float32) * 2.0 / ROPE_DIM)
    ang = pos.astype(jnp.float32)[:, None] * inv_freq[None, :]
    cos = jnp.cos(ang)[None, :, None, :]
    sin = jnp.sin(ang)[None, :, None, :]
    xr = x[..., :ROPE_DIM].astype(jnp.float32)
    x1, x2 = xr[..., :half], xr[..., half:]
    rot = jnp.concatenate([x1 * cos - x2 * sin, x2 * cos + x1 * sin], axis=-1).astype(x.dtype)
    return jnp.concatenate([rot, x[..., ROPE_DIM:]], axis=-1)


def _dilated_group_attention(q, k, v, window, dilation):
    b, s, h, d = q.shape
    nbr = window // (2 * dilation)
    qb = nbr
    l = s // dilation
    nb = -(-l // qb)
    lp = nb * qb

    def to_sub(t):
        t = t.reshape(b, l, dilation, h, d).transpose(0, 2, 1, 3, 4)
        return jnp.pad(t, ((0, 0), (0, 0), (0, lp - l), (0, 0), (0, 0)))

    def key_windows(t):
        t = jnp.pad(to_sub(t), ((0, 0), (0, 0), (qb, qb), (0, 0), (0, 0)))
        t = t.reshape(b, dilation, nb + 2, qb, h, d)
        return jnp.concatenate([t[:, :, :-2], t[:, :, 1:-1], t[:, :, 2:]], axis=3)

    qs = to_sub(q).reshape(b, dilation, nb, qb, h, d)
    ks = key_windows(k)
    vs = key_windows(v)
    qi = jnp.arange(nb)[:, None] * qb + jnp.arange(qb)[None, :]
    kj = (jnp.arange(nb)[:, None] - 1) * qb + jnp.arange(3 * qb)[None, :]
    valid = ((jnp.abs(qi[:, :, None] - kj[:, None, :]) <= nbr)
             & (kj[:, None, :] >= 0) & (kj[:, None, :] < l))
    sc = jnp.einsum('brnqhd,brnkhd->brnhqk', qs, ks).astype(jnp.float32) * ATTN_SCALE
    sc = jnp.where(valid[None, None, :, None], sc, MASK_VALUE)
    m = sc.max(-1, keepdims=True)
    p = jnp.exp(sc - m)
    den = p.sum(-1, keepdims=True)
    o = jnp.einsum('brnhqk,brnkhd->brnqhd', (p / den).astype(v.dtype), vs)
    lse = jnp.swapaxes((m + jnp.log(den))[..., 0], 3, 4)

    def from_sub(t):
        t = t.reshape((b, dilation, lp) + t.shape[4:])[:, :, :l]
        return jnp.swapaxes(t, 1, 2).reshape((b, s) + t.shape[3:])

    return from_sub(o), from_sub(lse)


def _token_mixing(x, w_in, b_gate, conv_w, w_attn_out, w_conv_out, w_o):
    b, s, _ = x.shape
    proj = jnp.einsum('bsd,de->bse', x, w_in)
    q, k, v, conv_b, conv_c, conv_h, g_attn, g_conv = jnp.split(proj, SPLITS, axis=-1)
    bg_attn, bg_conv = jnp.split(b_gate, 2)
    pos = jnp.arange(s)
    q = _partial_rope(q.reshape(b, s, N_HEADS, HEAD_DIM), pos)
    k = _partial_rope(k.reshape(b, s, N_HEADS, HEAD_DIM), pos)
    v = v.reshape(b, s, N_HEADS, HEAD_DIM)
    outs, lses = [], []
    for gi, (window, dilation) in enumerate(DILATED_GROUPS):
        hs = slice(gi * HEADS_PER_GROUP, (gi + 1) * HEADS_PER_GROUP)
        o, lse = _dilated_group_attention(q[:, :, hs], k[:, :, hs], v[:, :, hs], window, dilation)
        outs.append(o)
        lses.append(lse)
    weights = jax.nn.softmax(jnp.stack(lses, axis=0), axis=0)
    attn = jnp.concatenate([o * weights[gi][..., None].astype(o.dtype) for gi, o in enumerate(outs)],
                           axis=2).reshape(b, s, D_ATTN)
    attn_branch = jnp.einsum('bse,ed->bsd', attn, w_attn_out)
    u = conv_c * conv_h
    conv = lax.conv_general_dilated(u, conv_w[:, None, :], window_strides=(1,),
                                    padding=((CONV_WIDTH // 2, CONV_WIDTH // 2),),
                                    dimension_numbers=('NWC', 'WIO', 'NWC'),
                                    feature_group_count=D_CONV)
    conv_branch = jnp.einsum('bse,ed->bsd', conv_b * conv, w_conv_out)
    merged = (jax.nn.sigmoid(g_attn + bg_attn) * attn_branch
              + jax.nn.sigmoid(g_conv + bg_conv) * conv_branch)
    return jnp.einsum('bsd,de->bse', merged, w_o)


def _hierarchical_moe(x, w_route_group, b_route_group, w_route_expert, b_route_expert,
                      w_gate, w_up, w_down):
    b, s, d = x.shape
    t = b * s
    xf = x.reshape(t, d)
    grp_logits = jnp.einsum('td,dg->tg', xf, w_route_group).astype(jnp.float32) + b_route_group.astype(jnp.float32)
    grp_prob = jax.nn.softmax(grp_logits, axis=-1)
    grp_w, grp_idx = lax.top_k(grp_prob, 1)
    exp_logits = (jnp.einsum('td,de->te', xf, w_route_expert).astype(jnp.float32)
                  + b_route_expert.astype(jnp.float32)).reshape(t, N_EXPERT_GROUPS, EXPERTS_PER_GROUP)
    sel = exp_logits[jnp.arange(t), grp_idx[:, 0]]
    top_v, top_i = lax.top_k(sel, TOP_K_INNER)
    wts = jax.nn.softmax(top_v, axis=-1) * grp_w
    eid = (grp_idx * EXPERTS_PER_GROUP + top_i).reshape(-1).astype(jnp.int32)
    tok = jnp.repeat(jnp.arange(t, dtype=jnp.int32), TOP_K_INNER)
    wflat = wts.reshape(-1)
    order = jnp.argsort(eid)
    eid_s, tok_s, w_s = eid[order], tok[order], wflat[order]
    counts = jnp.zeros((N_EXPERTS,), jnp.int32).at[eid].add(1)
    starts = jnp.cumsum(counts) - counts
    padded = (counts + EXPERT_BLOCK - 1) // EXPERT_BLOCK * EXPERT_BLOCK
    pad_end = jnp.cumsum(padded)
    pad_start = pad_end - padded
    n_assign = t * TOP_K_INNER
    n_blocks = -(-n_assign // EXPERT_BLOCK) + N_EXPERTS
    slots = n_blocks * EXPERT_BLOCK
    dest = pad_start[eid_s] + jnp.arange(n_assign, dtype=jnp.int32) - starts[eid_s]
    slot_tok = jnp.full((slots,), t, jnp.int32).at[dest].set(tok_s)
    slot_w = jnp.zeros((slots,), jnp.float32).at[dest].set(w_s)
    block_e = jnp.minimum(jnp.searchsorted(pad_end, jnp.arange(n_blocks, dtype=jnp.int32) * EXPERT_BLOCK,
                                           side='right'), N_EXPERTS - 1)
    x_pad = jnp.concatenate([xf, jnp.zeros((1, d), xf.dtype)], axis=0)
    xb = x_pad[slot_tok].reshape(n_blocks, EXPERT_BLOCK, d)

    def expert_block(args):
        xblk, e = args
        hdn = jax.nn.silu(xblk @ w_gate[e]) * (xblk @ w_up[e])
        return hdn @ w_down[e]

    yb = lax.map(expert_block, (xb, block_e)).reshape(slots, d)
    y = jnp.zeros((t + 1, d), x.dtype).at[slot_tok].add(yb * slot_w[:, None].astype(x.dtype))
    return y[:t].reshape(b, s, d)


def _trunk(x, ln_in_g, ln_in_b, w_in, b_gate, conv_w, w_attn_out, w_conv_out, w_o, ln1_g, ln1_b,
           w_route_group, b_route_group, w_route_expert, b_route_expert, w_gate, w_up, w_down,
           ln2_g, ln2_b):
    h = _layer_norm(x, ln_in_g, ln_in_b)
    for i in range(DEPTH):
        mix = _token_mixing(h, w_in[i], b_gate[i], conv_w[i], w_attn_out[i], w_conv_out[i], w_o[i])
        h = _layer_norm(DEEPNORM_ALPHA * h + mix, ln1_g[i], ln1_b[i])
        ffn = _hierarchical_moe(h, w_route_group[i], b_route_group[i], w_route_expert[i],
                                b_route_expert[i], w_gate[i], w_up[i], w_down[i])
        h = _layer_norm(DEEPNORM_ALPHA * h + ffn, ln2_g[i], ln2_b[i])
    return h


def setup_inputs(seed: int = 0) -> dict:
    key = jax.random.key(seed)
    ks = jax.random.split(key, 24)
    f32 = jnp.float32

    def nrm(k, shape, scale):
        return jax.random.normal(k, shape, f32) * scale

    return {
        'x_prompt': nrm(ks[0], (BATCH, SEQ, D_MODEL), 1.0),
        'x_sample': nrm(ks[1], (DEC_BATCH, DEC_SEQ, D_MODEL), 1.0),
        'ln_in_g': 1.0 + nrm(ks[2], (D_MODEL,), 0.02),
        'ln_in_b': nrm(ks[3], (D_MODEL,), 0.02),
        'w_in': nrm(ks[4], (DEPTH, D_MODEL, D_IN), D_MODEL ** -0.5),
        'b_gate': nrm(ks[5], (DEPTH, 2 * D_MODEL), 0.02),
        'conv_w': nrm(ks[6], (DEPTH, CONV_WIDTH, D_CONV), CONV_WIDTH ** -0.5),
        'w_attn_out': nrm(ks[7], (DEPTH, D_ATTN, D_MODEL), D_ATTN ** -0.5),
        'w_conv_out': nrm(ks[8], (DEPTH, D_CONV, D_MODEL), D_CONV ** -0.5),
        'w_o': nrm(ks[9], (DEPTH, D_MODEL, D_MODEL), DEEPNORM_BETA * D_MODEL ** -0.5),
        'ln1_g': 1.0 + nrm(ks[10], (DEPTH, D_MODEL), 0.02),
        'ln1_b': nrm(ks[11], (DEPTH, D_MODEL), 0.02),
        'w_route_group': nrm(ks[12], (DEPTH, D_MODEL, N_EXPERT_GROUPS), D_MODEL ** -0.5),
        'b_route_group': nrm(ks[13], (DEPTH, N_EXPERT_GROUPS), 0.01),
        'w_route_expert': nrm(ks[14], (DEPTH, D_MODEL, N_EXPERTS), D_MODEL ** -0.5),
        'b_route_expert': nrm(ks[15], (DEPTH, N_EXPERTS), 0.01),
        'w_gate': nrm(ks[16], (DEPTH, N_EXPERTS, D_MODEL, D_EXPERT), D_MODEL ** -0.5),
        'w_up': nrm(ks[17], (DEPTH, N_EXPERTS, D_MODEL, D_EXPERT), D_MODEL ** -0.5),
        'w_down': nrm(ks[18], (DEPTH, N_EXPERTS, D_EXPERT, D_MODEL), DEEPNORM_BETA * D_EXPERT ** -0.5),
        'ln2_g': 1.0 + nrm(ks[19], (DEPTH, D_MODEL), 0.02),
        'ln2_b': nrm(ks[20], (DEPTH, D_MODEL), 0.02),
    }


def reference(x_prompt, x_sample, ln_in_g, ln_in_b, w_in, b_gate, conv_w, w_attn_out, w_conv_out,
              w_o, ln1_g, ln1_b, w_route_group, b_route_group, w_route_expert, b_route_expert,
              w_gate, w_up, w_down, ln2_g, ln2_b):
    y_prompt = _trunk(x_prompt, ln_in_g, ln_in_b, w_in, b_gate, conv_w, w_attn_out, w_conv_out, w_o,
                      ln1_g, ln1_b, w_route_group, b_route_group, w_route_expert, b_route_expert,
                      w_gate, w_up, w_down, ln2_g, ln2_b)
    y_sample = _trunk(x_sample, ln_in_g, ln_in_b, w_in, b_gate, conv_w, w_attn_out, w_conv_out, w_o,
                      ln1_g, ln1_b, w_route_group, b_route_group, w_route_expert, b_route_expert,
                      w_gate, w_up, w_down, ln2_g, ln2_b)
    return (y_prompt, y_sample)
```

```python
import functools

import jax
import jax.numpy as jnp
from jax import lax
from jax.experimental import pallas as pl
from jax.experimental.pallas import tpu as pltpu

F32 = jnp.float32
BF16 = jnp.bfloat16

D_MODEL = 1024
HEAD_DIM = 64
HEADS_PER_GROUP = 4
GROUP_W = HEADS_PER_GROUP * HEAD_DIM
DILATIONS = (1, 4, 16)
BAND = 64
N_GROUPS = len(DILATIONS)
D_ATTN = N_GROUPS * GROUP_W
ATTN_SCALE = HEAD_DIM ** -0.5
ROPE_DIM = HEAD_DIM // 4
ROPE_THETA = 500000.0
MASK_VALUE = -1e30
D_CONV = D_MODEL
COL_CONV_B = 3 * D_ATTN
COL_CONV_C = COL_CONV_B + D_CONV
COL_CONV_H = COL_CONV_C + D_CONV
COL_GATE_ATTN = COL_CONV_H + D_CONV
COL_GATE_CONV = COL_GATE_ATTN + D_MODEL
N_EXPERT_GROUPS = 4
EXPERTS_PER_GROUP = 8
N_EXPERTS = N_EXPERT_GROUPS * EXPERTS_PER_GROUP
TOP_K_INNER = 2
D_EXPERT = 512
EXPERT_BLOCK = 256
LN_EPS = 1e-5
DEPTH = 1
DEEPNORM_ALPHA = (2 * DEPTH) ** 0.25

LANES = 128
ROW_TILE = 512
HALO = 16
Q_BLOCK = 128
K_BLOCK = Q_BLOCK + 2 * BAND
Q_CHUNK = 2048
COMBINE_TILE = 256
ROUTE_LANE0 = N_EXPERT_GROUPS
VMEM_LIMIT = 56 * 1024 * 1024


def _layer_norm(v, g, b):
    mu = jnp.mean(v, axis=-1, keepdims=True)
    d = v - mu
    var = jnp.mean(d * d, axis=-1, keepdims=True)
    return d * lax.rsqrt(var + LN_EPS) * g + b


def _dot(a, b):
    return jnp.dot(a, b, preferred_element_type=F32)


def _resident(shape):
    nd = len(shape)
    return pl.BlockSpec(shape, lambda *_: (0,) * nd, pipeline_mode=pl.Buffered(1))


def _inproj_kernel(x_ref, xp_ref, xn_ref, g_ref, b_ref, win_ref, bg_ref, cw_ref, wco_ref,
                   cos_ref, sa_ref, sb_ref,
                   qkv0_ref, qkv1_ref, qkv2_ref, cg_ref, ga_ref,
                   hext_ref, u_ref, stage_ref):
    i = pl.program_id(1)
    nt = pl.num_programs(1)
    tm = x_ref.shape[0]
    g = g_ref[...]
    b = b_ref[...]
    hext_ref[pl.ds(HALO, tm), :] = _layer_norm(x_ref[...], g, b).astype(BF16)
    hext_ref[pl.ds(0, HALO), :] = _layer_norm(xp_ref[...], g, b).astype(BF16)
    hext_ref[pl.ds(HALO + tm, HALO), :] = _layer_norm(xn_ref[...], g, b).astype(BF16)
    hm = hext_ref[pl.ds(HALO, tm), :]
    he = hext_ref[...]

    def proj(lhs, col, n):
        return _dot(lhs, win_ref[:, col:col + n])

    u = proj(he, COL_CONV_C, D_CONV) * proj(he, COL_CONV_H, D_CONV)
    rows = lax.broadcasted_iota(jnp.int32, (tm + 2 * HALO, 1), 0)
    lo = jnp.where(i == 0, HALO, 0)
    hi = jnp.where(i == nt - 1, HALO + tm, tm + 2 * HALO)
    u_ref[...] = jnp.where((rows >= lo) & (rows < hi), u, 0.0)
    cw = cw_ref[...]
    conv = (cw[0:1] * u_ref[pl.ds(HALO - 1, tm), :] + cw[1:2] * u_ref[pl.ds(HALO, tm), :]
            + cw[2:3] * u_ref[pl.ds(HALO + 1, tm), :])
    cb = proj(hm, COL_CONV_B, D_CONV)
    conv_branch = _dot((cb * conv).astype(BF16), wco_ref[...])
    gate_conv = jax.nn.sigmoid(proj(hm, COL_GATE_CONV, D_MODEL) + bg_ref[:, D_MODEL:])
    cg_ref[...] = (gate_conv * conv_branch).astype(BF16)
    ga_ref[...] = jax.nn.sigmoid(proj(hm, COL_GATE_ATTN, D_MODEL) + bg_ref[:, :D_MODEL]).astype(BF16)

    cosv = cos_ref[...]
    sav = sa_ref[...]
    sbv = sb_ref[...]
    half = ROPE_DIM // 2
    for gi, (out_ref, r) in enumerate(zip((qkv0_ref, qkv1_ref, qkv2_ref), DILATIONS)):
        for t in range(3):
            p = proj(hm, t * D_ATTN + gi * GROUP_W, GROUP_W)
            if t < 2:
                parts = []
                for c in range(GROUP_W // LANES):
                    pc = p[:, c * LANES:(c + 1) * LANES]
                    parts.append(pc * cosv + pltpu.roll(pc, LANES - half, 1) * sav
                                 + pltpu.roll(pc, half, 1) * sbv)
                p = jnp.concatenate(parts, axis=1)
                if t == 0:
                    p = p * ATTN_SCALE
            if r == 1:
                out_ref[t, 0] = p.astype(BF16)
            else:
                for c in range(GROUP_W // LANES):
                    stage_ref[c] = p[:, c * LANES:(c + 1) * LANES]
                for res in range(r):
                    for c in range(GROUP_W // LANES):
                        out_ref[t, res, :, c * LANES:(c + 1) * LANES] = (
                            stage_ref[c, pl.ds(res, tm // r, stride=r), :].astype(BF16))


def _rope_tables(s):
    half = ROPE_DIM // 2
    inv_freq = ROPE_THETA ** (-jnp.arange(half, dtype=F32) * 2.0 / ROPE_DIM)
    ang = jnp.arange(s).astype(F32)[:, None] * inv_freq[None, :]
    cos = jnp.cos(ang)
    sin = jnp.sin(ang)
    pad = jnp.zeros((s, HEAD_DIM - ROPE_DIM), F32)
    zero = jnp.zeros((s, half), F32)
    cos_t = jnp.concatenate([cos, cos, pad + 1.0], axis=1)
    sa_t = jnp.concatenate([-sin, zero, pad], axis=1)
    sb_t = jnp.concatenate([zero, sin, pad], axis=1)
    rep = LANES // HEAD_DIM
    return tuple(jnp.tile(t, (1, rep)) for t in (cos_t, sa_t, sb_t))


def _inproj(x, ln_g, ln_b, w_in, b_gate, conv_w, w_conv_out):
    bsz, s, d = x.shape
    tm = ROW_TILE
    nt = s // tm
    hpt = tm // HALO
    cos_t, sa_t, sb_t = _rope_tables(s)
    row = lambda b, i: (b, i, 0)
    tab = pl.BlockSpec((tm, LANES), lambda b, i: (i, 0))
    qkv_shapes = [jax.ShapeDtypeStruct((3, bsz, r, s // r, GROUP_W), BF16) for r in DILATIONS]
    qkv_specs = [pl.BlockSpec((3, None, r, tm // r, GROUP_W), lambda b, i: (0, b, 0, i, 0))
                 for r in DILATIONS]
    return pl.pallas_call(
        _inproj_kernel,
        grid=(bsz, nt),
        in_specs=[
            pl.BlockSpec((None, tm, d), row),
            pl.BlockSpec((None, HALO, d), lambda b, i: (b, jnp.maximum(i * hpt - 1, 0), 0)),
            pl.BlockSpec((None, HALO, d), lambda b, i: (b, jnp.minimum((i + 1) * hpt, s // HALO - 1), 0)),
            _resident((1, d)), _resident((1, d)),
            _resident(w_in.shape), _resident((1, 2 * d)), _resident(conv_w.shape),
            _resident(w_conv_out.shape),
            tab, tab, tab,
        ],
        out_specs=qkv_specs + [pl.BlockSpec((None, tm, d), row), pl.BlockSpec((None, tm, d), row)],
        out_shape=qkv_shapes + [jax.ShapeDtypeStruct((bsz, s, d), BF16)] * 2,
        scratch_shapes=[
            pltpu.VMEM((tm + 2 * HALO, d), BF16),
            pltpu.VMEM((tm + 2 * HALO, D_CONV), F32),
            pltpu.VMEM((GROUP_W // LANES, tm, LANES), F32),
        ],
        compiler_params=pltpu.CompilerParams(
            dimension_semantics=("parallel", "parallel"), vmem_limit_bytes=VMEM_LIMIT),
        name="inproj",
    )(x, x, x, ln_g, ln_b, w_in, b_gate, conv_w, w_conv_out, cos_t, sa_t, sb_t)


def _attn_kernel(q_ref, k_ref, v_ref, o_ref, lse_ref, *, sub_len):
    s = k_ref.shape[0]
    qc = q_ref.shape[0]
    base = pl.program_id(1) * qc
    lane = lax.broadcasted_iota(jnp.int32, (1, GROUP_W), 1)
    head_masks = [(lane >= h * HEAD_DIM) & (lane < (h + 1) * HEAD_DIM) for h in range(HEADS_PER_GROUP)]
    qi = lax.broadcasted_iota(jnp.int32, (Q_BLOCK, 1), 0)
    kj = lax.broadcasted_iota(jnp.int32, (1, K_BLOCK), 1)

    def body(j, carry):
        r0 = pl.multiple_of(j * Q_BLOCK, Q_BLOCK)
        s0 = base + r0
        k0 = pl.multiple_of(jnp.clip(s0 - BAND, 0, s - K_BLOCK), BAND)
        q = q_ref[pl.ds(r0, Q_BLOCK), :]
        k = k_ref[pl.ds(k0, K_BLOCK), :]
        v = v_ref[pl.ds(k0, K_BLOCK), :]
        sub_lo = (s0 // sub_len) * sub_len
        qpos = s0 + qi
        kpos = k0 + kj
        valid = (jnp.abs(qpos - kpos) <= BAND) & (kpos >= sub_lo) & (kpos < sub_lo + sub_len)
        o_acc = jnp.zeros((Q_BLOCK, GROUP_W), F32)
        lse_acc = jnp.zeros((Q_BLOCK, GROUP_W), F32)
        for hmask in head_masks:
            qh = jnp.where(hmask, q, jnp.zeros_like(q))
            sc = lax.dot_general(qh, k, (((1,), (1,)), ((), ())), preferred_element_type=F32)
            sc = jnp.where(valid, sc, MASK_VALUE)
            m = jnp.max(sc, axis=-1, keepdims=True)
            p = jnp.exp(sc - m)
            den = jnp.sum(p, axis=-1, keepdims=True)
            oh = _dot((p / den).astype(BF16), v)
            o_acc = jnp.where(hmask, oh, o_acc)
            lse_acc = jnp.where(hmask, m + jnp.log(den), lse_acc)
        o_ref[pl.ds(r0, Q_BLOCK), :] = o_acc.astype(BF16)
        lse_ref[pl.ds(r0, Q_BLOCK), :] = lse_acc
        return carry

    lax.fori_loop(0, qc // Q_BLOCK, body, 0)


def _attention(qkv, sub_len):
    _, bsz, s, w = qkv.shape
    qc = min(Q_CHUNK, s)
    kv_spec = lambda t: pl.BlockSpec((None, None, s, w), lambda b, j: (t, b, 0, 0))
    return pl.pallas_call(
        functools.partial(_attn_kernel, sub_len=sub_len),
        grid=(bsz, s // qc),
        in_specs=[pl.BlockSpec((None, None, qc, w), lambda b, j: (0, b, j, 0)), kv_spec(1), kv_spec(2)],
        out_specs=[pl.BlockSpec((None, qc, w), lambda b, j: (b, j, 0))] * 2,
        out_shape=[jax.ShapeDtypeStruct((bsz, s, w), BF16), jax.ShapeDtypeStruct((bsz, s, w), F32)],
        compiler_params=pltpu.CompilerParams(
            dimension_semantics=("parallel", "arbitrary"), vmem_limit_bytes=VMEM_LIMIT),
        name="attn",
    )(qkv, qkv, qkv)


def _first_index_of_max(vals, lane):
    mx = jnp.max(vals, axis=-1, keepdims=True)
    idx = jnp.min(jnp.where(vals == mx, lane, LANES), axis=-1, keepdims=True)
    return mx, idx


def _mix_kernel(x_ref, o0_ref, o1_ref, o2_ref, l0_ref, l1_ref, l2_ref, cg_ref, ga_ref,
                g_ref, b_ref, wao_ref, wo_ref, g1_ref, b1_ref, wrh_ref, wrl_ref, br_ref,
                h1_ref, route_ref, so_ref, sl_ref):
    tm = x_ref.shape[0]
    h = _layer_norm(x_ref[...], g_ref[...], b_ref[...])
    halves = GROUP_W // LANES
    for gi, (o_ref, l_ref, r) in enumerate(zip((o0_ref, o1_ref, o2_ref), (l0_ref, l1_ref, l2_ref), DILATIONS)):
        for c in range(halves):
            cols = slice(c * LANES, (c + 1) * LANES)
            for res in range(r):
                rows = pl.ds(res, tm // r, stride=r) if r > 1 else slice(None)
                so_ref[gi * halves + c, rows, :] = o_ref[res, :, cols].astype(F32)
                sl_ref[gi * halves + c, rows, :] = l_ref[res, :, cols]

    def natural(ref, gi):
        return jnp.concatenate([ref[gi * halves + c] for c in range(halves)], axis=1)

    lses = [natural(sl_ref, gi) for gi in range(N_GROUPS)]
    mx = jnp.maximum(jnp.maximum(lses[0], lses[1]), lses[2])
    es = [jnp.exp(l - mx) for l in lses]
    den = es[0] + es[1] + es[2]
    attn_branch = jnp.zeros((tm, D_MODEL), F32)
    for gi in range(N_GROUPS):
        a = (natural(so_ref, gi) * (es[gi] / den)).astype(BF16)
        attn_branch = attn_branch + _dot(a, wao_ref[gi * GROUP_W:(gi + 1) * GROUP_W, :])
    merged = ga_ref[...].astype(F32) * attn_branch + cg_ref[...].astype(F32)
    mix = _dot(merged.astype(BF16), wo_ref[...])
    h1 = _layer_norm(DEEPNORM_ALPHA * h + mix, g1_ref[...], b1_ref[...])
    h1_ref[...] = h1

    hi = h1.astype(BF16)
    lo = (h1 - hi.astype(F32)).astype(BF16)
    wrh = wrh_ref[...]
    logits = _dot(hi, wrh) + _dot(lo, wrh) + _dot(hi, wrl_ref[...]) + br_ref[...]
    lane = lax.broadcasted_iota(jnp.int32, (1, LANES), 1)
    is_grp = lane < N_EXPERT_GROUPS
    gl = jnp.where(is_grp, logits, MASK_VALUE)
    ge = jnp.exp(gl - jnp.max(gl, axis=-1, keepdims=True))
    gp = jnp.where(is_grp, ge / jnp.sum(ge, axis=-1, keepdims=True), -1.0)
    grp_w, grp_idx = _first_index_of_max(gp, lane)
    e_lo = ROUTE_LANE0 + grp_idx * EXPERTS_PER_GROUP
    sel = jnp.where((lane >= e_lo) & (lane < e_lo + EXPERTS_PER_GROUP), logits, MASK_VALUE)
    v1, i1 = _first_index_of_max(sel, lane)
    v2, i2 = _first_index_of_max(jnp.where(lane == i1, MASK_VALUE, sel), lane)
    e2 = jnp.exp(v2 - v1)
    w1 = grp_w / (1.0 + e2)
    w2 = grp_w * e2 / (1.0 + e2)
    route = jnp.where(lane == 0, (i1 - ROUTE_LANE0).astype(F32),
                      jnp.where(lane == 1, (i2 - ROUTE_LANE0).astype(F32),
                                jnp.where(lane == 2, w1, jnp.where(lane == 3, w2, 0.0))))
    route_ref[...] = route


def _mix(x, o_list, lse_list, cg, ga, ln_g, ln_b, w_attn_out, w_o, ln1_g, ln1_b, wr_hi, wr_lo, b_route):
    bsz, s, d = x.shape
    tm = ROW_TILE
    row = lambda b, i: (b, i, 0)
    dil_specs = [pl.BlockSpec((None, r, tm // r, GROUP_W), lambda b, i: (b, 0, i, 0)) for r in DILATIONS]
    o_views = [o.reshape(bsz, r, s // r, GROUP_W) for o, r in zip(o_list, DILATIONS)]
    l_views = [l.reshape(bsz, r, s // r, GROUP_W) for l, r in zip(lse_list, DILATIONS)]
    return pl.pallas_call(
        _mix_kernel,
        grid=(bsz, s // tm),
        in_specs=[pl.BlockSpec((None, tm, d), row)] + dil_specs + dil_specs + [
            pl.BlockSpec((None, tm, d), row), pl.BlockSpec((None, tm, d), row),
            _resident((1, d)), _resident((1, d)),
            _resident(w_attn_out.shape), _resident(w_o.shape),
            _resident((1, d)), _resident((1, d)),
            _resident(wr_hi.shape), _resident(wr_lo.shape), _resident((1, LANES)),
        ],
        out_specs=[pl.BlockSpec((None, tm, d), row), pl.BlockSpec((None, tm, LANES), row)],
        out_shape=[jax.ShapeDtypeStruct((bsz, s, d), F32), jax.ShapeDtypeStruct((bsz, s, LANES), F32)],
        scratch_shapes=[pltpu.VMEM((N_GROUPS * GROUP_W // LANES, tm, LANES), F32)] * 2,
        compiler_params=pltpu.CompilerParams(
            dimension_semantics=("parallel", "parallel"), vmem_limit_bytes=VMEM_LIMIT),
        name="mix",
    )(x, *o_views, *l_views, cg, ga, ln_g, ln_b, w_attn_out, w_o, ln1_g, ln1_b, wr_hi, wr_lo, b_route)


def _row_copy(src_hbm, row, dst_buf, slot, j, sem):
    return pltpu.make_async_copy(src_hbm.at[pl.ds(row, 1), :], dst_buf.at[slot, pl.ds(j, 1), :], sem.at[slot])


def _expert_kernel(be_ref, nused_ref, tok_ref, tokn_ref, h1_hbm, wg_ref, wu_ref, wd_ref, y_ref, xbuf, sem):
    i = pl.program_id(0)
    nused = nused_ref[0]
    slot = i % 2

    def issue(idx_ref, s):
        def body(j, carry):
            _row_copy(h1_hbm, idx_ref[0, j], xbuf, s, j, sem).start()
            return carry
        lax.fori_loop(0, EXPERT_BLOCK, body, 0)

    @pl.when(i == 0)
    def _():
        issue(tok_ref, 0)

    @pl.when(i + 1 < nused)
    def _():
        issue(tokn_ref, 1 - slot)

    @pl.when(i < nused)
    def _():
        pltpu.make_async_copy(h1_hbm.at[pl.ds(0, EXPERT_BLOCK), :], xbuf.at[slot], sem.at[slot]).wait()
        xb = xbuf[slot].astype(BF16)
        gate = _dot(xb, wg_ref[...])
        up = _dot(xb, wu_ref[...])
        hidden = (gate * jax.nn.sigmoid(gate) * up).astype(BF16)
        y_ref[...] = _dot(hidden, wd_ref[...])

    @pl.when(i >= nused)
    def _():
        y_ref[...] = jnp.zeros_like(y_ref)


def _experts(h1, slot_tok, block_e, nused, w_gate, w_up, w_down):
    t, d = h1.shape
    n_blocks = block_e.shape[0]
    tok3 = slot_tok.reshape(n_blocks, 1, EXPERT_BLOCK)
    smem_blk = lambda f: pl.BlockSpec((None, 1, EXPERT_BLOCK), f, memory_space=pltpu.SMEM)
    grid_spec = pltpu.PrefetchScalarGridSpec(
        num_scalar_prefetch=2,
        grid=(n_blocks,),
        in_specs=[
            smem_blk(lambda i, be, nu: (i, 0, 0)),
            smem_blk(lambda i, be, nu: (jnp.minimum(i + 1, n_blocks - 1), 0, 0)),
            pl.BlockSpec(memory_space=pl.ANY),
            pl.BlockSpec((None, d, D_EXPERT), lambda i, be, nu: (be[i], 0, 0)),
            pl.BlockSpec((None, d, D_EXPERT), lambda i, be, nu: (be[i], 0, 0)),
            pl.BlockSpec((None, D_EXPERT, d), lambda i, be, nu: (be[i], 0, 0)),
        ],
        out_specs=pl.BlockSpec((EXPERT_BLOCK, d), lambda i, be, nu: (i, 0)),
        scratch_shapes=[pltpu.VMEM((2, EXPERT_BLOCK, d), F32), pltpu.SemaphoreType.DMA((2,))],
    )
    return pl.pallas_call(
        _expert_kernel,
        grid_spec=grid_spec,
        out_shape=jax.ShapeDtypeStruct((n_blocks * EXPERT_BLOCK, d), F32),
        compiler_params=pltpu.CompilerParams(
            dimension_semantics=("arbitrary",), vmem_limit_bytes=VMEM_LIMIT),
        name="experts",
    )(block_e, nused, tok3, tok3, h1, w_gate, w_up, w_down)


def _combine_kernel(idx_ref, idxn_ref, yb_hbm, h1_ref, route_ref, g_ref, b_ref, y_ref, gbuf, sem):
    i = pl.program_id(0)
    n = pl.num_programs(0)
    tm = h1_ref.shape[0]
    slot = i % 2

    def issue(src_ref, s):
        def body(j, carry):
            _row_copy(yb_hbm, src_ref[0, j], gbuf, s, j, sem).start()
            return carry
        lax.fori_loop(0, TOP_K_INNER * tm, body, 0)

    @pl.when(i == 0)
    def _():
        issue(idx_ref, 0)

    @pl.when(i + 1 < n)
    def _():
        issue(idxn_ref, 1 - slot)

    pltpu.make_async_copy(yb_hbm.at[pl.ds(0, TOP_K_INNER * tm), :], gbuf.at[slot], sem.at[slot]).wait()
    route = route_ref[...]
    ffn = route[:, 2:3] * gbuf[slot, pl.ds(0, tm), :] + route[:, 3:4] * gbuf[slot, pl.ds(tm, tm), :]
    y_ref[...] = _layer_norm(DEEPNORM_ALPHA * h1_ref[...] + ffn, g_ref[...], b_ref[...])


def _combine(h1, route, yb, dest, ln2_g, ln2_b):
    t, d = h1.shape
    tm = COMBINE_TILE
    nt = t // tm
    idx3 = dest.reshape(nt, tm, TOP_K_INNER).transpose(0, 2, 1).reshape(nt, 1, TOP_K_INNER * tm)
    smem_blk = lambda f: pl.BlockSpec((None, 1, TOP_K_INNER * tm), f, memory_space=pltpu.SMEM)
    return pl.pallas_call(
        _combine_kernel,
        grid=(nt,),
        in_specs=[
            smem_blk(lambda i: (i, 0, 0)),
            smem_blk(lambda i: (jnp.minimum(i + 1, nt - 1), 0, 0)),
            pl.BlockSpec(memory_space=pl.ANY),
            pl.BlockSpec((tm, d), lambda i: (i, 0)),
            pl.BlockSpec((tm, LANES), lambda i: (i, 0)),
            _resident((1, d)), _resident((1, d)),
        ],
        out_specs=pl.BlockSpec((tm, d), lambda i: (i, 0)),
        out_shape=jax.ShapeDtypeStruct((t, d), F32),
        scratch_shapes=[pltpu.VMEM((2, TOP_K_INNER * tm, d), F32), pltpu.SemaphoreType.DMA((2,))],
        compiler_params=pltpu.CompilerParams(
            dimension_semantics=("arbitrary",), vmem_limit_bytes=VMEM_LIMIT),
        name="combine",
    )(idx3, idx3, yb, h1, route, ln2_g, ln2_b)


def _dispatch_plan(route, t):
    eid = route[:, :TOP_K_INNER].astype(jnp.int32).reshape(-1)
    n_assign = t * TOP_K_INNER
    onehot = (eid[:, None] == jnp.arange(N_EXPERTS, dtype=jnp.int32)[None, :]).astype(jnp.int32)
    csum = jnp.cumsum(onehot, axis=0)
    counts = csum[-1]
    rank = jnp.sum(jnp.where(onehot > 0, csum, 0), axis=1) - 1
    padded = (counts + EXPERT_BLOCK - 1) // EXPERT_BLOCK * EXPERT_BLOCK
    pad_end = jnp.cumsum(padded)
    pad_start = pad_end - padded
    dest = pad_start[eid] + rank
    n_blocks = -(-n_assign // EXPERT_BLOCK) + N_EXPERTS
    tok = jnp.arange(n_assign, dtype=jnp.int32) // TOP_K_INNER
    slot_tok = jnp.zeros((n_blocks * EXPERT_BLOCK,), jnp.int32).at[dest].set(tok)
    block_e = jnp.minimum(
        jnp.searchsorted(pad_end, jnp.arange(n_blocks, dtype=jnp.int32) * EXPERT_BLOCK, side='right'),
        N_EXPERTS - 1).astype(jnp.int32)
    nused = (pad_end[-1:] // EXPERT_BLOCK).astype(jnp.int32)
    return dest.astype(jnp.int32), slot_tok, block_e, nused


def _trunk(x, p):
    bsz, s, d = x.shape
    t = bsz * s
    qkv0, qkv1, qkv2, cg, ga = _inproj(x, p['ln_in_g'], p['ln_in_b'], p['w_in'], p['b_gate'],
                                       p['conv_w'], p['w_conv_out'])
    o_list, lse_list = [], []
    for qkv, r in zip((qkv0, qkv1, qkv2), DILATIONS):
        o, lse = _attention(qkv.reshape(3, bsz, s, GROUP_W), s // r)
        o_list.append(o)
        lse_list.append(lse)
    h1, route = _mix(x, o_list, lse_list, cg, ga, p['ln_in_g'], p['ln_in_b'], p['w_attn_out'], p['w_o'],
                     p['ln1_g'], p['ln1_b'], p['wr_hi'], p['wr_lo'], p['b_route'])
    h1 = h1.reshape(t, d)
    route = route.reshape(t, LANES)
    dest, slot_tok, block_e, nused = _dispatch_plan(route, t)
    yb = _experts(h1, slot_tok, block_e, nused, p['w_gate'], p['w_up'], p['w_down'])
    y = _combine(h1, route, yb, dest, p['ln2_g'], p['ln2_b'])
    return y.reshape(bsz, s, d)


def kernel(x_prompt, x_sample, ln_in_g, ln_in_b, w_in, b_gate, conv_w, w_attn_out, w_conv_out, w_o, ln1_g, ln1_b, w_route_group, b_route_group, w_route_expert, b_route_expert, w_gate, w_up, w_down, ln2_g, ln2_b):
    d = D_MODEL
    w_route = jnp.concatenate([w_route_group[0], w_route_expert[0]], axis=1)
    w_route = jnp.pad(w_route, ((0, 0), (0, LANES - w_route.shape[1])))
    wr_hi = w_route.astype(BF16)
    b_route = jnp.concatenate([b_route_group[0], b_route_expert[0]]).astype(F32)
    p = {
        'ln_in_g': ln_in_g.reshape(1, d), 'ln_in_b': ln_in_b.reshape(1, d),
        'w_in': w_in[0].astype(BF16), 'b_gate': b_gate[0].reshape(1, 2 * d), 'conv_w': conv_w[0],
        'w_conv_out': w_conv_out[0].astype(BF16), 'w_attn_out': w_attn_out[0].astype(BF16),
        'w_o': w_o[0].astype(BF16),
        'ln1_g': ln1_g[0].reshape(1, d), 'ln1_b': ln1_b[0].reshape(1, d),
        'wr_hi': wr_hi, 'wr_lo': (w_route - wr_hi.astype(F32)).astype(BF16),
        'b_route': jnp.pad(b_route, (0, LANES - b_route.shape[0])).reshape(1, LANES),
        'w_gate': w_gate[0].astype(BF16), 'w_up': w_up[0].astype(BF16), 'w_down': w_down[0].astype(BF16),
        'ln2_g': ln2_g[0].reshape(1, d), 'ln2_b': ln2_b[0].reshape(1, d),
    }
    return (_trunk(x_prompt, p), _trunk(x_sample, p))
```

```python
import functools

import jax
import jax.numpy as jnp
from jax import lax
from jax.experimental import pallas as pl
from jax.experimental.pallas import tpu as pltpu

F32 = jnp.float32
BF16 = jnp.bfloat16

D_MODEL = 1024
HEAD_DIM = 64
HEADS_PER_GROUP = 4
GROUP_W = HEADS_PER_GROUP * HEAD_DIM
DILATIONS = (1, 4, 16)
BAND = 64
N_GROUPS = len(DILATIONS)
D_ATTN = N_GROUPS * GROUP_W
ATTN_SCALE = HEAD_DIM ** -0.5
ROPE_DIM = HEAD_DIM // 4
ROPE_THETA = 500000.0
MASK_VALUE = -1e30
D_CONV = D_MODEL
COL_CONV_B = 3 * D_ATTN
COL_CONV_C = COL_CONV_B + D_CONV
COL_CONV_H = COL_CONV_C + D_CONV
COL_GATE_ATTN = COL_CONV_H + D_CONV
COL_GATE_CONV = COL_GATE_ATTN + D_MODEL
N_EXPERT_GROUPS = 4
EXPERTS_PER_GROUP = 8
N_EXPERTS = N_EXPERT_GROUPS * EXPERTS_PER_GROUP
TOP_K_INNER = 2
D_EXPERT = 512
EXPERT_BLOCK = 256
EXPERT_CHUNK = 256
LN_EPS = 1e-5
DEPTH = 1
DEEPNORM_ALPHA = (2 * DEPTH) ** 0.25

LANES = 128
ROW_CHUNKS = D_MODEL // LANES
ROW_TILE = 512
HALO = 16
Q_BLOCK = 128
K_BLOCK = Q_BLOCK + 2 * BAND
Q_CHUNK = 2048
COMBINE_TILE = 256
ISSUE_UNROLL = 16
ROUTE_LANE0 = N_EXPERT_GROUPS
VMEM_LIMIT = 56 * 1024 * 1024


def _layer_norm(v, g, b):
    mu = jnp.mean(v, axis=-1, keepdims=True)
    d = v - mu
    var = jnp.mean(d * d, axis=-1, keepdims=True)
    return d * lax.rsqrt(var + LN_EPS) * g + b


def _dot(a, b):
    return jnp.dot(a, b, preferred_element_type=F32)


def _store_token_major(ref, val):
    n = val.shape[0]
    for k in range(ROW_CHUNKS):
        ref[pl.ds(k, n, stride=ROW_CHUNKS), :] = val[:, k * LANES:(k + 1) * LANES]


def _load_token_major(ref, first_token, n):
    return jnp.concatenate(
        [ref[pl.ds(first_token * ROW_CHUNKS + k, n, stride=ROW_CHUNKS), :] for k in range(ROW_CHUNKS)], axis=1)


def _resident(shape):
    nd = len(shape)
    return pl.BlockSpec(shape, lambda *_: (0,) * nd, pipeline_mode=pl.Buffered(1))


def _inproj_kernel(x_ref, xp_ref, xn_ref, g_ref, b_ref, win_ref, bg_ref, cw_ref, wco_ref,
                   cos_ref, sa_ref, sb_ref,
                   qkv0_ref, qkv1_ref, qkv2_ref, cg_ref, ga_ref,
                   hext_ref, u_ref, stage_ref):
    i = pl.program_id(1)
    nt = pl.num_programs(1)
    tm = x_ref.shape[0]
    g = g_ref[...]
    b = b_ref[...]
    hext_ref[pl.ds(HALO, tm), :] = _layer_norm(x_ref[...], g, b).astype(BF16)
    hext_ref[pl.ds(0, HALO), :] = _layer_norm(xp_ref[...], g, b).astype(BF16)
    hext_ref[pl.ds(HALO + tm, HALO), :] = _layer_norm(xn_ref[...], g, b).astype(BF16)
    hm = hext_ref[pl.ds(HALO, tm), :]
    he = hext_ref[...]

    def proj(lhs, col, n):
        return _dot(lhs, win_ref[:, col:col + n])

    u = proj(he, COL_CONV_C, D_CONV) * proj(he, COL_CONV_H, D_CONV)
    rows = lax.broadcasted_iota(jnp.int32, (tm + 2 * HALO, 1), 0)
    lo = jnp.where(i == 0, HALO, 0)
    hi = jnp.where(i == nt - 1, HALO + tm, tm + 2 * HALO)
    u_ref[...] = jnp.where((rows >= lo) & (rows < hi), u, 0.0)
    cw = cw_ref[...]
    conv = (cw[0:1] * u_ref[pl.ds(HALO - 1, tm), :] + cw[1:2] * u_ref[pl.ds(HALO, tm), :]
            + cw[2:3] * u_ref[pl.ds(HALO + 1, tm), :])
    cb = proj(hm, COL_CONV_B, D_CONV)
    conv_branch = _dot((cb * conv).astype(BF16), wco_ref[...])
    gate_conv = jax.nn.sigmoid(proj(hm, COL_GATE_CONV, D_MODEL) + bg_ref[:, D_MODEL:])
    cg_ref[...] = (gate_conv * conv_branch).astype(BF16)
    ga_ref[...] = jax.nn.sigmoid(proj(hm, COL_GATE_ATTN, D_MODEL) + bg_ref[:, :D_MODEL]).astype(BF16)

    cosv = cos_ref[...]
    sav = sa_ref[...]
    sbv = sb_ref[...]
    half = ROPE_DIM // 2
    for gi, (out_ref, r) in enumerate(zip((qkv0_ref, qkv1_ref, qkv2_ref), DILATIONS)):
        for t in range(3):
            p = proj(hm, t * D_ATTN + gi * GROUP_W, GROUP_W)
            if t < 2:
                parts = []
                for c in range(GROUP_W // LANES):
                    pc = p[:, c * LANES:(c + 1) * LANES]
                    parts.append(pc * cosv + pltpu.roll(pc, LANES - half, 1) * sav
                                 + pltpu.roll(pc, half, 1) * sbv)
                p = jnp.concatenate(parts, axis=1)
                if t == 0:
                    p = p * ATTN_SCALE
            if r == 1:
                out_ref[t, 0] = p.astype(BF16)
            else:
                for c in range(GROUP_W // LANES):
                    stage_ref[c] = p[:, c * LANES:(c + 1) * LANES]
                for res in range(r):
                    for c in range(GROUP_W // LANES):
                        out_ref[t, res, :, c * LANES:(c + 1) * LANES] = (
                            stage_ref[c, pl.ds(res, tm // r, stride=r), :].astype(BF16))


def _rope_tables(s):
    half = ROPE_DIM // 2
    inv_freq = ROPE_THETA ** (-jnp.arange(half, dtype=F32) * 2.0 / ROPE_DIM)
    ang = jnp.arange(s).astype(F32)[:, None] * inv_freq[None, :]
    cos = jnp.cos(ang)
    sin = jnp.sin(ang)
    pad = jnp.zeros((s, HEAD_DIM - ROPE_DIM), F32)
    zero = jnp.zeros((s, half), F32)
    cos_t = jnp.concatenate([cos, cos, pad + 1.0], axis=1)
    sa_t = jnp.concatenate([-sin, zero, pad], axis=1)
    sb_t = jnp.concatenate([zero, sin, pad], axis=1)
    rep = LANES // HEAD_DIM
    return tuple(jnp.tile(t, (1, rep)) for t in (cos_t, sa_t, sb_t))


def _inproj(x, ln_g, ln_b, w_in, b_gate, conv_w, w_conv_out):
    bsz, s, d = x.shape
    tm = ROW_TILE
    nt = s // tm
    hpt = tm // HALO
    cos_t, sa_t, sb_t = _rope_tables(s)
    row = lambda b, i: (b, i, 0)
    tab = pl.BlockSpec((tm, LANES), lambda b, i: (i, 0))
    qkv_shapes = [jax.ShapeDtypeStruct((3, bsz, r, s // r, GROUP_W), BF16) for r in DILATIONS]
    qkv_specs = [pl.BlockSpec((3, None, r, tm // r, GROUP_W), lambda b, i: (0, b, 0, i, 0))
                 for r in DILATIONS]
    return pl.pallas_call(
        _inproj_kernel,
        grid=(bsz, nt),
        in_specs=[
            pl.BlockSpec((None, tm, d), row),
            pl.BlockSpec((None, HALO, d), lambda b, i: (b, jnp.maximum(i * hpt - 1, 0), 0)),
            pl.BlockSpec((None, HALO, d), lambda b, i: (b, jnp.minimum((i + 1) * hpt, s // HALO - 1), 0)),
            _resident((1, d)), _resident((1, d)),
            _resident(w_in.shape), _resident((1, 2 * d)), _resident(conv_w.shape),
            _resident(w_conv_out.shape),
            tab, tab, tab,
        ],
        out_specs=qkv_specs + [pl.BlockSpec((None, tm, d), row), pl.BlockSpec((None, tm, d), row)],
        out_shape=qkv_shapes + [jax.ShapeDtypeStruct((bsz, s, d), BF16)] * 2,
        scratch_shapes=[
            pltpu.VMEM((tm + 2 * HALO, d), BF16),
            pltpu.VMEM((tm + 2 * HALO, D_CONV), F32),
            pltpu.VMEM((GROUP_W // LANES, tm, LANES), F32),
        ],
        compiler_params=pltpu.CompilerParams(
            dimension_semantics=("parallel", "parallel"), vmem_limit_bytes=VMEM_LIMIT),
        name="inproj",
    )(x, x, x, ln_g, ln_b, w_in, b_gate, conv_w, w_conv_out, cos_t, sa_t, sb_t)


def _attn_kernel(q_ref, k_ref, v_ref, o_ref, lse_ref, *, sub_len):
    s = k_ref.shape[0]
    qc = q_ref.shape[0]
    base = pl.program_id(1) * qc
    lane = lax.broadcasted_iota(jnp.int32, (1, GROUP_W), 1)
    head_masks = [(lane >= h * HEAD_DIM) & (lane < (h + 1) * HEAD_DIM) for h in range(HEADS_PER_GROUP)]
    qi = lax.broadcasted_iota(jnp.int32, (Q_BLOCK, 1), 0)
    kj = lax.broadcasted_iota(jnp.int32, (1, K_BLOCK), 1)

    def body(j, carry):
        r0 = pl.multiple_of(j * Q_BLOCK, Q_BLOCK)
        s0 = base + r0
        k0 = pl.multiple_of(jnp.clip(s0 - BAND, 0, s - K_BLOCK), BAND)
        q = q_ref[pl.ds(r0, Q_BLOCK), :]
        k = k_ref[pl.ds(k0, K_BLOCK), :]
        v = v_ref[pl.ds(k0, K_BLOCK), :]
        sub_lo = (s0 // sub_len) * sub_len
        qpos = s0 + qi
        kpos = k0 + kj
        valid = (jnp.abs(qpos - kpos) <= BAND) & (kpos >= sub_lo) & (kpos < sub_lo + sub_len)
        qm = jnp.concatenate([jnp.where(hm, q, jnp.zeros_like(q)) for hm in head_masks], axis=0)
        sc = lax.dot_general(qm, k, (((1,), (1,)), ((), ())), preferred_element_type=F32)
        sc = jnp.where(jnp.concatenate([valid] * HEADS_PER_GROUP, axis=0), sc, MASK_VALUE)
        m = jnp.max(sc, axis=-1, keepdims=True)
        p = jnp.exp(sc - m)
        den = jnp.sum(p, axis=-1, keepdims=True)
        o_all = _dot((p * (1.0 / den)).astype(BF16), v)
        lse_all = m + jnp.log(den)
        o_acc = jnp.zeros((Q_BLOCK, GROUP_W), F32)
        lse_acc = jnp.zeros((Q_BLOCK, GROUP_W), F32)
        for h, hm in enumerate(head_masks):
            rows = slice(h * Q_BLOCK, (h + 1) * Q_BLOCK)
            o_acc = jnp.where(hm, o_all[rows], o_acc)
            lse_acc = jnp.where(hm, lse_all[rows], lse_acc)
        o_ref[pl.ds(r0, Q_BLOCK), :] = o_acc.astype(BF16)
        lse_ref[pl.ds(r0, Q_BLOCK), :] = lse_acc
        return carry

    lax.fori_loop(0, qc // Q_BLOCK, body, 0, unroll=8)


def _attention(qkv, sub_len):
    _, bsz, s, w = qkv.shape
    qc = min(Q_CHUNK, s)
    kv_spec = lambda t: pl.BlockSpec((None, None, s, w), lambda b, j: (t, b, 0, 0))
    return pl.pallas_call(
        functools.partial(_attn_kernel, sub_len=sub_len),
        grid=(bsz, s // qc),
        in_specs=[pl.BlockSpec((None, None, qc, w), lambda b, j: (0, b, j, 0)), kv_spec(1), kv_spec(2)],
        out_specs=[pl.BlockSpec((None, qc, w), lambda b, j: (b, j, 0))] * 2,
        out_shape=[jax.ShapeDtypeStruct((bsz, s, w), BF16), jax.ShapeDtypeStruct((bsz, s, w), F32)],
        compiler_params=pltpu.CompilerParams(
            dimension_semantics=("parallel", "arbitrary"), vmem_limit_bytes=VMEM_LIMIT),
        name="attn",
    )(qkv, qkv, qkv)


def _first_index_of_max(vals, lane):
    mx = jnp.max(vals, axis=-1, keepdims=True)
    idx = jnp.min(jnp.where(vals == mx, lane, LANES), axis=-1, keepdims=True)
    return mx, idx


def _mix_kernel(x_ref, o0_ref, o1_ref, o2_ref, l0_ref, l1_ref, l2_ref, cg_ref, ga_ref,
                g_ref, b_ref, wao_ref, wo_ref, g1_ref, b1_ref, wrh_ref, wrl_ref, br_ref,
                h1_ref, route_ref, so_ref, sl_ref):
    tm = x_ref.shape[0]
    h = _layer_norm(x_ref[...], g_ref[...], b_ref[...])
    halves = GROUP_W // LANES
    for gi, (o_ref, l_ref, r) in enumerate(zip((o0_ref, o1_ref, o2_ref), (l0_ref, l1_ref, l2_ref), DILATIONS)):
        for c in range(halves):
            cols = slice(c * LANES, (c + 1) * LANES)
            for res in range(r):
                rows = pl.ds(res, tm // r, stride=r) if r > 1 else slice(None)
                so_ref[gi * halves + c, rows, :] = o_ref[res, :, cols].astype(F32)
                sl_ref[gi * halves + c, rows, :] = l_ref[res, :, cols]

    def natural(ref, gi):
        return jnp.concatenate([ref[gi * halves + c] for c in range(halves)], axis=1)

    lses = [natural(sl_ref, gi) for gi in range(N_GROUPS)]
    mx = jnp.maximum(jnp.maximum(lses[0], lses[1]), lses[2])
    es = [jnp.exp(l - mx) for l in lses]
    den = es[0] + es[1] + es[2]
    attn_branch = jnp.zeros((tm, D_MODEL), F32)
    for gi in range(N_GROUPS):
        a = (natural(so_ref, gi) * (es[gi] / den)).astype(BF16)
        attn_branch = attn_branch + _dot(a, wao_ref[gi * GROUP_W:(gi + 1) * GROUP_W, :])
    merged = ga_ref[...].astype(F32) * attn_branch + cg_ref[...].astype(F32)
    mix = _dot(merged.astype(BF16), wo_ref[...])
    h1 = _layer_norm(DEEPNORM_ALPHA * h + mix, g1_ref[...], b1_ref[...])
    _store_token_major(h1_ref, h1)

    hi = h1.astype(BF16)
    lo = (h1 - hi.astype(F32)).astype(BF16)
    wrh = wrh_ref[...]
    logits = _dot(hi, wrh) + _dot(lo, wrh) + _dot(hi, wrl_ref[...]) + br_ref[...]
    lane = lax.broadcasted_iota(jnp.int32, (1, LANES), 1)
    is_grp = lane < N_EXPERT_GROUPS
    gl = jnp.where(is_grp, logits, MASK_VALUE)
    ge = jnp.exp(gl - jnp.max(gl, axis=-1, keepdims=True))
    gp = jnp.where(is_grp, ge / jnp.sum(ge, axis=-1, keepdims=True), -1.0)
    grp_w, grp_idx = _first_index_of_max(gp, lane)
    e_lo = ROUTE_LANE0 + grp_idx * EXPERTS_PER_GROUP
    sel = jnp.where((lane >= e_lo) & (lane < e_lo + EXPERTS_PER_GROUP), logits, MASK_VALUE)
    v1, i1 = _first_index_of_max(sel, lane)
    v2, i2 = _first_index_of_max(jnp.where(lane == i1, MASK_VALUE, sel), lane)
    e2 = jnp.exp(v2 - v1)
    w1 = grp_w / (1.0 + e2)
    w2 = grp_w * e2 / (1.0 + e2)
    route = jnp.where(lane == 0, (i1 - ROUTE_LANE0).astype(F32),
                      jnp.where(lane == 1, (i2 - ROUTE_LANE0).astype(F32),
                                jnp.where(lane == 2, w1, jnp.where(lane == 3, w2, 0.0))))
    route_ref[...] = route


def _mix(x, o_list, lse_list, cg, ga, ln_g, ln_b, w_attn_out, w_o, ln1_g, ln1_b, wr_hi, wr_lo, b_route):
    bsz, s, d = x.shape
    tm = ROW_TILE
    row = lambda b, i: (b, i, 0)
    dil_specs = [pl.BlockSpec((None, r, tm // r, GROUP_W), lambda b, i: (b, 0, i, 0)) for r in DILATIONS]
    o_views = [o.reshape(bsz, r, s // r, GROUP_W) for o, r in zip(o_list, DILATIONS)]
    l_views = [l.reshape(bsz, r, s // r, GROUP_W) for l, r in zip(lse_list, DILATIONS)]
    return pl.pallas_call(
        _mix_kernel,
        grid=(bsz, s // tm),
        in_specs=[pl.BlockSpec((None, tm, d), row)] + dil_specs + dil_specs + [
            pl.BlockSpec((None, tm, d), row), pl.BlockSpec((None, tm, d), row),
            _resident((1, d)), _resident((1, d)),
            _resident(w_attn_out.shape), _resident(w_o.shape),
            _resident((1, d)), _resident((1, d)),
            _resident(wr_hi.shape), _resident(wr_lo.shape), _resident((1, LANES)),
        ],
        out_specs=[pl.BlockSpec((tm * ROW_CHUNKS, LANES), lambda b, i: (b * (s // tm) + i, 0)),
                   pl.BlockSpec((None, tm, LANES), row)],
        out_shape=[jax.ShapeDtypeStruct((bsz * s * ROW_CHUNKS, LANES), F32),
                   jax.ShapeDtypeStruct((bsz, s, LANES), F32)],
        scratch_shapes=[pltpu.VMEM((N_GROUPS * GROUP_W // LANES, tm, LANES), F32)] * 2,
        compiler_params=pltpu.CompilerParams(
            dimension_semantics=("parallel", "parallel"), vmem_limit_bytes=VMEM_LIMIT),
        name="mix",
    )(x, *o_views, *l_views, cg, ga, ln_g, ln_b, w_attn_out, w_o, ln1_g, ln1_b, wr_hi, wr_lo, b_route)


def _token_copy(src_hbm, row8, dst_buf, j, sem):
    return pltpu.make_async_copy(src_hbm.at[pl.ds(pl.multiple_of(row8, ROW_CHUNKS), ROW_CHUNKS), :],
                                 dst_buf.at[pl.ds(j * ROW_CHUNKS, ROW_CHUNKS), :], sem)


def _start_token_copies(src_hbm, idx_ref, dst_buf, n, sem):
    def body(c, carry):
        for u in range(ISSUE_UNROLL):
            j = c * ISSUE_UNROLL + u
            _token_copy(src_hbm, idx_ref[0, j], dst_buf, j, sem).start()
        return carry
    lax.fori_loop(0, n // ISSUE_UNROLL, body, 0)


def _wait_token_copies(src_hbm, dst_buf, sem):
    pltpu.make_async_copy(src_hbm.at[pl.ds(0, dst_buf.shape[0]), :], dst_buf, sem).wait()


def _expert_kernel(be_ref, tok_ref, tokn_ref, h1_hbm, wg_ref, wu_ref, wd_ref, y_ref, xbuf, sem):
    i = pl.program_id(0)
    n = pl.num_programs(0)
    slot = i % 2

    @pl.when(i == 0)
    def _():
        _start_token_copies(h1_hbm, tok_ref, xbuf.at[0], EXPERT_BLOCK, sem.at[0])

    @pl.when(i + 1 < n)
    def _():
        _start_token_copies(h1_hbm, tokn_ref, xbuf.at[1 - slot], EXPERT_BLOCK, sem.at[1 - slot])

    cur = xbuf.at[slot]
    _wait_token_copies(h1_hbm, cur, sem.at[slot])
    xb = _load_token_major(cur, 0, EXPERT_BLOCK).astype(BF16)
    acc = None
    for c in range(D_EXPERT // EXPERT_CHUNK):
        cols = slice(c * EXPERT_CHUNK, (c + 1) * EXPERT_CHUNK)
        gate = _dot(xb, wg_ref[:, cols])
        up = _dot(xb, wu_ref[:, cols])
        hidden = (gate * jax.nn.sigmoid(gate) * up).astype(BF16)
        part = _dot(hidden, wd_ref[cols, :])
        acc = part if acc is None else acc + part
    _store_token_major(y_ref, acc)


def _experts(h1, slot_row, block_e, w_gate, w_up, w_down):
    n_blocks = block_e.shape[0]
    d = D_MODEL
    tok3 = slot_row.reshape(n_blocks, 1, EXPERT_BLOCK)
    smem_blk = lambda f: pl.BlockSpec((None, 1, EXPERT_BLOCK), f, memory_space=pltpu.SMEM)
    grid_spec = pltpu.PrefetchScalarGridSpec(
        num_scalar_prefetch=1,
        grid=(n_blocks,),
        in_specs=[
            smem_blk(lambda i, be: (i, 0, 0)),
            smem_blk(lambda i, be: (jnp.minimum(i + 1, n_blocks - 1), 0, 0)),
            pl.BlockSpec(memory_space=pl.ANY),
            pl.BlockSpec((None, d, D_EXPERT), lambda i, be: (be[i], 0, 0)),
            pl.BlockSpec((None, d, D_EXPERT), lambda i, be: (be[i], 0, 0)),
            pl.BlockSpec((None, D_EXPERT, d), lambda i, be: (be[i], 0, 0)),
        ],
        out_specs=pl.BlockSpec((EXPERT_BLOCK * ROW_CHUNKS, LANES), lambda i, be: (i, 0)),
        scratch_shapes=[pltpu.VMEM((2, EXPERT_BLOCK * ROW_CHUNKS, LANES), F32), pltpu.SemaphoreType.DMA((2,))],
    )
    return pl.pallas_call(
        _expert_kernel,
        grid_spec=grid_spec,
        out_shape=jax.ShapeDtypeStruct((n_blocks * EXPERT_BLOCK * ROW_CHUNKS, LANES), F32),
        compiler_params=pltpu.CompilerParams(
            dimension_semantics=("arbitrary",), vmem_limit_bytes=VMEM_LIMIT),
        name="experts",
    )(block_e, tok3, tok3, h1, w_gate, w_up, w_down)


def _combine_kernel(idx_ref, idxn_ref, yb_hbm, h1_ref, route_ref, g_ref, b_ref, y_ref, gbuf, sem):
    i = pl.program_id(0)
    n = pl.num_programs(0)
    tm = y_ref.shape[0]
    slot = i % 2

    @pl.when(i == 0)
    def _():
        _start_token_copies(yb_hbm, idx_ref, gbuf.at[0], TOP_K_INNER * tm, sem.at[0])

    @pl.when(i + 1 < n)
    def _():
        _start_token_copies(yb_hbm, idxn_ref, gbuf.at[1 - slot], TOP_K_INNER * tm, sem.at[1 - slot])

    cur = gbuf.at[slot]
    _wait_token_copies(yb_hbm, cur, sem.at[slot])
    route = route_ref[...]
    ffn = route[:, 2:3] * _load_token_major(cur, 0, tm) + route[:, 3:4] * _load_token_major(cur, tm, tm)
    h1 = _load_token_major(h1_ref, 0, tm)
    y_ref[...] = _layer_norm(DEEPNORM_ALPHA * h1 + ffn, g_ref[...], b_ref[...])


def _combine(h1, route, yb, dest_row, ln2_g, ln2_b):
    t = route.shape[0]
    d = D_MODEL
    tm = COMBINE_TILE
    nt = t // tm
    idx3 = dest_row.reshape(nt, tm, TOP_K_INNER).transpose(0, 2, 1).reshape(nt, 1, TOP_K_INNER * tm)
    smem_blk = lambda f: pl.BlockSpec((None, 1, TOP_K_INNER * tm), f, memory_space=pltpu.SMEM)
    return pl.pallas_call(
        _combine_kernel,
        grid=(nt,),
        in_specs=[
            smem_blk(lambda i: (i, 0, 0)),
            smem_blk(lambda i: (jnp.minimum(i + 1, nt - 1), 0, 0)),
            pl.BlockSpec(memory_space=pl.ANY),
            pl.BlockSpec((tm * ROW_CHUNKS, LANES), lambda i: (i, 0)),
            pl.BlockSpec((tm, LANES), lambda i: (i, 0)),
            _resident((1, d)), _resident((1, d)),
        ],
        out_specs=pl.BlockSpec((tm, d), lambda i: (i, 0)),
        out_shape=jax.ShapeDtypeStruct((t, d), F32),
        scratch_shapes=[pltpu.VMEM((2, TOP_K_INNER * tm * ROW_CHUNKS, LANES), F32), pltpu.SemaphoreType.DMA((2,))],
        compiler_params=pltpu.CompilerParams(
            dimension_semantics=("arbitrary",), vmem_limit_bytes=VMEM_LIMIT),
        name="combine",
    )(idx3, idx3, yb, h1, route, ln2_g, ln2_b)


def _dispatch_plan(route, t):
    eid = route[:, :TOP_K_INNER].astype(jnp.int32).reshape(-1)
    n_assign = t * TOP_K_INNER
    onehot = (eid[:, None] == jnp.arange(N_EXPERTS, dtype=jnp.int32)[None, :]).astype(jnp.int32)
    csum = jnp.cumsum(onehot, axis=0)
    counts = csum[-1]
    rank = jnp.sum(jnp.where(onehot > 0, csum, 0), axis=1) - 1
    padded = (counts + EXPERT_BLOCK - 1) // EXPERT_BLOCK * EXPERT_BLOCK
    pad_end = jnp.cumsum(padded)
    pad_start = pad_end - padded
    dest = pad_start[eid] + rank
    n_blocks = -(-n_assign // EXPERT_BLOCK) + N_EXPERTS
    tok = jnp.arange(n_assign, dtype=jnp.int32) // TOP_K_INNER
    slot_tok = jnp.zeros((n_blocks * EXPERT_BLOCK,), jnp.int32).at[dest].set(tok)
    block_e = jnp.minimum(
        jnp.searchsorted(pad_end, jnp.arange(n_blocks, dtype=jnp.int32) * EXPERT_BLOCK, side='right'),
        N_EXPERTS - 1).astype(jnp.int32)
    return (dest * ROW_CHUNKS).astype(jnp.int32), slot_tok * ROW_CHUNKS, block_e


def _trunk(x, p):
    bsz, s, d = x.shape
    t = bsz * s
    qkv0, qkv1, qkv2, cg, ga = _inproj(x, p['ln_in_g'], p['ln_in_b'], p['w_in'], p['b_gate'],
                                       p['conv_w'], p['w_conv_out'])
    o_list, lse_list = [], []
    for qkv, r in zip((qkv0, qkv1, qkv2), DILATIONS):
        o, lse = _attention(qkv.reshape(3, bsz, s, GROUP_W), s // r)
        o_list.append(o)
        lse_list.append(lse)
    h1, route = _mix(x, o_list, lse_list, cg, ga, p['ln_in_g'], p['ln_in_b'], p['w_attn_out'], p['w_o'],
                     p['ln1_g'], p['ln1_b'], p['wr_hi'], p['wr_lo'], p['b_route'])
    route = route.reshape(t, LANES)
    dest_row, slot_row, block_e = _dispatch_plan(route, t)
    yb = _experts(h1, slot_row, block_e, p['w_gate'], p['w_up'], p['w_down'])
    y = _combine(h1, route, yb, dest_row, p['ln2_g'], p['ln2_b'])
    return y.reshape(bsz, s, d)


def kernel(x_prompt, x_sample, ln_in_g, ln_in_b, w_in, b_gate, conv_w, w_attn_out, w_conv_out, w_o, ln1_g, ln1_b, w_route_group, b_route_group, w_route_expert, b_route_expert, w_gate, w_up, w_down, ln2_g, ln2_b):
    d = D_MODEL
    w_route = jnp.concatenate([w_route_group[0], w_route_expert[0]], axis=1)
    w_route = jnp.pad(w_route, ((0, 0), (0, LANES - w_route.shape[1])))
    wr_hi = w_route.astype(BF16)
    b_route = jnp.concatenate([b_route_group[0], b_route_expert[0]]).astype(F32)
    p = {
        'ln_in_g': ln_in_g.reshape(1, d), 'ln_in_b': ln_in_b.reshape(1, d),
        'w_in': w_in[0].astype(BF16), 'b_gate': b_gate[0].reshape(1, 2 * d), 'conv_w': conv_w[0],
        'w_conv_out': w_conv_out[0].astype(BF16), 'w_attn_out': w_attn_out[0].astype(BF16),
        'w_o': w_o[0].astype(BF16),
        'ln1_g': ln1_g[0].reshape(1, d), 'ln1_b': ln1_b[0].reshape(1, d),
        'wr_hi': wr_hi, 'wr_lo': (w_route - wr_hi.astype(F32)).astype(BF16),
        'b_route': jnp.pad(b_route, (0, LANES - b_route.shape[0])).reshape(1, LANES),
        'w_gate': w_gate[0].astype(BF16), 'w_up': w_up[0].astype(BF16), 'w_down': w_down[0].astype(BF16),
        'ln2_g': ln2_g[0].reshape(1, d), 'ln2_b': ln2_b[0].reshape(1, d),
    }
    return (_trunk(x_prompt, p), _trunk(x_sample, p))
```

```python
import functools

import jax
import jax.numpy as jnp
from jax import lax
from jax.experimental import pallas as pl
from jax.experimental.pallas import tpu as pltpu

F32 = jnp.float32
BF16 = jnp.bfloat16

D_MODEL = 1024
HEAD_DIM = 64
HEADS_PER_GROUP = 4
GROUP_W = HEADS_PER_GROUP * HEAD_DIM
DILATIONS = (1, 4, 16)
BAND = 64
N_GROUPS = len(DILATIONS)
D_ATTN = N_GROUPS * GROUP_W
ATTN_SCALE = HEAD_DIM ** -0.5
ROPE_DIM = HEAD_DIM // 4
ROPE_THETA = 500000.0
MASK_VALUE = -1e30
D_CONV = D_MODEL
COL_CONV_B = 3 * D_ATTN
COL_CONV_C = COL_CONV_B + D_CONV
COL_CONV_H = COL_CONV_C + D_CONV
COL_GATE_ATTN = COL_CONV_H + D_CONV
COL_GATE_CONV = COL_GATE_ATTN + D_MODEL
N_EXPERT_GROUPS = 4
EXPERTS_PER_GROUP = 8
N_EXPERTS = N_EXPERT_GROUPS * EXPERTS_PER_GROUP
TOP_K_INNER = 2
D_EXPERT = 512
EXPERT_BLOCK = 256
EXPERT_CHUNK = 256
LN_EPS = 1e-5
DEPTH = 1
DEEPNORM_ALPHA = (2 * DEPTH) ** 0.25

LANES = 128
ROW_CHUNKS = D_MODEL // LANES
ROW_TILE = 512
HALO = 16
Q_BLOCK = 128
K_BLOCK = Q_BLOCK + 2 * BAND
Q_CHUNK = 2048
COMBINE_TILE = 256
ISSUE_UNROLL = 16
ROUTE_LANE0 = N_EXPERT_GROUPS
VMEM_LIMIT = 56 * 1024 * 1024


def _layer_norm(v, g, b):
    mu = jnp.mean(v, axis=-1, keepdims=True)
    d = v - mu
    var = jnp.mean(d * d, axis=-1, keepdims=True)
    return d * lax.rsqrt(var + LN_EPS) * g + b


def _dot(a, b):
    return jnp.dot(a, b, preferred_element_type=F32)


def _store_token_major(ref, val):
    n = val.shape[0]
    for k in range(ROW_CHUNKS):
        ref[pl.ds(k, n, stride=ROW_CHUNKS), :] = val[:, k * LANES:(k + 1) * LANES]


def _load_token_major(ref, first_token, n):
    return jnp.concatenate(
        [ref[pl.ds(first_token * ROW_CHUNKS + k, n, stride=ROW_CHUNKS), :] for k in range(ROW_CHUNKS)], axis=1)


def _resident(shape):
    nd = len(shape)
    return pl.BlockSpec(shape, lambda *_: (0,) * nd, pipeline_mode=pl.Buffered(1))


def _inproj_kernel(x_ref, xp_ref, xn_ref, g_ref, b_ref, win_ref, bg_ref, cw_ref, wco_ref,
                   cos_ref, sa_ref, sb_ref,
                   qkv0_ref, qkv1_ref, qkv2_ref, cg_ref, ga_ref,
                   hext_ref, u_ref, stage_ref):
    i = pl.program_id(1)
    nt = pl.num_programs(1)
    tm = x_ref.shape[0]
    g = g_ref[...]
    b = b_ref[...]
    hext_ref[pl.ds(HALO, tm), :] = _layer_norm(x_ref[...], g, b).astype(BF16)
    hext_ref[pl.ds(0, HALO), :] = _layer_norm(xp_ref[...], g, b).astype(BF16)
    hext_ref[pl.ds(HALO + tm, HALO), :] = _layer_norm(xn_ref[...], g, b).astype(BF16)
    hm = hext_ref[pl.ds(HALO, tm), :]
    he = hext_ref[...]

    def proj(lhs, col, n):
        return _dot(lhs, win_ref[:, col:col + n])

    u = proj(he, COL_CONV_C, D_CONV) * proj(he, COL_CONV_H, D_CONV)
    rows = lax.broadcasted_iota(jnp.int32, (tm + 2 * HALO, 1), 0)
    lo = jnp.where(i == 0, HALO, 0)
    hi = jnp.where(i == nt - 1, HALO + tm, tm + 2 * HALO)
    u_ref[...] = jnp.where((rows >= lo) & (rows < hi), u, 0.0)
    cw = cw_ref[...]
    conv = (cw[0:1] * u_ref[pl.ds(HALO - 1, tm), :] + cw[1:2] * u_ref[pl.ds(HALO, tm), :]
            + cw[2:3] * u_ref[pl.ds(HALO + 1, tm), :])
    cb = proj(hm, COL_CONV_B, D_CONV)
    conv_branch = _dot((cb * conv).astype(BF16), wco_ref[...])
    gate_conv = jax.nn.sigmoid(proj(hm, COL_GATE_CONV, D_MODEL) + bg_ref[:, D_MODEL:])
    cg_ref[...] = (gate_conv * conv_branch).astype(BF16)
    ga_ref[...] = jax.nn.sigmoid(proj(hm, COL_GATE_ATTN, D_MODEL) + bg_ref[:, :D_MODEL]).astype(BF16)

    cosv = cos_ref[...]
    sav = sa_ref[...]
    sbv = sb_ref[...]
    half = ROPE_DIM // 2
    for gi, (out_ref, r) in enumerate(zip((qkv0_ref, qkv1_ref, qkv2_ref), DILATIONS)):
        for t in range(3):
            p = proj(hm, t * D_ATTN + gi * GROUP_W, GROUP_W)
            if t < 2:
                parts = []
                for c in range(GROUP_W // LANES):
                    pc = p[:, c * LANES:(c + 1) * LANES]
                    parts.append(pc * cosv + pltpu.roll(pc, LANES - half, 1) * sav
                                 + pltpu.roll(pc, half, 1) * sbv)
                p = jnp.concatenate(parts, axis=1)
                if t == 0:
                    p = p * ATTN_SCALE
            if r == 1:
                out_ref[t, 0] = p.astype(BF16)
            else:
                for c in range(GROUP_W // LANES):
                    stage_ref[c] = p[:, c * LANES:(c + 1) * LANES]
                for res in range(r):
                    for c in range(GROUP_W // LANES):
                        out_ref[t, res, :, c * LANES:(c + 1) * LANES] = (
                            stage_ref[c, pl.ds(res, tm // r, stride=r), :].astype(BF16))


def _rope_tables(s):
    half = ROPE_DIM // 2
    inv_freq = ROPE_THETA ** (-jnp.arange(half, dtype=F32) * 2.0 / ROPE_DIM)
    ang = jnp.arange(s).astype(F32)[:, None] * inv_freq[None, :]
    cos = jnp.cos(ang)
    sin = jnp.sin(ang)
    pad = jnp.zeros((s, HEAD_DIM - ROPE_DIM), F32)
    zero = jnp.zeros((s, half), F32)
    cos_t = jnp.concatenate([cos, cos, pad + 1.0], axis=1)
    sa_t = jnp.concatenate([-sin, zero, pad], axis=1)
    sb_t = jnp.concatenate([zero, sin, pad], axis=1)
    rep = LANES // HEAD_DIM
    return tuple(jnp.tile(t, (1, rep)) for t in (cos_t, sa_t, sb_t))


def _inproj(x, ln_g, ln_b, w_in, b_gate, conv_w, w_conv_out):
    bsz, s, d = x.shape
    tm = ROW_TILE
    nt = s // tm
    hpt = tm // HALO
    cos_t, sa_t, sb_t = _rope_tables(s)
    row = lambda b, i: (b, i, 0)
    tab = pl.BlockSpec((tm, LANES), lambda b, i: (i, 0))
    qkv_shapes = [jax.ShapeDtypeStruct((3, bsz, r, s // r, GROUP_W), BF16) for r in DILATIONS]
    qkv_specs = [pl.BlockSpec((3, None, r, tm // r, GROUP_W), lambda b, i: (0, b, 0, i, 0))
                 for r in DILATIONS]
    return pl.pallas_call(
        _inproj_kernel,
        grid=(bsz, nt),
        in_specs=[
            pl.BlockSpec((None, tm, d), row),
            pl.BlockSpec((None, HALO, d), lambda b, i: (b, jnp.maximum(i * hpt - 1, 0), 0)),
            pl.BlockSpec((None, HALO, d), lambda b, i: (b, jnp.minimum((i + 1) * hpt, s // HALO - 1), 0)),
            _resident((1, d)), _resident((1, d)),
            _resident(w_in.shape), _resident((1, 2 * d)), _resident(conv_w.shape),
            _resident(w_conv_out.shape),
            tab, tab, tab,
        ],
        out_specs=qkv_specs + [pl.BlockSpec((None, tm, d), row), pl.BlockSpec((None, tm, d), row)],
        out_shape=qkv_shapes + [jax.ShapeDtypeStruct((bsz, s, d), BF16)] * 2,
        scratch_shapes=[
            pltpu.VMEM((tm + 2 * HALO, d), BF16),
            pltpu.VMEM((tm + 2 * HALO, D_CONV), F32),
            pltpu.VMEM((GROUP_W // LANES, tm, LANES), F32),
        ],
        compiler_params=pltpu.CompilerParams(
            dimension_semantics=("parallel", "parallel"), vmem_limit_bytes=VMEM_LIMIT),
        name="inproj",
    )(x, x, x, ln_g, ln_b, w_in, b_gate, conv_w, w_conv_out, cos_t, sa_t, sb_t)


def _attn_kernel(q_ref, k_ref, v_ref, o_ref, lse_ref, *, sub_len):
    s = k_ref.shape[0]
    qc = q_ref.shape[0]
    base = pl.program_id(1) * qc
    lane = lax.broadcasted_iota(jnp.int32, (1, GROUP_W), 1)
    head_masks = [(lane >= h * HEAD_DIM) & (lane < (h + 1) * HEAD_DIM) for h in range(HEADS_PER_GROUP)]
    qi = lax.broadcasted_iota(jnp.int32, (Q_BLOCK, 1), 0)
    kj = lax.broadcasted_iota(jnp.int32, (1, K_BLOCK), 1)

    def body(j, carry):
        r0 = pl.multiple_of(j * Q_BLOCK, Q_BLOCK)
        s0 = base + r0
        k0 = pl.multiple_of(jnp.clip(s0 - BAND, 0, s - K_BLOCK), BAND)
        q = q_ref[pl.ds(r0, Q_BLOCK), :]
        k = k_ref[pl.ds(k0, K_BLOCK), :]
        v = v_ref[pl.ds(k0, K_BLOCK), :]
        sub_lo = (s0 // sub_len) * sub_len
        qpos = s0 + qi
        kpos = k0 + kj
        valid = (jnp.abs(qpos - kpos) <= BAND) & (kpos >= sub_lo) & (kpos < sub_lo + sub_len)
        qm = jnp.concatenate([jnp.where(hm, q, jnp.zeros_like(q)) for hm in head_masks], axis=0)
        sc = lax.dot_general(qm, k, (((1,), (1,)), ((), ())), preferred_element_type=F32)
        sc = jnp.where(jnp.concatenate([valid] * HEADS_PER_GROUP, axis=0), sc, MASK_VALUE)
        m = jnp.max(sc, axis=-1, keepdims=True)
        p = jnp.exp(sc - m)
        den = jnp.sum(p, axis=-1, keepdims=True)
        o_all = _dot((p * (1.0 / den)).astype(BF16), v)
        lse_all = m + jnp.log(den)
        o_acc = jnp.zeros((Q_BLOCK, GROUP_W), F32)
        lse_acc = jnp.zeros((Q_BLOCK, GROUP_W), F32)
        for h, hm in enumerate(head_masks):
            rows = slice(h * Q_BLOCK, (h + 1) * Q_BLOCK)
            o_acc = jnp.where(hm, o_all[rows], o_acc)
            lse_acc = jnp.where(hm, lse_all[rows], lse_acc)
        o_ref[pl.ds(r0, Q_BLOCK), :] = o_acc.astype(BF16)
        lse_ref[pl.ds(r0, Q_BLOCK), :] = lse_acc
        return carry

    lax.fori_loop(0, qc // Q_BLOCK, body, 0, unroll=8)


def _attention(qkv, sub_len):
    _, bsz, s, w = qkv.shape
    qc = min(Q_CHUNK, s)
    kv_spec = lambda t: pl.BlockSpec((None, None, s, w), lambda b, j: (t, b, 0, 0))
    return pl.pallas_call(
        functools.partial(_attn_kernel, sub_len=sub_len),
        grid=(bsz, s // qc),
        in_specs=[pl.BlockSpec((None, None, qc, w), lambda b, j: (0, b, j, 0)), kv_spec(1), kv_spec(2)],
        out_specs=[pl.BlockSpec((None, qc, w), lambda b, j: (b, j, 0))] * 2,
        out_shape=[jax.ShapeDtypeStruct((bsz, s, w), BF16), jax.ShapeDtypeStruct((bsz, s, w), F32)],
        compiler_params=pltpu.CompilerParams(
            dimension_semantics=("parallel", "arbitrary"), vmem_limit_bytes=VMEM_LIMIT),
        name="attn",
    )(qkv, qkv, qkv)


def _first_index_of_max(vals, lane):
    mx = jnp.max(vals, axis=-1, keepdims=True)
    idx = jnp.min(jnp.where(vals == mx, lane, LANES), axis=-1, keepdims=True)
    return mx, idx


def _mix_kernel(x_ref, o0_ref, o1_ref, o2_ref, l0_ref, l1_ref, l2_ref, cg_ref, ga_ref,
                g_ref, b_ref, wao_ref, wo_ref, g1_ref, b1_ref, wrh_ref, wrl_ref, br_ref,
                h1_ref, route_ref, so_ref, sl_ref):
    tm = x_ref.shape[0]
    h = _layer_norm(x_ref[...], g_ref[...], b_ref[...])
    halves = GROUP_W // LANES
    for gi, (o_ref, l_ref, r) in enumerate(zip((o0_ref, o1_ref, o2_ref), (l0_ref, l1_ref, l2_ref), DILATIONS)):
        for c in range(halves):
            cols = slice(c * LANES, (c + 1) * LANES)
            for res in range(r):
                rows = pl.ds(res, tm // r, stride=r) if r > 1 else slice(None)
                so_ref[gi * halves + c, rows, :] = o_ref[res, :, cols].astype(F32)
                sl_ref[gi * halves + c, rows, :] = l_ref[res, :, cols]

    def natural(ref, gi):
        return jnp.concatenate([ref[gi * halves + c] for c in range(halves)], axis=1)

    lses = [natural(sl_ref, gi) for gi in range(N_GROUPS)]
    mx = jnp.maximum(jnp.maximum(lses[0], lses[1]), lses[2])
    es = [jnp.exp(l - mx) for l in lses]
    den = es[0] + es[1] + es[2]
    attn_branch = jnp.zeros((tm, D_MODEL), F32)
    for gi in range(N_GROUPS):
        a = (natural(so_ref, gi) * (es[gi] / den)).astype(BF16)
        attn_branch = attn_branch + _dot(a, wao_ref[gi * GROUP_W:(gi + 1) * GROUP_W, :])
    merged = ga_ref[...].astype(F32) * attn_branch + cg_ref[...].astype(F32)
    mix = _dot(merged.astype(BF16), wo_ref[...])
    h1 = _layer_norm(DEEPNORM_ALPHA * h + mix, g1_ref[...], b1_ref[...])
    _store_token_major(h1_ref, h1)

    hi = h1.astype(BF16)
    lo = (h1 - hi.astype(F32)).astype(BF16)
    wrh = wrh_ref[...]
    logits = _dot(hi, wrh) + _dot(lo, wrh) + _dot(hi, wrl_ref[...]) + br_ref[...]
    lane = lax.broadcasted_iota(jnp.int32, (1, LANES), 1)
    is_grp = lane < N_EXPERT_GROUPS
    gl = jnp.where(is_grp, logits, MASK_VALUE)
    ge = jnp.exp(gl - jnp.max(gl, axis=-1, keepdims=True))
    gp = jnp.where(is_grp, ge / jnp.sum(ge, axis=-1, keepdims=True), -1.0)
    grp_w, grp_idx = _first_index_of_max(gp, lane)
    e_lo = ROUTE_LANE0 + grp_idx * EXPERTS_PER_GROUP
    sel = jnp.where((lane >= e_lo) & (lane < e_lo + EXPERTS_PER_GROUP), logits, MASK_VALUE)
    v1, i1 = _first_index_of_max(sel, lane)
    v2, i2 = _first_index_of_max(jnp.where(lane == i1, MASK_VALUE, sel), lane)
    e2 = jnp.exp(v2 - v1)
    w1 = grp_w / (1.0 + e2)
    w2 = grp_w * e2 / (1.0 + e2)
    route = jnp.where(lane == 0, (i1 - ROUTE_LANE0).astype(F32),
                      jnp.where(lane == 1, (i2 - ROUTE_LANE0).astype(F32),
                                jnp.where(lane == 2, w1, jnp.where(lane == 3, w2, 0.0))))
    route_ref[...] = route


def _mix(x, o_list, lse_list, cg, ga, ln_g, ln_b, w_attn_out, w_o, ln1_g, ln1_b, wr_hi, wr_lo, b_route):
    bsz, s, d = x.shape
    tm = ROW_TILE
    row = lambda b, i: (b, i, 0)
    dil_specs = [pl.BlockSpec((None, r, tm // r, GROUP_W), lambda b, i: (b, 0, i, 0)) for r in DILATIONS]
    o_views = [o.reshape(bsz, r, s // r, GROUP_W) for o, r in zip(o_list, DILATIONS)]
    l_views = [l.reshape(bsz, r, s // r, GROUP_W) for l, r in zip(lse_list, DILATIONS)]
    return pl.pallas_call(
        _mix_kernel,
        grid=(bsz, s // tm),
        in_specs=[pl.BlockSpec((None, tm, d), row)] + dil_specs + dil_specs + [
            pl.BlockSpec((None, tm, d), row), pl.BlockSpec((None, tm, d), row),
            _resident((1, d)), _resident((1, d)),
            _resident(w_attn_out.shape), _resident(w_o.shape),
            _resident((1, d)), _resident((1, d)),
            _resident(wr_hi.shape), _resident(wr_lo.shape), _resident((1, LANES)),
        ],
        out_specs=[pl.BlockSpec((tm * ROW_CHUNKS, LANES), lambda b, i: (b * (s // tm) + i, 0)),
                   pl.BlockSpec((None, tm, LANES), row)],
        out_shape=[jax.ShapeDtypeStruct((bsz * s * ROW_CHUNKS, LANES), F32),
                   jax.ShapeDtypeStruct((bsz, s, LANES), F32)],
        scratch_shapes=[pltpu.VMEM((N_GROUPS * GROUP_W // LANES, tm, LANES), F32)] * 2,
        compiler_params=pltpu.CompilerParams(
            dimension_semantics=("parallel", "parallel"), vmem_limit_bytes=VMEM_LIMIT),
        name="mix",
    )(x, *o_views, *l_views, cg, ga, ln_g, ln_b, w_attn_out, w_o, ln1_g, ln1_b, wr_hi, wr_lo, b_route)


def _rank_kernel(route_ref, rank_ref, counts_ref, tri_ref):
    i = pl.program_id(0)
    tm = route_ref.shape[0]

    @pl.when(i == 0)
    def _():
        counts_ref[...] = jnp.zeros_like(counts_ref)
        r = lax.broadcasted_iota(jnp.int32, (tm, tm), 0)
        c = lax.broadcasted_iota(jnp.int32, (tm, tm), 1)
        tri_ref[...] = (c < r).astype(BF16)

    route = route_ref[...]
    lane = lax.broadcasted_iota(jnp.int32, (1, LANES), 1)
    oh0 = lane == route[:, 0:1].astype(jnp.int32)
    oh1 = lane == route[:, 1:2].astype(jnp.int32)
    oh = oh0.astype(F32) + oh1.astype(F32)
    before = counts_ref[...] + _dot(tri_ref[...], oh.astype(BF16))
    r0 = jnp.sum(jnp.where(oh0, before, 0.0), axis=-1, keepdims=True)
    r1 = jnp.sum(jnp.where(oh1, before + oh0.astype(F32), 0.0), axis=-1, keepdims=True)
    rank_ref[...] = jnp.where(lane == 0, r0, jnp.where(lane == 1, r1, 0.0))
    counts_ref[...] = counts_ref[...] + jnp.sum(oh, axis=0, keepdims=True)


def _rank(route):
    t = route.shape[0]
    tm = ROW_TILE
    return pl.pallas_call(
        _rank_kernel,
        grid=(t // tm,),
        in_specs=[pl.BlockSpec((tm, LANES), lambda i: (i, 0))],
        out_specs=[pl.BlockSpec((tm, LANES), lambda i: (i, 0)), pl.BlockSpec((1, LANES), lambda i: (0, 0))],
        out_shape=[jax.ShapeDtypeStruct((t, LANES), F32), jax.ShapeDtypeStruct((1, LANES), F32)],
        scratch_shapes=[pltpu.VMEM((tm, tm), BF16)],
        compiler_params=pltpu.CompilerParams(dimension_semantics=("arbitrary",), vmem_limit_bytes=VMEM_LIMIT),
        name="rank",
    )(route)


def _dispatch_plan(route, rank, counts, t):
    eid = route[:, :TOP_K_INNER].astype(jnp.int32)
    counts = counts[0, :N_EXPERTS].astype(jnp.int32)
    padded = (counts + EXPERT_BLOCK - 1) // EXPERT_BLOCK * EXPERT_BLOCK
    pad_end = jnp.cumsum(padded)
    pad_start = pad_end - padded
    experts = jnp.arange(N_EXPERTS, dtype=jnp.int32)
    start_of = jnp.sum(jnp.where(eid[..., None] == experts, pad_start, 0), axis=-1)
    dest = start_of + rank[:, :TOP_K_INNER].astype(jnp.int32)
    n_blocks = -(-t * TOP_K_INNER // EXPERT_BLOCK) + N_EXPERTS
    n_slots = n_blocks * EXPERT_BLOCK
    block_e = jnp.minimum(
        jnp.searchsorted(pad_end, jnp.arange(n_blocks, dtype=jnp.int32) * EXPERT_BLOCK, side='right'),
        N_EXPERTS - 1).astype(jnp.int32)
    n_empty = n_slots - t * TOP_K_INNER
    gap_end = jnp.cumsum(padded - counts)
    gap_start = gap_end - (padded - counts)
    k = jnp.arange(n_empty, dtype=jnp.int32)
    in_gap = (k[:, None] >= gap_start) & (k[:, None] < gap_end)
    empty = jnp.where(k < gap_end[-1],
                      jnp.sum(jnp.where(in_gap, pad_start + counts + k[:, None] - gap_start, 0), axis=-1),
                      pad_end[-1] + k - gap_end[-1])
    return (dest * ROW_CHUNKS).astype(jnp.int32), (empty * ROW_CHUNKS).astype(jnp.int32), block_e


def _token_copy(src_ref, src_row, dst_ref, dst_row, sem):
    return pltpu.make_async_copy(src_ref.at[pl.ds(pl.multiple_of(src_row, ROW_CHUNKS), ROW_CHUNKS), :],
                                 dst_ref.at[pl.ds(pl.multiple_of(dst_row, ROW_CHUNKS), ROW_CHUNKS), :], sem)


def _wait_rows(hbm_ref, n_rows, sem):
    rows = hbm_ref.at[pl.ds(0, n_rows), :]
    pltpu.make_async_copy(rows, rows, sem).wait()


def _dispatch_kernel(dest_ref, empty_ref, h1_ref, xs_hbm, zero_ref, sem):
    i = pl.program_id(0)
    tm = h1_ref.shape[0] // ROW_CHUNKS
    n_empty = empty_ref.shape[1]

    @pl.when(i == 0)
    def _():
        zero_ref[...] = jnp.zeros_like(zero_ref)

    def body(c, carry):
        for u in range(ISSUE_UNROLL):
            j = c * ISSUE_UNROLL + u
            for choice in range(TOP_K_INNER):
                _token_copy(h1_ref, j * ROW_CHUNKS, xs_hbm, dest_ref[0, choice * tm + j], sem.at[0]).start()
        return carry
    lax.fori_loop(0, tm // ISSUE_UNROLL, body, 0)

    def zero_body(c, carry):
        for u in range(ISSUE_UNROLL):
            _token_copy(zero_ref, 0, xs_hbm, empty_ref[0, c * ISSUE_UNROLL + u], sem.at[1]).start()
        return carry
    lax.fori_loop(0, n_empty // ISSUE_UNROLL, zero_body, 0)

    _wait_rows(xs_hbm, TOP_K_INNER * tm * ROW_CHUNKS, sem.at[0])
    _wait_rows(xs_hbm, n_empty * ROW_CHUNKS, sem.at[1])


def _dispatch(h1, dest_row, empty_row, n_slots):
    t = h1.shape[0] // ROW_CHUNKS
    tm = ROW_TILE
    nt = t // tm
    n_empty = empty_row.shape[0] // nt
    assert n_empty * nt == empty_row.shape[0] and n_empty % ISSUE_UNROLL == 0
    idx3 = dest_row.reshape(nt, tm, TOP_K_INNER).transpose(0, 2, 1).reshape(nt, 1, TOP_K_INNER * tm)
    return pl.pallas_call(
        _dispatch_kernel,
        grid=(nt,),
        in_specs=[
            pl.BlockSpec((None, 1, TOP_K_INNER * tm), lambda i: (i, 0, 0), memory_space=pltpu.SMEM),
            pl.BlockSpec((None, 1, n_empty), lambda i: (i, 0, 0), memory_space=pltpu.SMEM),
            pl.BlockSpec((tm * ROW_CHUNKS, LANES), lambda i: (i, 0)),
        ],
        out_specs=pl.BlockSpec(memory_space=pl.ANY),
        out_shape=jax.ShapeDtypeStruct((n_slots * ROW_CHUNKS, LANES), F32),
        scratch_shapes=[pltpu.VMEM((ROW_CHUNKS, LANES), F32), pltpu.SemaphoreType.DMA((2,))],
        compiler_params=pltpu.CompilerParams(dimension_semantics=("arbitrary",), vmem_limit_bytes=VMEM_LIMIT),
        name="dispatch",
    )(idx3, empty_row.reshape(nt, 1, n_empty), h1)


def _expert_kernel(be_ref, x_ref, wg_ref, wu_ref, wd_ref, y_ref):
    xb = _load_token_major(x_ref, 0, EXPERT_BLOCK).astype(BF16)
    acc = None
    for c in range(D_EXPERT // EXPERT_CHUNK):
        cols = slice(c * EXPERT_CHUNK, (c + 1) * EXPERT_CHUNK)
        gate = _dot(xb, wg_ref[:, cols])
        up = _dot(xb, wu_ref[:, cols])
        hidden = (gate * jax.nn.sigmoid(gate) * up).astype(BF16)
        part = _dot(hidden, wd_ref[cols, :])
        acc = part if acc is None else acc + part
    _store_token_major(y_ref, acc)


def _experts(xs, block_e, w_gate, w_up, w_down):
    n_blocks = block_e.shape[0]
    d = D_MODEL
    blk = pl.BlockSpec((EXPERT_BLOCK * ROW_CHUNKS, LANES), lambda i, be: (i, 0))
    grid_spec = pltpu.PrefetchScalarGridSpec(
        num_scalar_prefetch=1,
        grid=(n_blocks,),
        in_specs=[
            blk,
            pl.BlockSpec((None, d, D_EXPERT), lambda i, be: (be[i], 0, 0)),
            pl.BlockSpec((None, d, D_EXPERT), lambda i, be: (be[i], 0, 0)),
            pl.BlockSpec((None, D_EXPERT, d), lambda i, be: (be[i], 0, 0)),
        ],
        out_specs=blk,
    )
    return pl.pallas_call(
        _expert_kernel,
        grid_spec=grid_spec,
        out_shape=jax.ShapeDtypeStruct(xs.shape, F32),
        compiler_params=pltpu.CompilerParams(
            dimension_semantics=("arbitrary",), vmem_limit_bytes=VMEM_LIMIT),
        name="experts",
    )(block_e, xs, w_gate, w_up, w_down)


def _combine_kernel(idx_ref, idxn_ref, yb_hbm, h1_ref, route_ref, g_ref, b_ref, y_ref, gbuf, sem):
    i = pl.program_id(0)
    n = pl.num_programs(0)
    tm = y_ref.shape[0]
    slot = i % 2

    def gather(src_ref, s):
        def body(c, carry):
            for u in range(ISSUE_UNROLL):
                j = c * ISSUE_UNROLL + u
                _token_copy(yb_hbm, src_ref[0, j], gbuf.at[s], j * ROW_CHUNKS, sem.at[s]).start()
            return carry
        lax.fori_loop(0, TOP_K_INNER * tm // ISSUE_UNROLL, body, 0)

    @pl.when(i == 0)
    def _():
        gather(idx_ref, 0)

    @pl.when(i + 1 < n)
    def _():
        gather(idxn_ref, 1 - slot)

    cur = gbuf.at[slot]
    _wait_rows(yb_hbm, TOP_K_INNER * tm * ROW_CHUNKS, sem.at[slot])
    route = route_ref[...]
    ffn = route[:, 2:3] * _load_token_major(cur, 0, tm) + route[:, 3:4] * _load_token_major(cur, tm, tm)
    h1 = _load_token_major(h1_ref, 0, tm)
    y_ref[...] = _layer_norm(DEEPNORM_ALPHA * h1 + ffn, g_ref[...], b_ref[...])


def _combine(h1, route, yb, dest_row, ln2_g, ln2_b):
    t = route.shape[0]
    d = D_MODEL
    tm = COMBINE_TILE
    nt = t // tm
    idx3 = dest_row.reshape(nt, tm, TOP_K_INNER).transpose(0, 2, 1).reshape(nt, 1, TOP_K_INNER * tm)
    smem_blk = lambda f: pl.BlockSpec((None, 1, TOP_K_INNER * tm), f, memory_space=pltpu.SMEM)
    return pl.pallas_call(
        _combine_kernel,
        grid=(nt,),
        in_specs=[
            smem_blk(lambda i: (i, 0, 0)),
            smem_blk(lambda i: (jnp.minimum(i + 1, nt - 1), 0, 0)),
            pl.BlockSpec(memory_space=pl.ANY),
            pl.BlockSpec((tm * ROW_CHUNKS, LANES), lambda i: (i, 0)),
            pl.BlockSpec((tm, LANES), lambda i: (i, 0)),
            _resident((1, d)), _resident((1, d)),
        ],
        out_specs=pl.BlockSpec((tm, d), lambda i: (i, 0)),
        out_shape=jax.ShapeDtypeStruct((t, d), F32),
        scratch_shapes=[pltpu.VMEM((2, TOP_K_INNER * tm * ROW_CHUNKS, LANES), F32), pltpu.SemaphoreType.DMA((2,))],
        compiler_params=pltpu.CompilerParams(
            dimension_semantics=("arbitrary",), vmem_limit_bytes=VMEM_LIMIT),
        name="combine",
    )(idx3, idx3, yb, h1, route, ln2_g, ln2_b)


def _trunk(x, p):
    bsz, s, d = x.shape
    t = bsz * s
    qkv0, qkv1, qkv2, cg, ga = _inproj(x, p['ln_in_g'], p['ln_in_b'], p['w_in'], p['b_gate'],
                                       p['conv_w'], p['w_conv_out'])
    o_list, lse_list = [], []
    for qkv, r in zip((qkv0, qkv1, qkv2), DILATIONS):
        o, lse = _attention(qkv.reshape(3, bsz, s, GROUP_W), s // r)
        o_list.append(o)
        lse_list.append(lse)
    h1, route = _mix(x, o_list, lse_list, cg, ga, p['ln_in_g'], p['ln_in_b'], p['w_attn_out'], p['w_o'],
                     p['ln1_g'], p['ln1_b'], p['wr_hi'], p['wr_lo'], p['b_route'])
    route = route.reshape(t, LANES)
    rank, counts = _rank(route)
    dest_row, empty_row, block_e = _dispatch_plan(route, rank, counts, t)
    xs = _dispatch(h1, dest_row, empty_row, block_e.shape[0] * EXPERT_BLOCK)
    yb = _experts(xs, block_e, p['w_gate'], p['w_up'], p['w_down'])
    y = _combine(h1, route, yb, dest_row, p['ln2_g'], p['ln2_b'])
    return y.reshape(bsz, s, d)


def kernel(x_prompt, x_sample, ln_in_g, ln_in_b, w_in, b_gate, conv_w, w_attn_out, w_conv_out, w_o, ln1_g, ln1_b, w_route_group, b_route_group, w_route_expert, b_route_expert, w_gate, w_up, w_down, ln2_g, ln2_b):
    d = D_MODEL
    w_route = jnp.concatenate([w_route_group[0], w_route_expert[0]], axis=1)
    w_route = jnp.pad(w_route, ((0, 0), (0, LANES - w_route.shape[1])))
    wr_hi = w_route.astype(BF16)
    b_route = jnp.concatenate([b_route_group[0], b_route_expert[0]]).astype(F32)
    p = {
        'ln_in_g': ln_in_g.reshape(1, d), 'ln_in_b': ln_in_b.reshape(1, d),
        'w_in': w_in[0].astype(BF16), 'b_gate': b_gate[0].reshape(1, 2 * d), 'conv_w': conv_w[0],
        'w_conv_out': w_conv_out[0].astype(BF16), 'w_attn_out': w_attn_out[0].astype(BF16),
        'w_o': w_o[0].astype(BF16),
        'ln1_g': ln1_g[0].reshape(1, d), 'ln1_b': ln1_b[0].reshape(1, d),
        'wr_hi': wr_hi, 'wr_lo': (w_route - wr_hi.astype(F32)).astype(BF16),
        'b_route': jnp.pad(b_route, (0, LANES - b_route.shape[0])).reshape(1, LANES),
        'w_gate': w_gate[0].astype(BF16), 'w_up': w_up[0].astype(BF16), 'w_down': w_down[0].astype(BF16),
        'ln2_g': ln2_g[0].reshape(1, d), 'ln2_b': ln2_b[0].reshape(1, d),
    }
    return (_trunk(x_prompt, p), _trunk(x_sample, p))
```

```python
import functools

import jax
import jax.numpy as jnp
import numpy as np
from jax import lax
from jax.experimental import pallas as pl
from jax.experimental.pallas import tpu as pltpu

F32 = jnp.float32
BF16 = jnp.bfloat16

D_MODEL = 1024
HEAD_DIM = 64
HEADS_PER_GROUP = 4
GROUP_W = HEADS_PER_GROUP * HEAD_DIM
DILATIONS = (1, 4, 16)
BAND = 64
N_GROUPS = len(DILATIONS)
D_ATTN = N_GROUPS * GROUP_W
ATTN_SCALE = HEAD_DIM ** -0.5
ROPE_DIM = HEAD_DIM // 4
ROPE_THETA = 500000.0
MASK_VALUE = -1e30
D_CONV = D_MODEL
COL_CONV_B = 3 * D_ATTN
COL_CONV_C = COL_CONV_B + D_CONV
COL_CONV_H = COL_CONV_C + D_CONV
COL_GATE_ATTN = COL_CONV_H + D_CONV
COL_GATE_CONV = COL_GATE_ATTN + D_MODEL
N_EXPERT_GROUPS = 4
EXPERTS_PER_GROUP = 8
N_EXPERTS = N_EXPERT_GROUPS * EXPERTS_PER_GROUP
TOP_K_INNER = 2
D_EXPERT = 512
EXPERT_BLOCK = 256
EXPERT_CHUNK = 256
LN_EPS = 1e-5
DEPTH = 1
DEEPNORM_ALPHA = (2 * DEPTH) ** 0.25

LANES = 128
ROW_CHUNKS = D_MODEL // LANES
ROW_TILE = 512
HALO = 16
Q_BLOCK = 128
K_BLOCK = Q_BLOCK + 2 * BAND
Q_CHUNK = 2048
COMBINE_TILE = 256
ISSUE_UNROLL = 16
ROUTE_LANE0 = N_EXPERT_GROUPS
VMEM_LIMIT = 56 * 1024 * 1024


def _layer_norm(v, g, b):
    mu = jnp.mean(v, axis=-1, keepdims=True)
    d = v - mu
    var = jnp.mean(d * d, axis=-1, keepdims=True)
    return d * lax.rsqrt(var + LN_EPS) * g + b


def _dot(a, b):
    return jnp.dot(a, b, preferred_element_type=F32)


def _store_token_major(ref, first_token, val):
    n = val.shape[0]
    for k in range(ROW_CHUNKS):
        ref[pl.ds(first_token * ROW_CHUNKS + k, n, stride=ROW_CHUNKS), :] = val[:, k * LANES:(k + 1) * LANES]


def _load_token_major(ref, first_token, n):
    return jnp.concatenate(
        [ref[pl.ds(first_token * ROW_CHUNKS + k, n, stride=ROW_CHUNKS), :] for k in range(ROW_CHUNKS)], axis=1)


def _resident(shape):
    nd = len(shape)
    return pl.BlockSpec(shape, lambda *_: (0,) * nd, pipeline_mode=pl.Buffered(1))


def _inproj_kernel(x_ref, xp_ref, xn_ref, g_ref, b_ref, win_ref, bg_ref, cw_ref, wco_ref,
                   cos_ref, sa_ref, sb_ref,
                   qkv0_ref, qkv1_ref, qkv2_ref, cg_ref, ga_ref,
                   hext_ref, u_ref, stage_ref):
    i = pl.program_id(1)
    nt = pl.num_programs(1)
    tm = x_ref.shape[0]
    g = g_ref[...]
    b = b_ref[...]
    hext_ref[pl.ds(HALO, tm), :] = _layer_norm(x_ref[...], g, b).astype(BF16)
    hext_ref[pl.ds(0, HALO), :] = _layer_norm(xp_ref[...], g, b).astype(BF16)
    hext_ref[pl.ds(HALO + tm, HALO), :] = _layer_norm(xn_ref[...], g, b).astype(BF16)
    hm = hext_ref[pl.ds(HALO, tm), :]
    he = hext_ref[...]

    def proj(lhs, col, n):
        return _dot(lhs, win_ref[:, col:col + n])

    u = proj(he, COL_CONV_C, D_CONV) * proj(he, COL_CONV_H, D_CONV)
    rows = lax.broadcasted_iota(jnp.int32, (tm + 2 * HALO, 1), 0)
    lo = jnp.where(i == 0, HALO, 0)
    hi = jnp.where(i == nt - 1, HALO + tm, tm + 2 * HALO)
    u_ref[...] = jnp.where((rows >= lo) & (rows < hi), u, 0.0)
    cw = cw_ref[...]
    conv = (cw[0:1] * u_ref[pl.ds(HALO - 1, tm), :] + cw[1:2] * u_ref[pl.ds(HALO, tm), :]
            + cw[2:3] * u_ref[pl.ds(HALO + 1, tm), :])
    cb = proj(hm, COL_CONV_B, D_CONV)
    conv_branch = _dot((cb * conv).astype(BF16), wco_ref[...])
    gate_conv = jax.nn.sigmoid(proj(hm, COL_GATE_CONV, D_MODEL) + bg_ref[:, D_MODEL:])
    cg_ref[...] = (gate_conv * conv_branch).astype(BF16)
    ga_ref[...] = jax.nn.sigmoid(proj(hm, COL_GATE_ATTN, D_MODEL) + bg_ref[:, :D_MODEL]).astype(BF16)

    cosv = cos_ref[...]
    sav = sa_ref[...]
    sbv = sb_ref[...]
    half = ROPE_DIM // 2
    for gi, (out_ref, r) in enumerate(zip((qkv0_ref, qkv1_ref, qkv2_ref), DILATIONS)):
        for t in range(3):
            p = proj(hm, t * D_ATTN + gi * GROUP_W, GROUP_W)
            if t < 2:
                parts = []
                for c in range(GROUP_W // LANES):
                    pc = p[:, c * LANES:(c + 1) * LANES]
                    parts.append(pc * cosv + pltpu.roll(pc, LANES - half, 1) * sav
                                 + pltpu.roll(pc, half, 1) * sbv)
                p = jnp.concatenate(parts, axis=1)
                if t == 0:
                    p = p * ATTN_SCALE
            if r == 1:
                out_ref[t, 0] = p.astype(BF16)
            else:
                for c in range(GROUP_W // LANES):
                    stage_ref[c] = p[:, c * LANES:(c + 1) * LANES]
                for res in range(r):
                    for c in range(GROUP_W // LANES):
                        out_ref[t, res, :, c * LANES:(c + 1) * LANES] = (
                            stage_ref[c, pl.ds(res, tm // r, stride=r), :].astype(BF16))


def _rope_tables(s):
    half = ROPE_DIM // 2
    inv_freq = (np.float32(ROPE_THETA) ** (-np.arange(half, dtype=np.float32) * np.float32(2.0) / ROPE_DIM))
    ang = np.arange(s, dtype=np.float32)[:, None] * inv_freq.astype(np.float32)[None, :]
    cos = np.cos(ang).astype(np.float32)
    sin = np.sin(ang).astype(np.float32)
    pad = np.zeros((s, HEAD_DIM - ROPE_DIM), np.float32)
    zero = np.zeros((s, half), np.float32)
    cos_t = np.concatenate([cos, cos, pad + 1.0], axis=1)
    sa_t = np.concatenate([-sin, zero, pad], axis=1)
    sb_t = np.concatenate([zero, sin, pad], axis=1)
    rep = LANES // HEAD_DIM
    return tuple(jnp.asarray(np.tile(t, (1, rep))) for t in (cos_t, sa_t, sb_t))


def _inproj(x, ln_g, ln_b, w_in, b_gate, conv_w, w_conv_out):
    bsz, s, d = x.shape
    tm = ROW_TILE
    nt = s // tm
    hpt = tm // HALO
    cos_t, sa_t, sb_t = _rope_tables(s)
    row = lambda b, i: (b, i, 0)
    tab = pl.BlockSpec((tm, LANES), lambda b, i: (i, 0))
    qkv_shapes = [jax.ShapeDtypeStruct((3, bsz, r, s // r, GROUP_W), BF16) for r in DILATIONS]
    qkv_specs = [pl.BlockSpec((3, None, r, tm // r, GROUP_W), lambda b, i: (0, b, 0, i, 0))
                 for r in DILATIONS]
    return pl.pallas_call(
        _inproj_kernel,
        grid=(bsz, nt),
        in_specs=[
            pl.BlockSpec((None, tm, d), row),
            pl.BlockSpec((None, HALO, d), lambda b, i: (b, jnp.maximum(i * hpt - 1, 0), 0)),
            pl.BlockSpec((None, HALO, d), lambda b, i: (b, jnp.minimum((i + 1) * hpt, s // HALO - 1), 0)),
            _resident((1, d)), _resident((1, d)),
            _resident(w_in.shape), _resident((1, 2 * d)), _resident(conv_w.shape),
            _resident(w_conv_out.shape),
            tab, tab, tab,
        ],
        out_specs=qkv_specs + [pl.BlockSpec((None, tm, d), row), pl.BlockSpec((None, tm, d), row)],
        out_shape=qkv_shapes + [jax.ShapeDtypeStruct((bsz, s, d), BF16)] * 2,
        scratch_shapes=[
            pltpu.VMEM((tm + 2 * HALO, d), BF16),
            pltpu.VMEM((tm + 2 * HALO, D_CONV), F32),
            pltpu.VMEM((GROUP_W // LANES, tm, LANES), F32),
        ],
        compiler_params=pltpu.CompilerParams(
            dimension_semantics=("parallel", "parallel"), vmem_limit_bytes=VMEM_LIMIT),
        name="inproj",
    )(x, x, x, ln_g, ln_b, w_in, b_gate, conv_w, w_conv_out, cos_t, sa_t, sb_t)


def _attn_kernel(q_ref, k_ref, v_ref, o_ref, lse_ref, *, sub_len):
    s = k_ref.shape[0]
    qc = q_ref.shape[0]
    base = pl.program_id(1) * qc
    lane = lax.broadcasted_iota(jnp.int32, (1, GROUP_W), 1)
    head_masks = [(lane >= h * HEAD_DIM) & (lane < (h + 1) * HEAD_DIM) for h in range(HEADS_PER_GROUP)]
    qi = lax.broadcasted_iota(jnp.int32, (Q_BLOCK, 1), 0)
    kj = lax.broadcasted_iota(jnp.int32, (1, K_BLOCK), 1)

    def body(j, carry):
        r0 = pl.multiple_of(j * Q_BLOCK, Q_BLOCK)
        s0 = base + r0
        k0 = pl.multiple_of(jnp.clip(s0 - BAND, 0, s - K_BLOCK), BAND)
        q = q_ref[pl.ds(r0, Q_BLOCK), :]
        k = k_ref[pl.ds(k0, K_BLOCK), :]
        v = v_ref[pl.ds(k0, K_BLOCK), :]
        sub_lo = (s0 // sub_len) * sub_len
        qpos = s0 + qi
        kpos = k0 + kj
        valid = (jnp.abs(qpos - kpos) <= BAND) & (kpos >= sub_lo) & (kpos < sub_lo + sub_len)
        qm = jnp.concatenate([jnp.where(hm, q, jnp.zeros_like(q)) for hm in head_masks], axis=0)
        sc = lax.dot_general(qm, k, (((1,), (1,)), ((), ())), preferred_element_type=F32)
        sc = jnp.where(jnp.concatenate([valid] * HEADS_PER_GROUP, axis=0), sc, MASK_VALUE)
        m = jnp.max(sc, axis=-1, keepdims=True)
        p = jnp.exp(sc - m)
        den = jnp.sum(p, axis=-1, keepdims=True)
        o_all = _dot((p * (1.0 / den)).astype(BF16), v)
        lse_all = m + jnp.log(den)
        o_acc = jnp.zeros((Q_BLOCK, GROUP_W), F32)
        lse_acc = jnp.zeros((Q_BLOCK, GROUP_W), F32)
        for h, hm in enumerate(head_masks):
            rows = slice(h * Q_BLOCK, (h + 1) * Q_BLOCK)
            o_acc = jnp.where(hm, o_all[rows], o_acc)
            lse_acc = jnp.where(hm, lse_all[rows], lse_acc)
        o_ref[pl.ds(r0, Q_BLOCK), :] = o_acc.astype(BF16)
        lse_ref[pl.ds(r0, Q_BLOCK), :] = lse_acc
        return carry

    lax.fori_loop(0, qc // Q_BLOCK, body, 0, unroll=8)


def _attention(qkv, sub_len):
    _, bsz, s, w = qkv.shape
    qc = min(Q_CHUNK, s)
    kv_spec = lambda t: pl.BlockSpec((None, None, s, w), lambda b, j: (t, b, 0, 0))
    return pl.pallas_call(
        functools.partial(_attn_kernel, sub_len=sub_len),
        grid=(bsz, s // qc),
        in_specs=[pl.BlockSpec((None, None, qc, w), lambda b, j: (0, b, j, 0)), kv_spec(1), kv_spec(2)],
        out_specs=[pl.BlockSpec((None, qc, w), lambda b, j: (b, j, 0))] * 2,
        out_shape=[jax.ShapeDtypeStruct((bsz, s, w), BF16), jax.ShapeDtypeStruct((bsz, s, w), F32)],
        compiler_params=pltpu.CompilerParams(
            dimension_semantics=("parallel", "arbitrary"), vmem_limit_bytes=VMEM_LIMIT),
        name="attn",
    )(qkv, qkv, qkv)


def _first_index_of_max(vals, lane_f):
    mx = jnp.max(vals, axis=-1, keepdims=True)
    idx = jnp.min(jnp.where(vals == mx, lane_f, float(LANES)), axis=-1, keepdims=True)
    return mx, idx


def _mix_kernel(x_ref, o0_ref, o1_ref, o2_ref, l0_ref, l1_ref, l2_ref, cg_ref, ga_ref,
                g_ref, b_ref, wao_ref, wo_ref, g1_ref, b1_ref, wrh_ref, wrl_ref, br_ref,
                h1_ref, route_ref, so_ref, sl_ref):
    tm = x_ref.shape[0]
    h = _layer_norm(x_ref[...], g_ref[...], b_ref[...])
    halves = GROUP_W // LANES
    for gi, (o_ref, l_ref, r) in enumerate(zip((o0_ref, o1_ref, o2_ref), (l0_ref, l1_ref, l2_ref), DILATIONS)):
        for c in range(halves):
            cols = slice(c * LANES, (c + 1) * LANES)
            for res in range(r):
                rows = pl.ds(res, tm // r, stride=r) if r > 1 else slice(None)
                so_ref[gi * halves + c, rows, :] = o_ref[res, :, cols].astype(F32)
                sl_ref[gi * halves + c, rows, :] = l_ref[res, :, cols]

    def natural(ref, gi):
        return jnp.concatenate([ref[gi * halves + c] for c in range(halves)], axis=1)

    lses = [natural(sl_ref, gi) for gi in range(N_GROUPS)]
    mx = jnp.maximum(jnp.maximum(lses[0], lses[1]), lses[2])
    es = [jnp.exp(l - mx) for l in lses]
    inv_den = 1.0 / (es[0] + es[1] + es[2])
    attn = jnp.concatenate(
        [(natural(so_ref, gi) * (es[gi] * inv_den)).astype(BF16) for gi in range(N_GROUPS)], axis=1)
    attn_branch = _dot(attn, wao_ref[...])
    merged = ga_ref[...].astype(F32) * attn_branch + cg_ref[...].astype(F32)
    mix = _dot(merged.astype(BF16), wo_ref[...])
    h1 = _layer_norm(DEEPNORM_ALPHA * h + mix, g1_ref[...], b1_ref[...])
    _store_token_major(h1_ref, 0, h1)

    hi = h1.astype(BF16)
    lo = (h1 - hi.astype(F32)).astype(BF16)
    wrh = wrh_ref[...]
    logits = _dot(hi, wrh) + _dot(lo, wrh) + _dot(hi, wrl_ref[...]) + br_ref[...]
    lane = lax.broadcasted_iota(jnp.int32, (1, LANES), 1)
    lane_f = lane.astype(F32)
    is_grp = lane < N_EXPERT_GROUPS
    gl = jnp.where(is_grp, logits, MASK_VALUE)
    ge = jnp.exp(gl - jnp.max(gl, axis=-1, keepdims=True))
    gp = jnp.where(is_grp, ge / jnp.sum(ge, axis=-1, keepdims=True), -1.0)
    grp_w, grp_idx = _first_index_of_max(gp, lane_f)
    e_lo = ROUTE_LANE0 + grp_idx * EXPERTS_PER_GROUP
    sel = jnp.where((lane_f >= e_lo) & (lane_f < e_lo + EXPERTS_PER_GROUP), logits, MASK_VALUE)
    v1, i1 = _first_index_of_max(sel, lane_f)
    v2, i2 = _first_index_of_max(jnp.where(lane_f == i1, MASK_VALUE, sel), lane_f)
    e2 = jnp.exp(v2 - v1)
    w1 = grp_w / (1.0 + e2)
    w2 = grp_w * e2 / (1.0 + e2)
    route_ref[...] = jnp.where(
        lane == 0, i1 - ROUTE_LANE0,
        jnp.where(lane == 1, i2 - ROUTE_LANE0, jnp.where(lane == 2, w1, jnp.where(lane == 3, w2, 0.0))))


def _mix(x, o_list, lse_list, cg, ga, ln_g, ln_b, w_attn_out, w_o, ln1_g, ln1_b, wr_hi, wr_lo, b_route):
    bsz, s, d = x.shape
    tm = ROW_TILE
    row = lambda b, i: (b, i, 0)
    dil_specs = [pl.BlockSpec((None, r, tm // r, GROUP_W), lambda b, i: (b, 0, i, 0)) for r in DILATIONS]
    o_views = [o.reshape(bsz, r, s // r, GROUP_W) for o, r in zip(o_list, DILATIONS)]
    l_views = [l.reshape(bsz, r, s // r, GROUP_W) for l, r in zip(lse_list, DILATIONS)]
    return pl.pallas_call(
        _mix_kernel,
        grid=(bsz, s // tm),
        in_specs=[pl.BlockSpec((None, tm, d), row)] + dil_specs + dil_specs + [
            pl.BlockSpec((None, tm, d), row), pl.BlockSpec((None, tm, d), row),
            _resident((1, d)), _resident((1, d)),
            _resident(w_attn_out.shape), _resident(w_o.shape),
            _resident((1, d)), _resident((1, d)),
            _resident(wr_hi.shape), _resident(wr_lo.shape), _resident((1, LANES)),
        ],
        out_specs=[pl.BlockSpec((tm * ROW_CHUNKS, LANES), lambda b, i: (b * (s // tm) + i, 0)),
                   pl.BlockSpec((None, tm, LANES), row)],
        out_shape=[jax.ShapeDtypeStruct((bsz * s * ROW_CHUNKS, LANES), F32),
                   jax.ShapeDtypeStruct((bsz, s, LANES), F32)],
        scratch_shapes=[pltpu.VMEM((N_GROUPS * GROUP_W // LANES, tm, LANES), F32)] * 2,
        compiler_params=pltpu.CompilerParams(
            dimension_semantics=("parallel", "parallel"), vmem_limit_bytes=VMEM_LIMIT),
        name="mix",
    )(x, *o_views, *l_views, cg, ga, ln_g, ln_b, w_attn_out, w_o, ln1_g, ln1_b, wr_hi, wr_lo, b_route)


def _rank_kernel(route_ref, rank_ref, counts_ref, tri_ref):
    i = pl.program_id(0)
    tm = route_ref.shape[0]

    @pl.when(i == 0)
    def _():
        counts_ref[...] = jnp.zeros_like(counts_ref)
        r = lax.broadcasted_iota(jnp.int32, (tm, tm), 0)
        c = lax.broadcasted_iota(jnp.int32, (tm, tm), 1)
        tri_ref[...] = (c < r).astype(BF16)

    route = route_ref[...]
    lane = lax.broadcasted_iota(jnp.int32, (1, LANES), 1)
    oh0 = lane == route[:, 0:1].astype(jnp.int32)
    oh1 = lane == route[:, 1:2].astype(jnp.int32)
    oh = oh0.astype(F32) + oh1.astype(F32)
    before = counts_ref[...] + _dot(tri_ref[...], oh.astype(BF16))
    r0 = jnp.sum(jnp.where(oh0, before, 0.0), axis=-1, keepdims=True)
    r1 = jnp.sum(jnp.where(oh1, before + oh0.astype(F32), 0.0), axis=-1, keepdims=True)
    rank_ref[...] = jnp.where(lane == 0, r0, jnp.where(lane == 1, r1, 0.0))
    counts_ref[...] = counts_ref[...] + jnp.sum(oh, axis=0, keepdims=True)


def _rank(route):
    t = route.shape[0]
    tm = ROW_TILE
    return pl.pallas_call(
        _rank_kernel,
        grid=(t // tm,),
        in_specs=[pl.BlockSpec((tm, LANES), lambda i: (i, 0))],
        out_specs=[pl.BlockSpec((tm, LANES), lambda i: (i, 0)), pl.BlockSpec((1, LANES), lambda i: (0, 0))],
        out_shape=[jax.ShapeDtypeStruct((t, LANES), F32), jax.ShapeDtypeStruct((1, LANES), F32)],
        scratch_shapes=[pltpu.VMEM((tm, tm), BF16)],
        compiler_params=pltpu.CompilerParams(dimension_semantics=("arbitrary",), vmem_limit_bytes=VMEM_LIMIT),
        name="rank",
    )(route)


def _dispatch_plan(route, rank, counts, t):
    eid = route[:, :TOP_K_INNER].astype(jnp.int32)
    counts = counts[0, :N_EXPERTS].astype(jnp.int32)
    padded = (counts + EXPERT_BLOCK - 1) // EXPERT_BLOCK * EXPERT_BLOCK
    pad_end = jnp.cumsum(padded)
    pad_start = pad_end - padded
    experts = jnp.arange(N_EXPERTS, dtype=jnp.int32)
    start_of = jnp.sum(jnp.where(eid[..., None] == experts, pad_start, 0), axis=-1)
    dest = start_of + rank[:, :TOP_K_INNER].astype(jnp.int32)
    n_blocks = -(-t * TOP_K_INNER // EXPERT_BLOCK) + N_EXPERTS
    n_slots = n_blocks * EXPERT_BLOCK
    block_e = jnp.minimum(
        jnp.searchsorted(pad_end, jnp.arange(n_blocks, dtype=jnp.int32) * EXPERT_BLOCK, side='right'),
        N_EXPERTS - 1).astype(jnp.int32)
    n_empty = n_slots - t * TOP_K_INNER
    gap_end = jnp.cumsum(padded - counts)
    gap_start = gap_end - (padded - counts)
    k = jnp.arange(n_empty, dtype=jnp.int32)
    in_gap = (k[:, None] >= gap_start) & (k[:, None] < gap_end)
    empty = jnp.where(k < gap_end[-1],
                      jnp.sum(jnp.where(in_gap, pad_start + counts + k[:, None] - gap_start, 0), axis=-1),
                      pad_end[-1] + k - gap_end[-1])
    return (dest * ROW_CHUNKS).astype(jnp.int32), (empty * ROW_CHUNKS).astype(jnp.int32), block_e


def _token_copy(src_ref, src_row, dst_ref, dst_row, sem):
    return pltpu.make_async_copy(src_ref.at[pl.ds(pl.multiple_of(src_row, ROW_CHUNKS), ROW_CHUNKS), :],
                                 dst_ref.at[pl.ds(pl.multiple_of(dst_row, ROW_CHUNKS), ROW_CHUNKS), :], sem)


def _wait_rows(hbm_ref, n_rows, sem):
    rows = hbm_ref.at[pl.ds(0, n_rows), :]
    pltpu.make_async_copy(rows, rows, sem).wait()


def _dispatch_kernel(dest_ref, empty_ref, h1_ref, xs_hbm, zero_ref, sem):
    i = pl.program_id(0)
    tm = h1_ref.shape[0] // ROW_CHUNKS
    n_empty = empty_ref.shape[1]

    @pl.when(i == 0)
    def _():
        zero_ref[...] = jnp.zeros_like(zero_ref)

    def body(c, carry):
        for u in range(ISSUE_UNROLL):
            j = c * ISSUE_UNROLL + u
            for choice in range(TOP_K_INNER):
                _token_copy(h1_ref, j * ROW_CHUNKS, xs_hbm, dest_ref[0, choice * tm + j], sem.at[0]).start()
        return carry
    lax.fori_loop(0, tm // ISSUE_UNROLL, body, 0)

    def zero_body(c, carry):
        for u in range(ISSUE_UNROLL):
            _token_copy(zero_ref, 0, xs_hbm, empty_ref[0, c * ISSUE_UNROLL + u], sem.at[1]).start()
        return carry
    lax.fori_loop(0, n_empty // ISSUE_UNROLL, zero_body, 0)

    _wait_rows(xs_hbm, TOP_K_INNER * tm * ROW_CHUNKS, sem.at[0])
    _wait_rows(xs_hbm, n_empty * ROW_CHUNKS, sem.at[1])


def _dispatch(h1, dest_row, empty_row, n_slots):
    t = h1.shape[0] // ROW_CHUNKS
    tm = ROW_TILE
    nt = t // tm
    n_empty = empty_row.shape[0] // nt
    assert n_empty * nt == empty_row.shape[0] and n_empty % ISSUE_UNROLL == 0
    idx3 = dest_row.reshape(nt, tm, TOP_K_INNER).transpose(0, 2, 1).reshape(nt, 1, TOP_K_INNER * tm)
    return pl.pallas_call(
        _dispatch_kernel,
        grid=(nt,),
        in_specs=[
            pl.BlockSpec((None, 1, TOP_K_INNER * tm), lambda i: (i, 0, 0), memory_space=pltpu.SMEM),
            pl.BlockSpec((None, 1, n_empty), lambda i: (i, 0, 0), memory_space=pltpu.SMEM),
            pl.BlockSpec((tm * ROW_CHUNKS, LANES), lambda i: (i, 0)),
        ],
        out_specs=pl.BlockSpec(memory_space=pl.ANY),
        out_shape=jax.ShapeDtypeStruct((n_slots * ROW_CHUNKS, LANES), F32),
        scratch_shapes=[pltpu.VMEM((ROW_CHUNKS, LANES), F32), pltpu.SemaphoreType.DMA((2,))],
        compiler_params=pltpu.CompilerParams(dimension_semantics=("arbitrary",), vmem_limit_bytes=VMEM_LIMIT),
        name="dispatch",
    )(idx3, empty_row.reshape(nt, 1, n_empty), h1)


def _expert_mlp(xb, wg_ref, wu_ref, wd_ref):
    acc = None
    for c in range(D_EXPERT // EXPERT_CHUNK):
        cols = slice(c * EXPERT_CHUNK, (c + 1) * EXPERT_CHUNK)
        gate = _dot(xb, wg_ref[:, cols])
        up = _dot(xb, wu_ref[:, cols])
        hidden = (gate * jax.nn.sigmoid(gate) * up).astype(BF16)
        part = _dot(hidden, wd_ref[cols, :])
        acc = part if acc is None else acc + part
    return acc


def _expert_kernel(be_ref, x_ref, wga_ref, wua_ref, wda_ref, wgb_ref, wub_ref, wdb_ref, y_ref):
    i = pl.program_id(0)
    same = be_ref[2 * i] == be_ref[2 * i + 1]

    @pl.when(same)
    def _():
        xb = _load_token_major(x_ref, 0, 2 * EXPERT_BLOCK).astype(BF16)
        _store_token_major(y_ref, 0, _expert_mlp(xb, wga_ref, wua_ref, wda_ref))

    @pl.when(jnp.logical_not(same))
    def _():
        for half, w in enumerate(((wga_ref, wua_ref, wda_ref), (wgb_ref, wub_ref, wdb_ref))):
            xb = _load_token_major(x_ref, half * EXPERT_BLOCK, EXPERT_BLOCK).astype(BF16)
            _store_token_major(y_ref, half * EXPERT_BLOCK, _expert_mlp(xb, *w))


def _experts(xs, block_e, w_gate, w_up, w_down):
    n_blocks = block_e.shape[0]
    assert n_blocks % 2 == 0
    d = D_MODEL
    blk = pl.BlockSpec((2 * EXPERT_BLOCK * ROW_CHUNKS, LANES), lambda i, be: (i, 0))
    w_specs = [pl.BlockSpec(shape, (lambda i, be, k=k: (be[2 * i + k], 0, 0)))
               for k in range(2) for shape in ((None, d, D_EXPERT), (None, d, D_EXPERT), (None, D_EXPERT, d))]
    grid_spec = pltpu.PrefetchScalarGridSpec(
        num_scalar_prefetch=1,
        grid=(n_blocks // 2,),
        in_specs=[blk] + w_specs,
        out_specs=blk,
    )
    return pl.pallas_call(
        _expert_kernel,
        grid_spec=grid_spec,
        out_shape=jax.ShapeDtypeStruct(xs.shape, F32),
        compiler_params=pltpu.CompilerParams(
            dimension_semantics=("arbitrary",), vmem_limit_bytes=VMEM_LIMIT),
        name="experts",
    )(block_e, xs, w_gate, w_up, w_down, w_gate, w_up, w_down)


def _combine_kernel(idx_ref, idxn_ref, yb_hbm, h1_ref, route_ref, g_ref, b_ref, y_ref, gbuf, sem):
    i = pl.program_id(0)
    n = pl.num_programs(0)
    tm = y_ref.shape[0]
    slot = i % 2

    def gather(src_ref, s):
        def body(c, carry):
            for u in range(ISSUE_UNROLL):
                j = c * ISSUE_UNROLL + u
                _token_copy(yb_hbm, src_ref[0, j], gbuf.at[s], j * ROW_CHUNKS, sem.at[s]).start()
            return carry
        lax.fori_loop(0, TOP_K_INNER * tm // ISSUE_UNROLL, body, 0)

    @pl.when(i == 0)
    def _():
        gather(idx_ref, 0)

    @pl.when(i + 1 < n)
    def _():
        gather(idxn_ref, 1 - slot)

    cur = gbuf.at[slot]
    _wait_rows(yb_hbm, TOP_K_INNER * tm * ROW_CHUNKS, sem.at[slot])
    route = route_ref[...]
    ffn = route[:, 2:3] * _load_token_major(cur, 0, tm) + route[:, 3:4] * _load_token_major(cur, tm, tm)
    h1 = _load_token_major(h1_ref, 0, tm)
    y_ref[...] = _layer_norm(DEEPNORM_ALPHA * h1 + ffn, g_ref[...], b_ref[...])


def _combine(h1, route, yb, dest_row, ln2_g, ln2_b):
    t = route.shape[0]
    d = D_MODEL
    tm = COMBINE_TILE
    nt = t // tm
    idx3 = dest_row.reshape(nt, tm, TOP_K_INNER).transpose(0, 2, 1).reshape(nt, 1, TOP_K_INNER * tm)
    smem_blk = lambda f: pl.BlockSpec((None, 1, TOP_K_INNER * tm), f, memory_space=pltpu.SMEM)
    return pl.pallas_call(
        _combine_kernel,
        grid=(nt,),
        in_specs=[
            smem_blk(lambda i: (i, 0, 0)),
            smem_blk(lambda i: (jnp.minimum(i + 1, nt - 1), 0, 0)),
            pl.BlockSpec(memory_space=pl.ANY),
            pl.BlockSpec((tm * ROW_CHUNKS, LANES), lambda i: (i, 0)),
            pl.BlockSpec((tm, LANES), lambda i: (i, 0)),
            _resident((1, d)), _resident((1, d)),
        ],
        out_specs=pl.BlockSpec((tm, d), lambda i: (i, 0)),
        out_shape=jax.ShapeDtypeStruct((t, d), F32),
        scratch_shapes=[pltpu.VMEM((2, TOP_K_INNER * tm * ROW_CHUNKS, LANES), F32), pltpu.SemaphoreType.DMA((2,))],
        compiler_params=pltpu.CompilerParams(
            dimension_semantics=("arbitrary",), vmem_limit_bytes=VMEM_LIMIT),
        name="combine",
    )(idx3, idx3, yb, h1, route, ln2_g, ln2_b)


def _trunk(x, p):
    bsz, s, d = x.shape
    t = bsz * s
    qkv0, qkv1, qkv2, cg, ga = _inproj(x, p['ln_in_g'], p['ln_in_b'], p['w_in'], p['b_gate'],
                                       p['conv_w'], p['w_conv_out'])
    o_list, lse_list = [], []
    for qkv, r in zip((qkv0, qkv1, qkv2), DILATIONS):
        o, lse = _attention(qkv.reshape(3, bsz, s, GROUP_W), s // r)
        o_list.append(o)
        lse_list.append(lse)
    h1, route = _mix(x, o_list, lse_list, cg, ga, p['ln_in_g'], p['ln_in_b'], p['w_attn_out'], p['w_o'],
                     p['ln1_g'], p['ln1_b'], p['wr_hi'], p['wr_lo'], p['b_route'])
    route = route.reshape(t, LANES)
    rank, counts = _rank(route)
    dest_row, empty_row, block_e = _dispatch_plan(route, rank, counts, t)
    xs = _dispatch(h1, dest_row, empty_row, block_e.shape[0] * EXPERT_BLOCK)
    yb = _experts(xs, block_e, p['w_gate'], p['w_up'], p['w_down'])
    y = _combine(h1, route, yb, dest_row, p['ln2_g'], p['ln2_b'])
    return y.reshape(bsz, s, d)


def kernel(x_prompt, x_sample, ln_in_g, ln_in_b, w_in, b_gate, conv_w, w_attn_out, w_conv_out, w_o, ln1_g, ln1_b, w_route_group, b_route_group, w_route_expert, b_route_expert, w_gate, w_up, w_down, ln2_g, ln2_b):
    d = D_MODEL
    w_route = jnp.concatenate([w_route_group[0], w_route_expert[0]], axis=1)
    w_route = jnp.pad(w_route, ((0, 0), (0, LANES - w_route.shape[1])))
    wr_hi = w_route.astype(BF16)
    b_route = jnp.concatenate([b_route_group[0], b_route_expert[0]]).astype(F32)
    p = {
        'ln_in_g': ln_in_g.reshape(1, d), 'ln_in_b': ln_in_b.reshape(1, d),
        'w_in': w_in[0].astype(BF16), 'b_gate': b_gate[0].reshape(1, 2 * d), 'conv_w': conv_w[0],
        'w_conv_out': w_conv_out[0].astype(BF16), 'w_attn_out': w_attn_out[0].astype(BF16),
        'w_o': w_o[0].astype(BF16),
        'ln1_g': ln1_g[0].reshape(1, d), 'ln1_b': ln1_b[0].reshape(1, d),
        'wr_hi': wr_hi, 'wr_lo': (w_route - wr_hi.astype(F32)).astype(BF16),
        'b_route': jnp.pad(b_route, (0, LANES - b_route.shape[0])).reshape(1, LANES),
        'w_gate': w_gate[0].astype(BF16), 'w_up': w_up[0].astype(BF16), 'w_down': w_down[0].astype(BF16),
        'ln2_g': ln2_g[0].reshape(1, d), 'ln2_b': ln2_b[0].reshape(1, d),
    }
    return (_trunk(x_prompt, p), _trunk(x_sample, p))
```

```python
import functools

import jax
import jax.numpy as jnp
import numpy as np
from jax import lax
from jax.experimental import pallas as pl
from jax.experimental.pallas import tpu as pltpu

F32 = jnp.float32
BF16 = jnp.bfloat16

D_MODEL = 1024
HEAD_DIM = 64
HEADS_PER_GROUP = 4
GROUP_W = HEADS_PER_GROUP * HEAD_DIM
DILATIONS = (1, 4, 16)
BAND = 64
N_GROUPS = len(DILATIONS)
D_ATTN = N_GROUPS * GROUP_W
ATTN_SCALE = HEAD_DIM ** -0.5
ROPE_DIM = HEAD_DIM // 4
ROPE_THETA = 500000.0
MASK_VALUE = -1e30
D_CONV = D_MODEL
COL_CONV_B = 3 * D_ATTN
COL_CONV_C = COL_CONV_B + D_CONV
COL_CONV_H = COL_CONV_C + D_CONV
COL_GATE_ATTN = COL_CONV_H + D_CONV
COL_GATE_CONV = COL_GATE_ATTN + D_MODEL
N_EXPERT_GROUPS = 4
EXPERTS_PER_GROUP = 8
N_EXPERTS = N_EXPERT_GROUPS * EXPERTS_PER_GROUP
TOP_K_INNER = 2
D_EXPERT = 512
EXPERT_BLOCK = 256
EXPERT_CHUNK = 256
LN_EPS = 1e-5
DEPTH = 1
DEEPNORM_ALPHA = (2 * DEPTH) ** 0.25

LANES = 128
ROW_CHUNKS = D_MODEL // LANES
ROW_TILE = 512
HALO = 16
Q_BLOCK = 128
K_BLOCK = Q_BLOCK + 2 * BAND
Q_CHUNK = 2048
COMBINE_TILE = 256
ISSUE_UNROLL = 16
ROUTE_LANE0 = N_EXPERT_GROUPS
ROUTE_LANE_W = TOP_K_INNER
ROUTE_LANE_RANK = 2 * TOP_K_INNER
VMEM_LIMIT = 56 * 1024 * 1024


def _layer_norm(v, g, b):
    mu = jnp.mean(v, axis=-1, keepdims=True)
    d = v - mu
    var = jnp.mean(d * d, axis=-1, keepdims=True)
    return d * lax.rsqrt(var + LN_EPS) * g + b


def _dot(a, b):
    return jnp.dot(a, b, preferred_element_type=F32)


def _store_token_major(ref, first_token, val):
    n = val.shape[0]
    for k in range(ROW_CHUNKS):
        ref[pl.ds(first_token * ROW_CHUNKS + k, n, stride=ROW_CHUNKS), :] = val[:, k * LANES:(k + 1) * LANES]


def _load_token_major(ref, first_token, n):
    return jnp.concatenate(
        [ref[pl.ds(first_token * ROW_CHUNKS + k, n, stride=ROW_CHUNKS), :] for k in range(ROW_CHUNKS)], axis=1)


def _resident(shape):
    nd = len(shape)
    return pl.BlockSpec(shape, lambda *_: (0,) * nd, pipeline_mode=pl.Buffered(1))


def _inproj_kernel(x_ref, xp_ref, xn_ref, g_ref, b_ref, win_ref, bg_ref, cw_ref, wco_ref,
                   cos_ref, sa_ref, sb_ref,
                   qkv0_ref, qkv1_ref, qkv2_ref, cg_ref, ga_ref,
                   hext_ref, u_ref, stage_ref):
    i = pl.program_id(1)
    nt = pl.num_programs(1)
    tm = x_ref.shape[0]
    g = g_ref[...]
    b = b_ref[...]
    hext_ref[pl.ds(HALO, tm), :] = _layer_norm(x_ref[...], g, b).astype(BF16)
    hext_ref[pl.ds(0, HALO), :] = _layer_norm(xp_ref[...], g, b).astype(BF16)
    hext_ref[pl.ds(HALO + tm, HALO), :] = _layer_norm(xn_ref[...], g, b).astype(BF16)
    hm = hext_ref[pl.ds(HALO, tm), :]
    he = hext_ref[...]

    def proj(lhs, col, n):
        return _dot(lhs, win_ref[:, col:col + n])

    u = proj(he, COL_CONV_C, D_CONV) * proj(he, COL_CONV_H, D_CONV)
    rows = lax.broadcasted_iota(jnp.int32, (tm + 2 * HALO, 1), 0)
    lo = jnp.where(i == 0, HALO, 0)
    hi = jnp.where(i == nt - 1, HALO + tm, tm + 2 * HALO)
    u_ref[...] = jnp.where((rows >= lo) & (rows < hi), u, 0.0)
    cw = cw_ref[...]
    conv = (cw[0:1] * u_ref[pl.ds(HALO - 1, tm), :] + cw[1:2] * u_ref[pl.ds(HALO, tm), :]
            + cw[2:3] * u_ref[pl.ds(HALO + 1, tm), :])
    cb = proj(hm, COL_CONV_B, D_CONV)
    conv_branch = _dot((cb * conv).astype(BF16), wco_ref[...])
    gate_conv = jax.nn.sigmoid(proj(hm, COL_GATE_CONV, D_MODEL) + bg_ref[:, D_MODEL:])
    cg_ref[...] = (gate_conv * conv_branch).astype(BF16)
    ga_ref[...] = jax.nn.sigmoid(proj(hm, COL_GATE_ATTN, D_MODEL) + bg_ref[:, :D_MODEL]).astype(BF16)

    cosv = cos_ref[...]
    sav = sa_ref[...]
    sbv = sb_ref[...]
    half = ROPE_DIM // 2
    for gi, (out_ref, r) in enumerate(zip((qkv0_ref, qkv1_ref, qkv2_ref), DILATIONS)):
        for t in range(3):
            p = proj(hm, t * D_ATTN + gi * GROUP_W, GROUP_W)
            if t < 2:
                parts = []
                for c in range(GROUP_W // LANES):
                    pc = p[:, c * LANES:(c + 1) * LANES]
                    parts.append(pc * cosv + pltpu.roll(pc, LANES - half, 1) * sav
                                 + pltpu.roll(pc, half, 1) * sbv)
                p = jnp.concatenate(parts, axis=1)
                if t == 0:
                    p = p * ATTN_SCALE
            if r == 1:
                out_ref[t, 0] = p.astype(BF16)
            else:
                for c in range(GROUP_W // LANES):
                    stage_ref[c] = p[:, c * LANES:(c + 1) * LANES]
                for res in range(r):
                    for c in range(GROUP_W // LANES):
                        out_ref[t, res, :, c * LANES:(c + 1) * LANES] = (
                            stage_ref[c, pl.ds(res, tm // r, stride=r), :].astype(BF16))


def _rope_tables(s):
    half = ROPE_DIM // 2
    inv_freq = (np.float32(ROPE_THETA) ** (-np.arange(half, dtype=np.float32) * np.float32(2.0) / ROPE_DIM))
    ang = np.arange(s, dtype=np.float32)[:, None] * inv_freq.astype(np.float32)[None, :]
    cos = np.cos(ang).astype(np.float32)
    sin = np.sin(ang).astype(np.float32)
    pad = np.zeros((s, HEAD_DIM - ROPE_DIM), np.float32)
    zero = np.zeros((s, half), np.float32)
    cos_t = np.concatenate([cos, cos, pad + 1.0], axis=1)
    sa_t = np.concatenate([-sin, zero, pad], axis=1)
    sb_t = np.concatenate([zero, sin, pad], axis=1)
    rep = LANES // HEAD_DIM
    return tuple(jnp.asarray(np.tile(t, (1, rep))) for t in (cos_t, sa_t, sb_t))


def _inproj(x, ln_g, ln_b, w_in, b_gate, conv_w, w_conv_out):
    bsz, s, d = x.shape
    tm = ROW_TILE
    nt = s // tm
    hpt = tm // HALO
    cos_t, sa_t, sb_t = _rope_tables(s)
    row = lambda b, i: (b, i, 0)
    tab = pl.BlockSpec((tm, LANES), lambda b, i: (i, 0))
    qkv_shapes = [jax.ShapeDtypeStruct((3, bsz, r, s // r, GROUP_W), BF16) for r in DILATIONS]
    qkv_specs = [pl.BlockSpec((3, None, r, tm // r, GROUP_W), lambda b, i: (0, b, 0, i, 0))
                 for r in DILATIONS]
    return pl.pallas_call(
        _inproj_kernel,
        grid=(bsz, nt),
        in_specs=[
            pl.BlockSpec((None, tm, d), row),
            pl.BlockSpec((None, HALO, d), lambda b, i: (b, jnp.maximum(i * hpt - 1, 0), 0)),
            pl.BlockSpec((None, HALO, d), lambda b, i: (b, jnp.minimum((i + 1) * hpt, s // HALO - 1), 0)),
            _resident((1, d)), _resident((1, d)),
            _resident(w_in.shape), _resident((1, 2 * d)), _resident(conv_w.shape),
            _resident(w_conv_out.shape),
            tab, tab, tab,
        ],
        out_specs=qkv_specs + [pl.BlockSpec((None, tm, d), row), pl.BlockSpec((None, tm, d), row)],
        out_shape=qkv_shapes + [jax.ShapeDtypeStruct((bsz, s, d), BF16)] * 2,
        scratch_shapes=[
            pltpu.VMEM((tm + 2 * HALO, d), BF16),
            pltpu.VMEM((tm + 2 * HALO, D_CONV), F32),
            pltpu.VMEM((GROUP_W // LANES, tm, LANES), F32),
        ],
        compiler_params=pltpu.CompilerParams(
            dimension_semantics=("parallel", "parallel"), vmem_limit_bytes=VMEM_LIMIT),
        name="inproj",
    )(x, x, x, ln_g, ln_b, w_in, b_gate, conv_w, w_conv_out, cos_t, sa_t, sb_t)


def _attn_kernel(q_ref, k_ref, v_ref, o_ref, lse_ref, *, sub_len):
    s = k_ref.shape[0]
    qc = q_ref.shape[0]
    base = pl.program_id(1) * qc
    lane = lax.broadcasted_iota(jnp.int32, (1, GROUP_W), 1)
    head_masks = [(lane >= h * HEAD_DIM) & (lane < (h + 1) * HEAD_DIM) for h in range(HEADS_PER_GROUP)]
    qi = lax.broadcasted_iota(jnp.int32, (Q_BLOCK, 1), 0)
    kj = lax.broadcasted_iota(jnp.int32, (1, K_BLOCK), 1)

    def body(j, carry):
        r0 = pl.multiple_of(j * Q_BLOCK, Q_BLOCK)
        s0 = base + r0
        k0 = pl.multiple_of(jnp.clip(s0 - BAND, 0, s - K_BLOCK), BAND)
        q = q_ref[pl.ds(r0, Q_BLOCK), :]
        k = k_ref[pl.ds(k0, K_BLOCK), :]
        v = v_ref[pl.ds(k0, K_BLOCK), :]
        sub_lo = (s0 // sub_len) * sub_len
        qpos = s0 + qi
        kpos = k0 + kj
        valid = (jnp.abs(qpos - kpos) <= BAND) & (kpos >= sub_lo) & (kpos < sub_lo + sub_len)
        qm = jnp.concatenate([jnp.where(hm, q, jnp.zeros_like(q)) for hm in head_masks], axis=0)
        sc = lax.dot_general(qm, k, (((1,), (1,)), ((), ())), preferred_element_type=F32)
        sc = jnp.where(jnp.concatenate([valid] * HEADS_PER_GROUP, axis=0), sc, MASK_VALUE)
        m = jnp.max(sc, axis=-1, keepdims=True)
        p = jnp.exp(sc - m)
        den = jnp.sum(p, axis=-1, keepdims=True)
        o_all = _dot((p * (1.0 / den)).astype(BF16), v)
        lse_all = m + jnp.log(den)
        o_acc = jnp.zeros((Q_BLOCK, GROUP_W), F32)
        lse_acc = jnp.zeros((Q_BLOCK, GROUP_W), F32)
        for h, hm in enumerate(head_masks):
            rows = slice(h * Q_BLOCK, (h + 1) * Q_BLOCK)
            o_acc = jnp.where(hm, o_all[rows], o_acc)
            lse_acc = jnp.where(hm, lse_all[rows], lse_acc)
        o_ref[pl.ds(r0, Q_BLOCK), :] = o_acc.astype(BF16)
        lse_ref[pl.ds(r0, Q_BLOCK), :] = lse_acc
        return carry

    lax.fori_loop(0, qc // Q_BLOCK, body, 0, unroll=8)


def _attention(qkv, sub_len):
    _, bsz, s, w = qkv.shape
    qc = min(Q_CHUNK, s)
    kv_spec = lambda t: pl.BlockSpec((None, None, s, w), lambda b, j: (t, b, 0, 0))
    return pl.pallas_call(
        functools.partial(_attn_kernel, sub_len=sub_len),
        grid=(bsz, s // qc),
        in_specs=[pl.BlockSpec((None, None, qc, w), lambda b, j: (0, b, j, 0)), kv_spec(1), kv_spec(2)],
        out_specs=[pl.BlockSpec((None, qc, w), lambda b, j: (b, j, 0))] * 2,
        out_shape=[jax.ShapeDtypeStruct((bsz, s, w), BF16), jax.ShapeDtypeStruct((bsz, s, w), F32)],
        compiler_params=pltpu.CompilerParams(
            dimension_semantics=("parallel", "arbitrary"), vmem_limit_bytes=VMEM_LIMIT),
        name="attn",
    )(qkv, qkv, qkv)


def _first_index_of_max(vals, lane_f):
    mx = jnp.max(vals, axis=-1, keepdims=True)
    idx = jnp.min(jnp.where(vals == mx, lane_f, float(LANES)), axis=-1, keepdims=True)
    return mx, idx


def _mix_kernel(x_ref, o0_ref, o1_ref, o2_ref, l0_ref, l1_ref, l2_ref, cg_ref, ga_ref,
                g_ref, b_ref, wao_ref, wo_ref, g1_ref, b1_ref, wrh_ref, wrl_ref, br_ref,
                h1_ref, route_ref, counts_ref, so_ref, sl_ref, tri_ref):
    tm = x_ref.shape[0]
    h = _layer_norm(x_ref[...], g_ref[...], b_ref[...])
    halves = GROUP_W // LANES
    for gi, (o_ref, l_ref, r) in enumerate(zip((o0_ref, o1_ref, o2_ref), (l0_ref, l1_ref, l2_ref), DILATIONS)):
        for c in range(halves):
            cols = slice(c * LANES, (c + 1) * LANES)
            for res in range(r):
                rows = pl.ds(res, tm // r, stride=r) if r > 1 else slice(None)
                so_ref[gi * halves + c, rows, :] = o_ref[res, :, cols].astype(F32)
                sl_ref[gi * halves + c, rows, :] = l_ref[res, :, cols]

    def natural(ref, gi):
        return jnp.concatenate([ref[gi * halves + c] for c in range(halves)], axis=1)

    lses = [natural(sl_ref, gi) for gi in range(N_GROUPS)]
    mx = jnp.maximum(jnp.maximum(lses[0], lses[1]), lses[2])
    es = [jnp.exp(l - mx) for l in lses]
    inv_den = 1.0 / (es[0] + es[1] + es[2])
    attn = jnp.concatenate(
        [(natural(so_ref, gi) * (es[gi] * inv_den)).astype(BF16) for gi in range(N_GROUPS)], axis=1)
    attn_branch = _dot(attn, wao_ref[...])
    merged = ga_ref[...].astype(F32) * attn_branch + cg_ref[...].astype(F32)
    mix = _dot(merged.astype(BF16), wo_ref[...])
    h1 = _layer_norm(DEEPNORM_ALPHA * h + mix, g1_ref[...], b1_ref[...])
    _store_token_major(h1_ref, 0, h1)

    hi = h1.astype(BF16)
    lo = (h1 - hi.astype(F32)).astype(BF16)
    wrh = wrh_ref[...]
    logits = _dot(hi, wrh) + _dot(lo, wrh) + _dot(hi, wrl_ref[...]) + br_ref[...]
    lane = lax.broadcasted_iota(jnp.int32, (1, LANES), 1)
    lane_f = lane.astype(F32)
    is_grp = lane < N_EXPERT_GROUPS
    gl = jnp.where(is_grp, logits, MASK_VALUE)
    ge = jnp.exp(gl - jnp.max(gl, axis=-1, keepdims=True))
    gp = jnp.where(is_grp, ge / jnp.sum(ge, axis=-1, keepdims=True), -1.0)
    grp_w, grp_idx = _first_index_of_max(gp, lane_f)
    e_lo = ROUTE_LANE0 + grp_idx * EXPERTS_PER_GROUP
    sel = jnp.where((lane_f >= e_lo) & (lane_f < e_lo + EXPERTS_PER_GROUP), logits, MASK_VALUE)
    v1, i1 = _first_index_of_max(sel, lane_f)
    v2, i2 = _first_index_of_max(jnp.where(lane_f == i1, MASK_VALUE, sel), lane_f)
    e2 = jnp.exp(v2 - v1)
    w1 = grp_w / (1.0 + e2)
    w2 = grp_w * e2 / (1.0 + e2)

    first = (pl.program_id(0) == 0) & (pl.program_id(1) == 0)

    @pl.when(first)
    def _():
        counts_ref[...] = jnp.zeros_like(counts_ref)
        r = lax.broadcasted_iota(jnp.int32, (tm, tm), 0)
        c = lax.broadcasted_iota(jnp.int32, (tm, tm), 1)
        tri_ref[...] = (c < r).astype(BF16)

    oh0 = lane_f == i1 - ROUTE_LANE0
    oh1 = lane_f == i2 - ROUTE_LANE0
    oh = oh0.astype(F32) + oh1.astype(F32)
    before = counts_ref[...] + _dot(tri_ref[...], oh.astype(BF16))
    r0 = jnp.sum(jnp.where(oh0, before, 0.0), axis=-1, keepdims=True)
    r1 = jnp.sum(jnp.where(oh1, before + oh0.astype(F32), 0.0), axis=-1, keepdims=True)
    counts_ref[...] = counts_ref[...] + jnp.sum(oh, axis=0, keepdims=True)

    per_lane = (i1 - ROUTE_LANE0, i2 - ROUTE_LANE0, w1, w2, r0, r1)
    route = jnp.zeros((tm, LANES), F32)
    for k, v in enumerate(per_lane):
        route = jnp.where(lane == k, v, route)
    route_ref[...] = route


def _mix(x, o_list, lse_list, cg, ga, ln_g, ln_b, w_attn_out, w_o, ln1_g, ln1_b, wr_hi, wr_lo, b_route):
    bsz, s, d = x.shape
    tm = ROW_TILE
    row = lambda b, i: (b, i, 0)
    dil_specs = [pl.BlockSpec((None, r, tm // r, GROUP_W), lambda b, i: (b, 0, i, 0)) for r in DILATIONS]
    o_views = [o.reshape(bsz, r, s // r, GROUP_W) for o, r in zip(o_list, DILATIONS)]
    l_views = [l.reshape(bsz, r, s // r, GROUP_W) for l, r in zip(lse_list, DILATIONS)]
    return pl.pallas_call(
        _mix_kernel,
        grid=(bsz, s // tm),
        in_specs=[pl.BlockSpec((None, tm, d), row)] + dil_specs + dil_specs + [
            pl.BlockSpec((None, tm, d), row), pl.BlockSpec((None, tm, d), row),
            _resident((1, d)), _resident((1, d)),
            _resident(w_attn_out.shape), _resident(w_o.shape),
            _resident((1, d)), _resident((1, d)),
            _resident(wr_hi.shape), _resident(wr_lo.shape), _resident((1, LANES)),
        ],
        out_specs=[pl.BlockSpec((tm * ROW_CHUNKS, LANES), lambda b, i: (b * (s // tm) + i, 0)),
                   pl.BlockSpec((None, tm, LANES), row),
                   pl.BlockSpec((1, LANES), lambda b, i: (0, 0))],
        out_shape=[jax.ShapeDtypeStruct((bsz * s * ROW_CHUNKS, LANES), F32),
                   jax.ShapeDtypeStruct((bsz, s, LANES), F32),
                   jax.ShapeDtypeStruct((1, LANES), F32)],
        scratch_shapes=[pltpu.VMEM((N_GROUPS * GROUP_W // LANES, tm, LANES), F32)] * 2
        + [pltpu.VMEM((tm, tm), BF16)],
        compiler_params=pltpu.CompilerParams(
            dimension_semantics=("arbitrary", "arbitrary"), vmem_limit_bytes=VMEM_LIMIT),
        name="mix",
    )(x, *o_views, *l_views, cg, ga, ln_g, ln_b, w_attn_out, w_o, ln1_g, ln1_b, wr_hi, wr_lo, b_route)


def _dispatch_plan(route, counts, t):
    eid = route[:, :TOP_K_INNER].astype(jnp.int32)
    rank = route[:, ROUTE_LANE_RANK:ROUTE_LANE_RANK + TOP_K_INNER]
    counts = counts[0, :N_EXPERTS].astype(jnp.int32)
    padded = (counts + EXPERT_BLOCK - 1) // EXPERT_BLOCK * EXPERT_BLOCK
    pad_end = jnp.cumsum(padded)
    pad_start = pad_end - padded
    experts = jnp.arange(N_EXPERTS, dtype=jnp.int32)
    start_of = jnp.sum(jnp.where(eid[..., None] == experts, pad_start, 0), axis=-1)
    dest = start_of + rank.astype(jnp.int32)
    n_blocks = -(-t * TOP_K_INNER // EXPERT_BLOCK) + N_EXPERTS
    n_slots = n_blocks * EXPERT_BLOCK
    block_start = jnp.arange(n_blocks, dtype=jnp.int32) * EXPERT_BLOCK
    block_e = jnp.minimum(jnp.sum((block_start[:, None] >= pad_end[None, :]).astype(jnp.int32), axis=1),
                          N_EXPERTS - 1)
    n_empty = n_slots - t * TOP_K_INNER
    gap_end = jnp.cumsum(padded - counts)
    gap_start = gap_end - (padded - counts)
    k = jnp.arange(n_empty, dtype=jnp.int32)
    in_gap = (k[:, None] >= gap_start) & (k[:, None] < gap_end)
    empty = jnp.where(k < gap_end[-1],
                      jnp.sum(jnp.where(in_gap, pad_start + counts + k[:, None] - gap_start, 0), axis=-1),
                      pad_end[-1] + k - gap_end[-1])
    return (dest * ROW_CHUNKS).astype(jnp.int32), (empty * ROW_CHUNKS).astype(jnp.int32), block_e


def _token_copy(src_ref, src_row, dst_ref, dst_row, sem):
    return pltpu.make_async_copy(src_ref.at[pl.ds(pl.multiple_of(src_row, ROW_CHUNKS), ROW_CHUNKS), :],
                                 dst_ref.at[pl.ds(pl.multiple_of(dst_row, ROW_CHUNKS), ROW_CHUNKS), :], sem)


def _wait_rows(hbm_ref, n_rows, sem):
    rows = hbm_ref.at[pl.ds(0, n_rows), :]
    pltpu.make_async_copy(rows, rows, sem).wait()


def _dispatch_kernel(dest_ref, empty_ref, h1_ref, xs_hbm, zero_ref, sem):
    i = pl.program_id(0)
    tm = h1_ref.shape[0] // ROW_CHUNKS
    n_empty = empty_ref.shape[1]

    @pl.when(i == 0)
    def _():
        zero_ref[...] = jnp.zeros_like(zero_ref)

    def body(c, carry):
        for u in range(ISSUE_UNROLL):
            j = c * ISSUE_UNROLL + u
            for choice in range(TOP_K_INNER):
                _token_copy(h1_ref, j * ROW_CHUNKS, xs_hbm, dest_ref[0, choice * tm + j], sem.at[0]).start()
        return carry
    lax.fori_loop(0, tm // ISSUE_UNROLL, body, 0)

    def zero_body(c, carry):
        for u in range(ISSUE_UNROLL):
            _token_copy(zero_ref, 0, xs_hbm, empty_ref[0, c * ISSUE_UNROLL + u], sem.at[1]).start()
        return carry
    lax.fori_loop(0, n_empty // ISSUE_UNROLL, zero_body, 0)

    _wait_rows(xs_hbm, TOP_K_INNER * tm * ROW_CHUNKS, sem.at[0])
    _wait_rows(xs_hbm, n_empty * ROW_CHUNKS, sem.at[1])


def _dispatch(h1, dest_row, empty_row, n_slots):
    t = h1.shape[0] // ROW_CHUNKS
    tm = ROW_TILE
    nt = t // tm
    n_empty = empty_row.shape[0] // nt
    assert n_empty * nt == empty_row.shape[0] and n_empty % ISSUE_UNROLL == 0
    idx3 = dest_row.reshape(nt, tm, TOP_K_INNER).transpose(0, 2, 1).reshape(nt, 1, TOP_K_INNER * tm)
    return pl.pallas_call(
        _dispatch_kernel,
        grid=(nt,),
        in_specs=[
            pl.BlockSpec((None, 1, TOP_K_INNER * tm), lambda i: (i, 0, 0), memory_space=pltpu.SMEM),
            pl.BlockSpec((None, 1, n_empty), lambda i: (i, 0, 0), memory_space=pltpu.SMEM),
            pl.BlockSpec((tm * ROW_CHUNKS, LANES), lambda i: (i, 0)),
        ],
        out_specs=pl.BlockSpec(memory_space=pl.ANY),
        out_shape=jax.ShapeDtypeStruct((n_slots * ROW_CHUNKS, LANES), F32),
        scratch_shapes=[pltpu.VMEM((ROW_CHUNKS, LANES), F32), pltpu.SemaphoreType.DMA((2,))],
        compiler_params=pltpu.CompilerParams(dimension_semantics=("arbitrary",), vmem_limit_bytes=VMEM_LIMIT),
        name="dispatch",
    )(idx3, empty_row.reshape(nt, 1, n_empty), h1)


def _expert_mlp(xb, wg_ref, wu_ref, wd_ref):
    acc = None
    for c in range(D_EXPERT // EXPERT_CHUNK):
        cols = slice(c * EXPERT_CHUNK, (c + 1) * EXPERT_CHUNK)
        gate = _dot(xb, wg_ref[:, cols])
        up = _dot(xb, wu_ref[:, cols])
        hidden = (gate * jax.nn.sigmoid(gate) * up).astype(BF16)
        part = _dot(hidden, wd_ref[cols, :])
        acc = part if acc is None else acc + part
    return acc


def _expert_kernel(be_ref, x_ref, wga_ref, wua_ref, wda_ref, wgb_ref, wub_ref, wdb_ref, y_ref):
    i = pl.program_id(0)
    same = be_ref[2 * i] == be_ref[2 * i + 1]

    @pl.when(same)
    def _():
        xb = _load_token_major(x_ref, 0, 2 * EXPERT_BLOCK).astype(BF16)
        _store_token_major(y_ref, 0, _expert_mlp(xb, wga_ref, wua_ref, wda_ref))

    @pl.when(jnp.logical_not(same))
    def _():
        for half, w in enumerate(((wga_ref, wua_ref, wda_ref), (wgb_ref, wub_ref, wdb_ref))):
            xb = _load_token_major(x_ref, half * EXPERT_BLOCK, EXPERT_BLOCK).astype(BF16)
            _store_token_major(y_ref, half * EXPERT_BLOCK, _expert_mlp(xb, *w))


def _experts(xs, block_e, w_gate, w_up, w_down):
    n_blocks = block_e.shape[0]
    assert n_blocks % 2 == 0
    d = D_MODEL
    blk = pl.BlockSpec((2 * EXPERT_BLOCK * ROW_CHUNKS, LANES), lambda i, be: (i, 0))
    w_specs = [pl.BlockSpec(shape, (lambda i, be, k=k: (be[2 * i + k], 0, 0)))
               for k in range(2) for shape in ((None, d, D_EXPERT), (None, d, D_EXPERT), (None, D_EXPERT, d))]
    grid_spec = pltpu.PrefetchScalarGridSpec(
        num_scalar_prefetch=1,
        grid=(n_blocks // 2,),
        in_specs=[blk] + w_specs,
        out_specs=blk,
    )
    return pl.pallas_call(
        _expert_kernel,
        grid_spec=grid_spec,
        out_shape=jax.ShapeDtypeStruct(xs.shape, F32),
        compiler_params=pltpu.CompilerParams(
            dimension_semantics=("arbitrary",), vmem_limit_bytes=VMEM_LIMIT),
        name="experts",
    )(block_e, xs, w_gate, w_up, w_down, w_gate, w_up, w_down)


def _combine_kernel(idx_ref, idxn_ref, yb_hbm, h1_ref, route_ref, g_ref, b_ref, y_ref, gbuf, sem):
    i = pl.program_id(0)
    n = pl.num_programs(0)
    tm = y_ref.shape[0]
    slot = i % 2

    def gather(src_ref, s):
        def body(c, carry):
            for u in range(ISSUE_UNROLL):
                j = c * ISSUE_UNROLL + u
                _token_copy(yb_hbm, src_ref[0, j], gbuf.at[s], j * ROW_CHUNKS, sem.at[s]).start()
            return carry
        lax.fori_loop(0, TOP_K_INNER * tm // ISSUE_UNROLL, body, 0)

    @pl.when(i == 0)
    def _():
        gather(idx_ref, 0)

    @pl.when(i + 1 < n)
    def _():
        gather(idxn_ref, 1 - slot)

    cur = gbuf.at[slot]
    _wait_rows(yb_hbm, TOP_K_INNER * tm * ROW_CHUNKS, sem.at[slot])
    route = route_ref[...]
    w_first = route[:, ROUTE_LANE_W:ROUTE_LANE_W + 1]
    w_second = route[:, ROUTE_LANE_W + 1:ROUTE_LANE_W + 2]
    ffn = w_first * _load_token_major(cur, 0, tm) + w_second * _load_token_major(cur, tm, tm)
    h1 = _load_token_major(h1_ref, 0, tm)
    y_ref[...] = _layer_norm(DEEPNORM_ALPHA * h1 + ffn, g_ref[...], b_ref[...])


def _combine(h1, route, yb, dest_row, ln2_g, ln2_b):
    t = route.shape[0]
    d = D_MODEL
    tm = COMBINE_TILE
    nt = t // tm
    idx3 = dest_row.reshape(nt, tm, TOP_K_INNER).transpose(0, 2, 1).reshape(nt, 1, TOP_K_INNER * tm)
    smem_blk = lambda f: pl.BlockSpec((None, 1, TOP_K_INNER * tm), f, memory_space=pltpu.SMEM)
    return pl.pallas_call(
        _combine_kernel,
        grid=(nt,),
        in_specs=[
            smem_blk(lambda i: (i, 0, 0)),
            smem_blk(lambda i: (jnp.minimum(i + 1, nt - 1), 0, 0)),
            pl.BlockSpec(memory_space=pl.ANY),
            pl.BlockSpec((tm * ROW_CHUNKS, LANES), lambda i: (i, 0)),
            pl.BlockSpec((tm, LANES), lambda i: (i, 0)),
            _resident((1, d)), _resident((1, d)),
        ],
        out_specs=pl.BlockSpec((tm, d), lambda i: (i, 0)),
        out_shape=jax.ShapeDtypeStruct((t, d), F32),
        scratch_shapes=[pltpu.VMEM((2, TOP_K_INNER * tm * ROW_CHUNKS, LANES), F32), pltpu.SemaphoreType.DMA((2,))],
        compiler_params=pltpu.CompilerParams(
            dimension_semantics=("arbitrary",), vmem_limit_bytes=VMEM_LIMIT),
        name="combine",
    )(idx3, idx3, yb, h1, route, ln2_g, ln2_b)


def _trunk(x, p):
    bsz, s, d = x.shape
    t = bsz * s
    qkv0, qkv1, qkv2, cg, ga = _inproj(x, p['ln_in_g'], p['ln_in_b'], p['w_in'], p['b_gate'],
                                       p['conv_w'], p['w_conv_out'])
    o_list, lse_list = [], []
    for qkv, r in zip((qkv0, qkv1, qkv2), DILATIONS):
        o, lse = _attention(qkv.reshape(3, bsz, s, GROUP_W), s // r)
        o_list.append(o)
        lse_list.append(lse)
    h1, route, counts = _mix(x, o_list, lse_list, cg, ga, p['ln_in_g'], p['ln_in_b'], p['w_attn_out'], p['w_o'],
                     p['ln1_g'], p['ln1_b'], p['wr_hi'], p['wr_lo'], p['b_route'])
    route = route.reshape(t, LANES)
    dest_row, empty_row, block_e = _dispatch_plan(route, counts, t)
    xs = _dispatch(h1, dest_row, empty_row, block_e.shape[0] * EXPERT_BLOCK)
    yb = _experts(xs, block_e, p['w_gate'], p['w_up'], p['w_down'])
    y = _combine(h1, route, yb, dest_row, p['ln2_g'], p['ln2_b'])
    return y.reshape(bsz, s, d)


def kernel(x_prompt, x_sample, ln_in_g, ln_in_b, w_in, b_gate, conv_w, w_attn_out, w_conv_out, w_o, ln1_g, ln1_b, w_route_group, b_route_group, w_route_expert, b_route_expert, w_gate, w_up, w_down, ln2_g, ln2_b):
    d = D_MODEL
    w_route = jnp.concatenate([w_route_group[0], w_route_expert[0]], axis=1)
    w_route = jnp.pad(w_route, ((0, 0), (0, LANES - w_route.shape[1])))
    wr_hi = w_route.astype(BF16)
    b_route = jnp.concatenate([b_route_group[0], b_route_expert[0]]).astype(F32)
    p = {
        'ln_in_g': ln_in_g.reshape(1, d), 'ln_in_b': ln_in_b.reshape(1, d),
        'w_in': w_in[0].astype(BF16), 'b_gate': b_gate[0].reshape(1, 2 * d), 'conv_w': conv_w[0],
        'w_conv_out': w_conv_out[0].astype(BF16), 'w_attn_out': w_attn_out[0].astype(BF16),
        'w_o': w_o[0].astype(BF16),
        'ln1_g': ln1_g[0].reshape(1, d), 'ln1_b': ln1_b[0].reshape(1, d),
        'wr_hi': wr_hi, 'wr_lo': (w_route - wr_hi.astype(F32)).astype(BF16),
        'b_route': jnp.pad(b_route, (0, LANES - b_route.shape[0])).reshape(1, LANES),
        'w_gate': w_gate[0].astype(BF16), 'w_up': w_up[0].astype(BF16), 'w_down': w_down[0].astype(BF16),
        'ln2_g': ln2_g[0].reshape(1, d), 'ln2_b': ln2_b[0].reshape(1, d),
    }
    return (_trunk(x_prompt, p), _trunk(x_sample, p))
```

```python
import functools

import jax
import jax.numpy as jnp
import numpy as np
from jax import lax
from jax.experimental import pallas as pl
from jax.experimental.pallas import tpu as pltpu

F32 = jnp.float32
BF16 = jnp.bfloat16

D_MODEL = 1024
HEAD_DIM = 64
HEADS_PER_GROUP = 4
GROUP_W = HEADS_PER_GROUP * HEAD_DIM
DILATIONS = (1, 4, 16)
BAND = 64
N_GROUPS = len(DILATIONS)
D_ATTN = N_GROUPS * GROUP_W
ATTN_SCALE = HEAD_DIM ** -0.5
ROPE_DIM = HEAD_DIM // 4
ROPE_THETA = 500000.0
MASK_VALUE = -1e30
D_CONV = D_MODEL
COL_CONV_B = 3 * D_ATTN
COL_CONV_C = COL_CONV_B + D_CONV
COL_CONV_H = COL_CONV_C + D_CONV
COL_GATE_ATTN = COL_CONV_H + D_CONV
COL_GATE_CONV = COL_GATE_ATTN + D_MODEL
N_EXPERT_GROUPS = 4
EXPERTS_PER_GROUP = 8
N_EXPERTS = N_EXPERT_GROUPS * EXPERTS_PER_GROUP
TOP_K_INNER = 2
D_EXPERT = 512
EXPERT_BLOCK = 256
EXPERT_CHUNK = 256
LN_EPS = 1e-5
DEPTH = 1
DEEPNORM_ALPHA = (2 * DEPTH) ** 0.25

LANES = 128
ROW_CHUNKS = D_MODEL // LANES
ROW_TILE = 512
HALO = 16
Q_BLOCK = 128
K_BLOCK = Q_BLOCK + 2 * BAND
Q_CHUNK = 2048
COMBINE_TILE = 512
ISSUE_UNROLL = 16
ROUTE_LANE0 = N_EXPERT_GROUPS
ROUTE_LANE_W = TOP_K_INNER
ROUTE_LANE_RANK = 2 * TOP_K_INNER
VMEM_LIMIT = 56 * 1024 * 1024


def _layer_norm(v, g, b):
    mu = jnp.mean(v, axis=-1, keepdims=True)
    d = v - mu
    var = jnp.mean(d * d, axis=-1, keepdims=True)
    return d * lax.rsqrt(var + LN_EPS) * g + b


def _dot(a, b):
    return jnp.dot(a, b, preferred_element_type=F32)


def _store_token_major(ref, first_token, val):
    n = val.shape[0]
    for k in range(ROW_CHUNKS):
        ref[pl.ds(first_token * ROW_CHUNKS + k, n, stride=ROW_CHUNKS), :] = val[:, k * LANES:(k + 1) * LANES]


def _load_token_major(ref, first_token, n):
    return jnp.concatenate(
        [ref[pl.ds(first_token * ROW_CHUNKS + k, n, stride=ROW_CHUNKS), :] for k in range(ROW_CHUNKS)], axis=1)


def _resident(shape):
    nd = len(shape)
    return pl.BlockSpec(shape, lambda *_: (0,) * nd, pipeline_mode=pl.Buffered(1))


def _inproj_kernel(*refs, with_dispatch):
    (x_ref, xp_ref, xn_ref, g_ref, b_ref, win_ref, bg_ref, cw_ref, wco_ref, cos_ref, sa_ref, sb_ref), refs = (
        refs[:12], refs[12:])
    if with_dispatch:
        (dest_ref, empty_ref, h1_ref), refs = refs[:3], refs[3:]
    (qkv0_ref, qkv1_ref, qkv2_ref, cg_ref, ga_ref), refs = refs[:5], refs[5:]
    if with_dispatch:
        xs_hbm, hext_ref, u_ref, stage_ref, zero_ref, sem = refs
        first = (pl.program_id(0) == 0) & (pl.program_id(1) == 0)
        _dispatch_start(dest_ref, empty_ref, h1_ref, xs_hbm, zero_ref, sem, first)
    else:
        hext_ref, u_ref, stage_ref = refs
    i = pl.program_id(1)
    nt = pl.num_programs(1)
    tm = x_ref.shape[0]
    g = g_ref[...]
    b = b_ref[...]
    hext_ref[pl.ds(HALO, tm), :] = _layer_norm(x_ref[...], g, b).astype(BF16)
    hext_ref[pl.ds(0, HALO), :] = _layer_norm(xp_ref[...], g, b).astype(BF16)
    hext_ref[pl.ds(HALO + tm, HALO), :] = _layer_norm(xn_ref[...], g, b).astype(BF16)
    hm = hext_ref[pl.ds(HALO, tm), :]
    he = hext_ref[...]

    def proj(lhs, col, n):
        return _dot(lhs, win_ref[:, col:col + n])

    u = proj(he, COL_CONV_C, D_CONV) * proj(he, COL_CONV_H, D_CONV)
    rows = lax.broadcasted_iota(jnp.int32, (tm + 2 * HALO, 1), 0)
    lo = jnp.where(i == 0, HALO, 0)
    hi = jnp.where(i == nt - 1, HALO + tm, tm + 2 * HALO)
    u_ref[...] = jnp.where((rows >= lo) & (rows < hi), u, 0.0)
    cw = cw_ref[...]
    conv = (cw[0:1] * u_ref[pl.ds(HALO - 1, tm), :] + cw[1:2] * u_ref[pl.ds(HALO, tm), :]
            + cw[2:3] * u_ref[pl.ds(HALO + 1, tm), :])
    cb = proj(hm, COL_CONV_B, D_CONV)
    conv_branch = _dot((cb * conv).astype(BF16), wco_ref[...])
    gate_conv = jax.nn.sigmoid(proj(hm, COL_GATE_CONV, D_MODEL) + bg_ref[:, D_MODEL:])
    cg_ref[...] = (gate_conv * conv_branch).astype(BF16)
    ga_ref[...] = jax.nn.sigmoid(proj(hm, COL_GATE_ATTN, D_MODEL) + bg_ref[:, :D_MODEL]).astype(BF16)

    cosv = cos_ref[...]
    sav = sa_ref[...]
    sbv = sb_ref[...]
    half = ROPE_DIM // 2
    for gi, (out_ref, r) in enumerate(zip((qkv0_ref, qkv1_ref, qkv2_ref), DILATIONS)):
        for t in range(3):
            p = proj(hm, t * D_ATTN + gi * GROUP_W, GROUP_W)
            if t < 2:
                parts = []
                for c in range(GROUP_W // LANES):
                    pc = p[:, c * LANES:(c + 1) * LANES]
                    parts.append(pc * cosv + pltpu.roll(pc, LANES - half, 1) * sav
                                 + pltpu.roll(pc, half, 1) * sbv)
                p = jnp.concatenate(parts, axis=1)
                if t == 0:
                    p = p * ATTN_SCALE
            if r == 1:
                out_ref[t, 0] = p.astype(BF16)
            else:
                for c in range(GROUP_W // LANES):
                    stage_ref[c] = p[:, c * LANES:(c + 1) * LANES]
                for res in range(r):
                    for c in range(GROUP_W // LANES):
                        out_ref[t, res, :, c * LANES:(c + 1) * LANES] = (
                            stage_ref[c, pl.ds(res, tm // r, stride=r), :].astype(BF16))
    if with_dispatch:
        _dispatch_wait(empty_ref, h1_ref, xs_hbm, sem)


def _rope_tables(s):
    half = ROPE_DIM // 2
    inv_freq = (np.float32(ROPE_THETA) ** (-np.arange(half, dtype=np.float32) * np.float32(2.0) / ROPE_DIM))
    ang = np.arange(s, dtype=np.float32)[:, None] * inv_freq.astype(np.float32)[None, :]
    cos = np.cos(ang).astype(np.float32)
    sin = np.sin(ang).astype(np.float32)
    pad = np.zeros((s, HEAD_DIM - ROPE_DIM), np.float32)
    zero = np.zeros((s, half), np.float32)
    cos_t = np.concatenate([cos, cos, pad + 1.0], axis=1)
    sa_t = np.concatenate([-sin, zero, pad], axis=1)
    sb_t = np.concatenate([zero, sin, pad], axis=1)
    rep = LANES // HEAD_DIM
    return tuple(jnp.asarray(np.tile(t, (1, rep))) for t in (cos_t, sa_t, sb_t))


def _inproj(x, ln_g, ln_b, w_in, b_gate, conv_w, w_conv_out, dispatch_job=None):
    bsz, s, d = x.shape
    tm = ROW_TILE
    nt = s // tm
    job_specs, job_args, job_out_specs, job_out_shapes, job_scratch = [], (), [], [], []
    if dispatch_job is not None:
        h1, dest_row, empty_row, n_slots = dispatch_job
        job_specs, job_args = _dispatch_operands(h1, dest_row, empty_row, bsz * nt, lambda b, i: b * nt + i)
        job_out_specs = [pl.BlockSpec(memory_space=pl.ANY)]
        job_out_shapes = [jax.ShapeDtypeStruct((n_slots * ROW_CHUNKS, LANES), F32)]
        job_scratch = _dispatch_scratch()
    hpt = tm // HALO
    cos_t, sa_t, sb_t = _rope_tables(s)
    row = lambda b, i: (b, i, 0)
    tab = pl.BlockSpec((tm, LANES), lambda b, i: (i, 0))
    qkv_shapes = [jax.ShapeDtypeStruct((3, bsz, r, s // r, GROUP_W), BF16) for r in DILATIONS]
    qkv_specs = [pl.BlockSpec((3, None, r, tm // r, GROUP_W), lambda b, i: (0, b, 0, i, 0))
                 for r in DILATIONS]
    return pl.pallas_call(
        functools.partial(_inproj_kernel, with_dispatch=dispatch_job is not None),
        grid=(bsz, nt),
        in_specs=[
            pl.BlockSpec((None, tm, d), row),
            pl.BlockSpec((None, HALO, d), lambda b, i: (b, jnp.maximum(i * hpt - 1, 0), 0)),
            pl.BlockSpec((None, HALO, d), lambda b, i: (b, jnp.minimum((i + 1) * hpt, s // HALO - 1), 0)),
            _resident((1, d)), _resident((1, d)),
            _resident(w_in.shape), _resident((1, 2 * d)), _resident(conv_w.shape),
            _resident(w_conv_out.shape),
            tab, tab, tab,
        ] + job_specs,
        out_specs=(qkv_specs + [pl.BlockSpec((None, tm, d), row), pl.BlockSpec((None, tm, d), row)]
                   + job_out_specs),
        out_shape=qkv_shapes + [jax.ShapeDtypeStruct((bsz, s, d), BF16)] * 2 + job_out_shapes,
        scratch_shapes=[
            pltpu.VMEM((tm + 2 * HALO, d), BF16),
            pltpu.VMEM((tm + 2 * HALO, D_CONV), F32),
            pltpu.VMEM((GROUP_W // LANES, tm, LANES), F32),
        ] + job_scratch,
        compiler_params=pltpu.CompilerParams(
            dimension_semantics=("arbitrary", "arbitrary"), vmem_limit_bytes=VMEM_LIMIT),
        name="inproj",
    )(x, x, x, ln_g, ln_b, w_in, b_gate, conv_w, w_conv_out, cos_t, sa_t, sb_t, *job_args)


def _attn_kernel(q_ref, k_ref, v_ref, o_ref, lse_ref, *, sub_len):
    s = k_ref.shape[0]
    qc = q_ref.shape[0]
    base = pl.program_id(1) * qc
    lane = lax.broadcasted_iota(jnp.int32, (1, GROUP_W), 1)
    head_masks = [(lane >= h * HEAD_DIM) & (lane < (h + 1) * HEAD_DIM) for h in range(HEADS_PER_GROUP)]
    qi = lax.broadcasted_iota(jnp.int32, (Q_BLOCK, 1), 0)
    kj = lax.broadcasted_iota(jnp.int32, (1, K_BLOCK), 1)

    def body(j, carry):
        r0 = pl.multiple_of(j * Q_BLOCK, Q_BLOCK)
        s0 = base + r0
        k0 = pl.multiple_of(jnp.clip(s0 - BAND, 0, s - K_BLOCK), BAND)
        q = q_ref[pl.ds(r0, Q_BLOCK), :]
        k = k_ref[pl.ds(k0, K_BLOCK), :]
        v = v_ref[pl.ds(k0, K_BLOCK), :]
        sub_lo = (s0 // sub_len) * sub_len
        qpos = s0 + qi
        kpos = k0 + kj
        valid = (jnp.abs(qpos - kpos) <= BAND) & (kpos >= sub_lo) & (kpos < sub_lo + sub_len)
        qm = jnp.concatenate([jnp.where(hm, q, jnp.zeros_like(q)) for hm in head_masks], axis=0)
        sc = lax.dot_general(qm, k, (((1,), (1,)), ((), ())), preferred_element_type=F32)
        sc = jnp.where(jnp.concatenate([valid] * HEADS_PER_GROUP, axis=0), sc, MASK_VALUE)
        m = jnp.max(sc, axis=-1, keepdims=True)
        p = jnp.exp(sc - m)
        den = jnp.sum(p, axis=-1, keepdims=True)
        o_all = _dot((p * (1.0 / den)).astype(BF16), v)
        lse_all = m + jnp.log(den)
        o_acc = jnp.zeros((Q_BLOCK, GROUP_W), F32)
        lse_acc = jnp.zeros((Q_BLOCK, GROUP_W), F32)
        for h, hm in enumerate(head_masks):
            rows = slice(h * Q_BLOCK, (h + 1) * Q_BLOCK)
            o_acc = jnp.where(hm, o_all[rows], o_acc)
            lse_acc = jnp.where(hm, lse_all[rows], lse_acc)
        o_ref[pl.ds(r0, Q_BLOCK), :] = o_acc.astype(BF16)
        lse_ref[pl.ds(r0, Q_BLOCK), :] = lse_acc
        return carry

    lax.fori_loop(0, qc // Q_BLOCK, body, 0, unroll=8)


def _attention(qkv, sub_len):
    _, bsz, s, w = qkv.shape
    qc = min(Q_CHUNK, s)
    kv_spec = lambda t: pl.BlockSpec((None, None, s, w), lambda b, j: (t, b, 0, 0))
    return pl.pallas_call(
        functools.partial(_attn_kernel, sub_len=sub_len),
        grid=(bsz, s // qc),
        in_specs=[pl.BlockSpec((None, None, qc, w), lambda b, j: (0, b, j, 0)), kv_spec(1), kv_spec(2)],
        out_specs=[pl.BlockSpec((None, qc, w), lambda b, j: (b, j, 0))] * 2,
        out_shape=[jax.ShapeDtypeStruct((bsz, s, w), BF16), jax.ShapeDtypeStruct((bsz, s, w), F32)],
        compiler_params=pltpu.CompilerParams(
            dimension_semantics=("parallel", "arbitrary"), vmem_limit_bytes=VMEM_LIMIT),
        name="attn",
    )(qkv, qkv, qkv)


def _first_index_of_max(vals, lane_f):
    mx = jnp.max(vals, axis=-1, keepdims=True)
    idx = jnp.min(jnp.where(vals == mx, lane_f, float(LANES)), axis=-1, keepdims=True)
    return mx, idx


def _mix_kernel(x_ref, o0_ref, o1_ref, o2_ref, l0_ref, l1_ref, l2_ref, cg_ref, ga_ref,
                g_ref, b_ref, wao_ref, wo_ref, g1_ref, b1_ref, wrh_ref, wrl_ref, br_ref,
                h1_ref, route_ref, counts_ref, so_ref, sl_ref, tri_ref):
    tm = x_ref.shape[0]
    h = _layer_norm(x_ref[...], g_ref[...], b_ref[...])
    halves = GROUP_W // LANES
    for gi, (o_ref, l_ref, r) in enumerate(zip((o0_ref, o1_ref, o2_ref), (l0_ref, l1_ref, l2_ref), DILATIONS)):
        for c in range(halves):
            cols = slice(c * LANES, (c + 1) * LANES)
            for res in range(r):
                rows = pl.ds(res, tm // r, stride=r) if r > 1 else slice(None)
                so_ref[gi * halves + c, rows, :] = o_ref[res, :, cols].astype(F32)
                sl_ref[gi * halves + c, rows, :] = l_ref[res, :, cols]

    def natural(ref, gi):
        return jnp.concatenate([ref[gi * halves + c] for c in range(halves)], axis=1)

    lses = [natural(sl_ref, gi) for gi in range(N_GROUPS)]
    mx = jnp.maximum(jnp.maximum(lses[0], lses[1]), lses[2])
    es = [jnp.exp(l - mx) for l in lses]
    inv_den = 1.0 / (es[0] + es[1] + es[2])
    attn = jnp.concatenate(
        [(natural(so_ref, gi) * (es[gi] * inv_den)).astype(BF16) for gi in range(N_GROUPS)], axis=1)
    attn_branch = _dot(attn, wao_ref[...])
    merged = ga_ref[...].astype(F32) * attn_branch + cg_ref[...].astype(F32)
    mix = _dot(merged.astype(BF16), wo_ref[...])
    h1 = _layer_norm(DEEPNORM_ALPHA * h + mix, g1_ref[...], b1_ref[...])
    _store_token_major(h1_ref, 0, h1)

    hi = h1.astype(BF16)
    lo = (h1 - hi.astype(F32)).astype(BF16)
    wrh = wrh_ref[...]
    logits = _dot(hi, wrh) + _dot(lo, wrh) + _dot(hi, wrl_ref[...]) + br_ref[...]
    lane = lax.broadcasted_iota(jnp.int32, (1, LANES), 1)
    lane_f = lane.astype(F32)
    is_grp = lane < N_EXPERT_GROUPS
    gl = jnp.where(is_grp, logits, MASK_VALUE)
    ge = jnp.exp(gl - jnp.max(gl, axis=-1, keepdims=True))
    gp = jnp.where(is_grp, ge / jnp.sum(ge, axis=-1, keepdims=True), -1.0)
    grp_w, grp_idx = _first_index_of_max(gp, lane_f)
    e_lo = ROUTE_LANE0 + grp_idx * EXPERTS_PER_GROUP
    sel = jnp.where((lane_f >= e_lo) & (lane_f < e_lo + EXPERTS_PER_GROUP), logits, MASK_VALUE)
    v1, i1 = _first_index_of_max(sel, lane_f)
    v2, i2 = _first_index_of_max(jnp.where(lane_f == i1, MASK_VALUE, sel), lane_f)
    e2 = jnp.exp(v2 - v1)
    w1 = grp_w / (1.0 + e2)
    w2 = grp_w * e2 / (1.0 + e2)

    first = (pl.program_id(0) == 0) & (pl.program_id(1) == 0)

    @pl.when(first)
    def _():
        counts_ref[...] = jnp.zeros_like(counts_ref)
        r = lax.broadcasted_iota(jnp.int32, (tm, tm), 0)
        c = lax.broadcasted_iota(jnp.int32, (tm, tm), 1)
        tri_ref[...] = (c < r).astype(BF16)

    oh0 = lane_f == i1 - ROUTE_LANE0
    oh1 = lane_f == i2 - ROUTE_LANE0
    oh = oh0.astype(F32) + oh1.astype(F32)
    before = counts_ref[...] + _dot(tri_ref[...], oh.astype(BF16))
    r0 = jnp.sum(jnp.where(oh0, before, 0.0), axis=-1, keepdims=True)
    r1 = jnp.sum(jnp.where(oh1, before + oh0.astype(F32), 0.0), axis=-1, keepdims=True)
    counts_ref[...] = counts_ref[...] + jnp.sum(oh, axis=0, keepdims=True)

    per_lane = (i1 - ROUTE_LANE0, i2 - ROUTE_LANE0, w1, w2, r0, r1)
    route = jnp.zeros((tm, LANES), F32)
    for k, v in enumerate(per_lane):
        route = jnp.where(lane == k, v, route)
    route_ref[...] = route


def _mix(x, o_list, lse_list, cg, ga, ln_g, ln_b, w_attn_out, w_o, ln1_g, ln1_b, wr_hi, wr_lo, b_route):
    bsz, s, d = x.shape
    tm = ROW_TILE
    row = lambda b, i: (b, i, 0)
    dil_specs = [pl.BlockSpec((None, r, tm // r, GROUP_W), lambda b, i: (b, 0, i, 0)) for r in DILATIONS]
    o_views = [o.reshape(bsz, r, s // r, GROUP_W) for o, r in zip(o_list, DILATIONS)]
    l_views = [l.reshape(bsz, r, s // r, GROUP_W) for l, r in zip(lse_list, DILATIONS)]
    return pl.pallas_call(
        _mix_kernel,
        grid=(bsz, s // tm),
        in_specs=[pl.BlockSpec((None, tm, d), row)] + dil_specs + dil_specs + [
            pl.BlockSpec((None, tm, d), row), pl.BlockSpec((None, tm, d), row),
            _resident((1, d)), _resident((1, d)),
            _resident(w_attn_out.shape), _resident(w_o.shape),
            _resident((1, d)), _resident((1, d)),
            _resident(wr_hi.shape), _resident(wr_lo.shape), _resident((1, LANES)),
        ],
        out_specs=[pl.BlockSpec((tm * ROW_CHUNKS, LANES), lambda b, i: (b * (s // tm) + i, 0)),
                   pl.BlockSpec((None, tm, LANES), row),
                   pl.BlockSpec((1, LANES), lambda b, i: (0, 0))],
        out_shape=[jax.ShapeDtypeStruct((bsz * s * ROW_CHUNKS, LANES), F32),
                   jax.ShapeDtypeStruct((bsz, s, LANES), F32),
                   jax.ShapeDtypeStruct((1, LANES), F32)],
        scratch_shapes=[pltpu.VMEM((N_GROUPS * GROUP_W // LANES, tm, LANES), F32)] * 2
        + [pltpu.VMEM((tm, tm), BF16)],
        compiler_params=pltpu.CompilerParams(
            dimension_semantics=("arbitrary", "arbitrary"), vmem_limit_bytes=VMEM_LIMIT),
        name="mix",
    )(x, *o_views, *l_views, cg, ga, ln_g, ln_b, w_attn_out, w_o, ln1_g, ln1_b, wr_hi, wr_lo, b_route)


def _dispatch_plan(route, counts, t):
    eid = route[:, :TOP_K_INNER].astype(jnp.int32)
    rank = route[:, ROUTE_LANE_RANK:ROUTE_LANE_RANK + TOP_K_INNER]
    counts = counts[0, :N_EXPERTS].astype(jnp.int32)
    padded = (counts + EXPERT_BLOCK - 1) // EXPERT_BLOCK * EXPERT_BLOCK
    pad_end = jnp.cumsum(padded)
    pad_start = pad_end - padded
    experts = jnp.arange(N_EXPERTS, dtype=jnp.int32)
    start_of = jnp.sum(jnp.where(eid[..., None] == experts, pad_start, 0), axis=-1)
    dest = start_of + rank.astype(jnp.int32)
    n_blocks = -(-t * TOP_K_INNER // EXPERT_BLOCK) + N_EXPERTS
    n_slots = n_blocks * EXPERT_BLOCK
    block_start = jnp.arange(n_blocks, dtype=jnp.int32) * EXPERT_BLOCK
    block_e = jnp.minimum(jnp.sum((block_start[:, None] >= pad_end[None, :]).astype(jnp.int32), axis=1),
                          N_EXPERTS - 1)
    n_empty = n_slots - t * TOP_K_INNER
    gap_end = jnp.cumsum(padded - counts)
    gap_start = gap_end - (padded - counts)
    k = jnp.arange(n_empty, dtype=jnp.int32)
    in_gap = (k[:, None] >= gap_start) & (k[:, None] < gap_end)
    empty = jnp.where(k < gap_end[-1],
                      jnp.sum(jnp.where(in_gap, pad_start + counts + k[:, None] - gap_start, 0), axis=-1),
                      pad_end[-1] + k - gap_end[-1])
    return (dest * ROW_CHUNKS).astype(jnp.int32), (empty * ROW_CHUNKS).astype(jnp.int32), block_e


def _token_copy(src_ref, src_row, dst_ref, dst_row, sem):
    return pltpu.make_async_copy(src_ref.at[pl.ds(pl.multiple_of(src_row, ROW_CHUNKS), ROW_CHUNKS), :],
                                 dst_ref.at[pl.ds(pl.multiple_of(dst_row, ROW_CHUNKS), ROW_CHUNKS), :], sem)


def _wait_rows(hbm_ref, n_rows, sem):
    rows = hbm_ref.at[pl.ds(0, n_rows), :]
    pltpu.make_async_copy(rows, rows, sem).wait()


def _dispatch_start(dest_ref, empty_ref, h1_ref, xs_hbm, zero_ref, sem, first_step):
    tm = h1_ref.shape[0] // ROW_CHUNKS
    n_empty = empty_ref.shape[1]

    @pl.when(first_step)
    def _():
        zero_ref[...] = jnp.zeros_like(zero_ref)

    for j in range(tm):
        for choice in range(TOP_K_INNER):
            _token_copy(h1_ref, j * ROW_CHUNKS, xs_hbm, dest_ref[0, choice * tm + j], sem.at[0]).start()
    for j in range(n_empty):
        _token_copy(zero_ref, 0, xs_hbm, empty_ref[0, j], sem.at[1]).start()


def _dispatch_wait(empty_ref, h1_ref, xs_hbm, sem):
    _wait_rows(xs_hbm, TOP_K_INNER * h1_ref.shape[0], sem.at[0])
    _wait_rows(xs_hbm, empty_ref.shape[1] * ROW_CHUNKS, sem.at[1])


def _dispatch_kernel(dest_ref, empty_ref, h1_ref, xs_hbm, zero_ref, sem):
    _dispatch_start(dest_ref, empty_ref, h1_ref, xs_hbm, zero_ref, sem, pl.program_id(0) == 0)
    _dispatch_wait(empty_ref, h1_ref, xs_hbm, sem)


def _dispatch_operands(h1, dest_row, empty_row, n_steps, step_of):
    t = h1.shape[0] // ROW_CHUNKS
    tm = t // n_steps
    n_empty = empty_row.shape[0] // n_steps
    assert tm * n_steps == t and n_empty * n_steps == empty_row.shape[0]
    idx3 = dest_row.reshape(n_steps, tm, TOP_K_INNER).transpose(0, 2, 1).reshape(n_steps, 1, TOP_K_INNER * tm)
    specs = [
        pl.BlockSpec((None, 1, TOP_K_INNER * tm), lambda *g: (step_of(*g), 0, 0), memory_space=pltpu.SMEM),
        pl.BlockSpec((None, 1, n_empty), lambda *g: (step_of(*g), 0, 0), memory_space=pltpu.SMEM),
        pl.BlockSpec((tm * ROW_CHUNKS, LANES), lambda *g: (step_of(*g), 0)),
    ]
    return specs, (idx3, empty_row.reshape(n_steps, 1, n_empty), h1)


def _dispatch_scratch():
    return [pltpu.VMEM((ROW_CHUNKS, LANES), F32), pltpu.SemaphoreType.DMA((2,))]


def _dispatch(h1, dest_row, empty_row, n_slots):
    n_steps = h1.shape[0] // (ROW_CHUNKS * ROW_TILE)
    specs, args = _dispatch_operands(h1, dest_row, empty_row, n_steps, lambda i: i)
    return pl.pallas_call(
        _dispatch_kernel,
        grid=(n_steps,),
        in_specs=specs,
        out_specs=pl.BlockSpec(memory_space=pl.ANY),
        out_shape=jax.ShapeDtypeStruct((n_slots * ROW_CHUNKS, LANES), F32),
        scratch_shapes=_dispatch_scratch(),
        compiler_params=pltpu.CompilerParams(dimension_semantics=("arbitrary",), vmem_limit_bytes=VMEM_LIMIT),
        name="dispatch",
    )(*args)


def _expert_mlp(xb, wg_ref, wu_ref, wd_ref):
    acc = None
    for c in range(D_EXPERT // EXPERT_CHUNK):
        cols = slice(c * EXPERT_CHUNK, (c + 1) * EXPERT_CHUNK)
        gate = _dot(xb, wg_ref[:, cols])
        up = _dot(xb, wu_ref[:, cols])
        hidden = (gate * jax.nn.sigmoid(gate) * up).astype(BF16)
        part = _dot(hidden, wd_ref[cols, :])
        acc = part if acc is None else acc + part
    return acc


def _expert_kernel(be_ref, x_ref, wga_ref, wua_ref, wda_ref, wgb_ref, wub_ref, wdb_ref, y_ref):
    i = pl.program_id(0)
    same = be_ref[2 * i] == be_ref[2 * i + 1]

    @pl.when(same)
    def _():
        xb = _load_token_major(x_ref, 0, 2 * EXPERT_BLOCK).astype(BF16)
        _store_token_major(y_ref, 0, _expert_mlp(xb, wga_ref, wua_ref, wda_ref))

    @pl.when(jnp.logical_not(same))
    def _():
        for half, w in enumerate(((wga_ref, wua_ref, wda_ref), (wgb_ref, wub_ref, wdb_ref))):
            xb = _load_token_major(x_ref, half * EXPERT_BLOCK, EXPERT_BLOCK).astype(BF16)
            _store_token_major(y_ref, half * EXPERT_BLOCK, _expert_mlp(xb, *w))


def _experts(xs, block_e, w_gate, w_up, w_down):
    n_blocks = block_e.shape[0]
    assert n_blocks % 2 == 0
    d = D_MODEL
    blk = pl.BlockSpec((2 * EXPERT_BLOCK * ROW_CHUNKS, LANES), lambda i, be: (i, 0))
    w_specs = [pl.BlockSpec(shape, (lambda i, be, k=k: (be[2 * i + k], 0, 0)))
               for k in range(2) for shape in ((None, d, D_EXPERT), (None, d, D_EXPERT), (None, D_EXPERT, d))]
    grid_spec = pltpu.PrefetchScalarGridSpec(
        num_scalar_prefetch=1,
        grid=(n_blocks // 2,),
        in_specs=[blk] + w_specs,
        out_specs=blk,
    )
    return pl.pallas_call(
        _expert_kernel,
        grid_spec=grid_spec,
        out_shape=jax.ShapeDtypeStruct(xs.shape, F32),
        compiler_params=pltpu.CompilerParams(
            dimension_semantics=("arbitrary",), vmem_limit_bytes=VMEM_LIMIT),
        name="experts",
    )(block_e, xs, w_gate, w_up, w_down, w_gate, w_up, w_down)


def _combine_kernel(idx_ref, idxn_ref, yb_hbm, h1_ref, route_ref, g_ref, b_ref, y_ref, gbuf, sem):
    i = pl.program_id(0)
    n = pl.num_programs(0)
    tm = y_ref.shape[0]
    slot = i % 2

    def gather(src_ref, s):
        def body(c, carry):
            for u in range(ISSUE_UNROLL):
                j = c * ISSUE_UNROLL + u
                _token_copy(yb_hbm, src_ref[0, j], gbuf.at[s], j * ROW_CHUNKS, sem.at[s]).start()
            return carry
        lax.fori_loop(0, TOP_K_INNER * tm // ISSUE_UNROLL, body, 0)

    @pl.when(i == 0)
    def _():
        gather(idx_ref, 0)

    @pl.when(i + 1 < n)
    def _():
        gather(idxn_ref, 1 - slot)

    cur = gbuf.at[slot]
    _wait_rows(yb_hbm, TOP_K_INNER * tm * ROW_CHUNKS, sem.at[slot])
    route = route_ref[...]
    w_first = route[:, ROUTE_LANE_W:ROUTE_LANE_W + 1]
    w_second = route[:, ROUTE_LANE_W + 1:ROUTE_LANE_W + 2]
    ffn = w_first * _load_token_major(cur, 0, tm) + w_second * _load_token_major(cur, tm, tm)
    h1 = _load_token_major(h1_ref, 0, tm)
    y_ref[...] = _layer_norm(DEEPNORM_ALPHA * h1 + ffn, g_ref[...], b_ref[...])


def _combine(h1, route, yb, dest_row, ln2_g, ln2_b):
    t = route.shape[0]
    d = D_MODEL
    tm = COMBINE_TILE
    nt = t // tm
    idx3 = dest_row.reshape(nt, tm, TOP_K_INNER).transpose(0, 2, 1).reshape(nt, 1, TOP_K_INNER * tm)
    smem_blk = lambda f: pl.BlockSpec((None, 1, TOP_K_INNER * tm), f, memory_space=pltpu.SMEM)
    return pl.pallas_call(
        _combine_kernel,
        grid=(nt,),
        in_specs=[
            smem_blk(lambda i: (i, 0, 0)),
            smem_blk(lambda i: (jnp.minimum(i + 1, nt - 1), 0, 0)),
            pl.BlockSpec(memory_space=pl.ANY),
            pl.BlockSpec((tm * ROW_CHUNKS, LANES), lambda i: (i, 0)),
            pl.BlockSpec((tm, LANES), lambda i: (i, 0)),
            _resident((1, d)), _resident((1, d)),
        ],
        out_specs=pl.BlockSpec((tm, d), lambda i: (i, 0)),
        out_shape=jax.ShapeDtypeStruct((t, d), F32),
        scratch_shapes=[pltpu.VMEM((2, TOP_K_INNER * tm * ROW_CHUNKS, LANES), F32), pltpu.SemaphoreType.DMA((2,))],
        compiler_params=pltpu.CompilerParams(
            dimension_semantics=("arbitrary",), vmem_limit_bytes=VMEM_LIMIT),
        name="combine",
    )(idx3, idx3, yb, h1, route, ln2_g, ln2_b)


def _token_mixing(x, p, dispatch_job=None):
    bsz, s, d = x.shape
    outs = _inproj(x, p['ln_in_g'], p['ln_in_b'], p['w_in'], p['b_gate'], p['conv_w'], p['w_conv_out'],
                   dispatch_job)
    qkv0, qkv1, qkv2, cg, ga = outs[:5]
    o_list, lse_list = [], []
    for qkv, r in zip((qkv0, qkv1, qkv2), DILATIONS):
        o, lse = _attention(qkv.reshape(3, bsz, s, GROUP_W), s // r)
        o_list.append(o)
        lse_list.append(lse)
    h1, route, counts = _mix(x, o_list, lse_list, cg, ga, p['ln_in_g'], p['ln_in_b'], p['w_attn_out'], p['w_o'],
                             p['ln1_g'], p['ln1_b'], p['wr_hi'], p['wr_lo'], p['b_route'])
    t = bsz * s
    route = route.reshape(t, LANES)
    dest_row, empty_row, block_e = _dispatch_plan(route, counts, t)
    job = (h1, dest_row, empty_row, block_e.shape[0] * EXPERT_BLOCK)
    return (h1, route, dest_row, block_e, job), outs[5:]


def _channel_mixing(state, xs, p, shape):
    h1, route, dest_row, block_e, _ = state
    yb = _experts(xs, block_e, p['w_gate'], p['w_up'], p['w_down'])
    return _combine(h1, route, yb, dest_row, p['ln2_g'], p['ln2_b']).reshape(shape)


def kernel(x_prompt, x_sample, ln_in_g, ln_in_b, w_in, b_gate, conv_w, w_attn_out, w_conv_out, w_o, ln1_g, ln1_b, w_route_group, b_route_group, w_route_expert, b_route_expert, w_gate, w_up, w_down, ln2_g, ln2_b):
    d = D_MODEL
    w_route = jnp.concatenate([w_route_group[0], w_route_expert[0]], axis=1)
    w_route = jnp.pad(w_route, ((0, 0), (0, LANES - w_route.shape[1])))
    wr_hi = w_route.astype(BF16)
    b_route = jnp.concatenate([b_route_group[0], b_route_expert[0]]).astype(F32)
    p = {
        'ln_in_g': ln_in_g.reshape(1, d), 'ln_in_b': ln_in_b.reshape(1, d),
        'w_in': w_in[0].astype(BF16), 'b_gate': b_gate[0].reshape(1, 2 * d), 'conv_w': conv_w[0],
        'w_conv_out': w_conv_out[0].astype(BF16), 'w_attn_out': w_attn_out[0].astype(BF16),
        'w_o': w_o[0].astype(BF16),
        'ln1_g': ln1_g[0].reshape(1, d), 'ln1_b': ln1_b[0].reshape(1, d),
        'wr_hi': wr_hi, 'wr_lo': (w_route - wr_hi.astype(F32)).astype(BF16),
        'b_route': jnp.pad(b_route, (0, LANES - b_route.shape[0])).reshape(1, LANES),
        'w_gate': w_gate[0].astype(BF16), 'w_up': w_up[0].astype(BF16), 'w_down': w_down[0].astype(BF16),
        'ln2_g': ln2_g[0].reshape(1, d), 'ln2_b': ln2_b[0].reshape(1, d),
    }
    state_p, _ = _token_mixing(x_prompt, p)
    state_s, (xs_p,) = _token_mixing(x_sample, p, dispatch_job=state_p[4])
    y_prompt = _channel_mixing(state_p, xs_p, p, x_prompt.shape)
    y_sample = _channel_mixing(state_s, _dispatch(*state_s[4]), p, x_sample.shape)
    return (y_prompt, y_sample)
```

```python
import functools

import jax
import jax.numpy as jnp
import numpy as np
from jax import lax
from jax.experimental import pallas as pl
from jax.experimental.pallas import tpu as pltpu

F32 = jnp.float32
BF16 = jnp.bfloat16

D_MODEL = 1024
HEAD_DIM = 64
HEADS_PER_GROUP = 4
GROUP_W = HEADS_PER_GROUP * HEAD_DIM
DILATIONS = (1, 4, 16)
BAND = 64
N_GROUPS = len(DILATIONS)
D_ATTN = N_GROUPS * GROUP_W
ATTN_SCALE = HEAD_DIM ** -0.5
ROPE_DIM = HEAD_DIM // 4
ROPE_THETA = 500000.0
MASK_VALUE = -1e30
D_CONV = D_MODEL
COL_CONV_B = 3 * D_ATTN
COL_CONV_C = COL_CONV_B + D_CONV
COL_CONV_H = COL_CONV_C + D_CONV
COL_GATE_ATTN = COL_CONV_H + D_CONV
COL_GATE_CONV = COL_GATE_ATTN + D_MODEL
N_EXPERT_GROUPS = 4
EXPERTS_PER_GROUP = 8
N_EXPERTS = N_EXPERT_GROUPS * EXPERTS_PER_GROUP
TOP_K_INNER = 2
D_EXPERT = 512
EXPERT_BLOCK = 256
EXPERT_CHUNK = 256
LN_EPS = 1e-5
DEPTH = 1
DEEPNORM_ALPHA = (2 * DEPTH) ** 0.25

LANES = 128
ROW_CHUNKS = D_MODEL // LANES
ROW_TILE = 512
HALO = 16
Q_BLOCK = 128
K_BLOCK = Q_BLOCK + 2 * BAND
Q_CHUNK = 2048
COMBINE_TILE = 256
ISSUE_UNROLL = 16
ROUTE_LANE0 = N_EXPERT_GROUPS
ROUTE_LANE_W = TOP_K_INNER
ROUTE_LANE_RANK = 2 * TOP_K_INNER
VMEM_LIMIT = 56 * 1024 * 1024


def _layer_norm(v, g, b):
    mu = jnp.mean(v, axis=-1, keepdims=True)
    d = v - mu
    var = jnp.mean(d * d, axis=-1, keepdims=True)
    return d * lax.rsqrt(var + LN_EPS) * g + b


def _dot(a, b):
    return jnp.dot(a, b, preferred_element_type=F32)


def _store_token_major(ref, first_token, val):
    n = val.shape[0]
    for k in range(ROW_CHUNKS):
        ref[pl.ds(first_token * ROW_CHUNKS + k, n, stride=ROW_CHUNKS), :] = val[:, k * LANES:(k + 1) * LANES]


def _load_token_major(ref, first_token, n):
    return jnp.concatenate(
        [ref[pl.ds(first_token * ROW_CHUNKS + k, n, stride=ROW_CHUNKS), :] for k in range(ROW_CHUNKS)], axis=1)


def _resident(shape):
    nd = len(shape)
    return pl.BlockSpec(shape, lambda *_: (0,) * nd, pipeline_mode=pl.Buffered(1))


def _inproj_kernel(*refs, with_dispatch):
    (x_ref, xp_ref, xn_ref, g_ref, b_ref, win_ref, bg_ref, cw_ref, wco_ref, cos_ref, sa_ref, sb_ref), refs = (
        refs[:12], refs[12:])
    if with_dispatch:
        (dest_ref, empty_ref, h1_ref), refs = refs[:3], refs[3:]
    (qkv0_ref, qkv1_ref, qkv2_ref, cg_ref, ga_ref, h_ref), refs = refs[:6], refs[6:]
    if with_dispatch:
        xs_hbm, hext_ref, u_ref, stage_ref, zero_ref, sem = refs
        first = (pl.program_id(0) == 0) & (pl.program_id(1) == 0)
        _dispatch_start(dest_ref, empty_ref, h1_ref, xs_hbm, zero_ref, sem, first)
    else:
        hext_ref, u_ref, stage_ref = refs
    i = pl.program_id(1)
    nt = pl.num_programs(1)
    tm = x_ref.shape[0]
    g = g_ref[...]
    b = b_ref[...]
    h = _layer_norm(x_ref[...], g, b)
    h_ref[...] = h
    hext_ref[pl.ds(HALO, tm), :] = h.astype(BF16)
    hext_ref[pl.ds(0, HALO), :] = _layer_norm(xp_ref[...], g, b).astype(BF16)
    hext_ref[pl.ds(HALO + tm, HALO), :] = _layer_norm(xn_ref[...], g, b).astype(BF16)
    hm = hext_ref[pl.ds(HALO, tm), :]
    he = hext_ref[...]

    def proj(lhs, col, n):
        return _dot(lhs, win_ref[:, col:col + n])

    u = proj(he, COL_CONV_C, D_CONV) * proj(he, COL_CONV_H, D_CONV)
    rows = lax.broadcasted_iota(jnp.int32, (tm + 2 * HALO, 1), 0)
    lo = jnp.where(i == 0, HALO, 0)
    hi = jnp.where(i == nt - 1, HALO + tm, tm + 2 * HALO)
    u_ref[...] = jnp.where((rows >= lo) & (rows < hi), u, 0.0)
    cw = cw_ref[...]
    conv = (cw[0:1] * u_ref[pl.ds(HALO - 1, tm), :] + cw[1:2] * u_ref[pl.ds(HALO, tm), :]
            + cw[2:3] * u_ref[pl.ds(HALO + 1, tm), :])
    cb = proj(hm, COL_CONV_B, D_CONV)
    conv_branch = _dot((cb * conv).astype(BF16), wco_ref[...])
    gate_conv = jax.nn.sigmoid(proj(hm, COL_GATE_CONV, D_MODEL) + bg_ref[:, D_MODEL:])
    cg_ref[...] = (gate_conv * conv_branch).astype(BF16)
    ga_ref[...] = jax.nn.sigmoid(proj(hm, COL_GATE_ATTN, D_MODEL) + bg_ref[:, :D_MODEL]).astype(BF16)

    cosv = cos_ref[...]
    sav = sa_ref[...]
    sbv = sb_ref[...]
    half = ROPE_DIM // 2
    for gi, (out_ref, r) in enumerate(zip((qkv0_ref, qkv1_ref, qkv2_ref), DILATIONS)):
        for t in range(3):
            p = proj(hm, t * D_ATTN + gi * GROUP_W, GROUP_W)
            if t < 2:
                parts = []
                for c in range(GROUP_W // LANES):
                    pc = p[:, c * LANES:(c + 1) * LANES]
                    parts.append(pc * cosv + pltpu.roll(pc, LANES - half, 1) * sav
                                 + pltpu.roll(pc, half, 1) * sbv)
                p = jnp.concatenate(parts, axis=1)
                if t == 0:
                    p = p * ATTN_SCALE
            if r == 1:
                out_ref[t, 0] = p.astype(BF16)
            else:
                for c in range(GROUP_W // LANES):
                    stage_ref[c] = p[:, c * LANES:(c + 1) * LANES]
                for res in range(r):
                    for c in range(GROUP_W // LANES):
                        out_ref[t, res, :, c * LANES:(c + 1) * LANES] = (
                            stage_ref[c, pl.ds(res, tm // r, stride=r), :].astype(BF16))
    if with_dispatch:
        _dispatch_wait(empty_ref, h1_ref, xs_hbm, sem)


def _rope_tables(s):
    half = ROPE_DIM // 2
    inv_freq = (np.float32(ROPE_THETA) ** (-np.arange(half, dtype=np.float32) * np.float32(2.0) / ROPE_DIM))
    ang = np.arange(s, dtype=np.float32)[:, None] * inv_freq.astype(np.float32)[None, :]
    cos = np.cos(ang).astype(np.float32)
    sin = np.sin(ang).astype(np.float32)
    pad = np.zeros((s, HEAD_DIM - ROPE_DIM), np.float32)
    zero = np.zeros((s, half), np.float32)
    cos_t = np.concatenate([cos, cos, pad + 1.0], axis=1)
    sa_t = np.concatenate([-sin, zero, pad], axis=1)
    sb_t = np.concatenate([zero, sin, pad], axis=1)
    rep = LANES // HEAD_DIM
    return tuple(jnp.asarray(np.tile(t, (1, rep))) for t in (cos_t, sa_t, sb_t))


def _inproj(x, ln_g, ln_b, w_in, b_gate, conv_w, w_conv_out, dispatch_job=None):
    bsz, s, d = x.shape
    tm = ROW_TILE
    nt = s // tm
    job_specs, job_args, job_out_specs, job_out_shapes, job_scratch = [], (), [], [], []
    if dispatch_job is not None:
        h1, dest_row, empty_row, n_slots = dispatch_job
        job_specs, job_args = _dispatch_operands(h1, dest_row, empty_row, bsz * nt, lambda b, i: b * nt + i)
        job_out_specs = [pl.BlockSpec(memory_space=pl.ANY)]
        job_out_shapes = [jax.ShapeDtypeStruct((n_slots * ROW_CHUNKS, LANES), F32)]
        job_scratch = _dispatch_scratch()
    hpt = tm // HALO
    cos_t, sa_t, sb_t = _rope_tables(s)
    row = lambda b, i: (b, i, 0)
    tab = pl.BlockSpec((tm, LANES), lambda b, i: (i, 0))
    qkv_shapes = [jax.ShapeDtypeStruct((3, bsz, r, s // r, GROUP_W), BF16) for r in DILATIONS]
    qkv_specs = [pl.BlockSpec((3, None, r, tm // r, GROUP_W), lambda b, i: (0, b, 0, i, 0))
                 for r in DILATIONS]
    return pl.pallas_call(
        functools.partial(_inproj_kernel, with_dispatch=dispatch_job is not None),
        grid=(bsz, nt),
        in_specs=[
            pl.BlockSpec((None, tm, d), row),
            pl.BlockSpec((None, HALO, d), lambda b, i: (b, jnp.maximum(i * hpt - 1, 0), 0)),
            pl.BlockSpec((None, HALO, d), lambda b, i: (b, jnp.minimum((i + 1) * hpt, s // HALO - 1), 0)),
            _resident((1, d)), _resident((1, d)),
            _resident(w_in.shape), _resident((1, 2 * d)), _resident(conv_w.shape),
            _resident(w_conv_out.shape),
            tab, tab, tab,
        ] + job_specs,
        out_specs=qkv_specs + [pl.BlockSpec((None, tm, d), row)] * 3 + job_out_specs,
        out_shape=(qkv_shapes + [jax.ShapeDtypeStruct((bsz, s, d), BF16)] * 2
                   + [jax.ShapeDtypeStruct((bsz, s, d), F32)] + job_out_shapes),
        scratch_shapes=[
            pltpu.VMEM((tm + 2 * HALO, d), BF16),
            pltpu.VMEM((tm + 2 * HALO, D_CONV), F32),
            pltpu.VMEM((GROUP_W // LANES, tm, LANES), F32),
        ] + job_scratch,
        compiler_params=pltpu.CompilerParams(
            dimension_semantics=("arbitrary", "arbitrary"), vmem_limit_bytes=VMEM_LIMIT),
        name="inproj",
    )(x, x, x, ln_g, ln_b, w_in, b_gate, conv_w, w_conv_out, cos_t, sa_t, sb_t, *job_args)


def _attn_kernel(q_ref, k_ref, v_ref, o_ref, lse_ref, *, sub_len):
    s = k_ref.shape[0]
    qc = q_ref.shape[0]
    base = pl.program_id(1) * qc
    lane = lax.broadcasted_iota(jnp.int32, (1, GROUP_W), 1)
    head_masks = [(lane >= h * HEAD_DIM) & (lane < (h + 1) * HEAD_DIM) for h in range(HEADS_PER_GROUP)]
    qi = lax.broadcasted_iota(jnp.int32, (Q_BLOCK, 1), 0)
    kj = lax.broadcasted_iota(jnp.int32, (1, K_BLOCK), 1)

    def body(j, carry):
        r0 = pl.multiple_of(j * Q_BLOCK, Q_BLOCK)
        s0 = base + r0
        k0 = pl.multiple_of(jnp.clip(s0 - BAND, 0, s - K_BLOCK), BAND)
        q = q_ref[pl.ds(r0, Q_BLOCK), :]
        k = k_ref[pl.ds(k0, K_BLOCK), :]
        v = v_ref[pl.ds(k0, K_BLOCK), :]
        sub_lo = (s0 // sub_len) * sub_len
        qpos = s0 + qi
        kpos = k0 + kj
        valid = (jnp.abs(qpos - kpos) <= BAND) & (kpos >= sub_lo) & (kpos < sub_lo + sub_len)
        qm = jnp.concatenate([jnp.where(hm, q, jnp.zeros_like(q)) for hm in head_masks], axis=0)
        sc = lax.dot_general(qm, k, (((1,), (1,)), ((), ())), preferred_element_type=F32)
        sc = jnp.where(jnp.concatenate([valid] * HEADS_PER_GROUP, axis=0), sc, MASK_VALUE)
        m = jnp.max(sc, axis=-1, keepdims=True)
        p = jnp.exp(sc - m)
        den = jnp.sum(p, axis=-1, keepdims=True)
        o_all = _dot((p * (1.0 / den)).astype(BF16), v)
        lse_all = m + jnp.log(den)
        o_acc = jnp.zeros((Q_BLOCK, GROUP_W), F32)
        lse_acc = jnp.zeros((Q_BLOCK, GROUP_W), F32)
        for h, hm in enumerate(head_masks):
            rows = slice(h * Q_BLOCK, (h + 1) * Q_BLOCK)
            o_acc = jnp.where(hm, o_all[rows], o_acc)
            lse_acc = jnp.where(hm, lse_all[rows], lse_acc)
        o_ref[pl.ds(r0, Q_BLOCK), :] = o_acc.astype(BF16)
        lse_ref[pl.ds(r0, Q_BLOCK), :] = lse_acc
        return carry

    lax.fori_loop(0, qc // Q_BLOCK, body, 0, unroll=8)


def _attention(qkv, sub_len):
    _, bsz, s, w = qkv.shape
    qc = min(Q_CHUNK, s)
    kv_spec = lambda t: pl.BlockSpec((None, None, s, w), lambda b, j: (t, b, 0, 0))
    return pl.pallas_call(
        functools.partial(_attn_kernel, sub_len=sub_len),
        grid=(bsz, s // qc),
        in_specs=[pl.BlockSpec((None, None, qc, w), lambda b, j: (0, b, j, 0)), kv_spec(1), kv_spec(2)],
        out_specs=[pl.BlockSpec((None, qc, w), lambda b, j: (b, j, 0))] * 2,
        out_shape=[jax.ShapeDtypeStruct((bsz, s, w), BF16), jax.ShapeDtypeStruct((bsz, s, w), F32)],
        compiler_params=pltpu.CompilerParams(
            dimension_semantics=("parallel", "arbitrary"), vmem_limit_bytes=VMEM_LIMIT),
        name="attn",
    )(qkv, qkv, qkv)


def _first_index_of_max(vals, lane_f):
    mx = jnp.max(vals, axis=-1, keepdims=True)
    idx = jnp.min(jnp.where(vals == mx, lane_f, float(LANES)), axis=-1, keepdims=True)
    return mx, idx


def _mix_kernel(h_ref, o0_ref, o1_ref, o2_ref, l0_ref, l1_ref, l2_ref, cg_ref, ga_ref,
                wao_ref, wo_ref, g1_ref, b1_ref, wrh_ref, wrl_ref, br_ref,
                h1_ref, route_ref, counts_ref, so_ref, sl_ref, tri_ref):
    tm = h_ref.shape[0]
    halves = GROUP_W // LANES
    for gi, (o_ref, l_ref, r) in enumerate(zip((o0_ref, o1_ref, o2_ref), (l0_ref, l1_ref, l2_ref), DILATIONS)):
        for c in range(halves):
            cols = slice(c * LANES, (c + 1) * LANES)
            for res in range(r):
                rows = pl.ds(res, tm // r, stride=r) if r > 1 else slice(None)
                so_ref[gi * halves + c, rows, :] = o_ref[res, :, cols].astype(F32)
                sl_ref[gi * halves + c, rows, :] = l_ref[res, :, cols]

    def natural(ref, gi):
        return jnp.concatenate([ref[gi * halves + c] for c in range(halves)], axis=1)

    lses = [natural(sl_ref, gi) for gi in range(N_GROUPS)]
    mx = jnp.maximum(jnp.maximum(lses[0], lses[1]), lses[2])
    es = [jnp.exp(l - mx) for l in lses]
    inv_den = 1.0 / (es[0] + es[1] + es[2])
    attn = jnp.concatenate(
        [(natural(so_ref, gi) * (es[gi] * inv_den)).astype(BF16) for gi in range(N_GROUPS)], axis=1)
    attn_branch = _dot(attn, wao_ref[...])
    merged = ga_ref[...].astype(F32) * attn_branch + cg_ref[...].astype(F32)
    mix = _dot(merged.astype(BF16), wo_ref[...])
    h1 = _layer_norm(DEEPNORM_ALPHA * h_ref[...] + mix, g1_ref[...], b1_ref[...])
    _store_token_major(h1_ref, 0, h1)

    hi = h1.astype(BF16)
    lo = (h1 - hi.astype(F32)).astype(BF16)
    wrh = wrh_ref[...]
    logits = _dot(hi, wrh) + _dot(lo, wrh) + _dot(hi, wrl_ref[...]) + br_ref[...]
    lane = lax.broadcasted_iota(jnp.int32, (1, LANES), 1)
    lane_f = lane.astype(F32)
    is_grp = lane < N_EXPERT_GROUPS
    gl = jnp.where(is_grp, logits, MASK_VALUE)
    ge = jnp.exp(gl - jnp.max(gl, axis=-1, keepdims=True))
    gp = jnp.where(is_grp, ge / jnp.sum(ge, axis=-1, keepdims=True), -1.0)
    grp_w, grp_idx = _first_index_of_max(gp, lane_f)
    e_lo = ROUTE_LANE0 + grp_idx * EXPERTS_PER_GROUP
    sel = jnp.where((lane_f >= e_lo) & (lane_f < e_lo + EXPERTS_PER_GROUP), logits, MASK_VALUE)
    v1, i1 = _first_index_of_max(sel, lane_f)
    v2, i2 = _first_index_of_max(jnp.where(lane_f == i1, MASK_VALUE, sel), lane_f)
    e2 = jnp.exp(v2 - v1)
    w1 = grp_w / (1.0 + e2)
    w2 = grp_w * e2 / (1.0 + e2)

    first = (pl.program_id(0) == 0) & (pl.program_id(1) == 0)

    @pl.when(first)
    def _():
        counts_ref[...] = jnp.zeros_like(counts_ref)
        r = lax.broadcasted_iota(jnp.int32, (tm, tm), 0)
        c = lax.broadcasted_iota(jnp.int32, (tm, tm), 1)
        tri_ref[...] = (c < r).astype(BF16)

    oh0 = lane_f == i1 - ROUTE_LANE0
    oh1 = lane_f == i2 - ROUTE_LANE0
    oh = oh0.astype(F32) + oh1.astype(F32)
    before = counts_ref[...] + _dot(tri_ref[...], oh.astype(BF16))
    r0 = jnp.sum(jnp.where(oh0, before, 0.0), axis=-1, keepdims=True)
    r1 = jnp.sum(jnp.where(oh1, before + oh0.astype(F32), 0.0), axis=-1, keepdims=True)
    counts_ref[...] = counts_ref[...] + jnp.sum(oh, axis=0, keepdims=True)

    per_lane = (i1 - ROUTE_LANE0, i2 - ROUTE_LANE0, w1, w2, r0, r1)
    route = jnp.zeros((tm, LANES), F32)
    for k, v in enumerate(per_lane):
        route = jnp.where(lane == k, v, route)
    route_ref[...] = route


def _mix(h, o_list, lse_list, cg, ga, w_attn_out, w_o, ln1_g, ln1_b, wr_hi, wr_lo, b_route):
    bsz, s, d = h.shape
    tm = ROW_TILE
    row = lambda b, i: (b, i, 0)
    dil_specs = [pl.BlockSpec((None, r, tm // r, GROUP_W), lambda b, i: (b, 0, i, 0)) for r in DILATIONS]
    o_views = [o.reshape(bsz, r, s // r, GROUP_W) for o, r in zip(o_list, DILATIONS)]
    l_views = [l.reshape(bsz, r, s // r, GROUP_W) for l, r in zip(lse_list, DILATIONS)]
    return pl.pallas_call(
        _mix_kernel,
        grid=(bsz, s // tm),
        in_specs=[pl.BlockSpec((None, tm, d), row)] + dil_specs + dil_specs + [
            pl.BlockSpec((None, tm, d), row), pl.BlockSpec((None, tm, d), row),
            _resident(w_attn_out.shape), _resident(w_o.shape),
            _resident((1, d)), _resident((1, d)),
            _resident(wr_hi.shape), _resident(wr_lo.shape), _resident((1, LANES)),
        ],
        out_specs=[pl.BlockSpec((tm * ROW_CHUNKS, LANES), lambda b, i: (b * (s // tm) + i, 0)),
                   pl.BlockSpec((None, tm, LANES), row),
                   pl.BlockSpec((1, LANES), lambda b, i: (0, 0))],
        out_shape=[jax.ShapeDtypeStruct((bsz * s * ROW_CHUNKS, LANES), F32),
                   jax.ShapeDtypeStruct((bsz, s, LANES), F32),
                   jax.ShapeDtypeStruct((1, LANES), F32)],
        scratch_shapes=[pltpu.VMEM((N_GROUPS * GROUP_W // LANES, tm, LANES), F32)] * 2
        + [pltpu.VMEM((tm, tm), BF16)],
        compiler_params=pltpu.CompilerParams(
            dimension_semantics=("arbitrary", "arbitrary"), vmem_limit_bytes=VMEM_LIMIT),
        name="mix",
    )(h, *o_views, *l_views, cg, ga, w_attn_out, w_o, ln1_g, ln1_b, wr_hi, wr_lo, b_route)


def _dispatch_plan(route, counts, t):
    eid = route[:, :TOP_K_INNER].astype(jnp.int32)
    rank = route[:, ROUTE_LANE_RANK:ROUTE_LANE_RANK + TOP_K_INNER]
    counts = counts[0, :N_EXPERTS].astype(jnp.int32)
    padded = (counts + EXPERT_BLOCK - 1) // EXPERT_BLOCK * EXPERT_BLOCK
    pad_end = jnp.cumsum(padded)
    pad_start = pad_end - padded
    experts = jnp.arange(N_EXPERTS, dtype=jnp.int32)
    start_of = jnp.sum(jnp.where(eid[..., None] == experts, pad_start, 0), axis=-1)
    dest = start_of + rank.astype(jnp.int32)
    n_blocks = -(-t * TOP_K_INNER // EXPERT_BLOCK) + N_EXPERTS
    n_slots = n_blocks * EXPERT_BLOCK
    block_start = jnp.arange(n_blocks, dtype=jnp.int32) * EXPERT_BLOCK
    block_e = jnp.minimum(jnp.sum((block_start[:, None] >= pad_end[None, :]).astype(jnp.int32), axis=1),
                          N_EXPERTS - 1)
    n_empty = n_slots - t * TOP_K_INNER
    gap_end = jnp.cumsum(padded - counts)
    gap_start = gap_end - (padded - counts)
    k = jnp.arange(n_empty, dtype=jnp.int32)
    in_gap = (k[:, None] >= gap_start) & (k[:, None] < gap_end)
    empty = jnp.where(k < gap_end[-1],
                      jnp.sum(jnp.where(in_gap, pad_start + counts + k[:, None] - gap_start, 0), axis=-1),
                      pad_end[-1] + k - gap_end[-1])
    return (dest * ROW_CHUNKS).astype(jnp.int32), (empty * ROW_CHUNKS).astype(jnp.int32), block_e


def _token_copy(src_ref, src_row, dst_ref, dst_row, sem):
    return pltpu.make_async_copy(src_ref.at[pl.ds(pl.multiple_of(src_row, ROW_CHUNKS), ROW_CHUNKS), :],
                                 dst_ref.at[pl.ds(pl.multiple_of(dst_row, ROW_CHUNKS), ROW_CHUNKS), :], sem)


def _wait_rows(hbm_ref, n_rows, sem):
    rows = hbm_ref.at[pl.ds(0, n_rows), :]
    pltpu.make_async_copy(rows, rows, sem).wait()


def _dispatch_start(dest_ref, empty_ref, h1_ref, xs_hbm, zero_ref, sem, first_step):
    tm = h1_ref.shape[0] // ROW_CHUNKS
    n_empty = empty_ref.shape[1]

    @pl.when(first_step)
    def _():
        zero_ref[...] = jnp.zeros_like(zero_ref)

    def body(c, carry):
        for u in range(ISSUE_UNROLL):
            j = c * ISSUE_UNROLL + u
            for choice in range(TOP_K_INNER):
                _token_copy(h1_ref, j * ROW_CHUNKS, xs_hbm, dest_ref[0, choice * tm + j], sem.at[0]).start()
        return carry
    lax.fori_loop(0, tm // ISSUE_UNROLL, body, 0)

    def zero_body(c, carry):
        for u in range(ISSUE_UNROLL):
            _token_copy(zero_ref, 0, xs_hbm, empty_ref[0, c * ISSUE_UNROLL + u], sem.at[1]).start()
        return carry
    lax.fori_loop(0, n_empty // ISSUE_UNROLL, zero_body, 0)


def _dispatch_wait(empty_ref, h1_ref, xs_hbm, sem):
    _wait_rows(xs_hbm, TOP_K_INNER * h1_ref.shape[0], sem.at[0])
    _wait_rows(xs_hbm, empty_ref.shape[1] * ROW_CHUNKS, sem.at[1])


def _dispatch_kernel(dest_ref, empty_ref, h1_ref, xs_hbm, zero_ref, sem):
    _dispatch_start(dest_ref, empty_ref, h1_ref, xs_hbm, zero_ref, sem, pl.program_id(0) == 0)
    _dispatch_wait(empty_ref, h1_ref, xs_hbm, sem)


def _dispatch_operands(h1, dest_row, empty_row, n_steps, step_of):
    t = h1.shape[0] // ROW_CHUNKS
    tm = t // n_steps
    n_empty = empty_row.shape[0] // n_steps
    assert tm * n_steps == t and tm % ISSUE_UNROLL == 0
    assert n_empty * n_steps == empty_row.shape[0] and n_empty % ISSUE_UNROLL == 0
    idx3 = dest_row.reshape(n_steps, tm, TOP_K_INNER).transpose(0, 2, 1).reshape(n_steps, 1, TOP_K_INNER * tm)
    specs = [
        pl.BlockSpec((None, 1, TOP_K_INNER * tm), lambda *g: (step_of(*g), 0, 0), memory_space=pltpu.SMEM),
        pl.BlockSpec((None, 1, n_empty), lambda *g: (step_of(*g), 0, 0), memory_space=pltpu.SMEM),
        pl.BlockSpec((tm * ROW_CHUNKS, LANES), lambda *g: (step_of(*g), 0)),
    ]
    return specs, (idx3, empty_row.reshape(n_steps, 1, n_empty), h1)


def _dispatch_scratch():
    return [pltpu.VMEM((ROW_CHUNKS, LANES), F32), pltpu.SemaphoreType.DMA((2,))]


def _dispatch(h1, dest_row, empty_row, n_slots):
    n_steps = h1.shape[0] // (ROW_CHUNKS * ROW_TILE)
    specs, args = _dispatch_operands(h1, dest_row, empty_row, n_steps, lambda i: i)
    return pl.pallas_call(
        _dispatch_kernel,
        grid=(n_steps,),
        in_specs=specs,
        out_specs=pl.BlockSpec(memory_space=pl.ANY),
        out_shape=jax.ShapeDtypeStruct((n_slots * ROW_CHUNKS, LANES), F32),
        scratch_shapes=_dispatch_scratch(),
        compiler_params=pltpu.CompilerParams(dimension_semantics=("arbitrary",), vmem_limit_bytes=VMEM_LIMIT),
        name="dispatch",
    )(*args)


def _expert_mlp(xb, wg_ref, wu_ref, wd_ref):
    acc = None
    for c in range(D_EXPERT // EXPERT_CHUNK):
        cols = slice(c * EXPERT_CHUNK, (c + 1) * EXPERT_CHUNK)
        gate = _dot(xb, wg_ref[:, cols])
        up = _dot(xb, wu_ref[:, cols])
        hidden = (gate * jax.nn.sigmoid(gate) * up).astype(BF16)
        part = _dot(hidden, wd_ref[cols, :])
        acc = part if acc is None else acc + part
    return acc


def _expert_kernel(be_ref, x_ref, wga_ref, wua_ref, wda_ref, wgb_ref, wub_ref, wdb_ref, y_ref):
    i = pl.program_id(0)
    same = be_ref[2 * i] == be_ref[2 * i + 1]

    @pl.when(same)
    def _():
        xb = _load_token_major(x_ref, 0, 2 * EXPERT_BLOCK).astype(BF16)
        _store_token_major(y_ref, 0, _expert_mlp(xb, wga_ref, wua_ref, wda_ref))

    @pl.when(jnp.logical_not(same))
    def _():
        for half, w in enumerate(((wga_ref, wua_ref, wda_ref), (wgb_ref, wub_ref, wdb_ref))):
            xb = _load_token_major(x_ref, half * EXPERT_BLOCK, EXPERT_BLOCK).astype(BF16)
            _store_token_major(y_ref, half * EXPERT_BLOCK, _expert_mlp(xb, *w))


def _experts(xs, block_e, w_gate, w_up, w_down):
    n_blocks = block_e.shape[0]
    assert n_blocks % 2 == 0
    d = D_MODEL
    blk = pl.BlockSpec((2 * EXPERT_BLOCK * ROW_CHUNKS, LANES), lambda i, be: (i, 0))
    w_specs = [pl.BlockSpec(shape, (lambda i, be, k=k: (be[2 * i + k], 0, 0)))
               for k in range(2) for shape in ((None, d, D_EXPERT), (None, d, D_EXPERT), (None, D_EXPERT, d))]
    grid_spec = pltpu.PrefetchScalarGridSpec(
        num_scalar_prefetch=1,
        grid=(n_blocks // 2,),
        in_specs=[blk] + w_specs,
        out_specs=blk,
    )
    return pl.pallas_call(
        _expert_kernel,
        grid_spec=grid_spec,
        out_shape=jax.ShapeDtypeStruct(xs.shape, F32),
        compiler_params=pltpu.CompilerParams(
            dimension_semantics=("arbitrary",), vmem_limit_bytes=VMEM_LIMIT),
        name="experts",
    )(block_e, xs, w_gate, w_up, w_down, w_gate, w_up, w_down)


def _combine_kernel(idx_ref, idxn_ref, yb_hbm, h1_ref, route_ref, g_ref, b_ref, y_ref, gbuf, sem):
    i = pl.program_id(0)
    n = pl.num_programs(0)
    tm = y_ref.shape[0]
    slot = i % 2

    def gather(src_ref, s):
        def body(c, carry):
            for u in range(ISSUE_UNROLL):
                j = c * ISSUE_UNROLL + u
                _token_copy(yb_hbm, src_ref[0, j], gbuf.at[s], j * ROW_CHUNKS, sem.at[s]).start()
            return carry
        lax.fori_loop(0, TOP_K_INNER * tm // ISSUE_UNROLL, body, 0)

    @pl.when(i == 0)
    def _():
        gather(idx_ref, 0)

    @pl.when(i + 1 < n)
    def _():
        gather(idxn_ref, 1 - slot)

    cur = gbuf.at[slot]
    _wait_rows(yb_hbm, TOP_K_INNER * tm * ROW_CHUNKS, sem.at[slot])
    route = route_ref[...]
    w_first = route[:, ROUTE_LANE_W:ROUTE_LANE_W + 1]
    w_second = route[:, ROUTE_LANE_W + 1:ROUTE_LANE_W + 2]
    ffn = w_first * _load_token_major(cur, 0, tm) + w_second * _load_token_major(cur, tm, tm)
    h1 = _load_token_major(h1_ref, 0, tm)
    y_ref[...] = _layer_norm(DEEPNORM_ALPHA * h1 + ffn, g_ref[...], b_ref[...])


def _combine(h1, route, yb, dest_row, ln2_g, ln2_b):
    t = route.shape[0]
    d = D_MODEL
    tm = COMBINE_TILE
    nt = t // tm
    idx3 = dest_row.reshape(nt, tm, TOP_K_INNER).transpose(0, 2, 1).reshape(nt, 1, TOP_K_INNER * tm)
    smem_blk = lambda f: pl.BlockSpec((None, 1, TOP_K_INNER * tm), f, memory_space=pltpu.SMEM)
    return pl.pallas_call(
        _combine_kernel,
        grid=(nt,),
        in_specs=[
            smem_blk(lambda i: (i, 0, 0)),
            smem_blk(lambda i: (jnp.minimum(i + 1, nt - 1), 0, 0)),
            pl.BlockSpec(memory_space=pl.ANY),
            pl.BlockSpec((tm * ROW_CHUNKS, LANES), lambda i: (i, 0)),
            pl.BlockSpec((tm, LANES), lambda i: (i, 0)),
            _resident((1, d)), _resident((1, d)),
        ],
        out_specs=pl.BlockSpec((tm, d), lambda i: (i, 0)),
        out_shape=jax.ShapeDtypeStruct((t, d), F32),
        scratch_shapes=[pltpu.VMEM((2, TOP_K_INNER * tm * ROW_CHUNKS, LANES), F32), pltpu.SemaphoreType.DMA((2,))],
        compiler_params=pltpu.CompilerParams(
            dimension_semantics=("arbitrary",), vmem_limit_bytes=VMEM_LIMIT),
        name="combine",
    )(idx3, idx3, yb, h1, route, ln2_g, ln2_b)


def _token_mixing(x, p, dispatch_job=None):
    bsz, s, d = x.shape
    outs = _inproj(x, p['ln_in_g'], p['ln_in_b'], p['w_in'], p['b_gate'], p['conv_w'], p['w_conv_out'],
                   dispatch_job)
    qkv0, qkv1, qkv2, cg, ga, h = outs[:6]
    o_list, lse_list = [], []
    for qkv, r in zip((qkv0, qkv1, qkv2), DILATIONS):
        o, lse = _attention(qkv.reshape(3, bsz, s, GROUP_W), s // r)
        o_list.append(o)
        lse_list.append(lse)
    h1, route, counts = _mix(h, o_list, lse_list, cg, ga, p['w_attn_out'], p['w_o'],
                             p['ln1_g'], p['ln1_b'], p['wr_hi'], p['wr_lo'], p['b_route'])
    t = bsz * s
    route = route.reshape(t, LANES)
    dest_row, empty_row, block_e = _dispatch_plan(route, counts, t)
    job = (h1, dest_row, empty_row, block_e.shape[0] * EXPERT_BLOCK)
    return (h1, route, dest_row, block_e, job), outs[6:]


def _channel_mixing(state, xs, p, shape):
    h1, route, dest_row, block_e, _ = state
    yb = _experts(xs, block_e, p['w_gate'], p['w_up'], p['w_down'])
    return _combine(h1, route, yb, dest_row, p['ln2_g'], p['ln2_b']).reshape(shape)


def kernel(x_prompt, x_sample, ln_in_g, ln_in_b, w_in, b_gate, conv_w, w_attn_out, w_conv_out, w_o, ln1_g, ln1_b, w_route_group, b_route_group, w_route_expert, b_route_expert, w_gate, w_up, w_down, ln2_g, ln2_b):
    d = D_MODEL
    w_route = jnp.concatenate([w_route_group[0], w_route_expert[0]], axis=1)
    w_route = jnp.pad(w_route, ((0, 0), (0, LANES - w_route.shape[1])))
    wr_hi = w_route.astype(BF16)
    b_route = jnp.concatenate([b_route_group[0], b_route_expert[0]]).astype(F32)
    p = {
        'ln_in_g': ln_in_g.reshape(1, d), 'ln_in_b': ln_in_b.reshape(1, d),
        'w_in': w_in[0].astype(BF16), 'b_gate': b_gate[0].reshape(1, 2 * d), 'conv_w': conv_w[0],
        'w_conv_out': w_conv_out[0].astype(BF16), 'w_attn_out': w_attn_out[0].astype(BF16),
        'w_o': w_o[0].astype(BF16),
        'ln1_g': ln1_g[0].reshape(1, d), 'ln1_b': ln1_b[0].reshape(1, d),
        'wr_hi': wr_hi, 'wr_lo': (w_route - wr_hi.astype(F32)).astype(BF16),
        'b_route': jnp.pad(b_route, (0, LANES - b_route.shape[0])).reshape(1, LANES),
        'w_gate': w_gate[0].astype(BF16), 'w_up': w_up[0].astype(BF16), 'w_down': w_down[0].astype(BF16),
        'ln2_g': ln2_g[0].reshape(1, d), 'ln2_b': ln2_b[0].reshape(1, d),
    }
    state_p, _ = _token_mixing(x_prompt, p)
    state_s, (xs_p,) = _token_mixing(x_sample, p, dispatch_job=state_p[4])
    y_prompt = _channel_mixing(state_p, xs_p, p, x_prompt.shape)
    y_sample = _channel_mixing(state_s, _dispatch(*state_s[4]), p, x_sample.shape)
    return (y_prompt, y_sample)
```

```python
import functools

import jax
import jax.numpy as jnp
import numpy as np
from jax import lax
from jax.experimental import pallas as pl
from jax.experimental.pallas import tpu as pltpu
from jax.experimental.pallas import tpu_sc as plsc

F32 = jnp.float32
BF16 = jnp.bfloat16

D_MODEL = 1024
HEAD_DIM = 64
HEADS_PER_GROUP = 4
GROUP_W = HEADS_PER_GROUP * HEAD_DIM
DILATIONS = (1, 4, 16)
BAND = 64
N_GROUPS = len(DILATIONS)
D_ATTN = N_GROUPS * GROUP_W
ATTN_SCALE = HEAD_DIM ** -0.5
ROPE_DIM = HEAD_DIM // 4
ROPE_THETA = 500000.0
MASK_VALUE = -1e30
D_CONV = D_MODEL
COL_CONV_B = 3 * D_ATTN
COL_CONV_C = COL_CONV_B + D_CONV
COL_CONV_H = COL_CONV_C + D_CONV
COL_GATE_ATTN = COL_CONV_H + D_CONV
COL_GATE_CONV = COL_GATE_ATTN + D_MODEL
N_EXPERT_GROUPS = 4
EXPERTS_PER_GROUP = 8
N_EXPERTS = N_EXPERT_GROUPS * EXPERTS_PER_GROUP
TOP_K_INNER = 2
D_EXPERT = 512
EXPERT_BLOCK = 256
EXPERT_CHUNK = 256
LN_EPS = 1e-5
DEPTH = 1
DEEPNORM_ALPHA = (2 * DEPTH) ** 0.25

LANES = 128
ROW_CHUNKS = D_MODEL // LANES
ROW_TILE = 512
HALO = 16
Q_BLOCK = 128
K_BLOCK = Q_BLOCK + 2 * BAND
Q_CHUNK = 2048
SC_COLS = 256
SC_CHUNKS = D_MODEL // SC_COLS
SC_WINDOW = 128
ROUTE_LANE0 = N_EXPERT_GROUPS
ROUTE_LANE_W = TOP_K_INNER
ROUTE_LANE_RANK = 2 * TOP_K_INNER
VMEM_LIMIT = 56 * 1024 * 1024


def _layer_norm(v, g, b):
    mu = jnp.mean(v, axis=-1, keepdims=True)
    d = v - mu
    var = jnp.mean(d * d, axis=-1, keepdims=True)
    return d * lax.rsqrt(var + LN_EPS) * g + b


def _dot(a, b):
    return jnp.dot(a, b, preferred_element_type=F32)


def _store_token_major(ref, first_token, val):
    n = val.shape[0]
    for k in range(ROW_CHUNKS):
        ref[pl.ds(first_token * ROW_CHUNKS + k, n, stride=ROW_CHUNKS), :] = val[:, k * LANES:(k + 1) * LANES]


def _load_token_major(ref, first_token, n):
    return jnp.concatenate(
        [ref[pl.ds(first_token * ROW_CHUNKS + k, n, stride=ROW_CHUNKS), :] for k in range(ROW_CHUNKS)], axis=1)


def _resident(shape):
    nd = len(shape)
    return pl.BlockSpec(shape, lambda *_: (0,) * nd, pipeline_mode=pl.Buffered(1))


def _inproj_kernel(*refs, with_dispatch):
    (x_ref, xp_ref, xn_ref, g_ref, b_ref, win_ref, bg_ref, cw_ref, wco_ref, cos_ref, sa_ref, sb_ref), refs = (
        refs[:12], refs[12:])
    if with_dispatch:
        (dest_ref, empty_ref, h1_ref), refs = refs[:3], refs[3:]
    (qkv0_ref, qkv1_ref, qkv2_ref, cg_ref, ga_ref, h_ref), refs = refs[:6], refs[6:]
    if with_dispatch:
        xs_hbm, hext_ref, u_ref, stage_ref, zero_ref, sem = refs
        first = (pl.program_id(0) == 0) & (pl.program_id(1) == 0)
        _dispatch_start(dest_ref, empty_ref, h1_ref, xs_hbm, zero_ref, sem, first)
    else:
        hext_ref, u_ref, stage_ref = refs
    i = pl.program_id(1)
    nt = pl.num_programs(1)
    tm = x_ref.shape[0]
    g = g_ref[...]
    b = b_ref[...]
    h = _layer_norm(x_ref[...], g, b)
    h_ref[...] = h
    hext_ref[pl.ds(HALO, tm), :] = h.astype(BF16)
    hext_ref[pl.ds(0, HALO), :] = _layer_norm(xp_ref[...], g, b).astype(BF16)
    hext_ref[pl.ds(HALO + tm, HALO), :] = _layer_norm(xn_ref[...], g, b).astype(BF16)
    hm = hext_ref[pl.ds(HALO, tm), :]
    he = hext_ref[...]

    def proj(lhs, col, n):
        return _dot(lhs, win_ref[:, col:col + n])

    u = proj(he, COL_CONV_C, D_CONV) * proj(he, COL_CONV_H, D_CONV)
    rows = lax.broadcasted_iota(jnp.int32, (tm + 2 * HALO, 1), 0)
    lo = jnp.where(i == 0, HALO, 0)
    hi = jnp.where(i == nt - 1, HALO + tm, tm + 2 * HALO)
    u_ref[...] = jnp.where((rows >= lo) & (rows < hi), u, 0.0)
    cw = cw_ref[...]
    conv = (cw[0:1] * u_ref[pl.ds(HALO - 1, tm), :] + cw[1:2] * u_ref[pl.ds(HALO, tm), :]
            + cw[2:3] * u_ref[pl.ds(HALO + 1, tm), :])
    cb = proj(hm, COL_CONV_B, D_CONV)
    conv_branch = _dot((cb * conv).astype(BF16), wco_ref[...])
    gate_conv = jax.nn.sigmoid(proj(hm, COL_GATE_CONV, D_MODEL) + bg_ref[:, D_MODEL:])
    cg_ref[...] = (gate_conv * conv_branch).astype(BF16)
    ga_ref[...] = jax.nn.sigmoid(proj(hm, COL_GATE_ATTN, D_MODEL) + bg_ref[:, :D_MODEL]).astype(BF16)

    cosv = cos_ref[...]
    sav = sa_ref[...]
    sbv = sb_ref[...]
    half = ROPE_DIM // 2
    for gi, (out_ref, r) in enumerate(zip((qkv0_ref, qkv1_ref, qkv2_ref), DILATIONS)):
        for t in range(3):
            p = proj(hm, t * D_ATTN + gi * GROUP_W, GROUP_W)
            if t < 2:
                parts = []
                for c in range(GROUP_W // LANES):
                    pc = p[:, c * LANES:(c + 1) * LANES]
                    parts.append(pc * cosv + pltpu.roll(pc, LANES - half, 1) * sav
                                 + pltpu.roll(pc, half, 1) * sbv)
                p = jnp.concatenate(parts, axis=1)
                if t == 0:
                    p = p * ATTN_SCALE
            if r == 1:
                out_ref[t, 0] = p.astype(BF16)
            else:
                for c in range(GROUP_W // LANES):
                    stage_ref[c] = p[:, c * LANES:(c + 1) * LANES]
                for res in range(r):
                    for c in range(GROUP_W // LANES):
                        out_ref[t, res, :, c * LANES:(c + 1) * LANES] = (
                            stage_ref[c, pl.ds(res, tm // r, stride=r), :].astype(BF16))
    if with_dispatch:
        _dispatch_wait(empty_ref, h1_ref, xs_hbm, sem)


def _rope_tables(s):
    half = ROPE_DIM // 2
    inv_freq = (np.float32(ROPE_THETA) ** (-np.arange(half, dtype=np.float32) * np.float32(2.0) / ROPE_DIM))
    ang = np.arange(s, dtype=np.float32)[:, None] * inv_freq.astype(np.float32)[None, :]
    cos = np.cos(ang).astype(np.float32)
    sin = np.sin(ang).astype(np.float32)
    pad = np.zeros((s, HEAD_DIM - ROPE_DIM), np.float32)
    zero = np.zeros((s, half), np.float32)
    cos_t = np.concatenate([cos, cos, pad + 1.0], axis=1)
    sa_t = np.concatenate([-sin, zero, pad], axis=1)
    sb_t = np.concatenate([zero, sin, pad], axis=1)
    rep = LANES // HEAD_DIM
    return tuple(jnp.asarray(np.tile(t, (1, rep))) for t in (cos_t, sa_t, sb_t))


def _inproj(x, ln_g, ln_b, w_in, b_gate, conv_w, w_conv_out, dispatch_job=None):
    bsz, s, d = x.shape
    tm = ROW_TILE
    nt = s // tm
    job_specs, job_args, job_out_specs, job_out_shapes, job_scratch = [], (), [], [], []
    if dispatch_job is not None:
        h1, dest_row, empty_row, n_slots = dispatch_job
        job_specs, job_args = _dispatch_operands(h1, dest_row, empty_row, bsz * nt, lambda b, i: b * nt + i)
        job_out_specs = [pl.BlockSpec(memory_space=pl.ANY)]
        job_out_shapes = [jax.ShapeDtypeStruct((n_slots * ROW_CHUNKS, LANES), F32)]
        job_scratch = _dispatch_scratch()
    hpt = tm // HALO
    cos_t, sa_t, sb_t = _rope_tables(s)
    row = lambda b, i: (b, i, 0)
    tab = pl.BlockSpec((tm, LANES), lambda b, i: (i, 0))
    qkv_shapes = [jax.ShapeDtypeStruct((3, bsz, r, s // r, GROUP_W), BF16) for r in DILATIONS]
    qkv_specs = [pl.BlockSpec((3, None, r, tm // r, GROUP_W), lambda b, i: (0, b, 0, i, 0))
                 for r in DILATIONS]
    return pl.pallas_call(
        functools.partial(_inproj_kernel, with_dispatch=dispatch_job is not None),
        grid=(bsz, nt),
        in_specs=[
            pl.BlockSpec((None, tm, d), row),
            pl.BlockSpec((None, HALO, d), lambda b, i: (b, jnp.maximum(i * hpt - 1, 0), 0)),
            pl.BlockSpec((None, HALO, d), lambda b, i: (b, jnp.minimum((i + 1) * hpt, s // HALO - 1), 0)),
            _resident((1, d)), _resident((1, d)),
            _resident(w_in.shape), _resident((1, 2 * d)), _resident(conv_w.shape),
            _resident(w_conv_out.shape),
            tab, tab, tab,
        ] + job_specs,
        out_specs=qkv_specs + [pl.BlockSpec((None, tm, d), row)] * 3 + job_out_specs,
        out_shape=(qkv_shapes + [jax.ShapeDtypeStruct((bsz, s, d), BF16)] * 2
                   + [jax.ShapeDtypeStruct((bsz, s, d), F32)] + job_out_shapes),
        scratch_shapes=[
            pltpu.VMEM((tm + 2 * HALO, d), BF16),
            pltpu.VMEM((tm + 2 * HALO, D_CONV), F32),
            pltpu.VMEM((GROUP_W // LANES, tm, LANES), F32),
        ] + job_scratch,
        compiler_params=pltpu.CompilerParams(
            dimension_semantics=("arbitrary", "arbitrary"), vmem_limit_bytes=VMEM_LIMIT),
        name="inproj",
    )(x, x, x, ln_g, ln_b, w_in, b_gate, conv_w, w_conv_out, cos_t, sa_t, sb_t, *job_args)


def _attn_kernel(q_ref, k_ref, v_ref, o_ref, lse_ref, *, sub_len):
    s = k_ref.shape[0]
    qc = q_ref.shape[0]
    base = pl.program_id(1) * qc
    lane = lax.broadcasted_iota(jnp.int32, (1, GROUP_W), 1)
    head_masks = [(lane >= h * HEAD_DIM) & (lane < (h + 1) * HEAD_DIM) for h in range(HEADS_PER_GROUP)]
    qi = lax.broadcasted_iota(jnp.int32, (Q_BLOCK, 1), 0)
    kj = lax.broadcasted_iota(jnp.int32, (1, K_BLOCK), 1)

    def body(j, carry):
        r0 = pl.multiple_of(j * Q_BLOCK, Q_BLOCK)
        s0 = base + r0
        k0 = pl.multiple_of(jnp.clip(s0 - BAND, 0, s - K_BLOCK), BAND)
        q = q_ref[pl.ds(r0, Q_BLOCK), :]
        k = k_ref[pl.ds(k0, K_BLOCK), :]
        v = v_ref[pl.ds(k0, K_BLOCK), :]
        sub_lo = (s0 // sub_len) * sub_len
        qpos = s0 + qi
        kpos = k0 + kj
        valid = (jnp.abs(qpos - kpos) <= BAND) & (kpos >= sub_lo) & (kpos < sub_lo + sub_len)
        qm = jnp.concatenate([jnp.where(hm, q, jnp.zeros_like(q)) for hm in head_masks], axis=0)
        sc = lax.dot_general(qm, k, (((1,), (1,)), ((), ())), preferred_element_type=F32)
        sc = jnp.where(jnp.concatenate([valid] * HEADS_PER_GROUP, axis=0), sc, MASK_VALUE)
        m = jnp.max(sc, axis=-1, keepdims=True)
        p = jnp.exp(sc - m)
        den = jnp.sum(p, axis=-1, keepdims=True)
        o_all = _dot((p * (1.0 / den)).astype(BF16), v)
        lse_all = m + jnp.log(den)
        o_acc = jnp.zeros((Q_BLOCK, GROUP_W), F32)
        lse_acc = jnp.zeros((Q_BLOCK, GROUP_W), F32)
        for h, hm in enumerate(head_masks):
            rows = slice(h * Q_BLOCK, (h + 1) * Q_BLOCK)
            o_acc = jnp.where(hm, o_all[rows], o_acc)
            lse_acc = jnp.where(hm, lse_all[rows], lse_acc)
        o_ref[pl.ds(r0, Q_BLOCK), :] = o_acc.astype(BF16)
        lse_ref[pl.ds(r0, Q_BLOCK), :] = lse_acc
        return carry

    lax.fori_loop(0, qc // Q_BLOCK, body, 0, unroll=8)


def _attention(qkv, sub_len):
    _, bsz, s, w = qkv.shape
    qc = min(Q_CHUNK, s)
    kv_spec = lambda t: pl.BlockSpec((None, None, s, w), lambda b, j: (t, b, 0, 0))
    return pl.pallas_call(
        functools.partial(_attn_kernel, sub_len=sub_len),
        grid=(bsz, s // qc),
        in_specs=[pl.BlockSpec((None, None, qc, w), lambda b, j: (0, b, j, 0)), kv_spec(1), kv_spec(2)],
        out_specs=[pl.BlockSpec((None, qc, w), lambda b, j: (b, j, 0))] * 2,
        out_shape=[jax.ShapeDtypeStruct((bsz, s, w), BF16), jax.ShapeDtypeStruct((bsz, s, w), F32)],
        compiler_params=pltpu.CompilerParams(
            dimension_semantics=("parallel", "arbitrary"), vmem_limit_bytes=VMEM_LIMIT),
        name="attn",
    )(qkv, qkv, qkv)


def _first_index_of_max(vals, lane_f):
    mx = jnp.max(vals, axis=-1, keepdims=True)
    idx = jnp.min(jnp.where(vals == mx, lane_f, float(LANES)), axis=-1, keepdims=True)
    return mx, idx


def _mix_kernel(h_ref, o0_ref, o1_ref, o2_ref, l0_ref, l1_ref, l2_ref, cg_ref, ga_ref,
                wao_ref, wo_ref, g1_ref, b1_ref, wrh_ref, wrl_ref, br_ref,
                h1_ref, route_ref, counts_ref, so_ref, sl_ref, tri_ref):
    tm = h_ref.shape[0]
    halves = GROUP_W // LANES
    for gi, (o_ref, l_ref, r) in enumerate(zip((o0_ref, o1_ref, o2_ref), (l0_ref, l1_ref, l2_ref), DILATIONS)):
        for c in range(halves):
            cols = slice(c * LANES, (c + 1) * LANES)
            for res in range(r):
                rows = pl.ds(res, tm // r, stride=r) if r > 1 else slice(None)
                so_ref[gi * halves + c, rows, :] = o_ref[res, :, cols].astype(F32)
                sl_ref[gi * halves + c, rows, :] = l_ref[res, :, cols]

    def natural(ref, gi):
        return jnp.concatenate([ref[gi * halves + c] for c in range(halves)], axis=1)

    lses = [natural(sl_ref, gi) for gi in range(N_GROUPS)]
    mx = jnp.maximum(jnp.maximum(lses[0], lses[1]), lses[2])
    es = [jnp.exp(l - mx) for l in lses]
    inv_den = 1.0 / (es[0] + es[1] + es[2])
    attn = jnp.concatenate(
        [(natural(so_ref, gi) * (es[gi] * inv_den)).astype(BF16) for gi in range(N_GROUPS)], axis=1)
    attn_branch = _dot(attn, wao_ref[...])
    merged = ga_ref[...].astype(F32) * attn_branch + cg_ref[...].astype(F32)
    mix = _dot(merged.astype(BF16), wo_ref[...])
    h1 = _layer_norm(DEEPNORM_ALPHA * h_ref[...] + mix, g1_ref[...], b1_ref[...])
    _store_token_major(h1_ref, 0, h1)

    hi = h1.astype(BF16)
    lo = (h1 - hi.astype(F32)).astype(BF16)
    wrh = wrh_ref[...]
    logits = _dot(hi, wrh) + _dot(lo, wrh) + _dot(hi, wrl_ref[...]) + br_ref[...]
    lane = lax.broadcasted_iota(jnp.int32, (1, LANES), 1)
    lane_f = lane.astype(F32)
    is_grp = lane < N_EXPERT_GROUPS
    gl = jnp.where(is_grp, logits, MASK_VALUE)
    ge = jnp.exp(gl - jnp.max(gl, axis=-1, keepdims=True))
    gp = jnp.where(is_grp, ge / jnp.sum(ge, axis=-1, keepdims=True), -1.0)
    grp_w, grp_idx = _first_index_of_max(gp, lane_f)
    e_lo = ROUTE_LANE0 + grp_idx * EXPERTS_PER_GROUP
    sel = jnp.where((lane_f >= e_lo) & (lane_f < e_lo + EXPERTS_PER_GROUP), logits, MASK_VALUE)
    v1, i1 = _first_index_of_max(sel, lane_f)
    v2, i2 = _first_index_of_max(jnp.where(lane_f == i1, MASK_VALUE, sel), lane_f)
    e2 = jnp.exp(v2 - v1)
    w1 = grp_w / (1.0 + e2)
    w2 = grp_w * e2 / (1.0 + e2)

    first = (pl.program_id(0) == 0) & (pl.program_id(1) == 0)

    @pl.when(first)
    def _():
        counts_ref[...] = jnp.zeros_like(counts_ref)
        r = lax.broadcasted_iota(jnp.int32, (tm, tm), 0)
        c = lax.broadcasted_iota(jnp.int32, (tm, tm), 1)
        tri_ref[...] = (c < r).astype(BF16)

    oh0 = lane_f == i1 - ROUTE_LANE0
    oh1 = lane_f == i2 - ROUTE_LANE0
    oh = oh0.astype(F32) + oh1.astype(F32)
    before = counts_ref[...] + _dot(tri_ref[...], oh.astype(BF16))
    r0 = jnp.sum(jnp.where(oh0, before, 0.0), axis=-1, keepdims=True)
    r1 = jnp.sum(jnp.where(oh1, before + oh0.astype(F32), 0.0), axis=-1, keepdims=True)
    counts_ref[...] = counts_ref[...] + jnp.sum(oh, axis=0, keepdims=True)

    per_lane = (i1 - ROUTE_LANE0, i2 - ROUTE_LANE0, w1, w2, r0, r1)
    route = jnp.zeros((tm, LANES), F32)
    for k, v in enumerate(per_lane):
        route = jnp.where(lane == k, v, route)
    route_ref[...] = route


def _mix(h, o_list, lse_list, cg, ga, w_attn_out, w_o, ln1_g, ln1_b, wr_hi, wr_lo, b_route):
    bsz, s, d = h.shape
    tm = ROW_TILE
    row = lambda b, i: (b, i, 0)
    dil_specs = [pl.BlockSpec((None, r, tm // r, GROUP_W), lambda b, i: (b, 0, i, 0)) for r in DILATIONS]
    o_views = [o.reshape(bsz, r, s // r, GROUP_W) for o, r in zip(o_list, DILATIONS)]
    l_views = [l.reshape(bsz, r, s // r, GROUP_W) for l, r in zip(lse_list, DILATIONS)]
    return pl.pallas_call(
        _mix_kernel,
        grid=(bsz, s // tm),
        in_specs=[pl.BlockSpec((None, tm, d), row)] + dil_specs + dil_specs + [
            pl.BlockSpec((None, tm, d), row), pl.BlockSpec((None, tm, d), row),
            _resident(w_attn_out.shape), _resident(w_o.shape),
            _resident((1, d)), _resident((1, d)),
            _resident(wr_hi.shape), _resident(wr_lo.shape), _resident((1, LANES)),
        ],
        out_specs=[pl.BlockSpec((tm * ROW_CHUNKS, LANES), lambda b, i: (b * (s // tm) + i, 0)),
                   pl.BlockSpec((None, tm, LANES), row),
                   pl.BlockSpec((1, LANES), lambda b, i: (0, 0))],
        out_shape=[jax.ShapeDtypeStruct((bsz * s * ROW_CHUNKS, LANES), F32),
                   jax.ShapeDtypeStruct((bsz, s, LANES), F32),
                   jax.ShapeDtypeStruct((1, LANES), F32)],
        scratch_shapes=[pltpu.VMEM((N_GROUPS * GROUP_W // LANES, tm, LANES), F32)] * 2
        + [pltpu.VMEM((tm, tm), BF16)],
        compiler_params=pltpu.CompilerParams(
            dimension_semantics=("arbitrary", "arbitrary"), vmem_limit_bytes=VMEM_LIMIT),
        name="mix",
    )(h, *o_views, *l_views, cg, ga, w_attn_out, w_o, ln1_g, ln1_b, wr_hi, wr_lo, b_route)


def _dispatch_plan(route, counts, t):
    eid = route[:, :TOP_K_INNER].astype(jnp.int32)
    rank = route[:, ROUTE_LANE_RANK:ROUTE_LANE_RANK + TOP_K_INNER]
    counts = counts[0, :N_EXPERTS].astype(jnp.int32)
    padded = (counts + EXPERT_BLOCK - 1) // EXPERT_BLOCK * EXPERT_BLOCK
    pad_end = jnp.cumsum(padded)
    pad_start = pad_end - padded
    experts = jnp.arange(N_EXPERTS, dtype=jnp.int32)
    start_of = jnp.sum(jnp.where(eid[..., None] == experts, pad_start, 0), axis=-1)
    dest = start_of + rank.astype(jnp.int32)
    n_blocks = -(-t * TOP_K_INNER // EXPERT_BLOCK) + N_EXPERTS
    n_slots = n_blocks * EXPERT_BLOCK
    block_start = jnp.arange(n_blocks, dtype=jnp.int32) * EXPERT_BLOCK
    block_e = jnp.minimum(jnp.sum((block_start[:, None] >= pad_end[None, :]).astype(jnp.int32), axis=1),
                          N_EXPERTS - 1)
    n_empty = n_slots - t * TOP_K_INNER
    gap_end = jnp.cumsum(padded - counts)
    gap_start = gap_end - (padded - counts)
    k = jnp.arange(n_empty, dtype=jnp.int32)
    in_gap = (k[:, None] >= gap_start) & (k[:, None] < gap_end)
    empty = jnp.where(k < gap_end[-1],
                      jnp.sum(jnp.where(in_gap, pad_start + counts + k[:, None] - gap_start, 0), axis=-1),
                      pad_end[-1] + k - gap_end[-1])
    return (dest * ROW_CHUNKS).astype(jnp.int32), (empty * ROW_CHUNKS).astype(jnp.int32), block_e


def _token_copy(src_ref, src_row, dst_ref, dst_row, sem):
    return pltpu.make_async_copy(src_ref.at[pl.ds(pl.multiple_of(src_row, ROW_CHUNKS), ROW_CHUNKS), :],
                                 dst_ref.at[pl.ds(pl.multiple_of(dst_row, ROW_CHUNKS), ROW_CHUNKS), :], sem)


def _wait_rows(hbm_ref, n_rows, sem):
    rows = hbm_ref.at[pl.ds(0, n_rows), :]
    pltpu.make_async_copy(rows, rows, sem).wait()


def _dispatch_start(dest_ref, empty_ref, h1_ref, xs_hbm, zero_ref, sem, first_step):
    tm = h1_ref.shape[0] // ROW_CHUNKS
    n_empty = empty_ref.shape[1]

    @pl.when(first_step)
    def _():
        zero_ref[...] = jnp.zeros_like(zero_ref)

    for j in range(tm):
        for choice in range(TOP_K_INNER):
            _token_copy(h1_ref, j * ROW_CHUNKS, xs_hbm, dest_ref[0, choice * tm + j], sem.at[0]).start()
    for j in range(n_empty):
        _token_copy(zero_ref, 0, xs_hbm, empty_ref[0, j], sem.at[1]).start()


def _dispatch_wait(empty_ref, h1_ref, xs_hbm, sem):
    _wait_rows(xs_hbm, TOP_K_INNER * h1_ref.shape[0], sem.at[0])
    _wait_rows(xs_hbm, empty_ref.shape[1] * ROW_CHUNKS, sem.at[1])


def _dispatch_kernel(dest_ref, empty_ref, h1_ref, xs_hbm, zero_ref, sem):
    _dispatch_start(dest_ref, empty_ref, h1_ref, xs_hbm, zero_ref, sem, pl.program_id(0) == 0)
    _dispatch_wait(empty_ref, h1_ref, xs_hbm, sem)


def _dispatch_operands(h1, dest_row, empty_row, n_steps, step_of):
    t = h1.shape[0] // ROW_CHUNKS
    tm = t // n_steps
    n_empty = empty_row.shape[0] // n_steps
    assert tm * n_steps == t and n_empty * n_steps == empty_row.shape[0]
    idx3 = dest_row.reshape(n_steps, tm, TOP_K_INNER).transpose(0, 2, 1).reshape(n_steps, 1, TOP_K_INNER * tm)
    specs = [
        pl.BlockSpec((None, 1, TOP_K_INNER * tm), lambda *g: (step_of(*g), 0, 0), memory_space=pltpu.SMEM),
        pl.BlockSpec((None, 1, n_empty), lambda *g: (step_of(*g), 0, 0), memory_space=pltpu.SMEM),
        pl.BlockSpec((tm * ROW_CHUNKS, LANES), lambda *g: (step_of(*g), 0)),
    ]
    return specs, (idx3, empty_row.reshape(n_steps, 1, n_empty), h1)


def _dispatch_scratch():
    return [pltpu.VMEM((ROW_CHUNKS, LANES), F32), pltpu.SemaphoreType.DMA((2,))]


def _dispatch(h1, dest_row, empty_row, n_slots):
    n_steps = h1.shape[0] // (ROW_CHUNKS * ROW_TILE)
    specs, args = _dispatch_operands(h1, dest_row, empty_row, n_steps, lambda i: i)
    return pl.pallas_call(
        _dispatch_kernel,
        grid=(n_steps,),
        in_specs=specs,
        out_specs=pl.BlockSpec(memory_space=pl.ANY),
        out_shape=jax.ShapeDtypeStruct((n_slots * ROW_CHUNKS, LANES), F32),
        scratch_shapes=_dispatch_scratch(),
        compiler_params=pltpu.CompilerParams(dimension_semantics=("arbitrary",), vmem_limit_bytes=VMEM_LIMIT),
        name="dispatch",
    )(*args)


def _expert_mlp(xb, wg_ref, wu_ref, wd_ref):
    acc = None
    for c in range(D_EXPERT // EXPERT_CHUNK):
        cols = slice(c * EXPERT_CHUNK, (c + 1) * EXPERT_CHUNK)
        gate = _dot(xb, wg_ref[:, cols])
        up = _dot(xb, wu_ref[:, cols])
        hidden = (gate * jax.nn.sigmoid(gate) * up).astype(BF16)
        part = _dot(hidden, wd_ref[cols, :])
        acc = part if acc is None else acc + part
    return acc


def _expert_kernel(be_ref, x_ref, wga_ref, wua_ref, wda_ref, wgb_ref, wub_ref, wdb_ref, y_ref):
    i = pl.program_id(0)
    same = be_ref[2 * i] == be_ref[2 * i + 1]

    def store(first_row, val):
        for c in range(SC_CHUNKS):
            y_ref[c, pl.ds(first_row, val.shape[0]), :] = val[:, c * SC_COLS:(c + 1) * SC_COLS]

    @pl.when(same)
    def _():
        xb = _load_token_major(x_ref, 0, 2 * EXPERT_BLOCK).astype(BF16)
        store(0, _expert_mlp(xb, wga_ref, wua_ref, wda_ref))

    @pl.when(jnp.logical_not(same))
    def _():
        for half, w in enumerate(((wga_ref, wua_ref, wda_ref), (wgb_ref, wub_ref, wdb_ref))):
            xb = _load_token_major(x_ref, half * EXPERT_BLOCK, EXPERT_BLOCK).astype(BF16)
            store(half * EXPERT_BLOCK, _expert_mlp(xb, *w))


def _experts(xs, block_e, w_gate, w_up, w_down):
    n_blocks = block_e.shape[0]
    assert n_blocks % 2 == 0
    d = D_MODEL
    blk = pl.BlockSpec((2 * EXPERT_BLOCK * ROW_CHUNKS, LANES), lambda i, be: (i, 0))
    w_specs = [pl.BlockSpec(shape, (lambda i, be, k=k: (be[2 * i + k], 0, 0)))
               for k in range(2) for shape in ((None, d, D_EXPERT), (None, d, D_EXPERT), (None, D_EXPERT, d))]
    grid_spec = pltpu.PrefetchScalarGridSpec(
        num_scalar_prefetch=1,
        grid=(n_blocks // 2,),
        in_specs=[blk] + w_specs,
        out_specs=pl.BlockSpec((SC_CHUNKS, 2 * EXPERT_BLOCK, SC_COLS), lambda i, be: (0, i, 0)),
    )
    return pl.pallas_call(
        _expert_kernel,
        grid_spec=grid_spec,
        out_shape=jax.ShapeDtypeStruct((SC_CHUNKS, n_blocks * EXPERT_BLOCK, SC_COLS), F32),
        compiler_params=pltpu.CompilerParams(
            dimension_semantics=("arbitrary",), vmem_limit_bytes=VMEM_LIMIT),
        name="experts",
    )(block_e, xs, w_gate, w_up, w_down, w_gate, w_up, w_down)


def _sc_gather_rows(x, idx):
    m = idx.shape[0]
    mesh = plsc.VectorSubcoreMesh(core_axis_name="core", subcore_axis_name="subcore")

    @pl.kernel(out_type=jax.ShapeDtypeStruct((m, x.shape[1]), x.dtype), mesh=mesh, scratch_types=[])
    def gather(x_hbm, i_hbm, o_hbm):
        def body(i_vmem, o_vmem):
            pltpu.sync_copy(x_hbm.at[i_vmem.at[0]], o_vmem)

        pltpu.emit_pipeline(
            body,
            grid=(m // SC_WINDOW,),
            in_specs=[pl.BlockSpec((1, SC_WINDOW), index_map=lambda i: (0, i))],
            out_specs=[pl.BlockSpec((SC_WINDOW, x.shape[1]), index_map=lambda i: (i, 0))],
            core_axis_name=("core", "subcore"),
            dimension_semantics=(pltpu.PARALLEL,),
        )(i_hbm, o_hbm)

    return gather(x, idx.reshape(1, m))


def _combine_kernel(first_ref, second_ref, h1_ref, route_ref, g_ref, b_ref, y_ref):
    tm = y_ref.shape[0]
    route = route_ref[...]
    w_first = route[:, ROUTE_LANE_W:ROUTE_LANE_W + 1]
    w_second = route[:, ROUTE_LANE_W + 1:ROUTE_LANE_W + 2]
    first = jnp.concatenate([first_ref[c] for c in range(SC_CHUNKS)], axis=1)
    second = jnp.concatenate([second_ref[c] for c in range(SC_CHUNKS)], axis=1)
    h1 = _load_token_major(h1_ref, 0, tm)
    y_ref[...] = _layer_norm(DEEPNORM_ALPHA * h1 + w_first * first + w_second * second, g_ref[...], b_ref[...])


def _combine(h1, route, yb, dest_row, ln2_g, ln2_b):
    t = route.shape[0]
    d = D_MODEL
    tm = ROW_TILE
    nt = t // tm
    n_slots = yb.shape[1]
    slot = (dest_row // ROW_CHUNKS).T.reshape(-1)
    idx = (jnp.arange(SC_CHUNKS, dtype=jnp.int32)[:, None] * n_slots + slot[None, :]).reshape(-1)
    g = _sc_gather_rows(yb.reshape(SC_CHUNKS * n_slots, SC_COLS), idx)
    g = g.reshape(SC_CHUNKS, TOP_K_INNER * t, SC_COLS)
    return pl.pallas_call(
        _combine_kernel,
        grid=(nt,),
        in_specs=[
            pl.BlockSpec((SC_CHUNKS, tm, SC_COLS), lambda i: (0, i, 0)),
            pl.BlockSpec((SC_CHUNKS, tm, SC_COLS), lambda i: (0, nt + i, 0)),
            pl.BlockSpec((tm * ROW_CHUNKS, LANES), lambda i: (i, 0)),
            pl.BlockSpec((tm, LANES), lambda i: (i, 0)),
            _resident((1, d)), _resident((1, d)),
        ],
        out_specs=pl.BlockSpec((tm, d), lambda i: (i, 0)),
        out_shape=jax.ShapeDtypeStruct((t, d), F32),
        compiler_params=pltpu.CompilerParams(
            dimension_semantics=("parallel",), vmem_limit_bytes=VMEM_LIMIT),
        name="combine",
    )(g, g, h1, route, ln2_g, ln2_b)


def _token_mixing(x, p, dispatch_job=None):
    bsz, s, d = x.shape
    outs = _inproj(x, p['ln_in_g'], p['ln_in_b'], p['w_in'], p['b_gate'], p['conv_w'], p['w_conv_out'],
                   dispatch_job)
    qkv0, qkv1, qkv2, cg, ga, h = outs[:6]
    o_list, lse_list = [], []
    for qkv, r in zip((qkv0, qkv1, qkv2), DILATIONS):
        o, lse = _attention(qkv.reshape(3, bsz, s, GROUP_W), s // r)
        o_list.append(o)
        lse_list.append(lse)
    h1, route, counts = _mix(h, o_list, lse_list, cg, ga, p['w_attn_out'], p['w_o'],
                             p['ln1_g'], p['ln1_b'], p['wr_hi'], p['wr_lo'], p['b_route'])
    t = bsz * s
    route = route.reshape(t, LANES)
    dest_row, empty_row, block_e = _dispatch_plan(route, counts, t)
    job = (h1, dest_row, empty_row, block_e.shape[0] * EXPERT_BLOCK)
    return (h1, route, dest_row, block_e, job), outs[6:]


def _channel_mixing(state, xs, p, shape):
    h1, route, dest_row, block_e, _ = state
    yb = _experts(xs, block_e, p['w_gate'], p['w_up'], p['w_down'])
    return _combine(h1, route, yb, dest_row, p['ln2_g'], p['ln2_b']).reshape(shape)


def kernel(x_prompt, x_sample, ln_in_g, ln_in_b, w_in, b_gate, conv_w, w_attn_out, w_conv_out, w_o, ln1_g, ln1_b, w_route_group, b_route_group, w_route_expert, b_route_expert, w_gate, w_up, w_down, ln2_g, ln2_b):
    d = D_MODEL
    w_route = jnp.concatenate([w_route_group[0], w_route_expert[0]], axis=1)
    w_route = jnp.pad(w_route, ((0, 0), (0, LANES - w_route.shape[1])))
    wr_hi = w_route.astype(BF16)
    b_route = jnp.concatenate([b_route_group[0], b_route_expert[0]]).astype(F32)
    p = {
        'ln_in_g': ln_in_g.reshape(1, d), 'ln_in_b': ln_in_b.reshape(1, d),
        'w_in': w_in[0].astype(BF16), 'b_gate': b_gate[0].reshape(1, 2 * d), 'conv_w': conv_w[0],
        'w_conv_out': w_conv_out[0].astype(BF16), 'w_attn_out': w_attn_out[0].astype(BF16),
        'w_o': w_o[0].astype(BF16),
        'ln1_g': ln1_g[0].reshape(1, d), 'ln1_b': ln1_b[0].reshape(1, d),
        'wr_hi': wr_hi, 'wr_lo': (w_route - wr_hi.astype(F32)).astype(BF16),
        'b_route': jnp.pad(b_route, (0, LANES - b_route.shape[0])).reshape(1, LANES),
        'w_gate': w_gate[0].astype(BF16), 'w_up': w_up[0].astype(BF16), 'w_down': w_down[0].astype(BF16),
        'ln2_g': ln2_g[0].reshape(1, d), 'ln2_b': ln2_b[0].reshape(1, d),
    }
    state_p, _ = _token_mixing(x_prompt, p)
    state_s, (xs_p,) = _token_mixing(x_sample, p, dispatch_job=state_p[4])
    y_prompt = _channel_mixing(state_p, xs_p, p, x_prompt.shape)
    y_sample = _channel_mixing(state_s, _dispatch(*state_s[4]), p, x_sample.shape)
    return (y_prompt, y_sample)
```

```python
import functools

import jax
import jax.numpy as jnp
import numpy as np
from jax import lax
from jax.experimental import pallas as pl
from jax.experimental.pallas import tpu as pltpu
from jax.experimental.pallas import tpu_sc as plsc

F32 = jnp.float32
BF16 = jnp.bfloat16

D_MODEL = 1024
HEAD_DIM = 64
HEADS_PER_GROUP = 4
GROUP_W = HEADS_PER_GROUP * HEAD_DIM
DILATIONS = (1, 4, 16)
BAND = 64
N_GROUPS = len(DILATIONS)
D_ATTN = N_GROUPS * GROUP_W
ATTN_SCALE = HEAD_DIM ** -0.5
ROPE_DIM = HEAD_DIM // 4
ROPE_THETA = 500000.0
MASK_VALUE = -1e30
D_CONV = D_MODEL
COL_CONV_B = 3 * D_ATTN
COL_CONV_C = COL_CONV_B + D_CONV
COL_CONV_H = COL_CONV_C + D_CONV
COL_GATE_ATTN = COL_CONV_H + D_CONV
COL_GATE_CONV = COL_GATE_ATTN + D_MODEL
N_EXPERT_GROUPS = 4
EXPERTS_PER_GROUP = 8
N_EXPERTS = N_EXPERT_GROUPS * EXPERTS_PER_GROUP
TOP_K_INNER = 2
D_EXPERT = 512
EXPERT_BLOCK = 256
EXPERT_CHUNK = 256
LN_EPS = 1e-5
DEPTH = 1
DEEPNORM_ALPHA = (2 * DEPTH) ** 0.25

LANES = 128
ROW_TILE = 512
HALO = 16
Q_BLOCK = 128
K_BLOCK = Q_BLOCK + 2 * BAND
Q_CHUNK = 2048
SC_COLS = 256
SC_CHUNKS = D_MODEL // SC_COLS
SC_WINDOW = 128
ROUTE_LANE0 = N_EXPERT_GROUPS
ROUTE_LANE_W = TOP_K_INNER
ROUTE_LANE_RANK = 2 * TOP_K_INNER
VMEM_LIMIT = 56 * 1024 * 1024


def _layer_norm(v, g, b):
    mu = jnp.mean(v, axis=-1, keepdims=True)
    d = v - mu
    var = jnp.mean(d * d, axis=-1, keepdims=True)
    return d * lax.rsqrt(var + LN_EPS) * g + b


def _dot(a, b):
    return jnp.dot(a, b, preferred_element_type=F32)


def _store_chunk_major(ref, first_row, val):
    for c in range(SC_CHUNKS):
        ref[c, pl.ds(first_row, val.shape[0]), :] = val[:, c * SC_COLS:(c + 1) * SC_COLS]


def _load_chunk_major(ref, first_row, n):
    return jnp.concatenate([ref[c, pl.ds(first_row, n), :] for c in range(SC_CHUNKS)], axis=1)


def _chunk_major_spec(rows, index_map):
    return pl.BlockSpec((SC_CHUNKS, rows, SC_COLS), index_map)


def _resident(shape):
    nd = len(shape)
    return pl.BlockSpec(shape, lambda *_: (0,) * nd, pipeline_mode=pl.Buffered(1))


def _inproj_kernel(x_ref, xp_ref, xn_ref, g_ref, b_ref, win_ref, bg_ref, cw_ref, wco_ref,
                   cos_ref, sa_ref, sb_ref,
                   qkv0_ref, qkv1_ref, qkv2_ref, cg_ref, ga_ref, h_ref,
                   hext_ref, u_ref, stage_ref):
    i = pl.program_id(1)
    nt = pl.num_programs(1)
    tm = x_ref.shape[0]
    g = g_ref[...]
    b = b_ref[...]
    h = _layer_norm(x_ref[...], g, b)
    h_ref[...] = h
    hext_ref[pl.ds(HALO, tm), :] = h.astype(BF16)
    hext_ref[pl.ds(0, HALO), :] = _layer_norm(xp_ref[...], g, b).astype(BF16)
    hext_ref[pl.ds(HALO + tm, HALO), :] = _layer_norm(xn_ref[...], g, b).astype(BF16)
    hm = hext_ref[pl.ds(HALO, tm), :]
    he = hext_ref[...]

    def proj(lhs, col, n):
        return _dot(lhs, win_ref[:, col:col + n])

    u = proj(he, COL_CONV_C, D_CONV) * proj(he, COL_CONV_H, D_CONV)
    rows = lax.broadcasted_iota(jnp.int32, (tm + 2 * HALO, 1), 0)
    lo = jnp.where(i == 0, HALO, 0)
    hi = jnp.where(i == nt - 1, HALO + tm, tm + 2 * HALO)
    u_ref[...] = jnp.where((rows >= lo) & (rows < hi), u, 0.0)
    cw = cw_ref[...]
    conv = (cw[0:1] * u_ref[pl.ds(HALO - 1, tm), :] + cw[1:2] * u_ref[pl.ds(HALO, tm), :]
            + cw[2:3] * u_ref[pl.ds(HALO + 1, tm), :])
    cb = proj(hm, COL_CONV_B, D_CONV)
    conv_branch = _dot((cb * conv).astype(BF16), wco_ref[...])
    gate_conv = jax.nn.sigmoid(proj(hm, COL_GATE_CONV, D_MODEL) + bg_ref[:, D_MODEL:])
    cg_ref[...] = (gate_conv * conv_branch).astype(BF16)
    ga_ref[...] = jax.nn.sigmoid(proj(hm, COL_GATE_ATTN, D_MODEL) + bg_ref[:, :D_MODEL]).astype(BF16)

    cosv = cos_ref[...]
    sav = sa_ref[...]
    sbv = sb_ref[...]
    half = ROPE_DIM // 2
    for gi, (out_ref, r) in enumerate(zip((qkv0_ref, qkv1_ref, qkv2_ref), DILATIONS)):
        for t in range(3):
            p = proj(hm, t * D_ATTN + gi * GROUP_W, GROUP_W)
            if t < 2:
                parts = []
                for c in range(GROUP_W // LANES):
                    pc = p[:, c * LANES:(c + 1) * LANES]
                    parts.append(pc * cosv + pltpu.roll(pc, LANES - half, 1) * sav
                                 + pltpu.roll(pc, half, 1) * sbv)
                p = jnp.concatenate(parts, axis=1)
                if t == 0:
                    p = p * ATTN_SCALE
            if r == 1:
                out_ref[t, 0] = p.astype(BF16)
            else:
                for c in range(GROUP_W // LANES):
                    stage_ref[c] = p[:, c * LANES:(c + 1) * LANES]
                for res in range(r):
                    for c in range(GROUP_W // LANES):
                        out_ref[t, res, :, c * LANES:(c + 1) * LANES] = (
                            stage_ref[c, pl.ds(res, tm // r, stride=r), :].astype(BF16))


def _rope_tables(s):
    half = ROPE_DIM // 2
    inv_freq = (np.float32(ROPE_THETA) ** (-np.arange(half, dtype=np.float32) * np.float32(2.0) / ROPE_DIM))
    ang = np.arange(s, dtype=np.float32)[:, None] * inv_freq.astype(np.float32)[None, :]
    cos = np.cos(ang).astype(np.float32)
    sin = np.sin(ang).astype(np.float32)
    pad = np.zeros((s, HEAD_DIM - ROPE_DIM), np.float32)
    zero = np.zeros((s, half), np.float32)
    cos_t = np.concatenate([cos, cos, pad + 1.0], axis=1)
    sa_t = np.concatenate([-sin, zero, pad], axis=1)
    sb_t = np.concatenate([zero, sin, pad], axis=1)
    rep = LANES // HEAD_DIM
    return tuple(jnp.asarray(np.tile(t, (1, rep))) for t in (cos_t, sa_t, sb_t))


def _inproj(x, ln_g, ln_b, w_in, b_gate, conv_w, w_conv_out):
    bsz, s, d = x.shape
    tm = ROW_TILE
    nt = s // tm
    hpt = tm // HALO
    cos_t, sa_t, sb_t = _rope_tables(s)
    row = lambda b, i: (b, i, 0)
    tab = pl.BlockSpec((tm, LANES), lambda b, i: (i, 0))
    qkv_shapes = [jax.ShapeDtypeStruct((3, bsz, r, s // r, GROUP_W), BF16) for r in DILATIONS]
    qkv_specs = [pl.BlockSpec((3, None, r, tm // r, GROUP_W), lambda b, i: (0, b, 0, i, 0))
                 for r in DILATIONS]
    return pl.pallas_call(
        _inproj_kernel,
        grid=(bsz, nt),
        in_specs=[
            pl.BlockSpec((None, tm, d), row),
            pl.BlockSpec((None, HALO, d), lambda b, i: (b, jnp.maximum(i * hpt - 1, 0), 0)),
            pl.BlockSpec((None, HALO, d), lambda b, i: (b, jnp.minimum((i + 1) * hpt, s // HALO - 1), 0)),
            _resident((1, d)), _resident((1, d)),
            _resident(w_in.shape), _resident((1, 2 * d)), _resident(conv_w.shape),
            _resident(w_conv_out.shape),
            tab, tab, tab,
        ],
        out_specs=qkv_specs + [pl.BlockSpec((None, tm, d), row)] * 3,
        out_shape=(qkv_shapes + [jax.ShapeDtypeStruct((bsz, s, d), BF16)] * 2
                   + [jax.ShapeDtypeStruct((bsz, s, d), F32)]),
        scratch_shapes=[
            pltpu.VMEM((tm + 2 * HALO, d), BF16),
            pltpu.VMEM((tm + 2 * HALO, D_CONV), F32),
            pltpu.VMEM((GROUP_W // LANES, tm, LANES), F32),
        ],
        compiler_params=pltpu.CompilerParams(
            dimension_semantics=("parallel", "parallel"), vmem_limit_bytes=VMEM_LIMIT),
        name="inproj",
    )(x, x, x, ln_g, ln_b, w_in, b_gate, conv_w, w_conv_out, cos_t, sa_t, sb_t)


def _attn_kernel(q_ref, k_ref, v_ref, o_ref, lse_ref, *, sub_len):
    s = k_ref.shape[0]
    qc = q_ref.shape[0]
    base = pl.program_id(1) * qc
    lane = lax.broadcasted_iota(jnp.int32, (1, GROUP_W), 1)
    head_masks = [(lane >= h * HEAD_DIM) & (lane < (h + 1) * HEAD_DIM) for h in range(HEADS_PER_GROUP)]
    qi = lax.broadcasted_iota(jnp.int32, (Q_BLOCK, 1), 0)
    kj = lax.broadcasted_iota(jnp.int32, (1, K_BLOCK), 1)

    def body(j, carry):
        r0 = pl.multiple_of(j * Q_BLOCK, Q_BLOCK)
        s0 = base + r0
        k0 = pl.multiple_of(jnp.clip(s0 - BAND, 0, s - K_BLOCK), BAND)
        q = q_ref[pl.ds(r0, Q_BLOCK), :]
        k = k_ref[pl.ds(k0, K_BLOCK), :]
        v = v_ref[pl.ds(k0, K_BLOCK), :]
        sub_lo = (s0 // sub_len) * sub_len
        qpos = s0 + qi
        kpos = k0 + kj
        valid = (jnp.abs(qpos - kpos) <= BAND) & (kpos >= sub_lo) & (kpos < sub_lo + sub_len)
        qm = jnp.concatenate([jnp.where(hm, q, jnp.zeros_like(q)) for hm in head_masks], axis=0)
        sc = lax.dot_general(qm, k, (((1,), (1,)), ((), ())), preferred_element_type=F32)
        sc = jnp.where(jnp.concatenate([valid] * HEADS_PER_GROUP, axis=0), sc, MASK_VALUE)
        m = jnp.max(sc, axis=-1, keepdims=True)
        p = jnp.exp(sc - m)
        den = jnp.sum(p, axis=-1, keepdims=True)
        o_all = _dot((p * (1.0 / den)).astype(BF16), v)
        lse_all = m + jnp.log(den)
        o_acc = jnp.zeros((Q_BLOCK, GROUP_W), F32)
        lse_acc = jnp.zeros((Q_BLOCK, GROUP_W), F32)
        for h, hm in enumerate(head_masks):
            rows = slice(h * Q_BLOCK, (h + 1) * Q_BLOCK)
            o_acc = jnp.where(hm, o_all[rows], o_acc)
            lse_acc = jnp.where(hm, lse_all[rows], lse_acc)
        o_ref[pl.ds(r0, Q_BLOCK), :] = o_acc.astype(BF16)
        lse_ref[pl.ds(r0, Q_BLOCK), :] = lse_acc
        return carry

    lax.fori_loop(0, qc // Q_BLOCK, body, 0, unroll=8)


def _attention(qkv, sub_len):
    _, bsz, s, w = qkv.shape
    qc = min(Q_CHUNK, s)
    kv_spec = lambda t: pl.BlockSpec((None, None, s, w), lambda b, j: (t, b, 0, 0))
    return pl.pallas_call(
        functools.partial(_attn_kernel, sub_len=sub_len),
        grid=(bsz, s // qc),
        in_specs=[pl.BlockSpec((None, None, qc, w), lambda b, j: (0, b, j, 0)), kv_spec(1), kv_spec(2)],
        out_specs=[pl.BlockSpec((None, qc, w), lambda b, j: (b, j, 0))] * 2,
        out_shape=[jax.ShapeDtypeStruct((bsz, s, w), BF16), jax.ShapeDtypeStruct((bsz, s, w), F32)],
        compiler_params=pltpu.CompilerParams(
            dimension_semantics=("parallel", "arbitrary"), vmem_limit_bytes=VMEM_LIMIT),
        name="attn",
    )(qkv, qkv, qkv)


def _first_index_of_max(vals, lane_f):
    mx = jnp.max(vals, axis=-1, keepdims=True)
    idx = jnp.min(jnp.where(vals == mx, lane_f, float(LANES)), axis=-1, keepdims=True)
    return mx, idx


def _mix_kernel(h_ref, o0_ref, o1_ref, o2_ref, l0_ref, l1_ref, l2_ref, cg_ref, ga_ref,
                wao_ref, wo_ref, g1_ref, b1_ref, wrh_ref, wrl_ref, br_ref,
                h1_ref, route_ref, counts_ref, so_ref, sl_ref, tri_ref):
    tm = h_ref.shape[0]
    halves = GROUP_W // LANES
    for gi, (o_ref, l_ref, r) in enumerate(zip((o0_ref, o1_ref, o2_ref), (l0_ref, l1_ref, l2_ref), DILATIONS)):
        for c in range(halves):
            cols = slice(c * LANES, (c + 1) * LANES)
            for res in range(r):
                rows = pl.ds(res, tm // r, stride=r) if r > 1 else slice(None)
                so_ref[gi * halves + c, rows, :] = o_ref[res, :, cols].astype(F32)
                sl_ref[gi * halves + c, rows, :] = l_ref[res, :, cols]

    def natural(ref, gi):
        return jnp.concatenate([ref[gi * halves + c] for c in range(halves)], axis=1)

    lses = [natural(sl_ref, gi) for gi in range(N_GROUPS)]
    mx = jnp.maximum(jnp.maximum(lses[0], lses[1]), lses[2])
    es = [jnp.exp(l - mx) for l in lses]
    inv_den = 1.0 / (es[0] + es[1] + es[2])
    attn = jnp.concatenate(
        [(natural(so_ref, gi) * (es[gi] * inv_den)).astype(BF16) for gi in range(N_GROUPS)], axis=1)
    attn_branch = _dot(attn, wao_ref[...])
    merged = ga_ref[...].astype(F32) * attn_branch + cg_ref[...].astype(F32)
    mix = _dot(merged.astype(BF16), wo_ref[...])
    h1 = _layer_norm(DEEPNORM_ALPHA * h_ref[...] + mix, g1_ref[...], b1_ref[...])
    _store_chunk_major(h1_ref, 0, h1)

    hi = h1.astype(BF16)
    lo = (h1 - hi.astype(F32)).astype(BF16)
    wrh = wrh_ref[...]
    logits = _dot(hi, wrh) + _dot(lo, wrh) + _dot(hi, wrl_ref[...]) + br_ref[...]
    lane = lax.broadcasted_iota(jnp.int32, (1, LANES), 1)
    lane_f = lane.astype(F32)
    is_grp = lane < N_EXPERT_GROUPS
    gl = jnp.where(is_grp, logits, MASK_VALUE)
    ge = jnp.exp(gl - jnp.max(gl, axis=-1, keepdims=True))
    gp = jnp.where(is_grp, ge / jnp.sum(ge, axis=-1, keepdims=True), -1.0)
    grp_w, grp_idx = _first_index_of_max(gp, lane_f)
    e_lo = ROUTE_LANE0 + grp_idx * EXPERTS_PER_GROUP
    sel = jnp.where((lane_f >= e_lo) & (lane_f < e_lo + EXPERTS_PER_GROUP), logits, MASK_VALUE)
    v1, i1 = _first_index_of_max(sel, lane_f)
    v2, i2 = _first_index_of_max(jnp.where(lane_f == i1, MASK_VALUE, sel), lane_f)
    e2 = jnp.exp(v2 - v1)
    w1 = grp_w / (1.0 + e2)
    w2 = grp_w * e2 / (1.0 + e2)

    first = (pl.program_id(0) == 0) & (pl.program_id(1) == 0)

    @pl.when(first)
    def _():
        counts_ref[...] = jnp.zeros_like(counts_ref)
        r = lax.broadcasted_iota(jnp.int32, (tm, tm), 0)
        c = lax.broadcasted_iota(jnp.int32, (tm, tm), 1)
        tri_ref[...] = (c < r).astype(BF16)

    oh0 = lane_f == i1 - ROUTE_LANE0
    oh1 = lane_f == i2 - ROUTE_LANE0
    oh = oh0.astype(F32) + oh1.astype(F32)
    before = counts_ref[...] + _dot(tri_ref[...], oh.astype(BF16))
    r0 = jnp.sum(jnp.where(oh0, before, 0.0), axis=-1, keepdims=True)
    r1 = jnp.sum(jnp.where(oh1, before + oh0.astype(F32), 0.0), axis=-1, keepdims=True)
    counts_ref[...] = counts_ref[...] + jnp.sum(oh, axis=0, keepdims=True)

    per_lane = (i1 - ROUTE_LANE0, i2 - ROUTE_LANE0, w1, w2, r0, r1)
    route = jnp.zeros((tm, LANES), F32)
    for k, v in enumerate(per_lane):
        route = jnp.where(lane == k, v, route)
    route_ref[...] = route


def _mix(h, o_list, lse_list, cg, ga, w_attn_out, w_o, ln1_g, ln1_b, wr_hi, wr_lo, b_route):
    bsz, s, d = h.shape
    tm = ROW_TILE
    row = lambda b, i: (b, i, 0)
    dil_specs = [pl.BlockSpec((None, r, tm // r, GROUP_W), lambda b, i: (b, 0, i, 0)) for r in DILATIONS]
    o_views = [o.reshape(bsz, r, s // r, GROUP_W) for o, r in zip(o_list, DILATIONS)]
    l_views = [l.reshape(bsz, r, s // r, GROUP_W) for l, r in zip(lse_list, DILATIONS)]
    return pl.pallas_call(
        _mix_kernel,
        grid=(bsz, s // tm),
        in_specs=[pl.BlockSpec((None, tm, d), row)] + dil_specs + dil_specs + [
            pl.BlockSpec((None, tm, d), row), pl.BlockSpec((None, tm, d), row),
            _resident(w_attn_out.shape), _resident(w_o.shape),
            _resident((1, d)), _resident((1, d)),
            _resident(wr_hi.shape), _resident(wr_lo.shape), _resident((1, LANES)),
        ],
        out_specs=[_chunk_major_spec(tm, lambda b, i: (0, b * (s // tm) + i, 0)),
                   pl.BlockSpec((None, tm, LANES), row),
                   pl.BlockSpec((1, LANES), lambda b, i: (0, 0))],
        out_shape=[jax.ShapeDtypeStruct((SC_CHUNKS, bsz * s, SC_COLS), F32),
                   jax.ShapeDtypeStruct((bsz, s, LANES), F32),
                   jax.ShapeDtypeStruct((1, LANES), F32)],
        scratch_shapes=[pltpu.VMEM((N_GROUPS * GROUP_W // LANES, tm, LANES), F32)] * 2
        + [pltpu.VMEM((tm, tm), BF16)],
        compiler_params=pltpu.CompilerParams(
            dimension_semantics=("arbitrary", "arbitrary"), vmem_limit_bytes=VMEM_LIMIT),
        name="mix",
    )(h, *o_views, *l_views, cg, ga, w_attn_out, w_o, ln1_g, ln1_b, wr_hi, wr_lo, b_route)


def _dispatch_plan(route, counts, t):
    eid = route[:, :TOP_K_INNER].astype(jnp.int32)
    rank = route[:, ROUTE_LANE_RANK:ROUTE_LANE_RANK + TOP_K_INNER]
    counts = counts[0, :N_EXPERTS].astype(jnp.int32)
    padded = (counts + EXPERT_BLOCK - 1) // EXPERT_BLOCK * EXPERT_BLOCK
    pad_end = jnp.cumsum(padded)
    pad_start = pad_end - padded
    experts = jnp.arange(N_EXPERTS, dtype=jnp.int32)
    start_of = jnp.sum(jnp.where(eid[..., None] == experts, pad_start, 0), axis=-1)
    dest = start_of + rank.astype(jnp.int32)
    n_blocks = -(-t * TOP_K_INNER // EXPERT_BLOCK) + N_EXPERTS
    n_slots = n_blocks * EXPERT_BLOCK
    block_start = jnp.arange(n_blocks, dtype=jnp.int32) * EXPERT_BLOCK
    block_e = jnp.minimum(jnp.sum((block_start[:, None] >= pad_end[None, :]).astype(jnp.int32), axis=1),
                          N_EXPERTS - 1)
    n_empty = n_slots - t * TOP_K_INNER
    gap_end = jnp.cumsum(padded - counts)
    gap_start = gap_end - (padded - counts)
    k = jnp.arange(n_empty, dtype=jnp.int32)
    in_gap = (k[:, None] >= gap_start) & (k[:, None] < gap_end)
    empty = jnp.where(k < gap_end[-1],
                      jnp.sum(jnp.where(in_gap, pad_start + counts + k[:, None] - gap_start, 0), axis=-1),
                      pad_end[-1] + k - gap_end[-1])
    return dest.astype(jnp.int32), empty.astype(jnp.int32), block_e


def _sc_mesh():
    return plsc.VectorSubcoreMesh(core_axis_name="core", subcore_axis_name="subcore")


def _sc_rows_pipeline(body, n_rows, row_index_map, cols):
    return pltpu.emit_pipeline(
        body,
        grid=(n_rows // SC_WINDOW,),
        in_specs=[pl.BlockSpec((SC_WINDOW, cols), index_map=row_index_map),
                  pl.BlockSpec((1, SC_WINDOW), index_map=lambda i: (0, i))],
        out_specs=[],
        core_axis_name=("core", "subcore"),
        dimension_semantics=(pltpu.PARALLEL,),
    )


def _chunk_rows(rows, n_rows_per_chunk):
    base = jnp.arange(SC_CHUNKS, dtype=jnp.int32)[:, None] * n_rows_per_chunk
    return (base + rows[None, :]).reshape(1, -1)


def _dispatch(h1, dest, empty, n_slots):
    t = h1.shape[1]
    x = h1.reshape(SC_CHUNKS * t, SC_COLS)
    idx_first = _chunk_rows(dest[:, 0], n_slots)
    idx_second = _chunk_rows(dest[:, 1], n_slots)
    idx_empty = _chunk_rows(empty, n_slots)
    zeros = jnp.zeros((SC_WINDOW, SC_COLS), F32)

    @pl.kernel(out_type=jax.ShapeDtypeStruct((SC_CHUNKS * n_slots, SC_COLS), F32), mesh=_sc_mesh(),
               scratch_types=[])
    def scatter(x_hbm, i0_hbm, i1_hbm, z_hbm, ie_hbm, o_hbm):
        def body(x_vmem, i_vmem):
            pltpu.sync_copy(x_vmem, o_hbm.at[i_vmem.at[0]])

        for i_hbm in (i0_hbm, i1_hbm):
            _sc_rows_pipeline(body, x.shape[0], lambda i: (i, 0), SC_COLS)(x_hbm, i_hbm)
        _sc_rows_pipeline(body, idx_empty.shape[1], lambda i: (0, 0), SC_COLS)(z_hbm, ie_hbm)

    return scatter(x, idx_first, idx_second, zeros, idx_empty).reshape(SC_CHUNKS, n_slots, SC_COLS)


def _expert_mlp(xb, wg_ref, wu_ref, wd_ref):
    acc = None
    for c in range(D_EXPERT // EXPERT_CHUNK):
        cols = slice(c * EXPERT_CHUNK, (c + 1) * EXPERT_CHUNK)
        gate = _dot(xb, wg_ref[:, cols])
        up = _dot(xb, wu_ref[:, cols])
        hidden = (gate * jax.nn.sigmoid(gate) * up).astype(BF16)
        part = _dot(hidden, wd_ref[cols, :])
        acc = part if acc is None else acc + part
    return acc


def _expert_kernel(be_ref, x_ref, wga_ref, wua_ref, wda_ref, wgb_ref, wub_ref, wdb_ref, y_ref):
    i = pl.program_id(0)
    same = be_ref[2 * i] == be_ref[2 * i + 1]

    @pl.when(same)
    def _():
        xb = _load_chunk_major(x_ref, 0, 2 * EXPERT_BLOCK).astype(BF16)
        _store_chunk_major(y_ref, 0, _expert_mlp(xb, wga_ref, wua_ref, wda_ref))

    @pl.when(jnp.logical_not(same))
    def _():
        for half, w in enumerate(((wga_ref, wua_ref, wda_ref), (wgb_ref, wub_ref, wdb_ref))):
            xb = _load_chunk_major(x_ref, half * EXPERT_BLOCK, EXPERT_BLOCK).astype(BF16)
            _store_chunk_major(y_ref, half * EXPERT_BLOCK, _expert_mlp(xb, *w))


def _experts(xs, block_e, w_gate, w_up, w_down):
    n_blocks = block_e.shape[0]
    assert n_blocks % 2 == 0
    d = D_MODEL
    blk = _chunk_major_spec(2 * EXPERT_BLOCK, lambda i, be: (0, i, 0))
    w_specs = [pl.BlockSpec(shape, (lambda i, be, k=k: (be[2 * i + k], 0, 0)))
               for k in range(2) for shape in ((None, d, D_EXPERT), (None, d, D_EXPERT), (None, D_EXPERT, d))]
    grid_spec = pltpu.PrefetchScalarGridSpec(
        num_scalar_prefetch=1,
        grid=(n_blocks // 2,),
        in_specs=[blk] + w_specs,
        out_specs=blk,
    )
    return pl.pallas_call(
        _expert_kernel,
        grid_spec=grid_spec,
        out_shape=jax.ShapeDtypeStruct(xs.shape, F32),
        compiler_params=pltpu.CompilerParams(
            dimension_semantics=("arbitrary",), vmem_limit_bytes=VMEM_LIMIT),
        name="experts",
    )(block_e, xs, w_gate, w_up, w_down, w_gate, w_up, w_down)


def _gather(yb, dest):
    n_slots = yb.shape[1]
    x = yb.reshape(SC_CHUNKS * n_slots, SC_COLS)
    idx = _chunk_rows(dest.T.reshape(-1), n_slots)
    m = idx.shape[1]

    @pl.kernel(out_type=jax.ShapeDtypeStruct((m, SC_COLS), F32), mesh=_sc_mesh(), scratch_types=[])
    def gather(x_hbm, i_hbm, o_hbm):
        def body(i_vmem, o_vmem):
            pltpu.sync_copy(x_hbm.at[i_vmem.at[0]], o_vmem)

        pltpu.emit_pipeline(
            body,
            grid=(m // SC_WINDOW,),
            in_specs=[pl.BlockSpec((1, SC_WINDOW), index_map=lambda i: (0, i))],
            out_specs=[pl.BlockSpec((SC_WINDOW, SC_COLS), index_map=lambda i: (i, 0))],
            core_axis_name=("core", "subcore"),
            dimension_semantics=(pltpu.PARALLEL,),
        )(i_hbm, o_hbm)

    return gather(x, idx).reshape(SC_CHUNKS, m // SC_CHUNKS, SC_COLS)


def _combine_kernel(first_ref, second_ref, h1_ref, route_ref, g_ref, b_ref, y_ref):
    tm = y_ref.shape[0]
    route = route_ref[...]
    w_first = route[:, ROUTE_LANE_W:ROUTE_LANE_W + 1]
    w_second = route[:, ROUTE_LANE_W + 1:ROUTE_LANE_W + 2]
    ffn = w_first * _load_chunk_major(first_ref, 0, tm) + w_second * _load_chunk_major(second_ref, 0, tm)
    y_ref[...] = _layer_norm(DEEPNORM_ALPHA * _load_chunk_major(h1_ref, 0, tm) + ffn, g_ref[...], b_ref[...])


def _combine(h1, route, g, ln2_g, ln2_b):
    t = route.shape[0]
    d = D_MODEL
    tm = ROW_TILE
    nt = t // tm
    return pl.pallas_call(
        _combine_kernel,
        grid=(nt,),
        in_specs=[
            _chunk_major_spec(tm, lambda i: (0, i, 0)),
            _chunk_major_spec(tm, lambda i: (0, nt + i, 0)),
            _chunk_major_spec(tm, lambda i: (0, i, 0)),
            pl.BlockSpec((tm, LANES), lambda i: (i, 0)),
            _resident((1, d)), _resident((1, d)),
        ],
        out_specs=pl.BlockSpec((tm, d), lambda i: (i, 0)),
        out_shape=jax.ShapeDtypeStruct((t, d), F32),
        compiler_params=pltpu.CompilerParams(
            dimension_semantics=("parallel",), vmem_limit_bytes=VMEM_LIMIT),
        name="combine",
    )(g, g, h1, route, ln2_g, ln2_b)


def _token_mixing(x, p):
    bsz, s, d = x.shape
    qkv0, qkv1, qkv2, cg, ga, h = _inproj(x, p['ln_in_g'], p['ln_in_b'], p['w_in'], p['b_gate'], p['conv_w'],
                                          p['w_conv_out'])
    o_list, lse_list = [], []
    for qkv, r in zip((qkv0, qkv1, qkv2), DILATIONS):
        o, lse = _attention(qkv.reshape(3, bsz, s, GROUP_W), s // r)
        o_list.append(o)
        lse_list.append(lse)
    h1, route, counts = _mix(h, o_list, lse_list, cg, ga, p['w_attn_out'], p['w_o'],
                             p['ln1_g'], p['ln1_b'], p['wr_hi'], p['wr_lo'], p['b_route'])
    t = bsz * s
    route = route.reshape(t, LANES)
    dest, empty, block_e = _dispatch_plan(route, counts, t)
    return h1, route, dest, empty, block_e


def kernel(x_prompt, x_sample, ln_in_g, ln_in_b, w_in, b_gate, conv_w, w_attn_out, w_conv_out, w_o, ln1_g, ln1_b, w_route_group, b_route_group, w_route_expert, b_route_expert, w_gate, w_up, w_down, ln2_g, ln2_b):
    d = D_MODEL
    w_route = jnp.concatenate([w_route_group[0], w_route_expert[0]], axis=1)
    w_route = jnp.pad(w_route, ((0, 0), (0, LANES - w_route.shape[1])))
    wr_hi = w_route.astype(BF16)
    b_route = jnp.concatenate([b_route_group[0], b_route_expert[0]]).astype(F32)
    p = {
        'ln_in_g': ln_in_g.reshape(1, d), 'ln_in_b': ln_in_b.reshape(1, d),
        'w_in': w_in[0].astype(BF16), 'b_gate': b_gate[0].reshape(1, 2 * d), 'conv_w': conv_w[0],
        'w_conv_out': w_conv_out[0].astype(BF16), 'w_attn_out': w_attn_out[0].astype(BF16),
        'w_o': w_o[0].astype(BF16),
        'ln1_g': ln1_g[0].reshape(1, d), 'ln1_b': ln1_b[0].reshape(1, d),
        'wr_hi': wr_hi, 'wr_lo': (w_route - wr_hi.astype(F32)).astype(BF16),
        'b_route': jnp.pad(b_route, (0, LANES - b_route.shape[0])).reshape(1, LANES),
        'w_gate': w_gate[0].astype(BF16), 'w_up': w_up[0].astype(BF16), 'w_down': w_down[0].astype(BF16),
        'ln2_g': ln2_g[0].reshape(1, d), 'ln2_b': ln2_b[0].reshape(1, d),
    }
    batches = [_token_mixing(x, p) for x in (x_prompt, x_sample)]
    sorted_rows = [_dispatch(h1, dest, empty, block_e.shape[0] * EXPERT_BLOCK)
                   for h1, _, dest, empty, block_e in batches]
    gathered = [_gather(_experts(xs, b[4], p['w_gate'], p['w_up'], p['w_down']), b[2])
                for xs, b in zip(sorted_rows, batches)]
    outs = [_combine(b[0], b[1], g, p['ln2_g'], p['ln2_b']).reshape(x.shape)
            for b, g, x in zip(batches, gathered, (x_prompt, x_sample))]
    return tuple(outs)
```

```python
import functools

import jax
import jax.numpy as jnp
import numpy as np
from jax import lax
from jax.experimental import pallas as pl
from jax.experimental.pallas import tpu as pltpu
from jax.experimental.pallas import tpu_sc as plsc

F32 = jnp.float32
BF16 = jnp.bfloat16

D_MODEL = 1024
HEAD_DIM = 64
HEADS_PER_GROUP = 4
GROUP_W = HEADS_PER_GROUP * HEAD_DIM
DILATIONS = (1, 4, 16)
BAND = 64
N_GROUPS = len(DILATIONS)
D_ATTN = N_GROUPS * GROUP_W
ATTN_SCALE = HEAD_DIM ** -0.5
ROPE_DIM = HEAD_DIM // 4
ROPE_THETA = 500000.0
MASK_VALUE = -1e30
D_CONV = D_MODEL
COL_CONV_B = 3 * D_ATTN
COL_CONV_C = COL_CONV_B + D_CONV
COL_CONV_H = COL_CONV_C + D_CONV
COL_GATE_ATTN = COL_CONV_H + D_CONV
COL_GATE_CONV = COL_GATE_ATTN + D_MODEL
N_EXPERT_GROUPS = 4
EXPERTS_PER_GROUP = 8
N_EXPERTS = N_EXPERT_GROUPS * EXPERTS_PER_GROUP
TOP_K_INNER = 2
D_EXPERT = 512
EXPERT_BLOCK = 256
EXPERT_CHUNK = 256
LN_EPS = 1e-5
DEPTH = 1
DEEPNORM_ALPHA = (2 * DEPTH) ** 0.25

LANES = 128
ROW_TILE = 512
HALO = 16
Q_BLOCK = 128
K_BLOCK = Q_BLOCK + 2 * BAND
Q_CHUNK = 2048
SC_COLS = 256
SC_WINDOW = 128
F32_CHUNKS = D_MODEL // SC_COLS
PACKED_COLS = D_MODEL // 2
PACKED_CHUNKS = PACKED_COLS // SC_COLS
U32 = jnp.uint32
HIGH_HALF = np.uint32(0xFFFF0000)
ROUTE_LANE0 = N_EXPERT_GROUPS
ROUTE_LANE_W = TOP_K_INNER
ROUTE_LANE_RANK = 2 * TOP_K_INNER
VMEM_LIMIT = 56 * 1024 * 1024


def _layer_norm(v, g, b):
    mu = jnp.mean(v, axis=-1, keepdims=True)
    d = v - mu
    var = jnp.mean(d * d, axis=-1, keepdims=True)
    return d * lax.rsqrt(var + LN_EPS) * g + b


def _dot(a, b):
    return jnp.dot(a, b, preferred_element_type=F32)


def _store_chunk_major(ref, first_row, val):
    for c in range(ref.shape[0]):
        ref[c, pl.ds(first_row, val.shape[0]), :] = val[:, c * SC_COLS:(c + 1) * SC_COLS]


def _load_chunk_major(ref, first_row, n):
    return jnp.concatenate([ref[c, pl.ds(first_row, n), :] for c in range(ref.shape[0])], axis=1)


def _chunk_major_spec(chunks, rows, index_map):
    return pl.BlockSpec((chunks, rows, SC_COLS), index_map)


def _pack_bf16_pairs(val):
    bits = lax.bitcast_convert_type(val.astype(BF16).astype(F32), U32)
    return (bits[:, PACKED_COLS:] & HIGH_HALF) | (bits[:, :PACKED_COLS] >> 16)


def _unpack_bf16_pairs(packed):
    low = lax.bitcast_convert_type(packed << 16, F32)
    high = lax.bitcast_convert_type(packed & HIGH_HALF, F32)
    return jnp.concatenate([low, high], axis=1)


def _resident(shape):
    nd = len(shape)
    return pl.BlockSpec(shape, lambda *_: (0,) * nd, pipeline_mode=pl.Buffered(1))


def _inproj_kernel(x_ref, xp_ref, xn_ref, g_ref, b_ref, win_ref, bg_ref, cw_ref, wco_ref,
                   cos_ref, sa_ref, sb_ref,
                   qkv0_ref, qkv1_ref, qkv2_ref, cg_ref, ga_ref, h_ref,
                   hext_ref, u_ref, stage_ref):
    i = pl.program_id(1)
    nt = pl.num_programs(1)
    tm = x_ref.shape[0]
    g = g_ref[...]
    b = b_ref[...]
    h = _layer_norm(x_ref[...], g, b)
    h_ref[...] = h
    hext_ref[pl.ds(HALO, tm), :] = h.astype(BF16)
    hext_ref[pl.ds(0, HALO), :] = _layer_norm(xp_ref[...], g, b).astype(BF16)
    hext_ref[pl.ds(HALO + tm, HALO), :] = _layer_norm(xn_ref[...], g, b).astype(BF16)
    hm = hext_ref[pl.ds(HALO, tm), :]
    he = hext_ref[...]

    def proj(lhs, col, n):
        return _dot(lhs, win_ref[:, col:col + n])

    u = proj(he, COL_CONV_C, D_CONV) * proj(he, COL_CONV_H, D_CONV)
    rows = lax.broadcasted_iota(jnp.int32, (tm + 2 * HALO, 1), 0)
    lo = jnp.where(i == 0, HALO, 0)
    hi = jnp.where(i == nt - 1, HALO + tm, tm + 2 * HALO)
    u_ref[...] = jnp.where((rows >= lo) & (rows < hi), u, 0.0)
    cw = cw_ref[...]
    conv = (cw[0:1] * u_ref[pl.ds(HALO - 1, tm), :] + cw[1:2] * u_ref[pl.ds(HALO, tm), :]
            + cw[2:3] * u_ref[pl.ds(HALO + 1, tm), :])
    cb = proj(hm, COL_CONV_B, D_CONV)
    conv_branch = _dot((cb * conv).astype(BF16), wco_ref[...])
    gate_conv = jax.nn.sigmoid(proj(hm, COL_GATE_CONV, D_MODEL) + bg_ref[:, D_MODEL:])
    cg_ref[...] = (gate_conv * conv_branch).astype(BF16)
    ga_ref[...] = jax.nn.sigmoid(proj(hm, COL_GATE_ATTN, D_MODEL) + bg_ref[:, :D_MODEL]).astype(BF16)

    cosv = cos_ref[...]
    sav = sa_ref[...]
    sbv = sb_ref[...]
    half = ROPE_DIM // 2
    for gi, (out_ref, r) in enumerate(zip((qkv0_ref, qkv1_ref, qkv2_ref), DILATIONS)):
        for t in range(3):
            p = proj(hm, t * D_ATTN + gi * GROUP_W, GROUP_W)
            if t < 2:
                parts = []
                for c in range(GROUP_W // LANES):
                    pc = p[:, c * LANES:(c + 1) * LANES]
                    parts.append(pc * cosv + pltpu.roll(pc, LANES - half, 1) * sav
                                 + pltpu.roll(pc, half, 1) * sbv)
                p = jnp.concatenate(parts, axis=1)
                if t == 0:
                    p = p * ATTN_SCALE
            if r == 1:
                out_ref[t, 0] = p.astype(BF16)
            else:
                for c in range(GROUP_W // LANES):
                    stage_ref[c] = p[:, c * LANES:(c + 1) * LANES]
                for res in range(r):
                    for c in range(GROUP_W // LANES):
                        out_ref[t, res, :, c * LANES:(c + 1) * LANES] = (
                            stage_ref[c, pl.ds(res, tm // r, stride=r), :].astype(BF16))


def _rope_tables(s):
    half = ROPE_DIM // 2
    inv_freq = (np.float32(ROPE_THETA) ** (-np.arange(half, dtype=np.float32) * np.float32(2.0) / ROPE_DIM))
    ang = np.arange(s, dtype=np.float32)[:, None] * inv_freq.astype(np.float32)[None, :]
    cos = np.cos(ang).astype(np.float32)
    sin = np.sin(ang).astype(np.float32)
    pad = np.zeros((s, HEAD_DIM - ROPE_DIM), np.float32)
    zero = np.zeros((s, half), np.float32)
    cos_t = np.concatenate([cos, cos, pad + 1.0], axis=1)
    sa_t = np.concatenate([-sin, zero, pad], axis=1)
    sb_t = np.concatenate([zero, sin, pad], axis=1)
    rep = LANES // HEAD_DIM
    return tuple(jnp.asarray(np.tile(t, (1, rep))) for t in (cos_t, sa_t, sb_t))


def _inproj(x, ln_g, ln_b, w_in, b_gate, conv_w, w_conv_out):
    bsz, s, d = x.shape
    tm = ROW_TILE
    nt = s // tm
    hpt = tm // HALO
    cos_t, sa_t, sb_t = _rope_tables(s)
    row = lambda b, i: (b, i, 0)
    tab = pl.BlockSpec((tm, LANES), lambda b, i: (i, 0))
    qkv_shapes = [jax.ShapeDtypeStruct((3, bsz, r, s // r, GROUP_W), BF16) for r in DILATIONS]
    qkv_specs = [pl.BlockSpec((3, None, r, tm // r, GROUP_W), lambda b, i: (0, b, 0, i, 0))
                 for r in DILATIONS]
    return pl.pallas_call(
        _inproj_kernel,
        grid=(bsz, nt),
        in_specs=[
            pl.BlockSpec((None, tm, d), row),
            pl.BlockSpec((None, HALO, d), lambda b, i: (b, jnp.maximum(i * hpt - 1, 0), 0)),
            pl.BlockSpec((None, HALO, d), lambda b, i: (b, jnp.minimum((i + 1) * hpt, s // HALO - 1), 0)),
            _resident((1, d)), _resident((1, d)),
            _resident(w_in.shape), _resident((1, 2 * d)), _resident(conv_w.shape),
            _resident(w_conv_out.shape),
            tab, tab, tab,
        ],
        out_specs=qkv_specs + [pl.BlockSpec((None, tm, d), row)] * 3,
        out_shape=(qkv_shapes + [jax.ShapeDtypeStruct((bsz, s, d), BF16)] * 2
                   + [jax.ShapeDtypeStruct((bsz, s, d), F32)]),
        scratch_shapes=[
            pltpu.VMEM((tm + 2 * HALO, d), BF16),
            pltpu.VMEM((tm + 2 * HALO, D_CONV), F32),
            pltpu.VMEM((GROUP_W // LANES, tm, LANES), F32),
        ],
        compiler_params=pltpu.CompilerParams(
            dimension_semantics=("parallel", "parallel"), vmem_limit_bytes=VMEM_LIMIT),
        name="inproj",
    )(x, x, x, ln_g, ln_b, w_in, b_gate, conv_w, w_conv_out, cos_t, sa_t, sb_t)


def _attn_kernel(q_ref, k_ref, v_ref, o_ref, lse_ref, *, sub_len):
    s = k_ref.shape[0]
    qc = q_ref.shape[0]
    base = pl.program_id(1) * qc
    lane = lax.broadcasted_iota(jnp.int32, (1, GROUP_W), 1)
    head_masks = [(lane >= h * HEAD_DIM) & (lane < (h + 1) * HEAD_DIM) for h in range(HEADS_PER_GROUP)]
    qi = lax.broadcasted_iota(jnp.int32, (Q_BLOCK, 1), 0)
    kj = lax.broadcasted_iota(jnp.int32, (1, K_BLOCK), 1)

    def body(j, carry):
        r0 = pl.multiple_of(j * Q_BLOCK, Q_BLOCK)
        s0 = base + r0
        k0 = pl.multiple_of(jnp.clip(s0 - BAND, 0, s - K_BLOCK), BAND)
        q = q_ref[pl.ds(r0, Q_BLOCK), :]
        k = k_ref[pl.ds(k0, K_BLOCK), :]
        v = v_ref[pl.ds(k0, K_BLOCK), :]
        sub_lo = (s0 // sub_len) * sub_len
        qpos = s0 + qi
        kpos = k0 + kj
        valid = (jnp.abs(qpos - kpos) <= BAND) & (kpos >= sub_lo) & (kpos < sub_lo + sub_len)
        qm = jnp.concatenate([jnp.where(hm, q, jnp.zeros_like(q)) for hm in head_masks], axis=0)
        sc = lax.dot_general(qm, k, (((1,), (1,)), ((), ())), preferred_element_type=F32)
        sc = jnp.where(jnp.concatenate([valid] * HEADS_PER_GROUP, axis=0), sc, MASK_VALUE)
        m = jnp.max(sc, axis=-1, keepdims=True)
        p = jnp.exp(sc - m)
        den = jnp.sum(p, axis=-1, keepdims=True)
        o_all = _dot((p * (1.0 / den)).astype(BF16), v)
        lse_all = m + jnp.log(den)
        o_acc = jnp.zeros((Q_BLOCK, GROUP_W), F32)
        lse_acc = jnp.zeros((Q_BLOCK, GROUP_W), F32)
        for h, hm in enumerate(head_masks):
            rows = slice(h * Q_BLOCK, (h + 1) * Q_BLOCK)
            o_acc = jnp.where(hm, o_all[rows], o_acc)
            lse_acc = jnp.where(hm, lse_all[rows], lse_acc)
        o_ref[pl.ds(r0, Q_BLOCK), :] = o_acc.astype(BF16)
        lse_ref[pl.ds(r0, Q_BLOCK), :] = lse_acc
        return carry

    lax.fori_loop(0, qc // Q_BLOCK, body, 0, unroll=8)


def _attention(qkv, sub_len):
    _, bsz, s, w = qkv.shape
    qc = min(Q_CHUNK, s)
    kv_spec = lambda t: pl.BlockSpec((None, None, s, w), lambda b, j: (t, b, 0, 0))
    return pl.pallas_call(
        functools.partial(_attn_kernel, sub_len=sub_len),
        grid=(bsz, s // qc),
        in_specs=[pl.BlockSpec((None, None, qc, w), lambda b, j: (0, b, j, 0)), kv_spec(1), kv_spec(2)],
        out_specs=[pl.BlockSpec((None, qc, w), lambda b, j: (b, j, 0))] * 2,
        out_shape=[jax.ShapeDtypeStruct((bsz, s, w), BF16), jax.ShapeDtypeStruct((bsz, s, w), F32)],
        compiler_params=pltpu.CompilerParams(
            dimension_semantics=("parallel", "arbitrary"), vmem_limit_bytes=VMEM_LIMIT),
        name="attn",
    )(qkv, qkv, qkv)


def _first_index_of_max(vals, lane_f):
    mx = jnp.max(vals, axis=-1, keepdims=True)
    idx = jnp.min(jnp.where(vals == mx, lane_f, float(LANES)), axis=-1, keepdims=True)
    return mx, idx


def _mix_kernel(h_ref, o0_ref, o1_ref, o2_ref, l0_ref, l1_ref, l2_ref, cg_ref, ga_ref,
                wao_ref, wo_ref, g1_ref, b1_ref, wrh_ref, wrl_ref, br_ref,
                h1_ref, h1p_ref, route_ref, counts_ref, so_ref, sl_ref, tri_ref):
    tm = h_ref.shape[0]
    halves = GROUP_W // LANES
    for gi, (o_ref, l_ref, r) in enumerate(zip((o0_ref, o1_ref, o2_ref), (l0_ref, l1_ref, l2_ref), DILATIONS)):
        for c in range(halves):
            cols = slice(c * LANES, (c + 1) * LANES)
            for res in range(r):
                rows = pl.ds(res, tm // r, stride=r) if r > 1 else slice(None)
                so_ref[gi * halves + c, rows, :] = o_ref[res, :, cols].astype(F32)
                sl_ref[gi * halves + c, rows, :] = l_ref[res, :, cols]

    def natural(ref, gi):
        return jnp.concatenate([ref[gi * halves + c] for c in range(halves)], axis=1)

    lses = [natural(sl_ref, gi) for gi in range(N_GROUPS)]
    mx = jnp.maximum(jnp.maximum(lses[0], lses[1]), lses[2])
    es = [jnp.exp(l - mx) for l in lses]
    inv_den = 1.0 / (es[0] + es[1] + es[2])
    attn = jnp.concatenate(
        [(natural(so_ref, gi) * (es[gi] * inv_den)).astype(BF16) for gi in range(N_GROUPS)], axis=1)
    attn_branch = _dot(attn, wao_ref[...])
    merged = ga_ref[...].astype(F32) * attn_branch + cg_ref[...].astype(F32)
    mix = _dot(merged.astype(BF16), wo_ref[...])
    h1 = _layer_norm(DEEPNORM_ALPHA * h_ref[...] + mix, g1_ref[...], b1_ref[...])
    _store_chunk_major(h1_ref, 0, h1)
    _store_chunk_major(h1p_ref, 0, _pack_bf16_pairs(h1))

    hi = h1.astype(BF16)
    lo = (h1 - hi.astype(F32)).astype(BF16)
    wrh = wrh_ref[...]
    logits = _dot(hi, wrh) + _dot(lo, wrh) + _dot(hi, wrl_ref[...]) + br_ref[...]
    lane = lax.broadcasted_iota(jnp.int32, (1, LANES), 1)
    lane_f = lane.astype(F32)
    is_grp = lane < N_EXPERT_GROUPS
    gl = jnp.where(is_grp, logits, MASK_VALUE)
    ge = jnp.exp(gl - jnp.max(gl, axis=-1, keepdims=True))
    gp = jnp.where(is_grp, ge / jnp.sum(ge, axis=-1, keepdims=True), -1.0)
    grp_w, grp_idx = _first_index_of_max(gp, lane_f)
    e_lo = ROUTE_LANE0 + grp_idx * EXPERTS_PER_GROUP
    sel = jnp.where((lane_f >= e_lo) & (lane_f < e_lo + EXPERTS_PER_GROUP), logits, MASK_VALUE)
    v1, i1 = _first_index_of_max(sel, lane_f)
    v2, i2 = _first_index_of_max(jnp.where(lane_f == i1, MASK_VALUE, sel), lane_f)
    e2 = jnp.exp(v2 - v1)
    w1 = grp_w / (1.0 + e2)
    w2 = grp_w * e2 / (1.0 + e2)

    first = (pl.program_id(0) == 0) & (pl.program_id(1) == 0)

    @pl.when(first)
    def _():
        counts_ref[...] = jnp.zeros_like(counts_ref)
        r = lax.broadcasted_iota(jnp.int32, (tm, tm), 0)
        c = lax.broadcasted_iota(jnp.int32, (tm, tm), 1)
        tri_ref[...] = (c < r).astype(BF16)

    oh0 = lane_f == i1 - ROUTE_LANE0
    oh1 = lane_f == i2 - ROUTE_LANE0
    oh = oh0.astype(F32) + oh1.astype(F32)
    before = counts_ref[...] + _dot(tri_ref[...], oh.astype(BF16))
    r0 = jnp.sum(jnp.where(oh0, before, 0.0), axis=-1, keepdims=True)
    r1 = jnp.sum(jnp.where(oh1, before + oh0.astype(F32), 0.0), axis=-1, keepdims=True)
    counts_ref[...] = counts_ref[...] + jnp.sum(oh, axis=0, keepdims=True)

    per_lane = (i1 - ROUTE_LANE0, i2 - ROUTE_LANE0, w1, w2, r0, r1)
    route = jnp.zeros((tm, LANES), F32)
    for k, v in enumerate(per_lane):
        route = jnp.where(lane == k, v, route)
    route_ref[...] = route


def _mix(h, o_list, lse_list, cg, ga, w_attn_out, w_o, ln1_g, ln1_b, wr_hi, wr_lo, b_route):
    bsz, s, d = h.shape
    tm = ROW_TILE
    row = lambda b, i: (b, i, 0)
    dil_specs = [pl.BlockSpec((None, r, tm // r, GROUP_W), lambda b, i: (b, 0, i, 0)) for r in DILATIONS]
    o_views = [o.reshape(bsz, r, s // r, GROUP_W) for o, r in zip(o_list, DILATIONS)]
    l_views = [l.reshape(bsz, r, s // r, GROUP_W) for l, r in zip(lse_list, DILATIONS)]
    return pl.pallas_call(
        _mix_kernel,
        grid=(bsz, s // tm),
        in_specs=[pl.BlockSpec((None, tm, d), row)] + dil_specs + dil_specs + [
            pl.BlockSpec((None, tm, d), row), pl.BlockSpec((None, tm, d), row),
            _resident(w_attn_out.shape), _resident(w_o.shape),
            _resident((1, d)), _resident((1, d)),
            _resident(wr_hi.shape), _resident(wr_lo.shape), _resident((1, LANES)),
        ],
        out_specs=[_chunk_major_spec(F32_CHUNKS, tm, lambda b, i: (0, b * (s // tm) + i, 0)),
                   _chunk_major_spec(PACKED_CHUNKS, tm, lambda b, i: (0, b * (s // tm) + i, 0)),
                   pl.BlockSpec((None, tm, LANES), row),
                   pl.BlockSpec((1, LANES), lambda b, i: (0, 0))],
        out_shape=[jax.ShapeDtypeStruct((F32_CHUNKS, bsz * s, SC_COLS), F32),
                   jax.ShapeDtypeStruct((PACKED_CHUNKS, bsz * s, SC_COLS), U32),
                   jax.ShapeDtypeStruct((bsz, s, LANES), F32),
                   jax.ShapeDtypeStruct((1, LANES), F32)],
        scratch_shapes=[pltpu.VMEM((N_GROUPS * GROUP_W // LANES, tm, LANES), F32)] * 2
        + [pltpu.VMEM((tm, tm), BF16)],
        compiler_params=pltpu.CompilerParams(
            dimension_semantics=("arbitrary", "arbitrary"), vmem_limit_bytes=VMEM_LIMIT),
        name="mix",
    )(h, *o_views, *l_views, cg, ga, w_attn_out, w_o, ln1_g, ln1_b, wr_hi, wr_lo, b_route)


def _dispatch_plan(route, counts, t):
    eid = route[:, :TOP_K_INNER].astype(jnp.int32)
    rank = route[:, ROUTE_LANE_RANK:ROUTE_LANE_RANK + TOP_K_INNER]
    counts = counts[0, :N_EXPERTS].astype(jnp.int32)
    padded = (counts + EXPERT_BLOCK - 1) // EXPERT_BLOCK * EXPERT_BLOCK
    pad_end = jnp.cumsum(padded)
    pad_start = pad_end - padded
    experts = jnp.arange(N_EXPERTS, dtype=jnp.int32)
    start_of = jnp.sum(jnp.where(eid[..., None] == experts, pad_start, 0), axis=-1)
    dest = start_of + rank.astype(jnp.int32)
    n_blocks = -(-t * TOP_K_INNER // EXPERT_BLOCK) + N_EXPERTS
    n_slots = n_blocks * EXPERT_BLOCK
    block_start = jnp.arange(n_blocks, dtype=jnp.int32) * EXPERT_BLOCK
    block_e = jnp.minimum(jnp.sum((block_start[:, None] >= pad_end[None, :]).astype(jnp.int32), axis=1),
                          N_EXPERTS - 1)
    n_empty = n_slots - t * TOP_K_INNER
    gap_end = jnp.cumsum(padded - counts)
    gap_start = gap_end - (padded - counts)
    k = jnp.arange(n_empty, dtype=jnp.int32)
    in_gap = (k[:, None] >= gap_start) & (k[:, None] < gap_end)
    empty = jnp.where(k < gap_end[-1],
                      jnp.sum(jnp.where(in_gap, pad_start + counts + k[:, None] - gap_start, 0), axis=-1),
                      pad_end[-1] + k - gap_end[-1])
    return dest.astype(jnp.int32), empty.astype(jnp.int32), block_e


def _sc_mesh():
    return plsc.VectorSubcoreMesh(core_axis_name="core", subcore_axis_name="subcore")


def _sc_rows_pipeline(body, n_rows, row_index_map, cols):
    return pltpu.emit_pipeline(
        body,
        grid=(n_rows // SC_WINDOW,),
        in_specs=[pl.BlockSpec((SC_WINDOW, cols), index_map=row_index_map),
                  pl.BlockSpec((1, SC_WINDOW), index_map=lambda i: (0, i))],
        out_specs=[],
        core_axis_name=("core", "subcore"),
        dimension_semantics=(pltpu.PARALLEL,),
    )


def _chunk_rows(rows, chunks, n_rows_per_chunk):
    base = jnp.arange(chunks, dtype=jnp.int32)[:, None] * n_rows_per_chunk
    return (base + rows[None, :]).reshape(1, -1)


def _dispatch(rows, dest, empty, n_slots):
    chunks, t, _ = rows.shape
    x = rows.reshape(chunks * t, SC_COLS)
    idx_first = _chunk_rows(dest[:, 0], chunks, n_slots)
    idx_second = _chunk_rows(dest[:, 1], chunks, n_slots)
    idx_empty = _chunk_rows(empty, chunks, n_slots)
    zeros = jnp.zeros((SC_WINDOW, SC_COLS), rows.dtype)

    @pl.kernel(out_type=jax.ShapeDtypeStruct((chunks * n_slots, SC_COLS), rows.dtype), mesh=_sc_mesh(),
               scratch_types=[])
    def scatter(x_hbm, i0_hbm, i1_hbm, z_hbm, ie_hbm, o_hbm):
        def body(x_vmem, i_vmem):
            pltpu.sync_copy(x_vmem, o_hbm.at[i_vmem.at[0]])

        for i_hbm in (i0_hbm, i1_hbm):
            _sc_rows_pipeline(body, x.shape[0], lambda i: (i, 0), SC_COLS)(x_hbm, i_hbm)
        _sc_rows_pipeline(body, idx_empty.shape[1], lambda i: (0, 0), SC_COLS)(z_hbm, ie_hbm)

    return scatter(x, idx_first, idx_second, zeros, idx_empty).reshape(chunks, n_slots, SC_COLS)


def _expert_mlp(xb, wg_ref, wu_ref, wd_ref):
    acc = None
    for c in range(D_EXPERT // EXPERT_CHUNK):
        cols = slice(c * EXPERT_CHUNK, (c + 1) * EXPERT_CHUNK)
        gate = _dot(xb, wg_ref[:, cols])
        up = _dot(xb, wu_ref[:, cols])
        hidden = (gate * jax.nn.sigmoid(gate) * up).astype(BF16)
        part = _dot(hidden, wd_ref[cols, :])
        acc = part if acc is None else acc + part
    return acc


def _expert_kernel(be_ref, x_ref, wga_ref, wua_ref, wda_ref, wgb_ref, wub_ref, wdb_ref, y_ref):
    i = pl.program_id(0)
    same = be_ref[2 * i] == be_ref[2 * i + 1]

    @pl.when(same)
    def _():
        xb = _unpack_bf16_pairs(_load_chunk_major(x_ref, 0, 2 * EXPERT_BLOCK)).astype(BF16)
        _store_chunk_major(y_ref, 0, _pack_bf16_pairs(_expert_mlp(xb, wga_ref, wua_ref, wda_ref)))

    @pl.when(jnp.logical_not(same))
    def _():
        for half, w in enumerate(((wga_ref, wua_ref, wda_ref), (wgb_ref, wub_ref, wdb_ref))):
            xb = _unpack_bf16_pairs(_load_chunk_major(x_ref, half * EXPERT_BLOCK, EXPERT_BLOCK)).astype(BF16)
            _store_chunk_major(y_ref, half * EXPERT_BLOCK, _pack_bf16_pairs(_expert_mlp(xb, *w)))


def _experts(xs, block_e, w_gate, w_up, w_down):
    n_blocks = block_e.shape[0]
    assert n_blocks % 2 == 0
    d = D_MODEL
    blk = _chunk_major_spec(PACKED_CHUNKS, 2 * EXPERT_BLOCK, lambda i, be: (0, i, 0))
    w_specs = [pl.BlockSpec(shape, (lambda i, be, k=k: (be[2 * i + k], 0, 0)))
               for k in range(2) for shape in ((None, d, D_EXPERT), (None, d, D_EXPERT), (None, D_EXPERT, d))]
    grid_spec = pltpu.PrefetchScalarGridSpec(
        num_scalar_prefetch=1,
        grid=(n_blocks // 2,),
        in_specs=[blk] + w_specs,
        out_specs=blk,
    )
    return pl.pallas_call(
        _expert_kernel,
        grid_spec=grid_spec,
        out_shape=jax.ShapeDtypeStruct(xs.shape, xs.dtype),
        compiler_params=pltpu.CompilerParams(
            dimension_semantics=("arbitrary",), vmem_limit_bytes=VMEM_LIMIT),
        name="experts",
    )(block_e, xs, w_gate, w_up, w_down, w_gate, w_up, w_down)


def _gather(yb, dest):
    chunks, n_slots, _ = yb.shape
    x = yb.reshape(chunks * n_slots, SC_COLS)
    idx = _chunk_rows(dest.T.reshape(-1), chunks, n_slots)
    m = idx.shape[1]

    @pl.kernel(out_type=jax.ShapeDtypeStruct((m, SC_COLS), yb.dtype), mesh=_sc_mesh(), scratch_types=[])
    def gather(x_hbm, i_hbm, o_hbm):
        def body(i_vmem, o_vmem):
            pltpu.sync_copy(x_hbm.at[i_vmem.at[0]], o_vmem)

        pltpu.emit_pipeline(
            body,
            grid=(m // SC_WINDOW,),
            in_specs=[pl.BlockSpec((1, SC_WINDOW), index_map=lambda i: (0, i))],
            out_specs=[pl.BlockSpec((SC_WINDOW, SC_COLS), index_map=lambda i: (i, 0))],
            core_axis_name=("core", "subcore"),
            dimension_semantics=(pltpu.PARALLEL,),
        )(i_hbm, o_hbm)

    return gather(x, idx).reshape(chunks, m // chunks, SC_COLS)


def _combine_kernel(first_ref, second_ref, h1_ref, route_ref, g_ref, b_ref, y_ref):
    tm = y_ref.shape[0]
    route = route_ref[...]
    w_first = route[:, ROUTE_LANE_W:ROUTE_LANE_W + 1]
    w_second = route[:, ROUTE_LANE_W + 1:ROUTE_LANE_W + 2]
    ffn = (w_first * _unpack_bf16_pairs(_load_chunk_major(first_ref, 0, tm))
           + w_second * _unpack_bf16_pairs(_load_chunk_major(second_ref, 0, tm)))
    y_ref[...] = _layer_norm(DEEPNORM_ALPHA * _load_chunk_major(h1_ref, 0, tm) + ffn, g_ref[...], b_ref[...])


def _combine(h1, route, g, ln2_g, ln2_b):
    t = route.shape[0]
    d = D_MODEL
    tm = ROW_TILE
    nt = t // tm
    return pl.pallas_call(
        _combine_kernel,
        grid=(nt,),
        in_specs=[
            _chunk_major_spec(PACKED_CHUNKS, tm, lambda i: (0, i, 0)),
            _chunk_major_spec(PACKED_CHUNKS, tm, lambda i: (0, nt + i, 0)),
            _chunk_major_spec(F32_CHUNKS, tm, lambda i: (0, i, 0)),
            pl.BlockSpec((tm, LANES), lambda i: (i, 0)),
            _resident((1, d)), _resident((1, d)),
        ],
        out_specs=pl.BlockSpec((tm, d), lambda i: (i, 0)),
        out_shape=jax.ShapeDtypeStruct((t, d), F32),
        compiler_params=pltpu.CompilerParams(
            dimension_semantics=("parallel",), vmem_limit_bytes=VMEM_LIMIT),
        name="combine",
    )(g, g, h1, route, ln2_g, ln2_b)


def _token_mixing(x, p):
    bsz, s, d = x.shape
    qkv0, qkv1, qkv2, cg, ga, h = _inproj(x, p['ln_in_g'], p['ln_in_b'], p['w_in'], p['b_gate'], p['conv_w'],
                                          p['w_conv_out'])
    o_list, lse_list = [], []
    for qkv, r in zip((qkv0, qkv1, qkv2), DILATIONS):
        o, lse = _attention(qkv.reshape(3, bsz, s, GROUP_W), s // r)
        o_list.append(o)
        lse_list.append(lse)
    h1, h1_packed, route, counts = _mix(h, o_list, lse_list, cg, ga, p['w_attn_out'], p['w_o'],
                             p['ln1_g'], p['ln1_b'], p['wr_hi'], p['wr_lo'], p['b_route'])
    t = bsz * s
    route = route.reshape(t, LANES)
    dest, empty, block_e = _dispatch_plan(route, counts, t)
    return h1, route, dest, empty, block_e, h1_packed


def kernel(x_prompt, x_sample, ln_in_g, ln_in_b, w_in, b_gate, conv_w, w_attn_out, w_conv_out, w_o, ln1_g, ln1_b, w_route_group, b_route_group, w_route_expert, b_route_expert, w_gate, w_up, w_down, ln2_g, ln2_b):
    d = D_MODEL
    w_route = jnp.concatenate([w_route_group[0], w_route_expert[0]], axis=1)
    w_route = jnp.pad(w_route, ((0, 0), (0, LANES - w_route.shape[1])))
    wr_hi = w_route.astype(BF16)
    b_route = jnp.concatenate([b_route_group[0], b_route_expert[0]]).astype(F32)
    p = {
        'ln_in_g': ln_in_g.reshape(1, d), 'ln_in_b': ln_in_b.reshape(1, d),
        'w_in': w_in[0].astype(BF16), 'b_gate': b_gate[0].reshape(1, 2 * d), 'conv_w': conv_w[0],
        'w_conv_out': w_conv_out[0].astype(BF16), 'w_attn_out': w_attn_out[0].astype(BF16),
        'w_o': w_o[0].astype(BF16),
        'ln1_g': ln1_g[0].reshape(1, d), 'ln1_b': ln1_b[0].reshape(1, d),
        'wr_hi': wr_hi, 'wr_lo': (w_route - wr_hi.astype(F32)).astype(BF16),
        'b_route': jnp.pad(b_route, (0, LANES - b_route.shape[0])).reshape(1, LANES),
        'w_gate': w_gate[0].astype(BF16), 'w_up': w_up[0].astype(BF16), 'w_down': w_down[0].astype(BF16),
        'ln2_g': ln2_g[0].reshape(1, d), 'ln2_b': ln2_b[0].reshape(1, d),
    }
    batches = [_token_mixing(x, p) for x in (x_prompt, x_sample)]
    sorted_rows = [_dispatch(h1_packed, dest, empty, block_e.shape[0] * EXPERT_BLOCK)
                   for _, _, dest, empty, block_e, h1_packed in batches]
    gathered = [_gather(_experts(xs, b[4], p['w_gate'], p['w_up'], p['w_down']), b[2])
                for xs, b in zip(sorted_rows, batches)]
    outs = [_combine(b[0], b[1], g, p['ln2_g'], p['ln2_b']).reshape(x.shape)
            for b, g, x in zip(batches, gathered, (x_prompt, x_sample))]
    return tuple(outs)
```

```python
import functools

import jax
import jax.numpy as jnp
import numpy as np
from jax import lax
from jax.experimental import pallas as pl
from jax.experimental.pallas import tpu as pltpu
from jax.experimental.pallas import tpu_sc as plsc

F32 = jnp.float32
BF16 = jnp.bfloat16

D_MODEL = 1024
HEAD_DIM = 64
HEADS_PER_GROUP = 4
GROUP_W = HEADS_PER_GROUP * HEAD_DIM
DILATIONS = (1, 4, 16)
BAND = 64
N_GROUPS = len(DILATIONS)
D_ATTN = N_GROUPS * GROUP_W
ATTN_SCALE = HEAD_DIM ** -0.5
ROPE_DIM = HEAD_DIM // 4
ROPE_THETA = 500000.0
MASK_VALUE = -1e30
D_CONV = D_MODEL
COL_CONV_B = 3 * D_ATTN
COL_CONV_C = COL_CONV_B + D_CONV
COL_CONV_H = COL_CONV_C + D_CONV
COL_GATE_ATTN = COL_CONV_H + D_CONV
COL_GATE_CONV = COL_GATE_ATTN + D_MODEL
N_EXPERT_GROUPS = 4
EXPERTS_PER_GROUP = 8
N_EXPERTS = N_EXPERT_GROUPS * EXPERTS_PER_GROUP
TOP_K_INNER = 2
D_EXPERT = 512
EXPERT_BLOCK = 256
EXPERT_CHUNK = 256
LN_EPS = 1e-5
DEPTH = 1
DEEPNORM_ALPHA = (2 * DEPTH) ** 0.25

LANES = 128
ROW_TILE = 512
HALO = 16
Q_BLOCK = 128
K_BLOCK = Q_BLOCK + 2 * BAND
Q_CHUNK = 2048
SC_COLS = 256
SC_WINDOW = 128
F32_CHUNKS = D_MODEL // SC_COLS
PACKED_COLS = D_MODEL // 2
PACKED_CHUNKS = PACKED_COLS // SC_COLS
U32 = jnp.uint32
HIGH_HALF = np.uint32(0xFFFF0000)
ROUTE_LANE0 = N_EXPERT_GROUPS
ROUTE_LANE_W = TOP_K_INNER
ROUTE_LANE_RANK = 2 * TOP_K_INNER
VMEM_LIMIT = 56 * 1024 * 1024


def _layer_norm(v, g, b):
    mu = jnp.mean(v, axis=-1, keepdims=True)
    d = v - mu
    var = jnp.mean(d * d, axis=-1, keepdims=True)
    return d * lax.rsqrt(var + LN_EPS) * g + b


def _dot(a, b):
    return jnp.dot(a, b, preferred_element_type=F32)


def _store_chunk_major(ref, first_row, val):
    for c in range(ref.shape[0]):
        ref[c, pl.ds(first_row, val.shape[0]), :] = val[:, c * SC_COLS:(c + 1) * SC_COLS]


def _load_chunk_major(ref, first_row, n):
    return jnp.concatenate([ref[c, pl.ds(first_row, n), :] for c in range(ref.shape[0])], axis=1)


def _chunk_major_spec(chunks, rows, index_map):
    return pl.BlockSpec((chunks, rows, SC_COLS), index_map)


def _pack_bf16_pairs(val):
    bits = lax.bitcast_convert_type(val.astype(BF16).astype(F32), U32)
    return (bits[:, PACKED_COLS:] & HIGH_HALF) | (bits[:, :PACKED_COLS] >> 16)


def _unpack_bf16_pairs(packed):
    low = lax.bitcast_convert_type(packed << 16, F32)
    high = lax.bitcast_convert_type(packed & HIGH_HALF, F32)
    return jnp.concatenate([low, high], axis=1)


def _resident(shape):
    nd = len(shape)
    return pl.BlockSpec(shape, lambda *_: (0,) * nd, pipeline_mode=pl.Buffered(1))


def _inproj_kernel(x_ref, xp_ref, xn_ref, g_ref, b_ref, win_ref, bg_ref, cw_ref, wco_ref,
                   cos_ref, sa_ref, sb_ref,
                   qkv0_ref, qkv1_ref, qkv2_ref, cg_ref, ga_ref, h_ref,
                   hext_ref, u_ref, stage_ref):
    i = pl.program_id(1)
    nt = pl.num_programs(1)
    tm = x_ref.shape[0]
    g = g_ref[...]
    b = b_ref[...]
    h = _layer_norm(x_ref[...], g, b)
    h_ref[...] = h
    hext_ref[pl.ds(HALO, tm), :] = h.astype(BF16)
    hext_ref[pl.ds(0, HALO), :] = _layer_norm(xp_ref[...], g, b).astype(BF16)
    hext_ref[pl.ds(HALO + tm, HALO), :] = _layer_norm(xn_ref[...], g, b).astype(BF16)
    hm = hext_ref[pl.ds(HALO, tm), :]
    he = hext_ref[...]

    def proj(lhs, col, n):
        return _dot(lhs, win_ref[:, col:col + n])

    u = proj(he, COL_CONV_C, D_CONV) * proj(he, COL_CONV_H, D_CONV)
    rows = lax.broadcasted_iota(jnp.int32, (tm + 2 * HALO, 1), 0)
    lo = jnp.where(i == 0, HALO, 0)
    hi = jnp.where(i == nt - 1, HALO + tm, tm + 2 * HALO)
    u_ref[...] = jnp.where((rows >= lo) & (rows < hi), u, 0.0)
    cw = cw_ref[...]
    conv = (cw[0:1] * u_ref[pl.ds(HALO - 1, tm), :] + cw[1:2] * u_ref[pl.ds(HALO, tm), :]
            + cw[2:3] * u_ref[pl.ds(HALO + 1, tm), :])
    cb = proj(hm, COL_CONV_B, D_CONV)
    conv_branch = _dot((cb * conv).astype(BF16), wco_ref[...])
    gate_conv = jax.nn.sigmoid(proj(hm, COL_GATE_CONV, D_MODEL) + bg_ref[:, D_MODEL:])
    cg_ref[...] = (gate_conv * conv_branch).astype(BF16)
    ga_ref[...] = jax.nn.sigmoid(proj(hm, COL_GATE_ATTN, D_MODEL) + bg_ref[:, :D_MODEL]).astype(BF16)

    cosv = cos_ref[...]
    sav = sa_ref[...]
    sbv = sb_ref[...]
    half = ROPE_DIM // 2
    for gi, (out_ref, r) in enumerate(zip((qkv0_ref, qkv1_ref, qkv2_ref), DILATIONS)):
        for t in range(3):
            p = proj(hm, t * D_ATTN + gi * GROUP_W, GROUP_W)
            if t < 2:
                parts = []
                for c in range(GROUP_W // LANES):
                    pc = p[:, c * LANES:(c + 1) * LANES]
                    parts.append(pc * cosv + pltpu.roll(pc, LANES - half, 1) * sav
                                 + pltpu.roll(pc, half, 1) * sbv)
                p = jnp.concatenate(parts, axis=1)
                if t == 0:
                    p = p * ATTN_SCALE
            if r == 1:
                out_ref[t, 0] = p.astype(BF16)
            else:
                for c in range(GROUP_W // LANES):
                    stage_ref[c] = p[:, c * LANES:(c + 1) * LANES]
                for res in range(r):
                    for c in range(GROUP_W // LANES):
                        out_ref[t, res, :, c * LANES:(c + 1) * LANES] = (
                            stage_ref[c, pl.ds(res, tm // r, stride=r), :].astype(BF16))


def _rope_tables(s):
    half = ROPE_DIM // 2
    inv_freq = (np.float32(ROPE_THETA) ** (-np.arange(half, dtype=np.float32) * np.float32(2.0) / ROPE_DIM))
    ang = np.arange(s, dtype=np.float32)[:, None] * inv_freq.astype(np.float32)[None, :]
    cos = np.cos(ang).astype(np.float32)
    sin = np.sin(ang).astype(np.float32)
    pad = np.zeros((s, HEAD_DIM - ROPE_DIM), np.float32)
    zero = np.zeros((s, half), np.float32)
    cos_t = np.concatenate([cos, cos, pad + 1.0], axis=1)
    sa_t = np.concatenate([-sin, zero, pad], axis=1)
    sb_t = np.concatenate([zero, sin, pad], axis=1)
    rep = LANES // HEAD_DIM
    return tuple(jnp.asarray(np.tile(t, (1, rep))) for t in (cos_t, sa_t, sb_t))


def _inproj(x, ln_g, ln_b, w_in, b_gate, conv_w, w_conv_out):
    bsz, s, d = x.shape
    tm = ROW_TILE
    nt = s // tm
    hpt = tm // HALO
    cos_t, sa_t, sb_t = _rope_tables(s)
    row = lambda b, i: (b, i, 0)
    tab = pl.BlockSpec((tm, LANES), lambda b, i: (i, 0))
    qkv_shapes = [jax.ShapeDtypeStruct((3, bsz, r, s // r, GROUP_W), BF16) for r in DILATIONS]
    qkv_specs = [pl.BlockSpec((3, None, r, tm // r, GROUP_W), lambda b, i: (0, b, 0, i, 0))
                 for r in DILATIONS]
    return pl.pallas_call(
        _inproj_kernel,
        grid=(bsz, nt),
        in_specs=[
            pl.BlockSpec((None, tm, d), row),
            pl.BlockSpec((None, HALO, d), lambda b, i: (b, jnp.maximum(i * hpt - 1, 0), 0)),
            pl.BlockSpec((None, HALO, d), lambda b, i: (b, jnp.minimum((i + 1) * hpt, s // HALO - 1), 0)),
            _resident((1, d)), _resident((1, d)),
            _resident(w_in.shape), _resident((1, 2 * d)), _resident(conv_w.shape),
            _resident(w_conv_out.shape),
            tab, tab, tab,
        ],
        out_specs=qkv_specs + [pl.BlockSpec((None, tm, d), row)] * 3,
        out_shape=(qkv_shapes + [jax.ShapeDtypeStruct((bsz, s, d), BF16)] * 2
                   + [jax.ShapeDtypeStruct((bsz, s, d), F32)]),
        scratch_shapes=[
            pltpu.VMEM((tm + 2 * HALO, d), BF16),
            pltpu.VMEM((tm + 2 * HALO, D_CONV), F32),
            pltpu.VMEM((GROUP_W // LANES, tm, LANES), F32),
        ],
        compiler_params=pltpu.CompilerParams(
            dimension_semantics=("parallel", "parallel"), vmem_limit_bytes=VMEM_LIMIT),
        name="inproj",
    )(x, x, x, ln_g, ln_b, w_in, b_gate, conv_w, w_conv_out, cos_t, sa_t, sb_t)


def _attn_kernel(q_ref, k_ref, v_ref, o_ref, lse_ref, *, sub_len):
    s = k_ref.shape[0]
    qc = q_ref.shape[0]
    base = pl.program_id(1) * qc
    lane = lax.broadcasted_iota(jnp.int32, (1, GROUP_W), 1)
    head_masks = [(lane >= h * HEAD_DIM) & (lane < (h + 1) * HEAD_DIM) for h in range(HEADS_PER_GROUP)]
    qi = lax.broadcasted_iota(jnp.int32, (Q_BLOCK, 1), 0)
    kj = lax.broadcasted_iota(jnp.int32, (1, K_BLOCK), 1)

    def body(j, carry):
        r0 = pl.multiple_of(j * Q_BLOCK, Q_BLOCK)
        s0 = base + r0
        k0 = pl.multiple_of(jnp.clip(s0 - BAND, 0, s - K_BLOCK), BAND)
        q = q_ref[pl.ds(r0, Q_BLOCK), :]
        k = k_ref[pl.ds(k0, K_BLOCK), :]
        v = v_ref[pl.ds(k0, K_BLOCK), :]
        sub_lo = (s0 // sub_len) * sub_len
        qpos = s0 + qi
        kpos = k0 + kj
        valid = (jnp.abs(qpos - kpos) <= BAND) & (kpos >= sub_lo) & (kpos < sub_lo + sub_len)
        qm = jnp.concatenate([jnp.where(hm, q, jnp.zeros_like(q)) for hm in head_masks], axis=0)
        sc = lax.dot_general(qm, k, (((1,), (1,)), ((), ())), preferred_element_type=F32)
        sc = jnp.where(jnp.concatenate([valid] * HEADS_PER_GROUP, axis=0), sc, MASK_VALUE)
        m = jnp.max(sc, axis=-1, keepdims=True)
        p = jnp.exp(sc - m)
        den = jnp.sum(p, axis=-1, keepdims=True)
        o_all = _dot((p * (1.0 / den)).astype(BF16), v)
        lse_all = m + jnp.log(den)
        o_acc = jnp.zeros((Q_BLOCK, GROUP_W), F32)
        lse_acc = jnp.zeros((Q_BLOCK, GROUP_W), F32)
        for h, hm in enumerate(head_masks):
            rows = slice(h * Q_BLOCK, (h + 1) * Q_BLOCK)
            o_acc = jnp.where(hm, o_all[rows], o_acc)
            lse_acc = jnp.where(hm, lse_all[rows], lse_acc)
        o_ref[pl.ds(r0, Q_BLOCK), :] = o_acc.astype(BF16)
        lse_ref[pl.ds(r0, Q_BLOCK), :] = lse_acc
        return carry

    lax.fori_loop(0, qc // Q_BLOCK, body, 0, unroll=8)


def _attention(qkv, sub_len):
    _, bsz, s, w = qkv.shape
    qc = min(Q_CHUNK, s)
    kv_spec = lambda t: pl.BlockSpec((None, None, s, w), lambda b, j: (t, b, 0, 0))
    return pl.pallas_call(
        functools.partial(_attn_kernel, sub_len=sub_len),
        grid=(bsz, s // qc),
        in_specs=[pl.BlockSpec((None, None, qc, w), lambda b, j: (0, b, j, 0)), kv_spec(1), kv_spec(2)],
        out_specs=[pl.BlockSpec((None, qc, w), lambda b, j: (b, j, 0))] * 2,
        out_shape=[jax.ShapeDtypeStruct((bsz, s, w), BF16), jax.ShapeDtypeStruct((bsz, s, w), F32)],
        compiler_params=pltpu.CompilerParams(
            dimension_semantics=("parallel", "arbitrary"), vmem_limit_bytes=VMEM_LIMIT),
        name="attn",
    )(qkv, qkv, qkv)


def _first_index_of_max(vals, lane_f):
    mx = jnp.max(vals, axis=-1, keepdims=True)
    idx = jnp.min(jnp.where(vals == mx, lane_f, float(LANES)), axis=-1, keepdims=True)
    return mx, idx


def _mix_kernel(h_ref, o0_ref, o1_ref, o2_ref, l0_ref, l1_ref, l2_ref, cg_ref, ga_ref,
                wao_ref, wo_ref, g1_ref, b1_ref, wrh_ref, wrl_ref, br_ref,
                h1_ref, h1p_ref, route_ref, counts_ref, so_ref, sl_ref, tri_ref):
    tm = h_ref.shape[0]
    halves = GROUP_W // LANES
    for gi, (o_ref, l_ref, r) in enumerate(zip((o0_ref, o1_ref, o2_ref), (l0_ref, l1_ref, l2_ref), DILATIONS)):
        for c in range(halves):
            cols = slice(c * LANES, (c + 1) * LANES)
            for res in range(r):
                rows = pl.ds(res, tm // r, stride=r) if r > 1 else slice(None)
                so_ref[gi * halves + c, rows, :] = o_ref[res, :, cols].astype(F32)
                sl_ref[gi * halves + c, rows, :] = l_ref[res, :, cols]

    def natural(ref, gi):
        return jnp.concatenate([ref[gi * halves + c] for c in range(halves)], axis=1)

    lses = [natural(sl_ref, gi) for gi in range(N_GROUPS)]
    mx = jnp.maximum(jnp.maximum(lses[0], lses[1]), lses[2])
    es = [jnp.exp(l - mx) for l in lses]
    inv_den = 1.0 / (es[0] + es[1] + es[2])
    attn = jnp.concatenate(
        [(natural(so_ref, gi) * (es[gi] * inv_den)).astype(BF16) for gi in range(N_GROUPS)], axis=1)
    attn_branch = _dot(attn, wao_ref[...])
    merged = ga_ref[...].astype(F32) * attn_branch + cg_ref[...].astype(F32)
    mix = _dot(merged.astype(BF16), wo_ref[...])
    h1 = _layer_norm(DEEPNORM_ALPHA * h_ref[...] + mix, g1_ref[...], b1_ref[...])
    _store_chunk_major(h1_ref, 0, h1)
    _store_chunk_major(h1p_ref, 0, _pack_bf16_pairs(h1))

    hi = h1.astype(BF16)
    lo = (h1 - hi.astype(F32)).astype(BF16)
    wrh = wrh_ref[...]
    logits = _dot(hi, wrh) + _dot(lo, wrh) + _dot(hi, wrl_ref[...]) + br_ref[...]
    lane = lax.broadcasted_iota(jnp.int32, (1, LANES), 1)
    lane_f = lane.astype(F32)
    is_grp = lane < N_EXPERT_GROUPS
    gl = jnp.where(is_grp, logits, MASK_VALUE)
    ge = jnp.exp(gl - jnp.max(gl, axis=-1, keepdims=True))
    gp = jnp.where(is_grp, ge / jnp.sum(ge, axis=-1, keepdims=True), -1.0)
    grp_w, grp_idx = _first_index_of_max(gp, lane_f)
    e_lo = ROUTE_LANE0 + grp_idx * EXPERTS_PER_GROUP
    sel = jnp.where((lane_f >= e_lo) & (lane_f < e_lo + EXPERTS_PER_GROUP), logits, MASK_VALUE)
    v1, i1 = _first_index_of_max(sel, lane_f)
    v2, i2 = _first_index_of_max(jnp.where(lane_f == i1, MASK_VALUE, sel), lane_f)
    e2 = jnp.exp(v2 - v1)
    w1 = grp_w / (1.0 + e2)
    w2 = grp_w * e2 / (1.0 + e2)

    first = (pl.program_id(0) == 0) & (pl.program_id(1) == 0)

    @pl.when(first)
    def _():
        counts_ref[...] = jnp.zeros_like(counts_ref)
        r = lax.broadcasted_iota(jnp.int32, (tm, tm), 0)
        c = lax.broadcasted_iota(jnp.int32, (tm, tm), 1)
        tri_ref[...] = (c < r).astype(BF16)

    oh0 = lane_f == i1 - ROUTE_LANE0
    oh1 = lane_f == i2 - ROUTE_LANE0
    oh = oh0.astype(F32) + oh1.astype(F32)
    before = counts_ref[...] + _dot(tri_ref[...], oh.astype(BF16))
    r0 = jnp.sum(jnp.where(oh0, before, 0.0), axis=-1, keepdims=True)
    r1 = jnp.sum(jnp.where(oh1, before + oh0.astype(F32), 0.0), axis=-1, keepdims=True)
    counts_ref[...] = counts_ref[...] + jnp.sum(oh, axis=0, keepdims=True)

    per_lane = (i1 - ROUTE_LANE0, i2 - ROUTE_LANE0, w1, w2, r0, r1)
    route = jnp.zeros((tm, LANES), F32)
    for k, v in enumerate(per_lane):
        route = jnp.where(lane == k, v, route)
    route_ref[...] = route


def _mix(h, o_list, lse_list, cg, ga, w_attn_out, w_o, ln1_g, ln1_b, wr_hi, wr_lo, b_route):
    bsz, s, d = h.shape
    tm = ROW_TILE
    row = lambda b, i: (b, i, 0)
    dil_specs = [pl.BlockSpec((None, r, tm // r, GROUP_W), lambda b, i: (b, 0, i, 0)) for r in DILATIONS]
    o_views = [o.reshape(bsz, r, s // r, GROUP_W) for o, r in zip(o_list, DILATIONS)]
    l_views = [l.reshape(bsz, r, s // r, GROUP_W) for l, r in zip(lse_list, DILATIONS)]
    return pl.pallas_call(
        _mix_kernel,
        grid=(bsz, s // tm),
        in_specs=[pl.BlockSpec((None, tm, d), row)] + dil_specs + dil_specs + [
            pl.BlockSpec((None, tm, d), row), pl.BlockSpec((None, tm, d), row),
            _resident(w_attn_out.shape), _resident(w_o.shape),
            _resident((1, d)), _resident((1, d)),
            _resident(wr_hi.shape), _resident(wr_lo.shape), _resident((1, LANES)),
        ],
        out_specs=[_chunk_major_spec(F32_CHUNKS, tm, lambda b, i: (0, b * (s // tm) + i, 0)),
                   _chunk_major_spec(PACKED_CHUNKS, tm, lambda b, i: (0, b * (s // tm) + i, 0)),
                   pl.BlockSpec((None, tm, LANES), row),
                   pl.BlockSpec((1, LANES), lambda b, i: (0, 0))],
        out_shape=[jax.ShapeDtypeStruct((F32_CHUNKS, bsz * s, SC_COLS), F32),
                   jax.ShapeDtypeStruct((PACKED_CHUNKS, bsz * s, SC_COLS), U32),
                   jax.ShapeDtypeStruct((bsz, s, LANES), F32),
                   jax.ShapeDtypeStruct((1, LANES), F32)],
        scratch_shapes=[pltpu.VMEM((N_GROUPS * GROUP_W // LANES, tm, LANES), F32)] * 2
        + [pltpu.VMEM((tm, tm), BF16)],
        compiler_params=pltpu.CompilerParams(
            dimension_semantics=("arbitrary", "arbitrary"), vmem_limit_bytes=VMEM_LIMIT),
        name="mix",
    )(h, *o_views, *l_views, cg, ga, w_attn_out, w_o, ln1_g, ln1_b, wr_hi, wr_lo, b_route)


def _dispatch_plan(route, counts, t):
    eid = route[:, :TOP_K_INNER].astype(jnp.int32)
    rank = route[:, ROUTE_LANE_RANK:ROUTE_LANE_RANK + TOP_K_INNER]
    counts = counts[0, :N_EXPERTS].astype(jnp.int32)
    padded = (counts + EXPERT_BLOCK - 1) // EXPERT_BLOCK * EXPERT_BLOCK
    pad_end = jnp.cumsum(padded)
    pad_start = pad_end - padded
    experts = jnp.arange(N_EXPERTS, dtype=jnp.int32)
    start_of = jnp.sum(jnp.where(eid[..., None] == experts, pad_start, 0), axis=-1)
    dest = start_of + rank.astype(jnp.int32)
    n_blocks = -(-t * TOP_K_INNER // EXPERT_BLOCK) + N_EXPERTS
    n_slots = n_blocks * EXPERT_BLOCK
    block_start = jnp.arange(n_blocks, dtype=jnp.int32) * EXPERT_BLOCK
    block_e = jnp.minimum(jnp.sum((block_start[:, None] >= pad_end[None, :]).astype(jnp.int32), axis=1),
                          N_EXPERTS - 1)
    n_empty = n_slots - t * TOP_K_INNER
    gap_end = jnp.cumsum(padded - counts)
    gap_start = gap_end - (padded - counts)
    k = jnp.arange(n_empty, dtype=jnp.int32)
    in_gap = (k[:, None] >= gap_start) & (k[:, None] < gap_end)
    empty = jnp.where(k < gap_end[-1],
                      jnp.sum(jnp.where(in_gap, pad_start + counts + k[:, None] - gap_start, 0), axis=-1),
                      pad_end[-1] + k - gap_end[-1])
    return dest.astype(jnp.int32), empty.astype(jnp.int32), block_e


def _sc_mesh():
    return plsc.VectorSubcoreMesh(core_axis_name="core", subcore_axis_name="subcore")


def _sc_rows_pipeline(body, n_rows, row_index_map, cols):
    return pltpu.emit_pipeline(
        body,
        grid=(n_rows // SC_WINDOW,),
        in_specs=[pl.BlockSpec((SC_WINDOW, cols), index_map=row_index_map),
                  pl.BlockSpec((1, SC_WINDOW), index_map=lambda i: (0, i))],
        out_specs=[],
        core_axis_name=("core", "subcore"),
        dimension_semantics=(pltpu.PARALLEL,),
    )


def _chunk_rows(rows, chunks, n_rows_per_chunk):
    base = jnp.arange(chunks, dtype=jnp.int32)[:, None] * n_rows_per_chunk
    return (base + rows[None, :]).reshape(1, -1)


def _dispatch(rows, dest, empty, n_slots):
    chunks, t, _ = rows.shape
    x = rows.reshape(chunks * t, SC_COLS)
    idx_first = _chunk_rows(dest[:, 0], chunks, n_slots)
    idx_second = _chunk_rows(dest[:, 1], chunks, n_slots)
    idx_empty = _chunk_rows(empty, chunks, n_slots)
    zeros = jnp.zeros((SC_WINDOW, SC_COLS), rows.dtype)

    @pl.kernel(out_type=jax.ShapeDtypeStruct((chunks * n_slots, SC_COLS), rows.dtype), mesh=_sc_mesh(),
               scratch_types=[])
    def scatter(x_hbm, i0_hbm, i1_hbm, z_hbm, ie_hbm, o_hbm):
        def body(x_vmem, i_vmem):
            pltpu.sync_copy(x_vmem, o_hbm.at[i_vmem.at[0]])

        for i_hbm in (i0_hbm, i1_hbm):
            _sc_rows_pipeline(body, x.shape[0], lambda i: (i, 0), SC_COLS)(x_hbm, i_hbm)
        _sc_rows_pipeline(body, idx_empty.shape[1], lambda i: (0, 0), SC_COLS)(z_hbm, ie_hbm)

    return scatter(x, idx_first, idx_second, zeros, idx_empty).reshape(chunks, n_slots, SC_COLS)


def _expert_mlp(xb, wg_ref, wu_ref, wd_ref):
    acc = None
    for c in range(D_EXPERT // EXPERT_CHUNK):
        cols = slice(c * EXPERT_CHUNK, (c + 1) * EXPERT_CHUNK)
        gate = _dot(xb, wg_ref[:, cols].astype(BF16))
        up = _dot(xb, wu_ref[:, cols].astype(BF16))
        hidden = (gate * jax.nn.sigmoid(gate) * up).astype(BF16)
        part = _dot(hidden, wd_ref[cols, :].astype(BF16))
        acc = part if acc is None else acc + part
    return acc


def _expert_kernel(be_ref, x_ref, wga_ref, wua_ref, wda_ref, wgb_ref, wub_ref, wdb_ref, y_ref):
    i = pl.program_id(0)
    same = be_ref[2 * i] == be_ref[2 * i + 1]

    @pl.when(same)
    def _():
        xb = _unpack_bf16_pairs(_load_chunk_major(x_ref, 0, 2 * EXPERT_BLOCK)).astype(BF16)
        _store_chunk_major(y_ref, 0, _pack_bf16_pairs(_expert_mlp(xb, wga_ref, wua_ref, wda_ref)))

    @pl.when(jnp.logical_not(same))
    def _():
        for half, w in enumerate(((wga_ref, wua_ref, wda_ref), (wgb_ref, wub_ref, wdb_ref))):
            xb = _unpack_bf16_pairs(_load_chunk_major(x_ref, half * EXPERT_BLOCK, EXPERT_BLOCK)).astype(BF16)
            _store_chunk_major(y_ref, half * EXPERT_BLOCK, _pack_bf16_pairs(_expert_mlp(xb, *w)))


def _experts(xs, block_e, w_gate, w_up, w_down):
    n_blocks = block_e.shape[0]
    assert n_blocks % 2 == 0
    d = D_MODEL
    blk = _chunk_major_spec(PACKED_CHUNKS, 2 * EXPERT_BLOCK, lambda i, be: (0, i, 0))
    w_specs = [pl.BlockSpec(shape, (lambda i, be, k=k: (be[2 * i + k], 0, 0)))
               for k in range(2) for shape in ((None, d, D_EXPERT), (None, d, D_EXPERT), (None, D_EXPERT, d))]
    grid_spec = pltpu.PrefetchScalarGridSpec(
        num_scalar_prefetch=1,
        grid=(n_blocks // 2,),
        in_specs=[blk] + w_specs,
        out_specs=blk,
    )
    return pl.pallas_call(
        _expert_kernel,
        grid_spec=grid_spec,
        out_shape=jax.ShapeDtypeStruct(xs.shape, xs.dtype),
        compiler_params=pltpu.CompilerParams(
            dimension_semantics=("arbitrary",), vmem_limit_bytes=VMEM_LIMIT),
        name="experts",
    )(block_e, xs, w_gate, w_up, w_down, w_gate, w_up, w_down)


def _gather(yb, dest):
    chunks, n_slots, _ = yb.shape
    x = yb.reshape(chunks * n_slots, SC_COLS)
    idx = _chunk_rows(dest.T.reshape(-1), chunks, n_slots)
    m = idx.shape[1]

    @pl.kernel(out_type=jax.ShapeDtypeStruct((m, SC_COLS), yb.dtype), mesh=_sc_mesh(), scratch_types=[])
    def gather(x_hbm, i_hbm, o_hbm):
        def body(i_vmem, o_vmem):
            pltpu.sync_copy(x_hbm.at[i_vmem.at[0]], o_vmem)

        pltpu.emit_pipeline(
            body,
            grid=(m // SC_WINDOW,),
            in_specs=[pl.BlockSpec((1, SC_WINDOW), index_map=lambda i: (0, i))],
            out_specs=[pl.BlockSpec((SC_WINDOW, SC_COLS), index_map=lambda i: (i, 0))],
            core_axis_name=("core", "subcore"),
            dimension_semantics=(pltpu.PARALLEL,),
        )(i_hbm, o_hbm)

    return gather(x, idx).reshape(chunks, m // chunks, SC_COLS)


def _combine_kernel(first_ref, second_ref, h1_ref, route_ref, g_ref, b_ref, y_ref):
    tm = y_ref.shape[0]
    route = route_ref[...]
    w_first = route[:, ROUTE_LANE_W:ROUTE_LANE_W + 1]
    w_second = route[:, ROUTE_LANE_W + 1:ROUTE_LANE_W + 2]
    ffn = (w_first * _unpack_bf16_pairs(_load_chunk_major(first_ref, 0, tm))
           + w_second * _unpack_bf16_pairs(_load_chunk_major(second_ref, 0, tm)))
    y_ref[...] = _layer_norm(DEEPNORM_ALPHA * _load_chunk_major(h1_ref, 0, tm) + ffn, g_ref[...], b_ref[...])


def _combine(h1, route, g, ln2_g, ln2_b):
    t = route.shape[0]
    d = D_MODEL
    tm = ROW_TILE
    nt = t // tm
    return pl.pallas_call(
        _combine_kernel,
        grid=(nt,),
        in_specs=[
            _chunk_major_spec(PACKED_CHUNKS, tm, lambda i: (0, i, 0)),
            _chunk_major_spec(PACKED_CHUNKS, tm, lambda i: (0, nt + i, 0)),
            _chunk_major_spec(F32_CHUNKS, tm, lambda i: (0, i, 0)),
            pl.BlockSpec((tm, LANES), lambda i: (i, 0)),
            _resident((1, d)), _resident((1, d)),
        ],
        out_specs=pl.BlockSpec((tm, d), lambda i: (i, 0)),
        out_shape=jax.ShapeDtypeStruct((t, d), F32),
        compiler_params=pltpu.CompilerParams(
            dimension_semantics=("parallel",), vmem_limit_bytes=VMEM_LIMIT),
        name="combine",
    )(g, g, h1, route, ln2_g, ln2_b)


def _token_mixing(x, p):
    bsz, s, d = x.shape
    qkv0, qkv1, qkv2, cg, ga, h = _inproj(x, p['ln_in_g'], p['ln_in_b'], p['w_in'], p['b_gate'], p['conv_w'],
                                          p['w_conv_out'])
    o_list, lse_list = [], []
    for qkv, r in zip((qkv0, qkv1, qkv2), DILATIONS):
        o, lse = _attention(qkv.reshape(3, bsz, s, GROUP_W), s // r)
        o_list.append(o)
        lse_list.append(lse)
    h1, h1_packed, route, counts = _mix(h, o_list, lse_list, cg, ga, p['w_attn_out'], p['w_o'],
                             p['ln1_g'], p['ln1_b'], p['wr_hi'], p['wr_lo'], p['b_route'])
    t = bsz * s
    route = route.reshape(t, LANES)
    dest, empty, block_e = _dispatch_plan(route, counts, t)
    return h1, route, dest, empty, block_e, h1_packed


def kernel(x_prompt, x_sample, ln_in_g, ln_in_b, w_in, b_gate, conv_w, w_attn_out, w_conv_out, w_o, ln1_g, ln1_b, w_route_group, b_route_group, w_route_expert, b_route_expert, w_gate, w_up, w_down, ln2_g, ln2_b):
    d = D_MODEL
    w_route = jnp.concatenate([w_route_group[0], w_route_expert[0]], axis=1)
    w_route = jnp.pad(w_route, ((0, 0), (0, LANES - w_route.shape[1])))
    wr_hi = w_route.astype(BF16)
    b_route = jnp.concatenate([b_route_group[0], b_route_expert[0]]).astype(F32)
    p = {
        'ln_in_g': ln_in_g.reshape(1, d), 'ln_in_b': ln_in_b.reshape(1, d),
        'w_in': w_in[0].astype(BF16), 'b_gate': b_gate[0].reshape(1, 2 * d), 'conv_w': conv_w[0],
        'w_conv_out': w_conv_out[0].astype(BF16), 'w_attn_out': w_attn_out[0].astype(BF16),
        'w_o': w_o[0].astype(BF16),
        'ln1_g': ln1_g[0].reshape(1, d), 'ln1_b': ln1_b[0].reshape(1, d),
        'wr_hi': wr_hi, 'wr_lo': (w_route - wr_hi.astype(F32)).astype(BF16),
        'b_route': jnp.pad(b_route, (0, LANES - b_route.shape[0])).reshape(1, LANES),
        'w_gate': w_gate[0], 'w_up': w_up[0], 'w_down': w_down[0],
        'ln2_g': ln2_g[0].reshape(1, d), 'ln2_b': ln2_b[0].reshape(1, d),
    }
    batches = [_token_mixing(x, p) for x in (x_prompt, x_sample)]
    sorted_rows = [_dispatch(h1_packed, dest, empty, block_e.shape[0] * EXPERT_BLOCK)
                   for _, _, dest, empty, block_e, h1_packed in batches]
    gathered = [_gather(_experts(xs, b[4], p['w_gate'], p['w_up'], p['w_down']), b[2])
                for xs, b in zip(sorted_rows, batches)]
    outs = [_combine(b[0], b[1], g, p['ln2_g'], p['ln2_b']).reshape(x.shape)
            for b, g, x in zip(batches, gathered, (x_prompt, x_sample))]
    return tuple(outs)
```

```python
import functools

import jax
import jax.numpy as jnp
import numpy as np
from jax import lax
from jax.experimental import pallas as pl
from jax.experimental.pallas import tpu as pltpu
from jax.experimental.pallas import tpu_sc as plsc

F32 = jnp.float32
BF16 = jnp.bfloat16

D_MODEL = 1024
HEAD_DIM = 64
HEADS_PER_GROUP = 4
GROUP_W = HEADS_PER_GROUP * HEAD_DIM
DILATIONS = (1, 4, 16)
BAND = 64
N_GROUPS = len(DILATIONS)
D_ATTN = N_GROUPS * GROUP_W
ATTN_SCALE = HEAD_DIM ** -0.5
ROPE_DIM = HEAD_DIM // 4
ROPE_THETA = 500000.0
MASK_VALUE = -1e30
D_CONV = D_MODEL
COL_CONV_B = 3 * D_ATTN
COL_CONV_C = COL_CONV_B + D_CONV
COL_CONV_H = COL_CONV_C + D_CONV
COL_GATE_ATTN = COL_CONV_H + D_CONV
COL_GATE_CONV = COL_GATE_ATTN + D_MODEL
N_EXPERT_GROUPS = 4
EXPERTS_PER_GROUP = 8
N_EXPERTS = N_EXPERT_GROUPS * EXPERTS_PER_GROUP
TOP_K_INNER = 2
D_EXPERT = 512
EXPERT_BLOCK = 256
EXPERT_CHUNK = 256
LN_EPS = 1e-5
DEPTH = 1
DEEPNORM_ALPHA = (2 * DEPTH) ** 0.25

LANES = 128
ROW_TILE = 512
HALO = 16
Q_BLOCK = 128
K_BLOCK = Q_BLOCK + 2 * BAND
Q_CHUNK = 2048
SC_COLS = 256
SC_WINDOW = 128
F32_CHUNKS = D_MODEL // SC_COLS
PACKED_COLS = D_MODEL // 2
PACKED_CHUNKS = PACKED_COLS // SC_COLS
U32 = jnp.uint32
HIGH_HALF = np.uint32(0xFFFF0000)
ROUTE_LANE0 = N_EXPERT_GROUPS
ROUTE_LANE_W = TOP_K_INNER
ROUTE_LANE_RANK = 2 * TOP_K_INNER
VMEM_LIMIT = 56 * 1024 * 1024


def _layer_norm(v, g, b):
    mu = jnp.mean(v, axis=-1, keepdims=True)
    d = v - mu
    var = jnp.mean(d * d, axis=-1, keepdims=True)
    return d * lax.rsqrt(var + LN_EPS) * g + b


def _dot(a, b):
    return jnp.dot(a, b, preferred_element_type=F32)


def _store_chunk_major(ref, first_row, val):
    for c in range(ref.shape[0]):
        ref[c, pl.ds(first_row, val.shape[0]), :] = val[:, c * SC_COLS:(c + 1) * SC_COLS]


def _load_chunk_major(ref, first_row, n):
    return jnp.concatenate([ref[c, pl.ds(first_row, n), :] for c in range(ref.shape[0])], axis=1)


def _chunk_major_spec(chunks, rows, index_map):
    return pl.BlockSpec((chunks, rows, SC_COLS), index_map)


def _pack_bf16_pairs(val):
    bits = lax.bitcast_convert_type(val.astype(BF16).astype(F32), U32)
    return (bits[:, PACKED_COLS:] & HIGH_HALF) | (bits[:, :PACKED_COLS] >> 16)


def _unpack_bf16_pairs(packed):
    low = lax.bitcast_convert_type(packed << 16, F32)
    high = lax.bitcast_convert_type(packed & HIGH_HALF, F32)
    return jnp.concatenate([low, high], axis=1)


def _resident(shape):
    nd = len(shape)
    return pl.BlockSpec(shape, lambda *_: (0,) * nd, pipeline_mode=pl.Buffered(1))


def _inproj_kernel(x_ref, xp_ref, xn_ref, g_ref, b_ref, win_ref, bg_ref, cw_ref, wco_ref,
                   cos_ref, sa_ref, sb_ref,
                   qkv0_ref, qkv1_ref, qkv2_ref, cg_ref, ga_ref, h_ref,
                   hext_ref, u_ref, stage_ref):
    i = pl.program_id(1)
    nt = pl.num_programs(1)
    tm = x_ref.shape[0]
    g = g_ref[...]
    b = b_ref[...]
    h = _layer_norm(x_ref[...], g, b)
    h_ref[...] = h
    hext_ref[pl.ds(HALO, tm), :] = h.astype(BF16)
    hext_ref[pl.ds(0, HALO), :] = _layer_norm(xp_ref[...], g, b).astype(BF16)
    hext_ref[pl.ds(HALO + tm, HALO), :] = _layer_norm(xn_ref[...], g, b).astype(BF16)
    hm = hext_ref[pl.ds(HALO, tm), :]
    he = hext_ref[...]

    def proj(lhs, col, n):
        return _dot(lhs, win_ref[:, col:col + n])

    u = proj(he, COL_CONV_C, D_CONV) * proj(he, COL_CONV_H, D_CONV)
    rows = lax.broadcasted_iota(jnp.int32, (tm + 2 * HALO, 1), 0)
    lo = jnp.where(i == 0, HALO, 0)
    hi = jnp.where(i == nt - 1, HALO + tm, tm + 2 * HALO)
    u_ref[...] = jnp.where((rows >= lo) & (rows < hi), u, 0.0)
    cw = cw_ref[...]
    conv = (cw[0:1] * u_ref[pl.ds(HALO - 1, tm), :] + cw[1:2] * u_ref[pl.ds(HALO, tm), :]
            + cw[2:3] * u_ref[pl.ds(HALO + 1, tm), :])
    cb = proj(hm, COL_CONV_B, D_CONV)
    conv_branch = _dot((cb * conv).astype(BF16), wco_ref[...])
    gate_conv = jax.nn.sigmoid(proj(hm, COL_GATE_CONV, D_MODEL) + bg_ref[:, D_MODEL:])
    cg_ref[...] = (gate_conv * conv_branch).astype(BF16)
    ga_ref[...] = jax.nn.sigmoid(proj(hm, COL_GATE_ATTN, D_MODEL) + bg_ref[:, :D_MODEL]).astype(BF16)

    cosv = cos_ref[...]
    sav = sa_ref[...]
    sbv = sb_ref[...]
    half = ROPE_DIM // 2
    for gi, (out_ref, r) in enumerate(zip((qkv0_ref, qkv1_ref, qkv2_ref), DILATIONS)):
        for t in range(3):
            p = proj(hm, t * D_ATTN + gi * GROUP_W, GROUP_W)
            if t < 2:
                parts = []
                for c in range(GROUP_W // LANES):
                    pc = p[:, c * LANES:(c + 1) * LANES]
                    parts.append(pc * cosv + pltpu.roll(pc, LANES - half, 1) * sav
                                 + pltpu.roll(pc, half, 1) * sbv)
                p = jnp.concatenate(parts, axis=1)
                if t == 0:
                    p = p * ATTN_SCALE
            if r == 1:
                out_ref[t, 0] = p.astype(BF16)
            else:
                for c in range(GROUP_W // LANES):
                    stage_ref[c] = p[:, c * LANES:(c + 1) * LANES]
                for res in range(r):
                    for c in range(GROUP_W // LANES):
                        out_ref[t, res, :, c * LANES:(c + 1) * LANES] = (
                            stage_ref[c, pl.ds(res, tm // r, stride=r), :].astype(BF16))


def _rope_tables(s):
    half = ROPE_DIM // 2
    inv_freq = (np.float32(ROPE_THETA) ** (-np.arange(half, dtype=np.float32) * np.float32(2.0) / ROPE_DIM))
    ang = np.arange(s, dtype=np.float32)[:, None] * inv_freq.astype(np.float32)[None, :]
    cos = np.cos(ang).astype(np.float32)
    sin = np.sin(ang).astype(np.float32)
    pad = np.zeros((s, HEAD_DIM - ROPE_DIM), np.float32)
    zero = np.zeros((s, half), np.float32)
    cos_t = np.concatenate([cos, cos, pad + 1.0], axis=1)
    sa_t = np.concatenate([-sin, zero, pad], axis=1)
    sb_t = np.concatenate([zero, sin, pad], axis=1)
    rep = LANES // HEAD_DIM
    return tuple(jnp.asarray(np.tile(t, (1, rep))) for t in (cos_t, sa_t, sb_t))


def _inproj(x, ln_g, ln_b, w_in, b_gate, conv_w, w_conv_out):
    bsz, s, d = x.shape
    tm = ROW_TILE
    nt = s // tm
    hpt = tm // HALO
    cos_t, sa_t, sb_t = _rope_tables(s)
    row = lambda b, i: (b, i, 0)
    tab = pl.BlockSpec((tm, LANES), lambda b, i: (i, 0))
    qkv_shapes = [jax.ShapeDtypeStruct((3, bsz, r, s // r, GROUP_W), BF16) for r in DILATIONS]
    qkv_specs = [pl.BlockSpec((3, None, r, tm // r, GROUP_W), lambda b, i: (0, b, 0, i, 0))
                 for r in DILATIONS]
    return pl.pallas_call(
        _inproj_kernel,
        grid=(bsz, nt),
        in_specs=[
            pl.BlockSpec((None, tm, d), row),
            pl.BlockSpec((None, HALO, d), lambda b, i: (b, jnp.maximum(i * hpt - 1, 0), 0)),
            pl.BlockSpec((None, HALO, d), lambda b, i: (b, jnp.minimum((i + 1) * hpt, s // HALO - 1), 0)),
            _resident((1, d)), _resident((1, d)),
            _resident(w_in.shape), _resident((1, 2 * d)), _resident(conv_w.shape),
            _resident(w_conv_out.shape),
            tab, tab, tab,
        ],
        out_specs=qkv_specs + [pl.BlockSpec((None, tm, d), row)] * 3,
        out_shape=(qkv_shapes + [jax.ShapeDtypeStruct((bsz, s, d), BF16)] * 2
                   + [jax.ShapeDtypeStruct((bsz, s, d), F32)]),
        scratch_shapes=[
            pltpu.VMEM((tm + 2 * HALO, d), BF16),
            pltpu.VMEM((tm + 2 * HALO, D_CONV), F32),
            pltpu.VMEM((GROUP_W // LANES, tm, LANES), F32),
        ],
        compiler_params=pltpu.CompilerParams(
            dimension_semantics=("parallel", "parallel"), vmem_limit_bytes=VMEM_LIMIT),
        name="inproj",
    )(x, x, x, ln_g, ln_b, w_in, b_gate, conv_w, w_conv_out, cos_t, sa_t, sb_t)


def _attn_kernel(q_ref, k_ref, v_ref, o_ref, lse_ref, *, sub_len):
    s = k_ref.shape[0]
    qc = q_ref.shape[0]
    base = pl.program_id(1) * qc
    lane = lax.broadcasted_iota(jnp.int32, (1, GROUP_W), 1)
    head_masks = [(lane >= h * HEAD_DIM) & (lane < (h + 1) * HEAD_DIM) for h in range(HEADS_PER_GROUP)]
    qi = lax.broadcasted_iota(jnp.int32, (Q_BLOCK, 1), 0)
    kj = lax.broadcasted_iota(jnp.int32, (1, K_BLOCK), 1)

    def body(j, carry):
        r0 = pl.multiple_of(j * Q_BLOCK, Q_BLOCK)
        s0 = base + r0
        k0 = pl.multiple_of(jnp.clip(s0 - BAND, 0, s - K_BLOCK), BAND)
        q = q_ref[pl.ds(r0, Q_BLOCK), :]
        k = k_ref[pl.ds(k0, K_BLOCK), :]
        v = v_ref[pl.ds(k0, K_BLOCK), :]
        sub_lo = (s0 // sub_len) * sub_len
        qpos = s0 + qi
        kpos = k0 + kj
        valid = (jnp.abs(qpos - kpos) <= BAND) & (kpos >= sub_lo) & (kpos < sub_lo + sub_len)
        qm = jnp.concatenate([jnp.where(hm, q, jnp.zeros_like(q)) for hm in head_masks], axis=0)
        sc = lax.dot_general(qm, k, (((1,), (1,)), ((), ())), preferred_element_type=F32)
        sc = jnp.where(jnp.concatenate([valid] * HEADS_PER_GROUP, axis=0), sc, MASK_VALUE)
        m = jnp.max(sc, axis=-1, keepdims=True)
        p = jnp.exp(sc - m)
        den = jnp.sum(p, axis=-1, keepdims=True)
        o_all = _dot((p * (1.0 / den)).astype(BF16), v)
        lse_all = m + jnp.log(den)
        o_acc = jnp.zeros((Q_BLOCK, GROUP_W), F32)
        lse_acc = jnp.zeros((Q_BLOCK, GROUP_W), F32)
        for h, hm in enumerate(head_masks):
            rows = slice(h * Q_BLOCK, (h + 1) * Q_BLOCK)
            o_acc = jnp.where(hm, o_all[rows], o_acc)
            lse_acc = jnp.where(hm, lse_all[rows], lse_acc)
        o_ref[pl.ds(r0, Q_BLOCK), :] = o_acc.astype(BF16)
        lse_ref[pl.ds(r0, Q_BLOCK), :] = lse_acc
        return carry

    lax.fori_loop(0, qc // Q_BLOCK, body, 0, unroll=8)


def _attention(qkv, sub_len):
    _, bsz, s, w = qkv.shape
    qc = min(Q_CHUNK, s)
    kv_spec = lambda t: pl.BlockSpec((None, None, s, w), lambda b, j: (t, b, 0, 0))
    return pl.pallas_call(
        functools.partial(_attn_kernel, sub_len=sub_len),
        grid=(bsz, s // qc),
        in_specs=[pl.BlockSpec((None, None, qc, w), lambda b, j: (0, b, j, 0)), kv_spec(1), kv_spec(2)],
        out_specs=[pl.BlockSpec((None, qc, w), lambda b, j: (b, j, 0))] * 2,
        out_shape=[jax.ShapeDtypeStruct((bsz, s, w), BF16), jax.ShapeDtypeStruct((bsz, s, w), F32)],
        compiler_params=pltpu.CompilerParams(
            dimension_semantics=("parallel", "arbitrary"), vmem_limit_bytes=VMEM_LIMIT),
        name="attn",
    )(qkv, qkv, qkv)


def _first_index_of_max(vals, lane_f):
    mx = jnp.max(vals, axis=-1, keepdims=True)
    idx = jnp.min(jnp.where(vals == mx, lane_f, float(LANES)), axis=-1, keepdims=True)
    return mx, idx


def _mix_kernel(h_ref, o0_ref, o1_ref, o2_ref, l0_ref, l1_ref, l2_ref, cg_ref, ga_ref,
                wao_ref, wo_ref, g1_ref, b1_ref, wrh_ref, wrl_ref, br_ref,
                h1_ref, h1p_ref, route_ref, counts_ref, so_ref, sl_ref, tri_ref):
    tm = h_ref.shape[0]
    halves = GROUP_W // LANES
    for gi, (o_ref, l_ref, r) in enumerate(zip((o0_ref, o1_ref, o2_ref), (l0_ref, l1_ref, l2_ref), DILATIONS)):
        for c in range(halves):
            cols = slice(c * LANES, (c + 1) * LANES)
            for res in range(r):
                rows = pl.ds(res, tm // r, stride=r) if r > 1 else slice(None)
                so_ref[gi * halves + c, rows, :] = o_ref[res, :, cols].astype(F32)
                sl_ref[gi * halves + c, rows, :] = l_ref[res, :, cols]

    def natural(ref, gi):
        return jnp.concatenate([ref[gi * halves + c] for c in range(halves)], axis=1)

    lses = [natural(sl_ref, gi) for gi in range(N_GROUPS)]
    mx = jnp.maximum(jnp.maximum(lses[0], lses[1]), lses[2])
    es = [jnp.exp(l - mx) for l in lses]
    inv_den = 1.0 / (es[0] + es[1] + es[2])
    attn = jnp.concatenate(
        [(natural(so_ref, gi) * (es[gi] * inv_den)).astype(BF16) for gi in range(N_GROUPS)], axis=1)
    attn_branch = _dot(attn, wao_ref[...])
    merged = ga_ref[...].astype(F32) * attn_branch + cg_ref[...].astype(F32)
    mix = _dot(merged.astype(BF16), wo_ref[...])
    h1 = _layer_norm(DEEPNORM_ALPHA * h_ref[...] + mix, g1_ref[...], b1_ref[...])
    _store_chunk_major(h1_ref, 0, h1)
    _store_chunk_major(h1p_ref, 0, _pack_bf16_pairs(h1))

    hi = h1.astype(BF16)
    lo = (h1 - hi.astype(F32)).astype(BF16)
    wrh = wrh_ref[...]
    logits = _dot(hi, wrh) + _dot(lo, wrh) + _dot(hi, wrl_ref[...]) + br_ref[...]
    lane = lax.broadcasted_iota(jnp.int32, (1, LANES), 1)
    lane_f = lane.astype(F32)
    is_grp = lane < N_EXPERT_GROUPS
    gl = jnp.where(is_grp, logits, MASK_VALUE)
    ge = jnp.exp(gl - jnp.max(gl, axis=-1, keepdims=True))
    gp = jnp.where(is_grp, ge / jnp.sum(ge, axis=-1, keepdims=True), -1.0)
    grp_w, grp_idx = _first_index_of_max(gp, lane_f)
    e_lo = ROUTE_LANE0 + grp_idx * EXPERTS_PER_GROUP
    sel = jnp.where((lane_f >= e_lo) & (lane_f < e_lo + EXPERTS_PER_GROUP), logits, MASK_VALUE)
    v1, i1 = _first_index_of_max(sel, lane_f)
    v2, i2 = _first_index_of_max(jnp.where(lane_f == i1, MASK_VALUE, sel), lane_f)
    e2 = jnp.exp(v2 - v1)
    w1 = grp_w / (1.0 + e2)
    w2 = grp_w * e2 / (1.0 + e2)

    first = (pl.program_id(0) == 0) & (pl.program_id(1) == 0)

    @pl.when(first)
    def _():
        counts_ref[...] = jnp.zeros_like(counts_ref)
        r = lax.broadcasted_iota(jnp.int32, (tm, tm), 0)
        c = lax.broadcasted_iota(jnp.int32, (tm, tm), 1)
        tri_ref[...] = (c < r).astype(BF16)

    oh0 = lane_f == i1 - ROUTE_LANE0
    oh1 = lane_f == i2 - ROUTE_LANE0
    oh = oh0.astype(F32) + oh1.astype(F32)
    before = counts_ref[...] + _dot(tri_ref[...], oh.astype(BF16))
    r0 = jnp.sum(jnp.where(oh0, before, 0.0), axis=-1, keepdims=True)
    r1 = jnp.sum(jnp.where(oh1, before + oh0.astype(F32), 0.0), axis=-1, keepdims=True)
    counts_ref[...] = counts_ref[...] + jnp.sum(oh, axis=0, keepdims=True)

    per_lane = (i1 - ROUTE_LANE0, i2 - ROUTE_LANE0, w1, w2, r0, r1)
    route = jnp.zeros((tm, LANES), F32)
    for k, v in enumerate(per_lane):
        route = jnp.where(lane == k, v, route)
    route_ref[...] = route


def _mix(h, o_list, lse_list, cg, ga, w_attn_out, w_o, ln1_g, ln1_b, wr_hi, wr_lo, b_route):
    bsz, s, d = h.shape
    tm = ROW_TILE
    row = lambda b, i: (b, i, 0)
    dil_specs = [pl.BlockSpec((None, r, tm // r, GROUP_W), lambda b, i: (b, 0, i, 0)) for r in DILATIONS]
    o_views = [o.reshape(bsz, r, s // r, GROUP_W) for o, r in zip(o_list, DILATIONS)]
    l_views = [l.reshape(bsz, r, s // r, GROUP_W) for l, r in zip(lse_list, DILATIONS)]
    return pl.pallas_call(
        _mix_kernel,
        grid=(bsz, s // tm),
        in_specs=[pl.BlockSpec((None, tm, d), row)] + dil_specs + dil_specs + [
            pl.BlockSpec((None, tm, d), row), pl.BlockSpec((None, tm, d), row),
            _resident(w_attn_out.shape), _resident(w_o.shape),
            _resident((1, d)), _resident((1, d)),
            _resident(wr_hi.shape), _resident(wr_lo.shape), _resident((1, LANES)),
        ],
        out_specs=[_chunk_major_spec(F32_CHUNKS, tm, lambda b, i: (0, b * (s // tm) + i, 0)),
                   _chunk_major_spec(PACKED_CHUNKS, tm, lambda b, i: (0, b * (s // tm) + i, 0)),
                   pl.BlockSpec((None, tm, LANES), row),
                   pl.BlockSpec((1, LANES), lambda b, i: (0, 0))],
        out_shape=[jax.ShapeDtypeStruct((F32_CHUNKS, bsz * s, SC_COLS), F32),
                   jax.ShapeDtypeStruct((PACKED_CHUNKS, bsz * s, SC_COLS), U32),
                   jax.ShapeDtypeStruct((bsz, s, LANES), F32),
                   jax.ShapeDtypeStruct((1, LANES), F32)],
        scratch_shapes=[pltpu.VMEM((N_GROUPS * GROUP_W // LANES, tm, LANES), F32)] * 2
        + [pltpu.VMEM((tm, tm), BF16)],
        compiler_params=pltpu.CompilerParams(
            dimension_semantics=("arbitrary", "arbitrary"), vmem_limit_bytes=VMEM_LIMIT),
        name="mix",
    )(h, *o_views, *l_views, cg, ga, w_attn_out, w_o, ln1_g, ln1_b, wr_hi, wr_lo, b_route)


def _dispatch_plan(route, counts, t):
    lanes_of = lambda first: jnp.stack([route[:, first + k] for k in range(TOP_K_INNER)]).astype(jnp.int32)
    eid = lanes_of(0)
    rank = lanes_of(ROUTE_LANE_RANK)
    counts = counts[0, :N_EXPERTS].astype(jnp.int32)
    padded = (counts + EXPERT_BLOCK - 1) // EXPERT_BLOCK * EXPERT_BLOCK
    pad_end = jnp.cumsum(padded)
    pad_start = pad_end - padded
    experts = jnp.arange(N_EXPERTS, dtype=jnp.int32)[:, None, None]
    start_of = jnp.sum(jnp.where(eid[None] == experts, pad_start[:, None, None], 0), axis=0)
    dest = start_of + rank
    n_blocks = -(-t * TOP_K_INNER // EXPERT_BLOCK) + N_EXPERTS
    n_slots = n_blocks * EXPERT_BLOCK
    block_start = jnp.arange(n_blocks, dtype=jnp.int32) * EXPERT_BLOCK
    block_e = jnp.minimum(jnp.sum((block_start[:, None] >= pad_end[None, :]).astype(jnp.int32), axis=1),
                          N_EXPERTS - 1)
    n_empty = n_slots - t * TOP_K_INNER
    gap_end = jnp.cumsum(padded - counts)
    gap_start = gap_end - (padded - counts)
    k = jnp.arange(n_empty, dtype=jnp.int32)
    in_gap = (k[None, :] >= gap_start[:, None]) & (k[None, :] < gap_end[:, None])
    first_of_gap = (pad_start + counts - gap_start)[:, None]
    empty = jnp.where(k < gap_end[-1],
                      jnp.sum(jnp.where(in_gap, first_of_gap + k[None, :], 0), axis=0),
                      pad_end[-1] + k - gap_end[-1])
    return dest.astype(jnp.int32), empty.astype(jnp.int32), block_e


def _sc_mesh():
    return plsc.VectorSubcoreMesh(core_axis_name="core", subcore_axis_name="subcore")


def _sc_rows_pipeline(body, n_rows, row_index_map, cols):
    return pltpu.emit_pipeline(
        body,
        grid=(n_rows // SC_WINDOW,),
        in_specs=[pl.BlockSpec((SC_WINDOW, cols), index_map=row_index_map),
                  pl.BlockSpec((1, SC_WINDOW), index_map=lambda i: (0, i))],
        out_specs=[],
        core_axis_name=("core", "subcore"),
        dimension_semantics=(pltpu.PARALLEL,),
    )


def _chunk_rows(rows, chunks, n_rows_per_chunk):
    base = jnp.arange(chunks, dtype=jnp.int32)[:, None] * n_rows_per_chunk
    return (base + rows[None, :]).reshape(1, -1)


def _dispatch(rows, dest, empty, n_slots):
    chunks, t, _ = rows.shape
    x = rows.reshape(chunks * t, SC_COLS)
    idx_first = _chunk_rows(dest[0], chunks, n_slots)
    idx_second = _chunk_rows(dest[1], chunks, n_slots)
    idx_empty = _chunk_rows(empty, chunks, n_slots)
    zeros = jnp.zeros((SC_WINDOW, SC_COLS), rows.dtype)

    @pl.kernel(out_type=jax.ShapeDtypeStruct((chunks * n_slots, SC_COLS), rows.dtype), mesh=_sc_mesh(),
               scratch_types=[])
    def scatter(x_hbm, i0_hbm, i1_hbm, z_hbm, ie_hbm, o_hbm):
        def body(x_vmem, i_vmem):
            pltpu.sync_copy(x_vmem, o_hbm.at[i_vmem.at[0]])

        for i_hbm in (i0_hbm, i1_hbm):
            _sc_rows_pipeline(body, x.shape[0], lambda i: (i, 0), SC_COLS)(x_hbm, i_hbm)
        _sc_rows_pipeline(body, idx_empty.shape[1], lambda i: (0, 0), SC_COLS)(z_hbm, ie_hbm)

    return scatter(x, idx_first, idx_second, zeros, idx_empty).reshape(chunks, n_slots, SC_COLS)


def _expert_mlp(xb, wg_ref, wu_ref, wd_ref):
    acc = None
    for c in range(D_EXPERT // EXPERT_CHUNK):
        cols = slice(c * EXPERT_CHUNK, (c + 1) * EXPERT_CHUNK)
        gate = _dot(xb, wg_ref[:, cols])
        up = _dot(xb, wu_ref[:, cols])
        hidden = (gate * jax.nn.sigmoid(gate) * up).astype(BF16)
        part = _dot(hidden, wd_ref[cols, :])
        acc = part if acc is None else acc + part
    return acc


def _expert_kernel(be_ref, x_ref, wga_ref, wua_ref, wda_ref, wgb_ref, wub_ref, wdb_ref, y_ref):
    i = pl.program_id(0)
    same = be_ref[2 * i] == be_ref[2 * i + 1]

    @pl.when(same)
    def _():
        xb = _unpack_bf16_pairs(_load_chunk_major(x_ref, 0, 2 * EXPERT_BLOCK)).astype(BF16)
        _store_chunk_major(y_ref, 0, _pack_bf16_pairs(_expert_mlp(xb, wga_ref, wua_ref, wda_ref)))

    @pl.when(jnp.logical_not(same))
    def _():
        for half, w in enumerate(((wga_ref, wua_ref, wda_ref), (wgb_ref, wub_ref, wdb_ref))):
            xb = _unpack_bf16_pairs(_load_chunk_major(x_ref, half * EXPERT_BLOCK, EXPERT_BLOCK)).astype(BF16)
            _store_chunk_major(y_ref, half * EXPERT_BLOCK, _pack_bf16_pairs(_expert_mlp(xb, *w)))


def _experts(xs, block_e, w_gate, w_up, w_down):
    n_blocks = block_e.shape[0]
    assert n_blocks % 2 == 0
    d = D_MODEL
    blk = _chunk_major_spec(PACKED_CHUNKS, 2 * EXPERT_BLOCK, lambda i, be: (0, i, 0))
    w_specs = [pl.BlockSpec(shape, (lambda i, be, k=k: (be[2 * i + k], 0, 0)))
               for k in range(2) for shape in ((None, d, D_EXPERT), (None, d, D_EXPERT), (None, D_EXPERT, d))]
    grid_spec = pltpu.PrefetchScalarGridSpec(
        num_scalar_prefetch=1,
        grid=(n_blocks // 2,),
        in_specs=[blk] + w_specs,
        out_specs=blk,
    )
    return pl.pallas_call(
        _expert_kernel,
        grid_spec=grid_spec,
        out_shape=jax.ShapeDtypeStruct(xs.shape, xs.dtype),
        compiler_params=pltpu.CompilerParams(
            dimension_semantics=("arbitrary",), vmem_limit_bytes=VMEM_LIMIT),
        name="experts",
    )(block_e, xs, w_gate, w_up, w_down, w_gate, w_up, w_down)


def _gather(yb, dest):
    chunks, n_slots, _ = yb.shape
    x = yb.reshape(chunks * n_slots, SC_COLS)
    idx = _chunk_rows(dest.reshape(-1), chunks, n_slots)
    m = idx.shape[1]

    @pl.kernel(out_type=jax.ShapeDtypeStruct((m, SC_COLS), yb.dtype), mesh=_sc_mesh(), scratch_types=[])
    def gather(x_hbm, i_hbm, o_hbm):
        def body(i_vmem, o_vmem):
            pltpu.sync_copy(x_hbm.at[i_vmem.at[0]], o_vmem)

        pltpu.emit_pipeline(
            body,
            grid=(m // SC_WINDOW,),
            in_specs=[pl.BlockSpec((1, SC_WINDOW), index_map=lambda i: (0, i))],
            out_specs=[pl.BlockSpec((SC_WINDOW, SC_COLS), index_map=lambda i: (i, 0))],
            core_axis_name=("core", "subcore"),
            dimension_semantics=(pltpu.PARALLEL,),
        )(i_hbm, o_hbm)

    return gather(x, idx).reshape(chunks, m // chunks, SC_COLS)


def _combine_kernel(first_ref, second_ref, h1_ref, route_ref, g_ref, b_ref, y_ref):
    tm = y_ref.shape[0]
    route = route_ref[...]
    w_first = route[:, ROUTE_LANE_W:ROUTE_LANE_W + 1]
    w_second = route[:, ROUTE_LANE_W + 1:ROUTE_LANE_W + 2]
    ffn = (w_first * _unpack_bf16_pairs(_load_chunk_major(first_ref, 0, tm))
           + w_second * _unpack_bf16_pairs(_load_chunk_major(second_ref, 0, tm)))
    y_ref[...] = _layer_norm(DEEPNORM_ALPHA * _load_chunk_major(h1_ref, 0, tm) + ffn, g_ref[...], b_ref[...])


def _combine(h1, route, g, ln2_g, ln2_b):
    t = route.shape[0]
    d = D_MODEL
    tm = ROW_TILE
    nt = t // tm
    return pl.pallas_call(
        _combine_kernel,
        grid=(nt,),
        in_specs=[
            _chunk_major_spec(PACKED_CHUNKS, tm, lambda i: (0, i, 0)),
            _chunk_major_spec(PACKED_CHUNKS, tm, lambda i: (0, nt + i, 0)),
            _chunk_major_spec(F32_CHUNKS, tm, lambda i: (0, i, 0)),
            pl.BlockSpec((tm, LANES), lambda i: (i, 0)),
            _resident((1, d)), _resident((1, d)),
        ],
        out_specs=pl.BlockSpec((tm, d), lambda i: (i, 0)),
        out_shape=jax.ShapeDtypeStruct((t, d), F32),
        compiler_params=pltpu.CompilerParams(
            dimension_semantics=("parallel",), vmem_limit_bytes=VMEM_LIMIT),
        name="combine",
    )(g, g, h1, route, ln2_g, ln2_b)


def _token_mixing(x, p):
    bsz, s, d = x.shape
    qkv0, qkv1, qkv2, cg, ga, h = _inproj(x, p['ln_in_g'], p['ln_in_b'], p['w_in'], p['b_gate'], p['conv_w'],
                                          p['w_conv_out'])
    o_list, lse_list = [], []
    for qkv, r in zip((qkv0, qkv1, qkv2), DILATIONS):
        o, lse = _attention(qkv.reshape(3, bsz, s, GROUP_W), s // r)
        o_list.append(o)
        lse_list.append(lse)
    h1, h1_packed, route, counts = _mix(h, o_list, lse_list, cg, ga, p['w_attn_out'], p['w_o'],
                             p['ln1_g'], p['ln1_b'], p['wr_hi'], p['wr_lo'], p['b_route'])
    t = bsz * s
    route = route.reshape(t, LANES)
    dest, empty, block_e = _dispatch_plan(route, counts, t)
    return h1, route, dest, empty, block_e, h1_packed


def kernel(x_prompt, x_sample, ln_in_g, ln_in_b, w_in, b_gate, conv_w, w_attn_out, w_conv_out, w_o, ln1_g, ln1_b, w_route_group, b_route_group, w_route_expert, b_route_expert, w_gate, w_up, w_down, ln2_g, ln2_b):
    d = D_MODEL
    w_route = jnp.concatenate([w_route_group[0], w_route_expert[0]], axis=1)
    w_route = jnp.pad(w_route, ((0, 0), (0, LANES - w_route.shape[1])))
    wr_hi = w_route.astype(BF16)
    b_route = jnp.concatenate([b_route_group[0], b_route_expert[0]]).astype(F32)
    p = {
        'ln_in_g': ln_in_g.reshape(1, d), 'ln_in_b': ln_in_b.reshape(1, d),
        'w_in': w_in[0].astype(BF16), 'b_gate': b_gate[0].reshape(1, 2 * d), 'conv_w': conv_w[0],
        'w_conv_out': w_conv_out[0].astype(BF16), 'w_attn_out': w_attn_out[0].astype(BF16),
        'w_o': w_o[0].astype(BF16),
        'ln1_g': ln1_g[0].reshape(1, d), 'ln1_b': ln1_b[0].reshape(1, d),
        'wr_hi': wr_hi, 'wr_lo': (w_route - wr_hi.astype(F32)).astype(BF16),
        'b_route': jnp.pad(b_route, (0, LANES - b_route.shape[0])).reshape(1, LANES),
        'w_gate': w_gate[0].astype(BF16), 'w_up': w_up[0].astype(BF16), 'w_down': w_down[0].astype(BF16),
        'ln2_g': ln2_g[0].reshape(1, d), 'ln2_b': ln2_b[0].reshape(1, d),
    }
    batches = [_token_mixing(x, p) for x in (x_prompt, x_sample)]
    sorted_rows = [_dispatch(h1_packed, dest, empty, block_e.shape[0] * EXPERT_BLOCK)
                   for _, _, dest, empty, block_e, h1_packed in batches]
    gathered = [_gather(_experts(xs, b[4], p['w_gate'], p['w_up'], p['w_down']), b[2])
                for xs, b in zip(sorted_rows, batches)]
    outs = [_combine(b[0], b[1], g, p['ln2_g'], p['ln2_b']).reshape(x.shape)
            for b, g, x in zip(batches, gathered, (x_prompt, x_sample))]
    return tuple(outs)
```

```python
import functools

import jax
import jax.numpy as jnp
import numpy as np
from jax import lax
from jax.experimental import pallas as pl
from jax.experimental.pallas import tpu as pltpu
from jax.experimental.pallas import tpu_sc as plsc

F32 = jnp.float32
BF16 = jnp.bfloat16

D_MODEL = 1024
HEAD_DIM = 64
HEADS_PER_GROUP = 4
GROUP_W = HEADS_PER_GROUP * HEAD_DIM
DILATIONS = (1, 4, 16)
BAND = 64
N_GROUPS = len(DILATIONS)
D_ATTN = N_GROUPS * GROUP_W
ATTN_SCALE = HEAD_DIM ** -0.5
ROPE_DIM = HEAD_DIM // 4
ROPE_THETA = 500000.0
MASK_VALUE = -1e30
D_CONV = D_MODEL
COL_CONV_B = 3 * D_ATTN
COL_CONV_C = COL_CONV_B + D_CONV
COL_CONV_H = COL_CONV_C + D_CONV
COL_GATE_ATTN = COL_CONV_H + D_CONV
COL_GATE_CONV = COL_GATE_ATTN + D_MODEL
N_EXPERT_GROUPS = 4
EXPERTS_PER_GROUP = 8
N_EXPERTS = N_EXPERT_GROUPS * EXPERTS_PER_GROUP
TOP_K_INNER = 2
D_EXPERT = 512
EXPERT_BLOCK = 256
EXPERT_CHUNK = 256
LN_EPS = 1e-5
DEPTH = 1
DEEPNORM_ALPHA = (2 * DEPTH) ** 0.25

LANES = 128
ROW_TILE = 512
HALO = 16
Q_BLOCK = 128
K_BLOCK = Q_BLOCK + 2 * BAND
Q_CHUNK = 2048
SC_COLS = 256
SC_WINDOW = 128
F32_CHUNKS = D_MODEL // SC_COLS
PACKED_COLS = D_MODEL // 2
PACKED_CHUNKS = PACKED_COLS // SC_COLS
U32 = jnp.uint32
HIGH_HALF = np.uint32(0xFFFF0000)
ROUTE_LANE0 = N_EXPERT_GROUPS
ROUTE_LANE_W = TOP_K_INNER
ROUTE_LANE_RANK = 2 * TOP_K_INNER
VMEM_LIMIT = 56 * 1024 * 1024


def _layer_norm(v, g, b):
    mu = jnp.mean(v, axis=-1, keepdims=True)
    d = v - mu
    var = jnp.mean(d * d, axis=-1, keepdims=True)
    return d * lax.rsqrt(var + LN_EPS) * g + b


def _dot(a, b):
    return jnp.dot(a, b, preferred_element_type=F32)


def _store_chunk_major(ref, first_row, val):
    for c in range(ref.shape[0]):
        ref[c, pl.ds(first_row, val.shape[0]), :] = val[:, c * SC_COLS:(c + 1) * SC_COLS]


def _load_chunk_major(ref, first_row, n):
    return jnp.concatenate([ref[c, pl.ds(first_row, n), :] for c in range(ref.shape[0])], axis=1)


def _chunk_major_spec(chunks, rows, index_map):
    return pl.BlockSpec((chunks, rows, SC_COLS), index_map)


def _pack_bf16_pairs(val):
    bits = lax.bitcast_convert_type(val.astype(BF16).astype(F32), U32)
    return (bits[:, PACKED_COLS:] & HIGH_HALF) | (bits[:, :PACKED_COLS] >> 16)


def _unpack_bf16_pairs(packed):
    low = lax.bitcast_convert_type(packed << 16, F32)
    high = lax.bitcast_convert_type(packed & HIGH_HALF, F32)
    return jnp.concatenate([low, high], axis=1)


def _resident(shape):
    nd = len(shape)
    return pl.BlockSpec(shape, lambda *_: (0,) * nd, pipeline_mode=pl.Buffered(1))


def _inproj_kernel(x_ref, xp_ref, xn_ref, g_ref, b_ref, win_ref, bg_ref, cw_ref, wco_ref,
                   cos_ref, sa_ref, sb_ref,
                   qkv0_ref, qkv1_ref, qkv2_ref, cg_ref, ga_ref, h_ref,
                   hext_ref, u_ref, stage_ref):
    i = pl.program_id(1)
    nt = pl.num_programs(1)
    tm = x_ref.shape[0]
    g = g_ref[...]
    b = b_ref[...]
    h = _layer_norm(x_ref[...], g, b)
    h_ref[...] = h
    hext_ref[pl.ds(HALO, tm), :] = h.astype(BF16)
    hext_ref[pl.ds(0, HALO), :] = _layer_norm(xp_ref[...], g, b).astype(BF16)
    hext_ref[pl.ds(HALO + tm, HALO), :] = _layer_norm(xn_ref[...], g, b).astype(BF16)
    hm = hext_ref[pl.ds(HALO, tm), :]
    he = hext_ref[...]

    def proj(lhs, col, n):
        return _dot(lhs, win_ref[:, col:col + n])

    u = proj(he, COL_CONV_C, D_CONV) * proj(he, COL_CONV_H, D_CONV)
    rows = lax.broadcasted_iota(jnp.int32, (tm + 2 * HALO, 1), 0)
    lo = jnp.where(i == 0, HALO, 0)
    hi = jnp.where(i == nt - 1, HALO + tm, tm + 2 * HALO)
    u_ref[...] = jnp.where((rows >= lo) & (rows < hi), u, 0.0)
    cw = cw_ref[...]
    conv = (cw[0:1] * u_ref[pl.ds(HALO - 1, tm), :] + cw[1:2] * u_ref[pl.ds(HALO, tm), :]
            + cw[2:3] * u_ref[pl.ds(HALO + 1, tm), :])
    cb = proj(hm, COL_CONV_B, D_CONV)
    conv_branch = _dot((cb * conv).astype(BF16), wco_ref[...])
    gate_conv = jax.nn.sigmoid(proj(hm, COL_GATE_CONV, D_MODEL) + bg_ref[:, D_MODEL:])
    cg_ref[...] = (gate_conv * conv_branch).astype(BF16)
    ga_ref[...] = jax.nn.sigmoid(proj(hm, COL_GATE_ATTN, D_MODEL) + bg_ref[:, :D_MODEL]).astype(BF16)

    cosv = cos_ref[...]
    sav = sa_ref[...]
    sbv = sb_ref[...]
    half = ROPE_DIM // 2
    for gi, (out_ref, r) in enumerate(zip((qkv0_ref, qkv1_ref, qkv2_ref), DILATIONS)):
        for t in range(3):
            p = proj(hm, t * D_ATTN + gi * GROUP_W, GROUP_W)
            if t < 2:
                parts = []
                for c in range(GROUP_W // LANES):
                    pc = p[:, c * LANES:(c + 1) * LANES]
                    parts.append(pc * cosv + pltpu.roll(pc, LANES - half, 1) * sav
                                 + pltpu.roll(pc, half, 1) * sbv)
                p = jnp.concatenate(parts, axis=1)
                if t == 0:
                    p = p * ATTN_SCALE
            if r == 1:
                out_ref[t, 0] = p.astype(BF16)
            else:
                for c in range(GROUP_W // LANES):
                    stage_ref[c] = p[:, c * LANES:(c + 1) * LANES]
                for res in range(r):
                    for c in range(GROUP_W // LANES):
                        out_ref[t, res, :, c * LANES:(c + 1) * LANES] = (
                            stage_ref[c, pl.ds(res, tm // r, stride=r), :].astype(BF16))


def _rope_tables(s):
    half = ROPE_DIM // 2
    inv_freq = (np.float32(ROPE_THETA) ** (-np.arange(half, dtype=np.float32) * np.float32(2.0) / ROPE_DIM))
    ang = np.arange(s, dtype=np.float32)[:, None] * inv_freq.astype(np.float32)[None, :]
    cos = np.cos(ang).astype(np.float32)
    sin = np.sin(ang).astype(np.float32)
    pad = np.zeros((s, HEAD_DIM - ROPE_DIM), np.float32)
    zero = np.zeros((s, half), np.float32)
    cos_t = np.concatenate([cos, cos, pad + 1.0], axis=1)
    sa_t = np.concatenate([-sin, zero, pad], axis=1)
    sb_t = np.concatenate([zero, sin, pad], axis=1)
    rep = LANES // HEAD_DIM
    return tuple(jnp.asarray(np.tile(t, (1, rep))) for t in (cos_t, sa_t, sb_t))


def _inproj(x, ln_g, ln_b, w_in, b_gate, conv_w, w_conv_out):
    bsz, s, d = x.shape
    tm = ROW_TILE
    nt = s // tm
    hpt = tm // HALO
    cos_t, sa_t, sb_t = _rope_tables(s)
    row = lambda b, i: (b, i, 0)
    tab = pl.BlockSpec((tm, LANES), lambda b, i: (i, 0))
    qkv_shapes = [jax.ShapeDtypeStruct((3, bsz, r, s // r, GROUP_W), BF16) for r in DILATIONS]
    qkv_specs = [pl.BlockSpec((3, None, r, tm // r, GROUP_W), lambda b, i: (0, b, 0, i, 0))
                 for r in DILATIONS]
    return pl.pallas_call(
        _inproj_kernel,
        grid=(bsz, nt),
        in_specs=[
            pl.BlockSpec((None, tm, d), row),
            pl.BlockSpec((None, HALO, d), lambda b, i: (b, jnp.maximum(i * hpt - 1, 0), 0)),
            pl.BlockSpec((None, HALO, d), lambda b, i: (b, jnp.minimum((i + 1) * hpt, s // HALO - 1), 0)),
            _resident((1, d)), _resident((1, d)),
            _resident(w_in.shape), _resident((1, 2 * d)), _resident(conv_w.shape),
            _resident(w_conv_out.shape),
            tab, tab, tab,
        ],
        out_specs=qkv_specs + [pl.BlockSpec((None, tm, d), row)] * 3,
        out_shape=(qkv_shapes + [jax.ShapeDtypeStruct((bsz, s, d), BF16)] * 2
                   + [jax.ShapeDtypeStruct((bsz, s, d), F32)]),
        scratch_shapes=[
            pltpu.VMEM((tm + 2 * HALO, d), BF16),
            pltpu.VMEM((tm + 2 * HALO, D_CONV), F32),
            pltpu.VMEM((GROUP_W // LANES, tm, LANES), F32),
        ],
        compiler_params=pltpu.CompilerParams(
            dimension_semantics=("parallel", "parallel"), vmem_limit_bytes=VMEM_LIMIT),
        name="inproj",
    )(x, x, x, ln_g, ln_b, w_in, b_gate, conv_w, w_conv_out, cos_t, sa_t, sb_t)


def _attn_kernel(q_ref, k_ref, v_ref, o_ref, lse_ref, *, sub_len):
    s = k_ref.shape[0]
    qc = q_ref.shape[0]
    base = pl.program_id(1) * qc
    lane = lax.broadcasted_iota(jnp.int32, (1, GROUP_W), 1)
    head_masks = [(lane >= h * HEAD_DIM) & (lane < (h + 1) * HEAD_DIM) for h in range(HEADS_PER_GROUP)]
    qi = lax.broadcasted_iota(jnp.int32, (Q_BLOCK, 1), 0)
    kj = lax.broadcasted_iota(jnp.int32, (1, K_BLOCK), 1)

    def body(j, carry):
        r0 = pl.multiple_of(j * Q_BLOCK, Q_BLOCK)
        s0 = base + r0
        k0 = pl.multiple_of(jnp.clip(s0 - BAND, 0, s - K_BLOCK), BAND)
        q = q_ref[pl.ds(r0, Q_BLOCK), :]
        k = k_ref[pl.ds(k0, K_BLOCK), :]
        v = v_ref[pl.ds(k0, K_BLOCK), :]
        sub_lo = (s0 // sub_len) * sub_len
        qpos = s0 + qi
        kpos = k0 + kj
        valid = (jnp.abs(qpos - kpos) <= BAND) & (kpos >= sub_lo) & (kpos < sub_lo + sub_len)
        qm = jnp.concatenate([jnp.where(hm, q, jnp.zeros_like(q)) for hm in head_masks], axis=0)
        sc = lax.dot_general(qm, k, (((1,), (1,)), ((), ())), preferred_element_type=F32)
        sc = jnp.where(jnp.concatenate([valid] * HEADS_PER_GROUP, axis=0), sc, MASK_VALUE)
        m = jnp.max(sc, axis=-1, keepdims=True)
        p = jnp.exp(sc - m)
        den = jnp.sum(p, axis=-1, keepdims=True)
        o_all = _dot((p * (1.0 / den)).astype(BF16), v)
        lse_all = m + jnp.log(den)
        o_acc = jnp.zeros((Q_BLOCK, GROUP_W), F32)
        lse_acc = jnp.zeros((Q_BLOCK, GROUP_W), F32)
        for h, hm in enumerate(head_masks):
            rows = slice(h * Q_BLOCK, (h + 1) * Q_BLOCK)
            o_acc = jnp.where(hm, o_all[rows], o_acc)
            lse_acc = jnp.where(hm, lse_all[rows], lse_acc)
        o_ref[pl.ds(r0, Q_BLOCK), :] = o_acc.astype(BF16)
        lse_ref[pl.ds(r0, Q_BLOCK), :] = lse_acc
        return carry

    lax.fori_loop(0, qc // Q_BLOCK, body, 0, unroll=True)


def _attention(qkv, sub_len):
    _, bsz, s, w = qkv.shape
    qc = min(Q_CHUNK, s)
    kv_spec = lambda t: pl.BlockSpec((None, None, s, w), lambda b, j: (t, b, 0, 0))
    return pl.pallas_call(
        functools.partial(_attn_kernel, sub_len=sub_len),
        grid=(bsz, s // qc),
        in_specs=[pl.BlockSpec((None, None, qc, w), lambda b, j: (0, b, j, 0)), kv_spec(1), kv_spec(2)],
        out_specs=[pl.BlockSpec((None, qc, w), lambda b, j: (b, j, 0))] * 2,
        out_shape=[jax.ShapeDtypeStruct((bsz, s, w), BF16), jax.ShapeDtypeStruct((bsz, s, w), F32)],
        compiler_params=pltpu.CompilerParams(
            dimension_semantics=("parallel", "arbitrary"), vmem_limit_bytes=VMEM_LIMIT),
        name="attn",
    )(qkv, qkv, qkv)


def _first_index_of_max(vals, lane_f):
    mx = jnp.max(vals, axis=-1, keepdims=True)
    idx = jnp.min(jnp.where(vals == mx, lane_f, float(LANES)), axis=-1, keepdims=True)
    return mx, idx


def _mix_kernel(h_ref, o0_ref, o1_ref, o2_ref, l0_ref, l1_ref, l2_ref, cg_ref, ga_ref,
                wao_ref, wo_ref, g1_ref, b1_ref, wrh_ref, wrl_ref, br_ref,
                h1_ref, h1p_ref, route_ref, counts_ref, so_ref, sl_ref, tri_ref):
    tm = h_ref.shape[0]
    halves = GROUP_W // LANES
    for gi, (o_ref, l_ref, r) in enumerate(zip((o0_ref, o1_ref, o2_ref), (l0_ref, l1_ref, l2_ref), DILATIONS)):
        for c in range(halves):
            cols = slice(c * LANES, (c + 1) * LANES)
            for res in range(r):
                rows = pl.ds(res, tm // r, stride=r) if r > 1 else slice(None)
                so_ref[gi * halves + c, rows, :] = o_ref[res, :, cols].astype(F32)
                sl_ref[gi * halves + c, rows, :] = l_ref[res, :, cols]

    def natural(ref, gi):
        return jnp.concatenate([ref[gi * halves + c] for c in range(halves)], axis=1)

    lses = [natural(sl_ref, gi) for gi in range(N_GROUPS)]
    mx = jnp.maximum(jnp.maximum(lses[0], lses[1]), lses[2])
    es = [jnp.exp(l - mx) for l in lses]
    inv_den = 1.0 / (es[0] + es[1] + es[2])
    attn = jnp.concatenate(
        [(natural(so_ref, gi) * (es[gi] * inv_den)).astype(BF16) for gi in range(N_GROUPS)], axis=1)
    attn_branch = _dot(attn, wao_ref[...])
    merged = ga_ref[...].astype(F32) * attn_branch + cg_ref[...].astype(F32)
    mix = _dot(merged.astype(BF16), wo_ref[...])
    h1 = _layer_norm(DEEPNORM_ALPHA * h_ref[...] + mix, g1_ref[...], b1_ref[...])
    _store_chunk_major(h1_ref, 0, h1)
    _store_chunk_major(h1p_ref, 0, _pack_bf16_pairs(h1))

    hi = h1.astype(BF16)
    lo = (h1 - hi.astype(F32)).astype(BF16)
    wrh = wrh_ref[...]
    logits = _dot(hi, wrh) + _dot(lo, wrh) + _dot(hi, wrl_ref[...]) + br_ref[...]
    lane = lax.broadcasted_iota(jnp.int32, (1, LANES), 1)
    lane_f = lane.astype(F32)
    is_grp = lane < N_EXPERT_GROUPS
    gl = jnp.where(is_grp, logits, MASK_VALUE)
    ge = jnp.exp(gl - jnp.max(gl, axis=-1, keepdims=True))
    gp = jnp.where(is_grp, ge / jnp.sum(ge, axis=-1, keepdims=True), -1.0)
    grp_w, grp_idx = _first_index_of_max(gp, lane_f)
    e_lo = ROUTE_LANE0 + grp_idx * EXPERTS_PER_GROUP
    sel = jnp.where((lane_f >= e_lo) & (lane_f < e_lo + EXPERTS_PER_GROUP), logits, MASK_VALUE)
    v1, i1 = _first_index_of_max(sel, lane_f)
    v2, i2 = _first_index_of_max(jnp.where(lane_f == i1, MASK_VALUE, sel), lane_f)
    e2 = jnp.exp(v2 - v1)
    w1 = grp_w / (1.0 + e2)
    w2 = grp_w * e2 / (1.0 + e2)

    first = (pl.program_id(0) == 0) & (pl.program_id(1) == 0)

    @pl.when(first)
    def _():
        counts_ref[...] = jnp.zeros_like(counts_ref)
        r = lax.broadcasted_iota(jnp.int32, (tm, tm), 0)
        c = lax.broadcasted_iota(jnp.int32, (tm, tm), 1)
        tri_ref[...] = (c < r).astype(BF16)

    oh0 = lane_f == i1 - ROUTE_LANE0
    oh1 = lane_f == i2 - ROUTE_LANE0
    oh = oh0.astype(F32) + oh1.astype(F32)
    before = counts_ref[...] + _dot(tri_ref[...], oh.astype(BF16))
    r0 = jnp.sum(jnp.where(oh0, before, 0.0), axis=-1, keepdims=True)
    r1 = jnp.sum(jnp.where(oh1, before + oh0.astype(F32), 0.0), axis=-1, keepdims=True)
    counts_ref[...] = counts_ref[...] + jnp.sum(oh, axis=0, keepdims=True)

    per_lane = (i1 - ROUTE_LANE0, i2 - ROUTE_LANE0, w1, w2, r0, r1)
    route = jnp.zeros((tm, LANES), F32)
    for k, v in enumerate(per_lane):
        route = jnp.where(lane == k, v, route)
    route_ref[...] = route


def _mix(h, o_list, lse_list, cg, ga, w_attn_out, w_o, ln1_g, ln1_b, wr_hi, wr_lo, b_route):
    bsz, s, d = h.shape
    tm = ROW_TILE
    row = lambda b, i: (b, i, 0)
    dil_specs = [pl.BlockSpec((None, r, tm // r, GROUP_W), lambda b, i: (b, 0, i, 0)) for r in DILATIONS]
    o_views = [o.reshape(bsz, r, s // r, GROUP_W) for o, r in zip(o_list, DILATIONS)]
    l_views = [l.reshape(bsz, r, s // r, GROUP_W) for l, r in zip(lse_list, DILATIONS)]
    return pl.pallas_call(
        _mix_kernel,
        grid=(bsz, s // tm),
        in_specs=[pl.BlockSpec((None, tm, d), row)] + dil_specs + dil_specs + [
            pl.BlockSpec((None, tm, d), row), pl.BlockSpec((None, tm, d), row),
            _resident(w_attn_out.shape), _resident(w_o.shape),
            _resident((1, d)), _resident((1, d)),
            _resident(wr_hi.shape), _resident(wr_lo.shape), _resident((1, LANES)),
        ],
        out_specs=[_chunk_major_spec(F32_CHUNKS, tm, lambda b, i: (0, b * (s // tm) + i, 0)),
                   _chunk_major_spec(PACKED_CHUNKS, tm, lambda b, i: (0, b * (s // tm) + i, 0)),
                   pl.BlockSpec((None, tm, LANES), row),
                   pl.BlockSpec((1, LANES), lambda b, i: (0, 0))],
        out_shape=[jax.ShapeDtypeStruct((F32_CHUNKS, bsz * s, SC_COLS), F32),
                   jax.ShapeDtypeStruct((PACKED_CHUNKS, bsz * s, SC_COLS), U32),
                   jax.ShapeDtypeStruct((bsz, s, LANES), F32),
                   jax.ShapeDtypeStruct((1, LANES), F32)],
        scratch_shapes=[pltpu.VMEM((N_GROUPS * GROUP_W // LANES, tm, LANES), F32)] * 2
        + [pltpu.VMEM((tm, tm), BF16)],
        compiler_params=pltpu.CompilerParams(
            dimension_semantics=("arbitrary", "arbitrary"), vmem_limit_bytes=VMEM_LIMIT),
        name="mix",
    )(h, *o_views, *l_views, cg, ga, w_attn_out, w_o, ln1_g, ln1_b, wr_hi, wr_lo, b_route)


def _dispatch_plan(route, counts, t):
    lanes_of = lambda first: jnp.stack([route[:, first + k] for k in range(TOP_K_INNER)]).astype(jnp.int32)
    eid = lanes_of(0)
    rank = lanes_of(ROUTE_LANE_RANK)
    counts = counts[0, :N_EXPERTS].astype(jnp.int32)
    padded = (counts + EXPERT_BLOCK - 1) // EXPERT_BLOCK * EXPERT_BLOCK
    pad_end = jnp.cumsum(padded)
    pad_start = pad_end - padded
    experts = jnp.arange(N_EXPERTS, dtype=jnp.int32)[:, None, None]
    start_of = jnp.sum(jnp.where(eid[None] == experts, pad_start[:, None, None], 0), axis=0)
    dest = start_of + rank
    n_blocks = -(-t * TOP_K_INNER // EXPERT_BLOCK) + N_EXPERTS
    n_slots = n_blocks * EXPERT_BLOCK
    block_start = jnp.arange(n_blocks, dtype=jnp.int32) * EXPERT_BLOCK
    block_e = jnp.minimum(jnp.sum((block_start[:, None] >= pad_end[None, :]).astype(jnp.int32), axis=1),
                          N_EXPERTS - 1)
    n_empty = n_slots - t * TOP_K_INNER
    gap_end = jnp.cumsum(padded - counts)
    gap_start = gap_end - (padded - counts)
    k = jnp.arange(n_empty, dtype=jnp.int32)
    in_gap = (k[None, :] >= gap_start[:, None]) & (k[None, :] < gap_end[:, None])
    first_of_gap = (pad_start + counts - gap_start)[:, None]
    empty = jnp.where(k < gap_end[-1],
                      jnp.sum(jnp.where(in_gap, first_of_gap + k[None, :], 0), axis=0),
                      pad_end[-1] + k - gap_end[-1])
    return dest.astype(jnp.int32), empty.astype(jnp.int32), block_e


def _sc_mesh():
    return plsc.VectorSubcoreMesh(core_axis_name="core", subcore_axis_name="subcore")


def _sc_rows_pipeline(body, n_rows, row_index_map, cols):
    return pltpu.emit_pipeline(
        body,
        grid=(n_rows // SC_WINDOW,),
        in_specs=[pl.BlockSpec((SC_WINDOW, cols), index_map=row_index_map),
                  pl.BlockSpec((1, SC_WINDOW), index_map=lambda i: (0, i))],
        out_specs=[],
        core_axis_name=("core", "subcore"),
        dimension_semantics=(pltpu.PARALLEL,),
    )


def _chunk_rows(rows, chunks, n_rows_per_chunk):
    base = jnp.arange(chunks, dtype=jnp.int32)[:, None] * n_rows_per_chunk
    return (base + rows[None, :]).reshape(1, -1)


def _dispatch(rows, dest, empty, n_slots):
    chunks, t, _ = rows.shape
    x = rows.reshape(chunks * t, SC_COLS)
    idx_first = _chunk_rows(dest[0], chunks, n_slots)
    idx_second = _chunk_rows(dest[1], chunks, n_slots)
    idx_empty = _chunk_rows(empty, chunks, n_slots)
    zeros = jnp.zeros((SC_WINDOW, SC_COLS), rows.dtype)

    @pl.kernel(out_type=jax.ShapeDtypeStruct((chunks * n_slots, SC_COLS), rows.dtype), mesh=_sc_mesh(),
               scratch_types=[])
    def scatter(x_hbm, i0_hbm, i1_hbm, z_hbm, ie_hbm, o_hbm):
        def body(x_vmem, i_vmem):
            pltpu.sync_copy(x_vmem, o_hbm.at[i_vmem.at[0]])

        for i_hbm in (i0_hbm, i1_hbm):
            _sc_rows_pipeline(body, x.shape[0], lambda i: (i, 0), SC_COLS)(x_hbm, i_hbm)
        _sc_rows_pipeline(body, idx_empty.shape[1], lambda i: (0, 0), SC_COLS)(z_hbm, ie_hbm)

    return scatter(x, idx_first, idx_second, zeros, idx_empty).reshape(chunks, n_slots, SC_COLS)


def _cast_kernel(*refs):
    n = len(refs) // 2
    for src, dst in zip(refs[:n], refs[n:]):
        dst[...] = src[...].astype(dst.dtype)


def _expert_weights_bf16(w_gate, w_up, w_down):
    ws = (w_gate, w_up, w_down)
    specs = [pl.BlockSpec((None,) + w.shape[1:], lambda e: (e, 0, 0)) for w in ws]
    return pl.pallas_call(
        _cast_kernel,
        grid=(N_EXPERTS,),
        in_specs=specs,
        out_specs=specs,
        out_shape=[jax.ShapeDtypeStruct(w.shape, BF16) for w in ws],
        compiler_params=pltpu.CompilerParams(dimension_semantics=("parallel",), vmem_limit_bytes=VMEM_LIMIT),
        name="cast_weights",
    )(*ws)


def _expert_mlp(xb, wg_ref, wu_ref, wd_ref):
    acc = None
    for c in range(D_EXPERT // EXPERT_CHUNK):
        cols = slice(c * EXPERT_CHUNK, (c + 1) * EXPERT_CHUNK)
        gate = _dot(xb, wg_ref[:, cols])
        up = _dot(xb, wu_ref[:, cols])
        hidden = (gate * jax.nn.sigmoid(gate) * up).astype(BF16)
        part = _dot(hidden, wd_ref[cols, :])
        acc = part if acc is None else acc + part
    return acc


def _expert_kernel(be_ref, x_ref, wga_ref, wua_ref, wda_ref, wgb_ref, wub_ref, wdb_ref, y_ref):
    i = pl.program_id(0)
    same = be_ref[2 * i] == be_ref[2 * i + 1]

    @pl.when(same)
    def _():
        xb = _unpack_bf16_pairs(_load_chunk_major(x_ref, 0, 2 * EXPERT_BLOCK)).astype(BF16)
        _store_chunk_major(y_ref, 0, _pack_bf16_pairs(_expert_mlp(xb, wga_ref, wua_ref, wda_ref)))

    @pl.when(jnp.logical_not(same))
    def _():
        for half, w in enumerate(((wga_ref, wua_ref, wda_ref), (wgb_ref, wub_ref, wdb_ref))):
            xb = _unpack_bf16_pairs(_load_chunk_major(x_ref, half * EXPERT_BLOCK, EXPERT_BLOCK)).astype(BF16)
            _store_chunk_major(y_ref, half * EXPERT_BLOCK, _pack_bf16_pairs(_expert_mlp(xb, *w)))


def _experts(xs, block_e, w_gate, w_up, w_down):
    n_blocks = block_e.shape[0]
    assert n_blocks % 2 == 0
    d = D_MODEL
    blk = _chunk_major_spec(PACKED_CHUNKS, 2 * EXPERT_BLOCK, lambda i, be: (0, i, 0))
    w_specs = [pl.BlockSpec(shape, (lambda i, be, k=k: (be[2 * i + k], 0, 0)))
               for k in range(2) for shape in ((None, d, D_EXPERT), (None, d, D_EXPERT), (None, D_EXPERT, d))]
    grid_spec = pltpu.PrefetchScalarGridSpec(
        num_scalar_prefetch=1,
        grid=(n_blocks // 2,),
        in_specs=[blk] + w_specs,
        out_specs=blk,
    )
    return pl.pallas_call(
        _expert_kernel,
        grid_spec=grid_spec,
        out_shape=jax.ShapeDtypeStruct(xs.shape, xs.dtype),
        compiler_params=pltpu.CompilerParams(
            dimension_semantics=("arbitrary",), vmem_limit_bytes=VMEM_LIMIT),
        name="experts",
    )(block_e, xs, w_gate, w_up, w_down, w_gate, w_up, w_down)


def _gather(yb, dest):
    chunks, n_slots, _ = yb.shape
    x = yb.reshape(chunks * n_slots, SC_COLS)
    idx = _chunk_rows(dest.reshape(-1), chunks, n_slots)
    m = idx.shape[1]

    @pl.kernel(out_type=jax.ShapeDtypeStruct((m, SC_COLS), yb.dtype), mesh=_sc_mesh(), scratch_types=[])
    def gather(x_hbm, i_hbm, o_hbm):
        def body(i_vmem, o_vmem):
            pltpu.sync_copy(x_hbm.at[i_vmem.at[0]], o_vmem)

        pltpu.emit_pipeline(
            body,
            grid=(m // SC_WINDOW,),
            in_specs=[pl.BlockSpec((1, SC_WINDOW), index_map=lambda i: (0, i))],
            out_specs=[pl.BlockSpec((SC_WINDOW, SC_COLS), index_map=lambda i: (i, 0))],
            core_axis_name=("core", "subcore"),
            dimension_semantics=(pltpu.PARALLEL,),
        )(i_hbm, o_hbm)

    return gather(x, idx).reshape(chunks, m // chunks, SC_COLS)


def _combine_kernel(first_ref, second_ref, h1_ref, route_ref, g_ref, b_ref, y_ref):
    tm = y_ref.shape[0]
    route = route_ref[...]
    w_first = route[:, ROUTE_LANE_W:ROUTE_LANE_W + 1]
    w_second = route[:, ROUTE_LANE_W + 1:ROUTE_LANE_W + 2]
    ffn = (w_first * _unpack_bf16_pairs(_load_chunk_major(first_ref, 0, tm))
           + w_second * _unpack_bf16_pairs(_load_chunk_major(second_ref, 0, tm)))
    y_ref[...] = _layer_norm(DEEPNORM_ALPHA * _load_chunk_major(h1_ref, 0, tm) + ffn, g_ref[...], b_ref[...])


def _combine(h1, route, g, ln2_g, ln2_b):
    t = route.shape[0]
    d = D_MODEL
    tm = ROW_TILE
    nt = t // tm
    return pl.pallas_call(
        _combine_kernel,
        grid=(nt,),
        in_specs=[
            _chunk_major_spec(PACKED_CHUNKS, tm, lambda i: (0, i, 0)),
            _chunk_major_spec(PACKED_CHUNKS, tm, lambda i: (0, nt + i, 0)),
            _chunk_major_spec(F32_CHUNKS, tm, lambda i: (0, i, 0)),
            pl.BlockSpec((tm, LANES), lambda i: (i, 0)),
            _resident((1, d)), _resident((1, d)),
        ],
        out_specs=pl.BlockSpec((tm, d), lambda i: (i, 0)),
        out_shape=jax.ShapeDtypeStruct((t, d), F32),
        compiler_params=pltpu.CompilerParams(
            dimension_semantics=("parallel",), vmem_limit_bytes=VMEM_LIMIT),
        name="combine",
    )(g, g, h1, route, ln2_g, ln2_b)


def _token_mixing(x, p):
    bsz, s, d = x.shape
    qkv0, qkv1, qkv2, cg, ga, h = _inproj(x, p['ln_in_g'], p['ln_in_b'], p['w_in'], p['b_gate'], p['conv_w'],
                                          p['w_conv_out'])
    o_list, lse_list = [], []
    for qkv, r in zip((qkv0, qkv1, qkv2), DILATIONS):
        o, lse = _attention(qkv.reshape(3, bsz, s, GROUP_W), s // r)
        o_list.append(o)
        lse_list.append(lse)
    h1, h1_packed, route, counts = _mix(h, o_list, lse_list, cg, ga, p['w_attn_out'], p['w_o'],
                             p['ln1_g'], p['ln1_b'], p['wr_hi'], p['wr_lo'], p['b_route'])
    t = bsz * s
    route = route.reshape(t, LANES)
    dest, empty, block_e = _dispatch_plan(route, counts, t)
    return h1, route, dest, empty, block_e, h1_packed


def kernel(x_prompt, x_sample, ln_in_g, ln_in_b, w_in, b_gate, conv_w, w_attn_out, w_conv_out, w_o, ln1_g, ln1_b, w_route_group, b_route_group, w_route_expert, b_route_expert, w_gate, w_up, w_down, ln2_g, ln2_b):
    d = D_MODEL
    w_route = jnp.concatenate([w_route_group[0], w_route_expert[0]], axis=1)
    w_route = jnp.pad(w_route, ((0, 0), (0, LANES - w_route.shape[1])))
    wr_hi = w_route.astype(BF16)
    b_route = jnp.concatenate([b_route_group[0], b_route_expert[0]]).astype(F32)
    p = {
        'ln_in_g': ln_in_g.reshape(1, d), 'ln_in_b': ln_in_b.reshape(1, d),
        'w_in': w_in[0].astype(BF16), 'b_gate': b_gate[0].reshape(1, 2 * d), 'conv_w': conv_w[0],
        'w_conv_out': w_conv_out[0].astype(BF16), 'w_attn_out': w_attn_out[0].astype(BF16),
        'w_o': w_o[0].astype(BF16),
        'ln1_g': ln1_g[0].reshape(1, d), 'ln1_b': ln1_b[0].reshape(1, d),
        'wr_hi': wr_hi, 'wr_lo': (w_route - wr_hi.astype(F32)).astype(BF16),
        'b_route': jnp.pad(b_route, (0, LANES - b_route.shape[0])).reshape(1, LANES),
        'ln2_g': ln2_g[0].reshape(1, d), 'ln2_b': ln2_b[0].reshape(1, d),
    }
    p['w_gate'], p['w_up'], p['w_down'] = _expert_weights_bf16(w_gate[0], w_up[0], w_down[0])
    batches = [_token_mixing(x, p) for x in (x_prompt, x_sample)]
    sorted_rows = [_dispatch(h1_packed, dest, empty, block_e.shape[0] * EXPERT_BLOCK)
                   for _, _, dest, empty, block_e, h1_packed in batches]
    gathered = [_gather(_experts(xs, b[4], p['w_gate'], p['w_up'], p['w_down']), b[2])
                for xs, b in zip(sorted_rows, batches)]
    outs = [_combine(b[0], b[1], g, p['ln2_g'], p['ln2_b']).reshape(x.shape)
            for b, g, x in zip(batches, gathered, (x_prompt, x_sample))]
    return tuple(outs)
```

```python
import functools

import jax
import jax.numpy as jnp
import numpy as np
from jax import lax
from jax.experimental import pallas as pl
from jax.experimental.pallas import tpu as pltpu
from jax.experimental.pallas import tpu_sc as plsc

F32 = jnp.float32
BF16 = jnp.bfloat16

D_MODEL = 1024
HEAD_DIM = 64
HEADS_PER_GROUP = 4
GROUP_W = HEADS_PER_GROUP * HEAD_DIM
DILATIONS = (1, 4, 16)
BAND = 64
N_GROUPS = len(DILATIONS)
D_ATTN = N_GROUPS * GROUP_W
ATTN_SCALE = HEAD_DIM ** -0.5
ROPE_DIM = HEAD_DIM // 4
ROPE_THETA = 500000.0
MASK_VALUE = -1e30
D_CONV = D_MODEL
COL_CONV_B = 3 * D_ATTN
COL_CONV_C = COL_CONV_B + D_CONV
COL_CONV_H = COL_CONV_C + D_CONV
COL_GATE_ATTN = COL_CONV_H + D_CONV
COL_GATE_CONV = COL_GATE_ATTN + D_MODEL
N_EXPERT_GROUPS = 4
EXPERTS_PER_GROUP = 8
N_EXPERTS = N_EXPERT_GROUPS * EXPERTS_PER_GROUP
TOP_K_INNER = 2
D_EXPERT = 512
EXPERT_BLOCK = 256
EXPERT_CHUNK = 256
LN_EPS = 1e-5
DEPTH = 1
DEEPNORM_ALPHA = (2 * DEPTH) ** 0.25

LANES = 128
ROW_TILE = 512
HALO = 16
Q_BLOCK = 128
K_BLOCK = Q_BLOCK + 2 * BAND
Q_CHUNK = 2048
SC_COLS = 256
SC_WINDOW = 128
F32_CHUNKS = D_MODEL // SC_COLS
PACKED_COLS = D_MODEL // 2
PACKED_CHUNKS = PACKED_COLS // SC_COLS
U32 = jnp.uint32
HIGH_HALF = np.uint32(0xFFFF0000)
ROUTE_LANE0 = N_EXPERT_GROUPS
ROUTE_LANE_W = TOP_K_INNER
ROUTE_LANE_RANK = 2 * TOP_K_INNER
PLAN_ROWS = 8
VMEM_LIMIT = 56 * 1024 * 1024


def _layer_norm(v, g, b):
    mu = jnp.mean(v, axis=-1, keepdims=True)
    d = v - mu
    var = jnp.mean(d * d, axis=-1, keepdims=True)
    return d * lax.rsqrt(var + LN_EPS) * g + b


def _dot(a, b):
    return jnp.dot(a, b, preferred_element_type=F32)


def _store_chunk_major(ref, first_row, val):
    for c in range(ref.shape[0]):
        ref[c, pl.ds(first_row, val.shape[0]), :] = val[:, c * SC_COLS:(c + 1) * SC_COLS]


def _load_chunk_major(ref, first_row, n):
    return jnp.concatenate([ref[c, pl.ds(first_row, n), :] for c in range(ref.shape[0])], axis=1)


def _chunk_major_spec(chunks, rows, index_map):
    return pl.BlockSpec((chunks, rows, SC_COLS), index_map)


def _pack_bf16_pairs(val):
    bits = lax.bitcast_convert_type(val.astype(BF16).astype(F32), U32)
    return (bits[:, PACKED_COLS:] & HIGH_HALF) | (bits[:, :PACKED_COLS] >> 16)


def _unpack_bf16_pairs(packed):
    low = lax.bitcast_convert_type(packed << 16, F32)
    high = lax.bitcast_convert_type(packed & HIGH_HALF, F32)
    return jnp.concatenate([low, high], axis=1)


def _resident(shape):
    nd = len(shape)
    return pl.BlockSpec(shape, lambda *_: (0,) * nd, pipeline_mode=pl.Buffered(1))


def _inproj_kernel(x_ref, xp_ref, xn_ref, g_ref, b_ref, win_ref, bg_ref, cw_ref, wco_ref,
                   cos_ref, sa_ref, sb_ref,
                   qkv0_ref, qkv1_ref, qkv2_ref, cg_ref, ga_ref, h_ref,
                   hext_ref, u_ref, stage_ref):
    i = pl.program_id(1)
    nt = pl.num_programs(1)
    tm = x_ref.shape[0]
    g = g_ref[...]
    b = b_ref[...]
    h = _layer_norm(x_ref[...], g, b)
    h_ref[...] = h
    hext_ref[pl.ds(HALO, tm), :] = h.astype(BF16)
    hext_ref[pl.ds(0, HALO), :] = _layer_norm(xp_ref[...], g, b).astype(BF16)
    hext_ref[pl.ds(HALO + tm, HALO), :] = _layer_norm(xn_ref[...], g, b).astype(BF16)
    hm = hext_ref[pl.ds(HALO, tm), :]
    he = hext_ref[...]

    def proj(lhs, col, n):
        return _dot(lhs, win_ref[:, col:col + n])

    u = proj(he, COL_CONV_C, D_CONV) * proj(he, COL_CONV_H, D_CONV)
    rows = lax.broadcasted_iota(jnp.int32, (tm + 2 * HALO, 1), 0)
    lo = jnp.where(i == 0, HALO, 0)
    hi = jnp.where(i == nt - 1, HALO + tm, tm + 2 * HALO)
    u_ref[...] = jnp.where((rows >= lo) & (rows < hi), u, 0.0)
    cw = cw_ref[...]
    conv = (cw[0:1] * u_ref[pl.ds(HALO - 1, tm), :] + cw[1:2] * u_ref[pl.ds(HALO, tm), :]
            + cw[2:3] * u_ref[pl.ds(HALO + 1, tm), :])
    cb = proj(hm, COL_CONV_B, D_CONV)
    conv_branch = _dot((cb * conv).astype(BF16), wco_ref[...])
    gate_conv = jax.nn.sigmoid(proj(hm, COL_GATE_CONV, D_MODEL) + bg_ref[:, D_MODEL:])
    cg_ref[...] = (gate_conv * conv_branch).astype(BF16)
    ga_ref[...] = jax.nn.sigmoid(proj(hm, COL_GATE_ATTN, D_MODEL) + bg_ref[:, :D_MODEL]).astype(BF16)

    cosv = cos_ref[...]
    sav = sa_ref[...]
    sbv = sb_ref[...]
    half = ROPE_DIM // 2
    for gi, (out_ref, r) in enumerate(zip((qkv0_ref, qkv1_ref, qkv2_ref), DILATIONS)):
        for t in range(3):
            p = proj(hm, t * D_ATTN + gi * GROUP_W, GROUP_W)
            if t < 2:
                parts = []
                for c in range(GROUP_W // LANES):
                    pc = p[:, c * LANES:(c + 1) * LANES]
                    parts.append(pc * cosv + pltpu.roll(pc, LANES - half, 1) * sav
                                 + pltpu.roll(pc, half, 1) * sbv)
                p = jnp.concatenate(parts, axis=1)
                if t == 0:
                    p = p * ATTN_SCALE
            if r == 1:
                out_ref[t, 0] = p.astype(BF16)
            else:
                for c in range(GROUP_W // LANES):
                    stage_ref[c] = p[:, c * LANES:(c + 1) * LANES]
                for res in range(r):
                    for c in range(GROUP_W // LANES):
                        out_ref[t, res, :, c * LANES:(c + 1) * LANES] = (
                            stage_ref[c, pl.ds(res, tm // r, stride=r), :].astype(BF16))


def _rope_tables(s):
    half = ROPE_DIM // 2
    inv_freq = (np.float32(ROPE_THETA) ** (-np.arange(half, dtype=np.float32) * np.float32(2.0) / ROPE_DIM))
    ang = np.arange(s, dtype=np.float32)[:, None] * inv_freq.astype(np.float32)[None, :]
    cos = np.cos(ang).astype(np.float32)
    sin = np.sin(ang).astype(np.float32)
    pad = np.zeros((s, HEAD_DIM - ROPE_DIM), np.float32)
    zero = np.zeros((s, half), np.float32)
    cos_t = np.concatenate([cos, cos, pad + 1.0], axis=1)
    sa_t = np.concatenate([-sin, zero, pad], axis=1)
    sb_t = np.concatenate([zero, sin, pad], axis=1)
    rep = LANES // HEAD_DIM
    return tuple(jnp.asarray(np.tile(t, (1, rep))) for t in (cos_t, sa_t, sb_t))


def _inproj(x, ln_g, ln_b, w_in, b_gate, conv_w, w_conv_out):
    bsz, s, d = x.shape
    tm = ROW_TILE
    nt = s // tm
    hpt = tm // HALO
    cos_t, sa_t, sb_t = _rope_tables(s)
    row = lambda b, i: (b, i, 0)
    tab = pl.BlockSpec((tm, LANES), lambda b, i: (i, 0))
    qkv_shapes = [jax.ShapeDtypeStruct((3, bsz, r, s // r, GROUP_W), BF16) for r in DILATIONS]
    qkv_specs = [pl.BlockSpec((3, None, r, tm // r, GROUP_W), lambda b, i: (0, b, 0, i, 0))
                 for r in DILATIONS]
    return pl.pallas_call(
        _inproj_kernel,
        grid=(bsz, nt),
        in_specs=[
            pl.BlockSpec((None, tm, d), row),
            pl.BlockSpec((None, HALO, d), lambda b, i: (b, jnp.maximum(i * hpt - 1, 0), 0)),
            pl.BlockSpec((None, HALO, d), lambda b, i: (b, jnp.minimum((i + 1) * hpt, s // HALO - 1), 0)),
            _resident((1, d)), _resident((1, d)),
            _resident(w_in.shape), _resident((1, 2 * d)), _resident(conv_w.shape),
            _resident(w_conv_out.shape),
            tab, tab, tab,
        ],
        out_specs=qkv_specs + [pl.BlockSpec((None, tm, d), row)] * 3,
        out_shape=(qkv_shapes + [jax.ShapeDtypeStruct((bsz, s, d), BF16)] * 2
                   + [jax.ShapeDtypeStruct((bsz, s, d), F32)]),
        scratch_shapes=[
            pltpu.VMEM((tm + 2 * HALO, d), BF16),
            pltpu.VMEM((tm + 2 * HALO, D_CONV), F32),
            pltpu.VMEM((GROUP_W // LANES, tm, LANES), F32),
        ],
        compiler_params=pltpu.CompilerParams(
            dimension_semantics=("parallel", "parallel"), vmem_limit_bytes=VMEM_LIMIT),
        name="inproj",
    )(x, x, x, ln_g, ln_b, w_in, b_gate, conv_w, w_conv_out, cos_t, sa_t, sb_t)


def _attn_kernel(q_ref, k_ref, v_ref, o_ref, lse_ref, *, sub_len):
    s = k_ref.shape[0]
    qc = q_ref.shape[0]
    base = pl.program_id(1) * qc
    lane = lax.broadcasted_iota(jnp.int32, (1, GROUP_W), 1)
    head_masks = [(lane >= h * HEAD_DIM) & (lane < (h + 1) * HEAD_DIM) for h in range(HEADS_PER_GROUP)]
    qi = lax.broadcasted_iota(jnp.int32, (Q_BLOCK, 1), 0)
    kj = lax.broadcasted_iota(jnp.int32, (1, K_BLOCK), 1)

    def body(j, carry):
        r0 = pl.multiple_of(j * Q_BLOCK, Q_BLOCK)
        s0 = base + r0
        k0 = pl.multiple_of(jnp.clip(s0 - BAND, 0, s - K_BLOCK), BAND)
        q = q_ref[pl.ds(r0, Q_BLOCK), :]
        k = k_ref[pl.ds(k0, K_BLOCK), :]
        v = v_ref[pl.ds(k0, K_BLOCK), :]
        sub_lo = (s0 // sub_len) * sub_len
        qpos = s0 + qi
        kpos = k0 + kj
        valid = (jnp.abs(qpos - kpos) <= BAND) & (kpos >= sub_lo) & (kpos < sub_lo + sub_len)
        qm = jnp.concatenate([jnp.where(hm, q, jnp.zeros_like(q)) for hm in head_masks], axis=0)
        sc = lax.dot_general(qm, k, (((1,), (1,)), ((), ())), preferred_element_type=F32)
        sc = jnp.where(jnp.concatenate([valid] * HEADS_PER_GROUP, axis=0), sc, MASK_VALUE)
        m = jnp.max(sc, axis=-1, keepdims=True)
        p = jnp.exp(sc - m)
        den = jnp.sum(p, axis=-1, keepdims=True)
        o_all = _dot((p * (1.0 / den)).astype(BF16), v)
        lse_all = m + jnp.log(den)
        o_acc = jnp.zeros((Q_BLOCK, GROUP_W), F32)
        lse_acc = jnp.zeros((Q_BLOCK, GROUP_W), F32)
        for h, hm in enumerate(head_masks):
            rows = slice(h * Q_BLOCK, (h + 1) * Q_BLOCK)
            o_acc = jnp.where(hm, o_all[rows], o_acc)
            lse_acc = jnp.where(hm, lse_all[rows], lse_acc)
        o_ref[pl.ds(r0, Q_BLOCK), :] = o_acc.astype(BF16)
        lse_ref[pl.ds(r0, Q_BLOCK), :] = lse_acc
        return carry

    lax.fori_loop(0, qc // Q_BLOCK, body, 0, unroll=True)


def _attention(qkv, sub_len):
    _, bsz, s, w = qkv.shape
    qc = min(Q_CHUNK, s)
    kv_spec = lambda t: pl.BlockSpec((None, None, s, w), lambda b, j: (t, b, 0, 0))
    return pl.pallas_call(
        functools.partial(_attn_kernel, sub_len=sub_len),
        grid=(bsz, s // qc),
        in_specs=[pl.BlockSpec((None, None, qc, w), lambda b, j: (0, b, j, 0)), kv_spec(1), kv_spec(2)],
        out_specs=[pl.BlockSpec((None, qc, w), lambda b, j: (b, j, 0))] * 2,
        out_shape=[jax.ShapeDtypeStruct((bsz, s, w), BF16), jax.ShapeDtypeStruct((bsz, s, w), F32)],
        compiler_params=pltpu.CompilerParams(
            dimension_semantics=("parallel", "arbitrary"), vmem_limit_bytes=VMEM_LIMIT),
        name="attn",
    )(qkv, qkv, qkv)


def _first_index_of_max(vals, lane_f):
    mx = jnp.max(vals, axis=-1, keepdims=True)
    idx = jnp.min(jnp.where(vals == mx, lane_f, float(LANES)), axis=-1, keepdims=True)
    return mx, idx


def _mix_kernel(h_ref, o0_ref, o1_ref, o2_ref, l0_ref, l1_ref, l2_ref, cg_ref, ga_ref,
                wao_ref, wo_ref, g1_ref, b1_ref, wrh_ref, wrl_ref, br_ref,
                h1_ref, h1p_ref, route_ref, plan_ref, counts_ref, so_ref, sl_ref, tri_ref):
    tm = h_ref.shape[0]
    halves = GROUP_W // LANES
    for gi, (o_ref, l_ref, r) in enumerate(zip((o0_ref, o1_ref, o2_ref), (l0_ref, l1_ref, l2_ref), DILATIONS)):
        for c in range(halves):
            cols = slice(c * LANES, (c + 1) * LANES)
            for res in range(r):
                rows = pl.ds(res, tm // r, stride=r) if r > 1 else slice(None)
                so_ref[gi * halves + c, rows, :] = o_ref[res, :, cols].astype(F32)
                sl_ref[gi * halves + c, rows, :] = l_ref[res, :, cols]

    def natural(ref, gi):
        return jnp.concatenate([ref[gi * halves + c] for c in range(halves)], axis=1)

    lses = [natural(sl_ref, gi) for gi in range(N_GROUPS)]
    mx = jnp.maximum(jnp.maximum(lses[0], lses[1]), lses[2])
    es = [jnp.exp(l - mx) for l in lses]
    inv_den = 1.0 / (es[0] + es[1] + es[2])
    attn = jnp.concatenate(
        [(natural(so_ref, gi) * (es[gi] * inv_den)).astype(BF16) for gi in range(N_GROUPS)], axis=1)
    attn_branch = _dot(attn, wao_ref[...])
    merged = ga_ref[...].astype(F32) * attn_branch + cg_ref[...].astype(F32)
    mix = _dot(merged.astype(BF16), wo_ref[...])
    h1 = _layer_norm(DEEPNORM_ALPHA * h_ref[...] + mix, g1_ref[...], b1_ref[...])
    _store_chunk_major(h1_ref, 0, h1)
    _store_chunk_major(h1p_ref, 0, _pack_bf16_pairs(h1))

    hi = h1.astype(BF16)
    lo = (h1 - hi.astype(F32)).astype(BF16)
    wrh = wrh_ref[...]
    logits = _dot(hi, wrh) + _dot(lo, wrh) + _dot(hi, wrl_ref[...]) + br_ref[...]
    lane = lax.broadcasted_iota(jnp.int32, (1, LANES), 1)
    lane_f = lane.astype(F32)
    is_grp = lane < N_EXPERT_GROUPS
    gl = jnp.where(is_grp, logits, MASK_VALUE)
    ge = jnp.exp(gl - jnp.max(gl, axis=-1, keepdims=True))
    gp = jnp.where(is_grp, ge / jnp.sum(ge, axis=-1, keepdims=True), -1.0)
    grp_w, grp_idx = _first_index_of_max(gp, lane_f)
    e_lo = ROUTE_LANE0 + grp_idx * EXPERTS_PER_GROUP
    sel = jnp.where((lane_f >= e_lo) & (lane_f < e_lo + EXPERTS_PER_GROUP), logits, MASK_VALUE)
    v1, i1 = _first_index_of_max(sel, lane_f)
    v2, i2 = _first_index_of_max(jnp.where(lane_f == i1, MASK_VALUE, sel), lane_f)
    e2 = jnp.exp(v2 - v1)
    w1 = grp_w / (1.0 + e2)
    w2 = grp_w * e2 / (1.0 + e2)

    first = (pl.program_id(0) == 0) & (pl.program_id(1) == 0)

    @pl.when(first)
    def _():
        counts_ref[...] = jnp.zeros_like(counts_ref)
        r = lax.broadcasted_iota(jnp.int32, (tm, tm), 0)
        c = lax.broadcasted_iota(jnp.int32, (tm, tm), 1)
        tri_ref[...] = (c < r).astype(BF16)

    oh0 = lane_f == i1 - ROUTE_LANE0
    oh1 = lane_f == i2 - ROUTE_LANE0
    oh = oh0.astype(F32) + oh1.astype(F32)
    before = counts_ref[...] + _dot(tri_ref[...], oh.astype(BF16))
    r0 = jnp.sum(jnp.where(oh0, before, 0.0), axis=-1, keepdims=True)
    r1 = jnp.sum(jnp.where(oh1, before + oh0.astype(F32), 0.0), axis=-1, keepdims=True)
    counts_ref[...] = counts_ref[...] + jnp.sum(oh, axis=0, keepdims=True)

    per_lane = (i1 - ROUTE_LANE0, i2 - ROUTE_LANE0, w1, w2, r0, r1)
    route = jnp.zeros((tm, LANES), F32)
    for k, v in enumerate(per_lane):
        route = jnp.where(lane == k, v, route)
    route_ref[...] = route
    plan_ref[...] = route.T[:PLAN_ROWS, :]


def _mix(h, o_list, lse_list, cg, ga, w_attn_out, w_o, ln1_g, ln1_b, wr_hi, wr_lo, b_route):
    bsz, s, d = h.shape
    tm = ROW_TILE
    row = lambda b, i: (b, i, 0)
    dil_specs = [pl.BlockSpec((None, r, tm // r, GROUP_W), lambda b, i: (b, 0, i, 0)) for r in DILATIONS]
    o_views = [o.reshape(bsz, r, s // r, GROUP_W) for o, r in zip(o_list, DILATIONS)]
    l_views = [l.reshape(bsz, r, s // r, GROUP_W) for l, r in zip(lse_list, DILATIONS)]
    return pl.pallas_call(
        _mix_kernel,
        grid=(bsz, s // tm),
        in_specs=[pl.BlockSpec((None, tm, d), row)] + dil_specs + dil_specs + [
            pl.BlockSpec((None, tm, d), row), pl.BlockSpec((None, tm, d), row),
            _resident(w_attn_out.shape), _resident(w_o.shape),
            _resident((1, d)), _resident((1, d)),
            _resident(wr_hi.shape), _resident(wr_lo.shape), _resident((1, LANES)),
        ],
        out_specs=[_chunk_major_spec(F32_CHUNKS, tm, lambda b, i: (0, b * (s // tm) + i, 0)),
                   _chunk_major_spec(PACKED_CHUNKS, tm, lambda b, i: (0, b * (s // tm) + i, 0)),
                   pl.BlockSpec((None, tm, LANES), row),
                   pl.BlockSpec((PLAN_ROWS, tm), lambda b, i: (0, b * (s // tm) + i)),
                   pl.BlockSpec((1, LANES), lambda b, i: (0, 0))],
        out_shape=[jax.ShapeDtypeStruct((F32_CHUNKS, bsz * s, SC_COLS), F32),
                   jax.ShapeDtypeStruct((PACKED_CHUNKS, bsz * s, SC_COLS), U32),
                   jax.ShapeDtypeStruct((bsz, s, LANES), F32),
                   jax.ShapeDtypeStruct((PLAN_ROWS, bsz * s), F32),
                   jax.ShapeDtypeStruct((1, LANES), F32)],
        scratch_shapes=[pltpu.VMEM((N_GROUPS * GROUP_W // LANES, tm, LANES), F32)] * 2
        + [pltpu.VMEM((tm, tm), BF16)],
        compiler_params=pltpu.CompilerParams(
            dimension_semantics=("arbitrary", "arbitrary"), vmem_limit_bytes=VMEM_LIMIT),
        name="mix",
    )(h, *o_views, *l_views, cg, ga, w_attn_out, w_o, ln1_g, ln1_b, wr_hi, wr_lo, b_route)


def _dispatch_plan(plan, counts, t):
    eid = plan[:TOP_K_INNER].astype(jnp.int32)
    rank = plan[ROUTE_LANE_RANK:ROUTE_LANE_RANK + TOP_K_INNER].astype(jnp.int32)
    counts = counts[0, :N_EXPERTS].astype(jnp.int32)
    padded = (counts + EXPERT_BLOCK - 1) // EXPERT_BLOCK * EXPERT_BLOCK
    pad_end = jnp.cumsum(padded)
    pad_start = pad_end - padded
    experts = jnp.arange(N_EXPERTS, dtype=jnp.int32)[:, None, None]
    start_of = jnp.sum(jnp.where(eid[None] == experts, pad_start[:, None, None], 0), axis=0)
    dest = start_of + rank
    n_blocks = -(-t * TOP_K_INNER // EXPERT_BLOCK) + N_EXPERTS
    n_slots = n_blocks * EXPERT_BLOCK
    block_start = jnp.arange(n_blocks, dtype=jnp.int32) * EXPERT_BLOCK
    block_e = jnp.minimum(jnp.sum((block_start[:, None] >= pad_end[None, :]).astype(jnp.int32), axis=1),
                          N_EXPERTS - 1)
    n_empty = n_slots - t * TOP_K_INNER
    gap_end = jnp.cumsum(padded - counts)
    gap_start = gap_end - (padded - counts)
    k = jnp.arange(n_empty, dtype=jnp.int32)
    in_gap = (k[None, :] >= gap_start[:, None]) & (k[None, :] < gap_end[:, None])
    first_of_gap = (pad_start + counts - gap_start)[:, None]
    empty = jnp.where(k < gap_end[-1],
                      jnp.sum(jnp.where(in_gap, first_of_gap + k[None, :], 0), axis=0),
                      pad_end[-1] + k - gap_end[-1])
    return dest.astype(jnp.int32), empty.astype(jnp.int32), block_e


def _sc_mesh():
    return plsc.VectorSubcoreMesh(core_axis_name="core", subcore_axis_name="subcore")


def _sc_rows_pipeline(body, n_rows, row_index_map, cols):
    return pltpu.emit_pipeline(
        body,
        grid=(n_rows // SC_WINDOW,),
        in_specs=[pl.BlockSpec((SC_WINDOW, cols), index_map=row_index_map),
                  pl.BlockSpec((1, SC_WINDOW), index_map=lambda i: (0, i))],
        out_specs=[],
        core_axis_name=("core", "subcore"),
        dimension_semantics=(pltpu.PARALLEL,),
    )


def _chunk_rows(rows, chunks, n_rows_per_chunk):
    base = jnp.arange(chunks, dtype=jnp.int32)[:, None] * n_rows_per_chunk
    return (base + rows[None, :]).reshape(1, -1)


def _dispatch(rows, dest, empty, n_slots):
    chunks, t, _ = rows.shape
    x = rows.reshape(chunks * t, SC_COLS)
    idx_first = _chunk_rows(dest[0], chunks, n_slots)
    idx_second = _chunk_rows(dest[1], chunks, n_slots)
    idx_empty = _chunk_rows(empty, chunks, n_slots)
    zeros = jnp.zeros((SC_WINDOW, SC_COLS), rows.dtype)

    @pl.kernel(out_type=jax.ShapeDtypeStruct((chunks * n_slots, SC_COLS), rows.dtype), mesh=_sc_mesh(),
               scratch_types=[])
    def scatter(x_hbm, i0_hbm, i1_hbm, z_hbm, ie_hbm, o_hbm):
        def body(x_vmem, i_vmem):
            pltpu.sync_copy(x_vmem, o_hbm.at[i_vmem.at[0]])

        for i_hbm in (i0_hbm, i1_hbm):
            _sc_rows_pipeline(body, x.shape[0], lambda i: (i, 0), SC_COLS)(x_hbm, i_hbm)
        _sc_rows_pipeline(body, idx_empty.shape[1], lambda i: (0, 0), SC_COLS)(z_hbm, ie_hbm)

    return scatter(x, idx_first, idx_second, zeros, idx_empty).reshape(chunks, n_slots, SC_COLS)


def _cast_kernel(*refs):
    n = len(refs) // 2
    for src, dst in zip(refs[:n], refs[n:]):
        dst[...] = src[...].astype(dst.dtype)


def _expert_weights_bf16(w_gate, w_up, w_down):
    ws = (w_gate, w_up, w_down)
    specs = [pl.BlockSpec((None,) + w.shape[1:], lambda e: (e, 0, 0)) for w in ws]
    return pl.pallas_call(
        _cast_kernel,
        grid=(N_EXPERTS,),
        in_specs=specs,
        out_specs=specs,
        out_shape=[jax.ShapeDtypeStruct(w.shape, BF16) for w in ws],
        compiler_params=pltpu.CompilerParams(dimension_semantics=("parallel",), vmem_limit_bytes=VMEM_LIMIT),
        name="cast_weights",
    )(*ws)


def _expert_mlp(xb, wg_ref, wu_ref, wd_ref):
    acc = None
    for c in range(D_EXPERT // EXPERT_CHUNK):
        cols = slice(c * EXPERT_CHUNK, (c + 1) * EXPERT_CHUNK)
        gate = _dot(xb, wg_ref[:, cols])
        up = _dot(xb, wu_ref[:, cols])
        hidden = (gate * jax.nn.sigmoid(gate) * up).astype(BF16)
        part = _dot(hidden, wd_ref[cols, :])
        acc = part if acc is None else acc + part
    return acc


def _expert_kernel(be_ref, x_ref, wga_ref, wua_ref, wda_ref, wgb_ref, wub_ref, wdb_ref, y_ref):
    i = pl.program_id(0)
    same = be_ref[2 * i] == be_ref[2 * i + 1]

    @pl.when(same)
    def _():
        xb = _unpack_bf16_pairs(_load_chunk_major(x_ref, 0, 2 * EXPERT_BLOCK)).astype(BF16)
        _store_chunk_major(y_ref, 0, _pack_bf16_pairs(_expert_mlp(xb, wga_ref, wua_ref, wda_ref)))

    @pl.when(jnp.logical_not(same))
    def _():
        for half, w in enumerate(((wga_ref, wua_ref, wda_ref), (wgb_ref, wub_ref, wdb_ref))):
            xb = _unpack_bf16_pairs(_load_chunk_major(x_ref, half * EXPERT_BLOCK, EXPERT_BLOCK)).astype(BF16)
            _store_chunk_major(y_ref, half * EXPERT_BLOCK, _pack_bf16_pairs(_expert_mlp(xb, *w)))


def _experts(xs, block_e, w_gate, w_up, w_down):
    n_blocks = block_e.shape[0]
    assert n_blocks % 2 == 0
    d = D_MODEL
    blk = _chunk_major_spec(PACKED_CHUNKS, 2 * EXPERT_BLOCK, lambda i, be: (0, i, 0))
    w_specs = [pl.BlockSpec(shape, (lambda i, be, k=k: (be[2 * i + k], 0, 0)))
               for k in range(2) for shape in ((None, d, D_EXPERT), (None, d, D_EXPERT), (None, D_EXPERT, d))]
    grid_spec = pltpu.PrefetchScalarGridSpec(
        num_scalar_prefetch=1,
        grid=(n_blocks // 2,),
        in_specs=[blk] + w_specs,
        out_specs=blk,
    )
    return pl.pallas_call(
        _expert_kernel,
        grid_spec=grid_spec,
        out_shape=jax.ShapeDtypeStruct(xs.shape, xs.dtype),
        compiler_params=pltpu.CompilerParams(
            dimension_semantics=("arbitrary",), vmem_limit_bytes=VMEM_LIMIT),
        name="experts",
    )(block_e, xs, w_gate, w_up, w_down, w_gate, w_up, w_down)


def _gather(yb, dest):
    chunks, n_slots, _ = yb.shape
    x = yb.reshape(chunks * n_slots, SC_COLS)
    idx = _chunk_rows(dest.reshape(-1), chunks, n_slots)
    m = idx.shape[1]

    @pl.kernel(out_type=jax.ShapeDtypeStruct((m, SC_COLS), yb.dtype), mesh=_sc_mesh(), scratch_types=[])
    def gather(x_hbm, i_hbm, o_hbm):
        def body(i_vmem, o_vmem):
            pltpu.sync_copy(x_hbm.at[i_vmem.at[0]], o_vmem)

        pltpu.emit_pipeline(
            body,
            grid=(m // SC_WINDOW,),
            in_specs=[pl.BlockSpec((1, SC_WINDOW), index_map=lambda i: (0, i))],
            out_specs=[pl.BlockSpec((SC_WINDOW, SC_COLS), index_map=lambda i: (i, 0))],
            core_axis_name=("core", "subcore"),
            dimension_semantics=(pltpu.PARALLEL,),
        )(i_hbm, o_hbm)

    return gather(x, idx).reshape(chunks, m // chunks, SC_COLS)


def _combine_kernel(first_ref, second_ref, h1_ref, route_ref, g_ref, b_ref, y_ref):
    tm = y_ref.shape[0]
    route = route_ref[...]
    w_first = route[:, ROUTE_LANE_W:ROUTE_LANE_W + 1]
    w_second = route[:, ROUTE_LANE_W + 1:ROUTE_LANE_W + 2]
    ffn = (w_first * _unpack_bf16_pairs(_load_chunk_major(first_ref, 0, tm))
           + w_second * _unpack_bf16_pairs(_load_chunk_major(second_ref, 0, tm)))
    y_ref[...] = _layer_norm(DEEPNORM_ALPHA * _load_chunk_major(h1_ref, 0, tm) + ffn, g_ref[...], b_ref[...])


def _combine(h1, route, g, ln2_g, ln2_b):
    t = route.shape[0]
    d = D_MODEL
    tm = ROW_TILE
    nt = t // tm
    return pl.pallas_call(
        _combine_kernel,
        grid=(nt,),
        in_specs=[
            _chunk_major_spec(PACKED_CHUNKS, tm, lambda i: (0, i, 0)),
            _chunk_major_spec(PACKED_CHUNKS, tm, lambda i: (0, nt + i, 0)),
            _chunk_major_spec(F32_CHUNKS, tm, lambda i: (0, i, 0)),
            pl.BlockSpec((tm, LANES), lambda i: (i, 0)),
            _resident((1, d)), _resident((1, d)),
        ],
        out_specs=pl.BlockSpec((tm, d), lambda i: (i, 0)),
        out_shape=jax.ShapeDtypeStruct((t, d), F32),
        compiler_params=pltpu.CompilerParams(
            dimension_semantics=("parallel",), vmem_limit_bytes=VMEM_LIMIT),
        name="combine",
    )(g, g, h1, route, ln2_g, ln2_b)


def _token_mixing(x, p):
    bsz, s, d = x.shape
    qkv0, qkv1, qkv2, cg, ga, h = _inproj(x, p['ln_in_g'], p['ln_in_b'], p['w_in'], p['b_gate'], p['conv_w'],
                                          p['w_conv_out'])
    o_list, lse_list = [], []
    for qkv, r in zip((qkv0, qkv1, qkv2), DILATIONS):
        o, lse = _attention(qkv.reshape(3, bsz, s, GROUP_W), s // r)
        o_list.append(o)
        lse_list.append(lse)
    h1, h1_packed, route, plan, counts = _mix(h, o_list, lse_list, cg, ga, p['w_attn_out'], p['w_o'],
                             p['ln1_g'], p['ln1_b'], p['wr_hi'], p['wr_lo'], p['b_route'])
    t = bsz * s
    route = route.reshape(t, LANES)
    dest, empty, block_e = _dispatch_plan(plan, counts, t)
    return h1, route, dest, empty, block_e, h1_packed


def kernel(x_prompt, x_sample, ln_in_g, ln_in_b, w_in, b_gate, conv_w, w_attn_out, w_conv_out, w_o, ln1_g, ln1_b, w_route_group, b_route_group, w_route_expert, b_route_expert, w_gate, w_up, w_down, ln2_g, ln2_b):
    d = D_MODEL
    w_route = jnp.concatenate([w_route_group[0], w_route_expert[0]], axis=1)
    w_route = jnp.pad(w_route, ((0, 0), (0, LANES - w_route.shape[1])))
    wr_hi = w_route.astype(BF16)
    b_route = jnp.concatenate([b_route_group[0], b_route_expert[0]]).astype(F32)
    p = {
        'ln_in_g': ln_in_g.reshape(1, d), 'ln_in_b': ln_in_b.reshape(1, d),
        'w_in': w_in[0].astype(BF16), 'b_gate': b_gate[0].reshape(1, 2 * d), 'conv_w': conv_w[0],
        'w_conv_out': w_conv_out[0].astype(BF16), 'w_attn_out': w_attn_out[0].astype(BF16),
        'w_o': w_o[0].astype(BF16),
        'ln1_g': ln1_g[0].reshape(1, d), 'ln1_b': ln1_b[0].reshape(1, d),
        'wr_hi': wr_hi, 'wr_lo': (w_route - wr_hi.astype(F32)).astype(BF16),
        'b_route': jnp.pad(b_route, (0, LANES - b_route.shape[0])).reshape(1, LANES),
        'ln2_g': ln2_g[0].reshape(1, d), 'ln2_b': ln2_b[0].reshape(1, d),
    }
    p['w_gate'], p['w_up'], p['w_down'] = _expert_weights_bf16(w_gate[0], w_up[0], w_down[0])
    batches = [_token_mixing(x, p) for x in (x_prompt, x_sample)]
    sorted_rows = [_dispatch(h1_packed, dest, empty, block_e.shape[0] * EXPERT_BLOCK)
                   for _, _, dest, empty, block_e, h1_packed in batches]
    gathered = [_gather(_experts(xs, b[4], p['w_gate'], p['w_up'], p['w_down']), b[2])
                for xs, b in zip(sorted_rows, batches)]
    outs = [_combine(b[0], b[1], g, p['ln2_g'], p['ln2_b']).reshape(x.shape)
            for b, g, x in zip(batches, gathered, (x_prompt, x_sample))]
    return tuple(outs)
```

```python
import functools

import jax
import jax.numpy as jnp
import numpy as np
from jax import lax
from jax.experimental import pallas as pl
from jax.experimental.pallas import tpu as pltpu
from jax.experimental.pallas import tpu_sc as plsc

F32 = jnp.float32
BF16 = jnp.bfloat16

D_MODEL = 1024
HEAD_DIM = 64
HEADS_PER_GROUP = 4
GROUP_W = HEADS_PER_GROUP * HEAD_DIM
DILATIONS = (1, 4, 16)
BAND = 64
N_GROUPS = len(DILATIONS)
D_ATTN = N_GROUPS * GROUP_W
ATTN_SCALE = HEAD_DIM ** -0.5
ROPE_DIM = HEAD_DIM // 4
ROPE_THETA = 500000.0
MASK_VALUE = -1e30
D_CONV = D_MODEL
COL_CONV_B = 3 * D_ATTN
COL_CONV_C = COL_CONV_B + D_CONV
COL_CONV_H = COL_CONV_C + D_CONV
COL_GATE_ATTN = COL_CONV_H + D_CONV
COL_GATE_CONV = COL_GATE_ATTN + D_MODEL
N_EXPERT_GROUPS = 4
EXPERTS_PER_GROUP = 8
N_EXPERTS = N_EXPERT_GROUPS * EXPERTS_PER_GROUP
TOP_K_INNER = 2
D_EXPERT = 512
EXPERT_BLOCK = 256
EXPERT_CHUNK = 256
LN_EPS = 1e-5
DEPTH = 1
DEEPNORM_ALPHA = (2 * DEPTH) ** 0.25

LANES = 128
HEADS_PER_BLOCK = LANES // HEAD_DIM
ROW_TILE = 512
HALO = 16
Q_BLOCK = 128
K_BLOCK = Q_BLOCK + 2 * BAND
Q_CHUNK = 2048
SC_COLS = 256
SC_WINDOW = 128
F32_CHUNKS = D_MODEL // SC_COLS
PACKED_COLS = D_MODEL // 2
PACKED_CHUNKS = PACKED_COLS // SC_COLS
U32 = jnp.uint32
HIGH_HALF = np.uint32(0xFFFF0000)
ROUTE_LANE0 = N_EXPERT_GROUPS
ROUTE_LANE_W = TOP_K_INNER
ROUTE_LANE_RANK = 2 * TOP_K_INNER
PLAN_ROWS = 8
VMEM_LIMIT = 56 * 1024 * 1024


def _layer_norm(v, g, b):
    mu = jnp.mean(v, axis=-1, keepdims=True)
    d = v - mu
    var = jnp.mean(d * d, axis=-1, keepdims=True)
    return d * lax.rsqrt(var + LN_EPS) * g + b


def _dot(a, b):
    return jnp.dot(a, b, preferred_element_type=F32)


def _store_chunk_major(ref, first_row, val):
    for c in range(ref.shape[0]):
        ref[c, pl.ds(first_row, val.shape[0]), :] = val[:, c * SC_COLS:(c + 1) * SC_COLS]


def _load_chunk_major(ref, first_row, n):
    return jnp.concatenate([ref[c, pl.ds(first_row, n), :] for c in range(ref.shape[0])], axis=1)


def _chunk_major_spec(chunks, rows, index_map):
    return pl.BlockSpec((chunks, rows, SC_COLS), index_map)


def _pack_bf16_pairs(val):
    bits = lax.bitcast_convert_type(val.astype(BF16).astype(F32), U32)
    return (bits[:, PACKED_COLS:] & HIGH_HALF) | (bits[:, :PACKED_COLS] >> 16)


def _unpack_bf16_pairs(packed):
    low = lax.bitcast_convert_type(packed << 16, F32)
    high = lax.bitcast_convert_type(packed & HIGH_HALF, F32)
    return jnp.concatenate([low, high], axis=1)


def _resident(shape):
    nd = len(shape)
    return pl.BlockSpec(shape, lambda *_: (0,) * nd, pipeline_mode=pl.Buffered(1))


def _inproj_kernel(x_ref, xp_ref, xn_ref, g_ref, b_ref, win_ref, bg_ref, cw_ref, wco_ref,
                   cos_ref, sa_ref, sb_ref,
                   qkv0_ref, qkv1_ref, qkv2_ref, cg_ref, ga_ref, h_ref,
                   hext_ref, u_ref, stage_ref):
    i = pl.program_id(1)
    nt = pl.num_programs(1)
    tm = x_ref.shape[0]
    g = g_ref[...]
    b = b_ref[...]
    h = _layer_norm(x_ref[...], g, b)
    h_ref[...] = h
    hext_ref[pl.ds(HALO, tm), :] = h.astype(BF16)
    hext_ref[pl.ds(0, HALO), :] = _layer_norm(xp_ref[...], g, b).astype(BF16)
    hext_ref[pl.ds(HALO + tm, HALO), :] = _layer_norm(xn_ref[...], g, b).astype(BF16)
    hm = hext_ref[pl.ds(HALO, tm), :]
    he = hext_ref[...]

    def proj(lhs, col, n):
        return _dot(lhs, win_ref[:, col:col + n])

    u = proj(he, COL_CONV_C, D_CONV) * proj(he, COL_CONV_H, D_CONV)
    rows = lax.broadcasted_iota(jnp.int32, (tm + 2 * HALO, 1), 0)
    lo = jnp.where(i == 0, HALO, 0)
    hi = jnp.where(i == nt - 1, HALO + tm, tm + 2 * HALO)
    u_ref[...] = jnp.where((rows >= lo) & (rows < hi), u, 0.0)
    cw = cw_ref[...]
    conv = (cw[0:1] * u_ref[pl.ds(HALO - 1, tm), :] + cw[1:2] * u_ref[pl.ds(HALO, tm), :]
            + cw[2:3] * u_ref[pl.ds(HALO + 1, tm), :])
    cb = proj(hm, COL_CONV_B, D_CONV)
    conv_branch = _dot((cb * conv).astype(BF16), wco_ref[...])
    gate_conv = jax.nn.sigmoid(proj(hm, COL_GATE_CONV, D_MODEL) + bg_ref[:, D_MODEL:])
    cg_ref[...] = (gate_conv * conv_branch).astype(BF16)
    ga_ref[...] = jax.nn.sigmoid(proj(hm, COL_GATE_ATTN, D_MODEL) + bg_ref[:, :D_MODEL]).astype(BF16)

    cosv = cos_ref[...]
    sav = sa_ref[...]
    sbv = sb_ref[...]
    half = ROPE_DIM // 2
    for gi, (out_ref, r) in enumerate(zip((qkv0_ref, qkv1_ref, qkv2_ref), DILATIONS)):
        for t in range(3):
            p = proj(hm, t * D_ATTN + gi * GROUP_W, GROUP_W)
            if t < 2:
                parts = []
                for c in range(GROUP_W // LANES):
                    pc = p[:, c * LANES:(c + 1) * LANES]
                    parts.append(pc * cosv + pltpu.roll(pc, LANES - half, 1) * sav
                                 + pltpu.roll(pc, half, 1) * sbv)
                p = jnp.concatenate(parts, axis=1)
                if t == 0:
                    p = p * ATTN_SCALE
            if r == 1:
                out_ref[t, 0] = p.astype(BF16)
            else:
                for c in range(GROUP_W // LANES):
                    stage_ref[c] = p[:, c * LANES:(c + 1) * LANES]
                for res in range(r):
                    for c in range(GROUP_W // LANES):
                        out_ref[t, res, :, c * LANES:(c + 1) * LANES] = (
                            stage_ref[c, pl.ds(res, tm // r, stride=r), :].astype(BF16))


def _rope_tables(s):
    half = ROPE_DIM // 2
    inv_freq = (np.float32(ROPE_THETA) ** (-np.arange(half, dtype=np.float32) * np.float32(2.0) / ROPE_DIM))
    ang = np.arange(s, dtype=np.float32)[:, None] * inv_freq.astype(np.float32)[None, :]
    cos = np.cos(ang).astype(np.float32)
    sin = np.sin(ang).astype(np.float32)
    pad = np.zeros((s, HEAD_DIM - ROPE_DIM), np.float32)
    zero = np.zeros((s, half), np.float32)
    cos_t = np.concatenate([cos, cos, pad + 1.0], axis=1)
    sa_t = np.concatenate([-sin, zero, pad], axis=1)
    sb_t = np.concatenate([zero, sin, pad], axis=1)
    rep = LANES // HEAD_DIM
    return tuple(jnp.asarray(np.tile(t, (1, rep))) for t in (cos_t, sa_t, sb_t))


def _inproj(x, ln_g, ln_b, w_in, b_gate, conv_w, w_conv_out):
    bsz, s, d = x.shape
    tm = ROW_TILE
    nt = s // tm
    hpt = tm // HALO
    cos_t, sa_t, sb_t = _rope_tables(s)
    row = lambda b, i: (b, i, 0)
    tab = pl.BlockSpec((tm, LANES), lambda b, i: (i, 0))
    qkv_shapes = [jax.ShapeDtypeStruct((3, bsz, r, s // r, GROUP_W), BF16) for r in DILATIONS]
    qkv_specs = [pl.BlockSpec((3, None, r, tm // r, GROUP_W), lambda b, i: (0, b, 0, i, 0))
                 for r in DILATIONS]
    return pl.pallas_call(
        _inproj_kernel,
        grid=(bsz, nt),
        in_specs=[
            pl.BlockSpec((None, tm, d), row),
            pl.BlockSpec((None, HALO, d), lambda b, i: (b, jnp.maximum(i * hpt - 1, 0), 0)),
            pl.BlockSpec((None, HALO, d), lambda b, i: (b, jnp.minimum((i + 1) * hpt, s // HALO - 1), 0)),
            _resident((1, d)), _resident((1, d)),
            _resident(w_in.shape), _resident((1, 2 * d)), _resident(conv_w.shape),
            _resident(w_conv_out.shape),
            tab, tab, tab,
        ],
        out_specs=qkv_specs + [pl.BlockSpec((None, tm, d), row)] * 3,
        out_shape=(qkv_shapes + [jax.ShapeDtypeStruct((bsz, s, d), BF16)] * 2
                   + [jax.ShapeDtypeStruct((bsz, s, d), F32)]),
        scratch_shapes=[
            pltpu.VMEM((tm + 2 * HALO, d), BF16),
            pltpu.VMEM((tm + 2 * HALO, D_CONV), F32),
            pltpu.VMEM((GROUP_W // LANES, tm, LANES), F32),
        ],
        compiler_params=pltpu.CompilerParams(
            dimension_semantics=("parallel", "parallel"), vmem_limit_bytes=VMEM_LIMIT),
        name="inproj",
    )(x, x, x, ln_g, ln_b, w_in, b_gate, conv_w, w_conv_out, cos_t, sa_t, sb_t)


def _attn_kernel(q_ref, k_ref, v_ref, o_ref, lse_ref, *, sub_len):
    s = k_ref.shape[0]
    qc = q_ref.shape[0]
    base = pl.program_id(1) * qc
    lane = lax.broadcasted_iota(jnp.int32, (1, GROUP_W), 1)
    head_masks = [(lane >= h * HEAD_DIM) & (lane < (h + 1) * HEAD_DIM) for h in range(HEADS_PER_GROUP)]
    qi = lax.broadcasted_iota(jnp.int32, (Q_BLOCK, 1), 0)
    kj = lax.broadcasted_iota(jnp.int32, (1, K_BLOCK), 1)
    low_half = lax.broadcasted_iota(jnp.int32, (1, LANES), 1) < HEAD_DIM

    def body(j, carry):
        r0 = pl.multiple_of(j * Q_BLOCK, Q_BLOCK)
        s0 = base + r0
        k0 = pl.multiple_of(jnp.clip(s0 - BAND, 0, s - K_BLOCK), BAND)
        q = q_ref[pl.ds(r0, Q_BLOCK), :]
        k = k_ref[pl.ds(k0, K_BLOCK), :]
        v = v_ref[pl.ds(k0, K_BLOCK), :]
        sub_lo = (s0 // sub_len) * sub_len
        qpos = s0 + qi
        kpos = k0 + kj
        valid = (jnp.abs(qpos - kpos) <= BAND) & (kpos >= sub_lo) & (kpos < sub_lo + sub_len)
        qm = jnp.concatenate([jnp.where(hm, q, jnp.zeros_like(q)) for hm in head_masks], axis=0)
        sc = lax.dot_general(qm, k, (((1,), (1,)), ((), ())), preferred_element_type=F32)
        sc = jnp.where(jnp.concatenate([valid] * HEADS_PER_GROUP, axis=0), sc, MASK_VALUE)
        m = jnp.max(sc, axis=-1, keepdims=True)
        p = jnp.exp(sc - m)
        den = jnp.sum(p, axis=-1, keepdims=True)
        o_all = _dot((p * (1.0 / den)).astype(BF16), v)
        lse_all = m + jnp.log(den)
        o_cols, lse_cols = [], []
        for c in range(GROUP_W // LANES):
            lanes = slice(c * LANES, (c + 1) * LANES)
            h_lo, h_hi = HEADS_PER_BLOCK * c, HEADS_PER_BLOCK * c + 1
            rows_lo = slice(h_lo * Q_BLOCK, (h_lo + 1) * Q_BLOCK)
            rows_hi = slice(h_hi * Q_BLOCK, (h_hi + 1) * Q_BLOCK)
            o_cols.append(jnp.where(low_half, o_all[rows_lo, lanes], o_all[rows_hi, lanes]))
            lse_cols.append(jnp.where(low_half, lse_all[rows_lo], lse_all[rows_hi]))
        o_ref[pl.ds(r0, Q_BLOCK), :] = jnp.concatenate(o_cols, axis=1).astype(BF16)
        lse_ref[pl.ds(r0, Q_BLOCK), :] = jnp.concatenate(lse_cols, axis=1)
        return carry

    lax.fori_loop(0, qc // Q_BLOCK, body, 0, unroll=True)


def _attention(qkv, sub_len):
    _, bsz, s, w = qkv.shape
    qc = min(Q_CHUNK, s)
    kv_spec = lambda t: pl.BlockSpec((None, None, s, w), lambda b, j: (t, b, 0, 0))
    return pl.pallas_call(
        functools.partial(_attn_kernel, sub_len=sub_len),
        grid=(bsz, s // qc),
        in_specs=[pl.BlockSpec((None, None, qc, w), lambda b, j: (0, b, j, 0)), kv_spec(1), kv_spec(2)],
        out_specs=[pl.BlockSpec((None, qc, w), lambda b, j: (b, j, 0))] * 2,
        out_shape=[jax.ShapeDtypeStruct((bsz, s, w), BF16), jax.ShapeDtypeStruct((bsz, s, w), F32)],
        compiler_params=pltpu.CompilerParams(
            dimension_semantics=("parallel", "arbitrary"), vmem_limit_bytes=VMEM_LIMIT),
        name="attn",
    )(qkv, qkv, qkv)


def _first_index_of_max(vals, lane_f):
    mx = jnp.max(vals, axis=-1, keepdims=True)
    idx = jnp.min(jnp.where(vals == mx, lane_f, float(LANES)), axis=-1, keepdims=True)
    return mx, idx


def _mix_kernel(h_ref, o0_ref, o1_ref, o2_ref, l0_ref, l1_ref, l2_ref, cg_ref, ga_ref,
                wao_ref, wo_ref, g1_ref, b1_ref, wrh_ref, wrl_ref, br_ref,
                h1_ref, h1p_ref, route_ref, plan_ref, counts_ref, so_ref, sl_ref, tri_ref):
    tm = h_ref.shape[0]
    halves = GROUP_W // LANES
    for gi, (o_ref, l_ref, r) in enumerate(zip((o0_ref, o1_ref, o2_ref), (l0_ref, l1_ref, l2_ref), DILATIONS)):
        for c in range(halves):
            cols = slice(c * LANES, (c + 1) * LANES)
            for res in range(r):
                rows = pl.ds(res, tm // r, stride=r) if r > 1 else slice(None)
                so_ref[gi * halves + c, rows, :] = o_ref[res, :, cols].astype(F32)
                sl_ref[gi * halves + c, rows, :] = l_ref[res, :, cols]

    def natural(ref, gi):
        return jnp.concatenate([ref[gi * halves + c] for c in range(halves)], axis=1)

    lses = [natural(sl_ref, gi) for gi in range(N_GROUPS)]
    mx = jnp.maximum(jnp.maximum(lses[0], lses[1]), lses[2])
    es = [jnp.exp(l - mx) for l in lses]
    inv_den = 1.0 / (es[0] + es[1] + es[2])
    attn = jnp.concatenate(
        [(natural(so_ref, gi) * (es[gi] * inv_den)).astype(BF16) for gi in range(N_GROUPS)], axis=1)
    attn_branch = _dot(attn, wao_ref[...])
    merged = ga_ref[...].astype(F32) * attn_branch + cg_ref[...].astype(F32)
    mix = _dot(merged.astype(BF16), wo_ref[...])
    h1 = _layer_norm(DEEPNORM_ALPHA * h_ref[...] + mix, g1_ref[...], b1_ref[...])
    _store_chunk_major(h1_ref, 0, h1)
    _store_chunk_major(h1p_ref, 0, _pack_bf16_pairs(h1))

    hi = h1.astype(BF16)
    lo = (h1 - hi.astype(F32)).astype(BF16)
    wrh = wrh_ref[...]
    logits = _dot(hi, wrh) + _dot(lo, wrh) + _dot(hi, wrl_ref[...]) + br_ref[...]
    lane = lax.broadcasted_iota(jnp.int32, (1, LANES), 1)
    lane_f = lane.astype(F32)
    is_grp = lane < N_EXPERT_GROUPS
    gl = jnp.where(is_grp, logits, MASK_VALUE)
    ge = jnp.exp(gl - jnp.max(gl, axis=-1, keepdims=True))
    gp = jnp.where(is_grp, ge / jnp.sum(ge, axis=-1, keepdims=True), -1.0)
    grp_w, grp_idx = _first_index_of_max(gp, lane_f)
    e_lo = ROUTE_LANE0 + grp_idx * EXPERTS_PER_GROUP
    sel = jnp.where((lane_f >= e_lo) & (lane_f < e_lo + EXPERTS_PER_GROUP), logits, MASK_VALUE)
    v1, i1 = _first_index_of_max(sel, lane_f)
    v2, i2 = _first_index_of_max(jnp.where(lane_f == i1, MASK_VALUE, sel), lane_f)
    e2 = jnp.exp(v2 - v1)
    w1 = grp_w / (1.0 + e2)
    w2 = grp_w * e2 / (1.0 + e2)

    first = (pl.program_id(0) == 0) & (pl.program_id(1) == 0)

    @pl.when(first)
    def _():
        counts_ref[...] = jnp.zeros_like(counts_ref)
        r = lax.broadcasted_iota(jnp.int32, (tm, tm), 0)
        c = lax.broadcasted_iota(jnp.int32, (tm, tm), 1)
        tri_ref[...] = (c < r).astype(BF16)

    oh0 = lane_f == i1 - ROUTE_LANE0
    oh1 = lane_f == i2 - ROUTE_LANE0
    oh = oh0.astype(F32) + oh1.astype(F32)
    before = counts_ref[...] + _dot(tri_ref[...], oh.astype(BF16))
    r0 = jnp.sum(jnp.where(oh0, before, 0.0), axis=-1, keepdims=True)
    r1 = jnp.sum(jnp.where(oh1, before + oh0.astype(F32), 0.0), axis=-1, keepdims=True)
    counts_ref[...] = counts_ref[...] + jnp.sum(oh, axis=0, keepdims=True)

    per_lane = (i1 - ROUTE_LANE0, i2 - ROUTE_LANE0, w1, w2, r0, r1)
    route = jnp.zeros((tm, LANES), F32)
    for k, v in enumerate(per_lane):
        route = jnp.where(lane == k, v, route)
    route_ref[...] = route
    plan_ref[...] = route.T[:PLAN_ROWS, :]


def _mix(h, o_list, lse_list, cg, ga, w_attn_out, w_o, ln1_g, ln1_b, wr_hi, wr_lo, b_route):
    bsz, s, d = h.shape
    tm = ROW_TILE
    row = lambda b, i: (b, i, 0)
    dil_specs = [pl.BlockSpec((None, r, tm // r, GROUP_W), lambda b, i: (b, 0, i, 0)) for r in DILATIONS]
    o_views = [o.reshape(bsz, r, s // r, GROUP_W) for o, r in zip(o_list, DILATIONS)]
    l_views = [l.reshape(bsz, r, s // r, GROUP_W) for l, r in zip(lse_list, DILATIONS)]
    return pl.pallas_call(
        _mix_kernel,
        grid=(bsz, s // tm),
        in_specs=[pl.BlockSpec((None, tm, d), row)] + dil_specs + dil_specs + [
            pl.BlockSpec((None, tm, d), row), pl.BlockSpec((None, tm, d), row),
            _resident(w_attn_out.shape), _resident(w_o.shape),
            _resident((1, d)), _resident((1, d)),
            _resident(wr_hi.shape), _resident(wr_lo.shape), _resident((1, LANES)),
        ],
        out_specs=[_chunk_major_spec(F32_CHUNKS, tm, lambda b, i: (0, b * (s // tm) + i, 0)),
                   _chunk_major_spec(PACKED_CHUNKS, tm, lambda b, i: (0, b * (s // tm) + i, 0)),
                   pl.BlockSpec((None, tm, LANES), row),
                   pl.BlockSpec((PLAN_ROWS, tm), lambda b, i: (0, b * (s // tm) + i)),
                   pl.BlockSpec((1, LANES), lambda b, i: (0, 0))],
        out_shape=[jax.ShapeDtypeStruct((F32_CHUNKS, bsz * s, SC_COLS), F32),
                   jax.ShapeDtypeStruct((PACKED_CHUNKS, bsz * s, SC_COLS), U32),
                   jax.ShapeDtypeStruct((bsz, s, LANES), F32),
                   jax.ShapeDtypeStruct((PLAN_ROWS, bsz * s), F32),
                   jax.ShapeDtypeStruct((1, LANES), F32)],
        scratch_shapes=[pltpu.VMEM((N_GROUPS * GROUP_W // LANES, tm, LANES), F32)] * 2
        + [pltpu.VMEM((tm, tm), BF16)],
        compiler_params=pltpu.CompilerParams(
            dimension_semantics=("arbitrary", "arbitrary"), vmem_limit_bytes=VMEM_LIMIT),
        name="mix",
    )(h, *o_views, *l_views, cg, ga, w_attn_out, w_o, ln1_g, ln1_b, wr_hi, wr_lo, b_route)


def _dispatch_plan(plan, counts, t):
    eid = plan[:TOP_K_INNER].astype(jnp.int32)
    rank = plan[ROUTE_LANE_RANK:ROUTE_LANE_RANK + TOP_K_INNER].astype(jnp.int32)
    counts = counts[0, :N_EXPERTS].astype(jnp.int32)
    padded = (counts + EXPERT_BLOCK - 1) // EXPERT_BLOCK * EXPERT_BLOCK
    pad_end = jnp.cumsum(padded)
    pad_start = pad_end - padded
    experts = jnp.arange(N_EXPERTS, dtype=jnp.int32)[:, None, None]
    start_of = jnp.sum(jnp.where(eid[None] == experts, pad_start[:, None, None], 0), axis=0)
    dest = start_of + rank
    n_blocks = -(-t * TOP_K_INNER // EXPERT_BLOCK) + N_EXPERTS
    n_slots = n_blocks * EXPERT_BLOCK
    block_start = jnp.arange(n_blocks, dtype=jnp.int32) * EXPERT_BLOCK
    block_e = jnp.minimum(jnp.sum((block_start[:, None] >= pad_end[None, :]).astype(jnp.int32), axis=1),
                          N_EXPERTS - 1)
    n_empty = n_slots - t * TOP_K_INNER
    gap_end = jnp.cumsum(padded - counts)
    gap_start = gap_end - (padded - counts)
    k = jnp.arange(n_empty, dtype=jnp.int32)
    in_gap = (k[None, :] >= gap_start[:, None]) & (k[None, :] < gap_end[:, None])
    first_of_gap = (pad_start + counts - gap_start)[:, None]
    empty = jnp.where(k < gap_end[-1],
                      jnp.sum(jnp.where(in_gap, first_of_gap + k[None, :], 0), axis=0),
                      pad_end[-1] + k - gap_end[-1])
    return dest.astype(jnp.int32), empty.astype(jnp.int32), block_e


def _sc_mesh():
    return plsc.VectorSubcoreMesh(core_axis_name="core", subcore_axis_name="subcore")


def _sc_rows_pipeline(body, n_rows, row_index_map, cols):
    return pltpu.emit_pipeline(
        body,
        grid=(n_rows // SC_WINDOW,),
        in_specs=[pl.BlockSpec((SC_WINDOW, cols), index_map=row_index_map),
                  pl.BlockSpec((1, SC_WINDOW), index_map=lambda i: (0, i))],
        out_specs=[],
        core_axis_name=("core", "subcore"),
        dimension_semantics=(pltpu.PARALLEL,),
    )


def _chunk_rows(rows, chunks, n_rows_per_chunk):
    base = jnp.arange(chunks, dtype=jnp.int32)[:, None] * n_rows_per_chunk
    return (base + rows[None, :]).reshape(1, -1)


def _dispatch(rows, dest, empty, n_slots):
    chunks, t, _ = rows.shape
    x = rows.reshape(chunks * t, SC_COLS)
    idx_first = _chunk_rows(dest[0], chunks, n_slots)
    idx_second = _chunk_rows(dest[1], chunks, n_slots)
    idx_empty = _chunk_rows(empty, chunks, n_slots)
    zeros = jnp.zeros((SC_WINDOW, SC_COLS), rows.dtype)

    @pl.kernel(out_type=jax.ShapeDtypeStruct((chunks * n_slots, SC_COLS), rows.dtype), mesh=_sc_mesh(),
               scratch_types=[])
    def scatter(x_hbm, i0_hbm, i1_hbm, z_hbm, ie_hbm, o_hbm):
        def body(x_vmem, i_vmem):
            pltpu.sync_copy(x_vmem, o_hbm.at[i_vmem.at[0]])

        for i_hbm in (i0_hbm, i1_hbm):
            _sc_rows_pipeline(body, x.shape[0], lambda i: (i, 0), SC_COLS)(x_hbm, i_hbm)
        _sc_rows_pipeline(body, idx_empty.shape[1], lambda i: (0, 0), SC_COLS)(z_hbm, ie_hbm)

    return scatter(x, idx_first, idx_second, zeros, idx_empty).reshape(chunks, n_slots, SC_COLS)


def _cast_kernel(*refs):
    n = len(refs) // 2
    for src, dst in zip(refs[:n], refs[n:]):
        dst[...] = src[...].astype(dst.dtype)


def _expert_weights_bf16(w_gate, w_up, w_down):
    ws = (w_gate, w_up, w_down)
    specs = [pl.BlockSpec((None,) + w.shape[1:], lambda e: (e, 0, 0)) for w in ws]
    return pl.pallas_call(
        _cast_kernel,
        grid=(N_EXPERTS,),
        in_specs=specs,
        out_specs=specs,
        out_shape=[jax.ShapeDtypeStruct(w.shape, BF16) for w in ws],
        compiler_params=pltpu.CompilerParams(dimension_semantics=("parallel",), vmem_limit_bytes=VMEM_LIMIT),
        name="cast_weights",
    )(*ws)


def _expert_mlp(xb, wg_ref, wu_ref, wd_ref):
    acc = None
    for c in range(D_EXPERT // EXPERT_CHUNK):
        cols = slice(c * EXPERT_CHUNK, (c + 1) * EXPERT_CHUNK)
        gate = _dot(xb, wg_ref[:, cols])
        up = _dot(xb, wu_ref[:, cols])
        hidden = (gate * jax.nn.sigmoid(gate) * up).astype(BF16)
        part = _dot(hidden, wd_ref[cols, :])
        acc = part if acc is None else acc + part
    return acc


def _expert_kernel(be_ref, x_ref, wga_ref, wua_ref, wda_ref, wgb_ref, wub_ref, wdb_ref, y_ref):
    i = pl.program_id(0)
    same = be_ref[2 * i] == be_ref[2 * i + 1]

    @pl.when(same)
    def _():
        xb = _unpack_bf16_pairs(_load_chunk_major(x_ref, 0, 2 * EXPERT_BLOCK)).astype(BF16)
        _store_chunk_major(y_ref, 0, _pack_bf16_pairs(_expert_mlp(xb, wga_ref, wua_ref, wda_ref)))

    @pl.when(jnp.logical_not(same))
    def _():
        for half, w in enumerate(((wga_ref, wua_ref, wda_ref), (wgb_ref, wub_ref, wdb_ref))):
            xb = _unpack_bf16_pairs(_load_chunk_major(x_ref, half * EXPERT_BLOCK, EXPERT_BLOCK)).astype(BF16)
            _store_chunk_major(y_ref, half * EXPERT_BLOCK, _pack_bf16_pairs(_expert_mlp(xb, *w)))


def _experts(xs, block_e, w_gate, w_up, w_down):
    n_blocks = block_e.shape[0]
    assert n_blocks % 2 == 0
    d = D_MODEL
    blk = _chunk_major_spec(PACKED_CHUNKS, 2 * EXPERT_BLOCK, lambda i, be: (0, i, 0))
    w_specs = [pl.BlockSpec(shape, (lambda i, be, k=k: (be[2 * i + k], 0, 0)))
               for k in range(2) for shape in ((None, d, D_EXPERT), (None, d, D_EXPERT), (None, D_EXPERT, d))]
    grid_spec = pltpu.PrefetchScalarGridSpec(
        num_scalar_prefetch=1,
        grid=(n_blocks // 2,),
        in_specs=[blk] + w_specs,
        out_specs=blk,
    )
    return pl.pallas_call(
        _expert_kernel,
        grid_spec=grid_spec,
        out_shape=jax.ShapeDtypeStruct(xs.shape, xs.dtype),
        compiler_params=pltpu.CompilerParams(
            dimension_semantics=("arbitrary",), vmem_limit_bytes=VMEM_LIMIT),
        name="experts",
    )(block_e, xs, w_gate, w_up, w_down, w_gate, w_up, w_down)


def _gather(yb, dest):
    chunks, n_slots, _ = yb.shape
    x = yb.reshape(chunks * n_slots, SC_COLS)
    idx = _chunk_rows(dest.reshape(-1), chunks, n_slots)
    m = idx.shape[1]

    @pl.kernel(out_type=jax.ShapeDtypeStruct((m, SC_COLS), yb.dtype), mesh=_sc_mesh(), scratch_types=[])
    def gather(x_hbm, i_hbm, o_hbm):
        def body(i_vmem, o_vmem):
            pltpu.sync_copy(x_hbm.at[i_vmem.at[0]], o_vmem)

        pltpu.emit_pipeline(
            body,
            grid=(m // SC_WINDOW,),
            in_specs=[pl.BlockSpec((1, SC_WINDOW), index_map=lambda i: (0, i))],
            out_specs=[pl.BlockSpec((SC_WINDOW, SC_COLS), index_map=lambda i: (i, 0))],
            core_axis_name=("core", "subcore"),
            dimension_semantics=(pltpu.PARALLEL,),
        )(i_hbm, o_hbm)

    return gather(x, idx).reshape(chunks, m // chunks, SC_COLS)


def _combine_kernel(first_ref, second_ref, h1_ref, route_ref, g_ref, b_ref, y_ref):
    tm = y_ref.shape[0]
    route = route_ref[...]
    w_first = route[:, ROUTE_LANE_W:ROUTE_LANE_W + 1]
    w_second = route[:, ROUTE_LANE_W + 1:ROUTE_LANE_W + 2]
    ffn = (w_first * _unpack_bf16_pairs(_load_chunk_major(first_ref, 0, tm))
           + w_second * _unpack_bf16_pairs(_load_chunk_major(second_ref, 0, tm)))
    y_ref[...] = _layer_norm(DEEPNORM_ALPHA * _load_chunk_major(h1_ref, 0, tm) + ffn, g_ref[...], b_ref[...])


def _combine(h1, route, g, ln2_g, ln2_b):
    t = route.shape[0]
    d = D_MODEL
    tm = ROW_TILE
    nt = t // tm
    return pl.pallas_call(
        _combine_kernel,
        grid=(nt,),
        in_specs=[
            _chunk_major_spec(PACKED_CHUNKS, tm, lambda i: (0, i, 0)),
            _chunk_major_spec(PACKED_CHUNKS, tm, lambda i: (0, nt + i, 0)),
            _chunk_major_spec(F32_CHUNKS, tm, lambda i: (0, i, 0)),
            pl.BlockSpec((tm, LANES), lambda i: (i, 0)),
            _resident((1, d)), _resident((1, d)),
        ],
        out_specs=pl.BlockSpec((tm, d), lambda i: (i, 0)),
        out_shape=jax.ShapeDtypeStruct((t, d), F32),
        compiler_params=pltpu.CompilerParams(
            dimension_semantics=("parallel",), vmem_limit_bytes=VMEM_LIMIT),
        name="combine",
    )(g, g, h1, route, ln2_g, ln2_b)


def _token_mixing(x, p):
    bsz, s, d = x.shape
    qkv0, qkv1, qkv2, cg, ga, h = _inproj(x, p['ln_in_g'], p['ln_in_b'], p['w_in'], p['b_gate'], p['conv_w'],
                                          p['w_conv_out'])
    o_list, lse_list = [], []
    for qkv, r in zip((qkv0, qkv1, qkv2), DILATIONS):
        o, lse = _attention(qkv.reshape(3, bsz, s, GROUP_W), s // r)
        o_list.append(o)
        lse_list.append(lse)
    h1, h1_packed, route, plan, counts = _mix(h, o_list, lse_list, cg, ga, p['w_attn_out'], p['w_o'],
                             p['ln1_g'], p['ln1_b'], p['wr_hi'], p['wr_lo'], p['b_route'])
    t = bsz * s
    route = route.reshape(t, LANES)
    dest, empty, block_e = _dispatch_plan(plan, counts, t)
    return h1, route, dest, empty, block_e, h1_packed


def kernel(x_prompt, x_sample, ln_in_g, ln_in_b, w_in, b_gate, conv_w, w_attn_out, w_conv_out, w_o, ln1_g, ln1_b, w_route_group, b_route_group, w_route_expert, b_route_expert, w_gate, w_up, w_down, ln2_g, ln2_b):
    d = D_MODEL
    w_route = jnp.concatenate([w_route_group[0], w_route_expert[0]], axis=1)
    w_route = jnp.pad(w_route, ((0, 0), (0, LANES - w_route.shape[1])))
    wr_hi = w_route.astype(BF16)
    b_route = jnp.concatenate([b_route_group[0], b_route_expert[0]]).astype(F32)
    p = {
        'ln_in_g': ln_in_g.reshape(1, d), 'ln_in_b': ln_in_b.reshape(1, d),
        'w_in': w_in[0].astype(BF16), 'b_gate': b_gate[0].reshape(1, 2 * d), 'conv_w': conv_w[0],
        'w_conv_out': w_conv_out[0].astype(BF16), 'w_attn_out': w_attn_out[0].astype(BF16),
        'w_o': w_o[0].astype(BF16),
        'ln1_g': ln1_g[0].reshape(1, d), 'ln1_b': ln1_b[0].reshape(1, d),
        'wr_hi': wr_hi, 'wr_lo': (w_route - wr_hi.astype(F32)).astype(BF16),
        'b_route': jnp.pad(b_route, (0, LANES - b_route.shape[0])).reshape(1, LANES),
        'ln2_g': ln2_g[0].reshape(1, d), 'ln2_b': ln2_b[0].reshape(1, d),
    }
    p['w_gate'], p['w_up'], p['w_down'] = _expert_weights_bf16(w_gate[0], w_up[0], w_down[0])
    batches = [_token_mixing(x, p) for x in (x_prompt, x_sample)]
    sorted_rows = [_dispatch(h1_packed, dest, empty, block_e.shape[0] * EXPERT_BLOCK)
                   for _, _, dest, empty, block_e, h1_packed in batches]
    gathered = [_gather(_experts(xs, b[4], p['w_gate'], p['w_up'], p['w_down']), b[2])
                for xs, b in zip(sorted_rows, batches)]
    outs = [_combine(b[0], b[1], g, p['ln2_g'], p['ln2_b']).reshape(x.shape)
            for b, g, x in zip(batches, gathered, (x_prompt, x_sample))]
    return tuple(outs)
```

```python
import functools

import jax
import jax.numpy as jnp
import numpy as np
from jax import lax
from jax.experimental import pallas as pl
from jax.experimental.pallas import tpu as pltpu
from jax.experimental.pallas import tpu_sc as plsc

F32 = jnp.float32
BF16 = jnp.bfloat16

D_MODEL = 1024
HEAD_DIM = 64
HEADS_PER_GROUP = 4
GROUP_W = HEADS_PER_GROUP * HEAD_DIM
DILATIONS = (1, 4, 16)
BAND = 64
N_GROUPS = len(DILATIONS)
D_ATTN = N_GROUPS * GROUP_W
ATTN_SCALE = HEAD_DIM ** -0.5
ROPE_DIM = HEAD_DIM // 4
ROPE_THETA = 500000.0
MASK_VALUE = -1e30
D_CONV = D_MODEL
COL_CONV_B = 3 * D_ATTN
COL_CONV_C = COL_CONV_B + D_CONV
COL_CONV_H = COL_CONV_C + D_CONV
COL_GATE_ATTN = COL_CONV_H + D_CONV
COL_GATE_CONV = COL_GATE_ATTN + D_MODEL
N_EXPERT_GROUPS = 4
EXPERTS_PER_GROUP = 8
N_EXPERTS = N_EXPERT_GROUPS * EXPERTS_PER_GROUP
TOP_K_INNER = 2
D_EXPERT = 512
EXPERT_BLOCK = 512
EXPERT_CHUNK = 256
LN_EPS = 1e-5
DEPTH = 1
DEEPNORM_ALPHA = (2 * DEPTH) ** 0.25

LANES = 128
HEADS_PER_BLOCK = LANES // HEAD_DIM
ROW_TILE = 512
HALO = 16
Q_BLOCK = 128
K_BLOCK = Q_BLOCK + 2 * BAND
Q_CHUNK = 2048
SC_COLS = 256
SC_WINDOW = 128
F32_CHUNKS = D_MODEL // SC_COLS
PACKED_COLS = D_MODEL // 2
PACKED_CHUNKS = PACKED_COLS // SC_COLS
U32 = jnp.uint32
HIGH_HALF = np.uint32(0xFFFF0000)
ROUTE_LANE0 = N_EXPERT_GROUPS
ROUTE_LANE_W = TOP_K_INNER
ROUTE_LANE_RANK = 2 * TOP_K_INNER
PLAN_ROWS = 8
VMEM_LIMIT = 56 * 1024 * 1024


def _layer_norm(v, g, b):
    mu = jnp.mean(v, axis=-1, keepdims=True)
    d = v - mu
    var = jnp.mean(d * d, axis=-1, keepdims=True)
    return d * lax.rsqrt(var + LN_EPS) * g + b


def _dot(a, b):
    return jnp.dot(a, b, preferred_element_type=F32)


def _store_chunk_major(ref, first_row, val):
    for c in range(ref.shape[0]):
        ref[c, pl.ds(first_row, val.shape[0]), :] = val[:, c * SC_COLS:(c + 1) * SC_COLS]


def _load_chunk_major(ref, first_row, n):
    return jnp.concatenate([ref[c, pl.ds(first_row, n), :] for c in range(ref.shape[0])], axis=1)


def _chunk_major_spec(chunks, rows, index_map):
    return pl.BlockSpec((chunks, rows, SC_COLS), index_map)


def _pack_bf16_pairs(val):
    bits = lax.bitcast_convert_type(val.astype(BF16).astype(F32), U32)
    return (bits[:, PACKED_COLS:] & HIGH_HALF) | (bits[:, :PACKED_COLS] >> 16)


def _unpack_bf16_pairs(packed):
    low = lax.bitcast_convert_type(packed << 16, F32)
    high = lax.bitcast_convert_type(packed & HIGH_HALF, F32)
    return jnp.concatenate([low, high], axis=1)


def _resident(shape):
    nd = len(shape)
    return pl.BlockSpec(shape, lambda *_: (0,) * nd, pipeline_mode=pl.Buffered(1))


def _inproj_kernel(x_ref, xp_ref, xn_ref, g_ref, b_ref, win_ref, bg_ref, cw_ref, wco_ref,
                   cos_ref, sa_ref, sb_ref,
                   qkv0_ref, qkv1_ref, qkv2_ref, cg_ref, ga_ref, h_ref,
                   hext_ref, u_ref, stage_ref):
    i = pl.program_id(1)
    nt = pl.num_programs(1)
    tm = x_ref.shape[0]
    g = g_ref[...]
    b = b_ref[...]
    h = _layer_norm(x_ref[...], g, b)
    h_ref[...] = h
    hext_ref[pl.ds(HALO, tm), :] = h.astype(BF16)
    hext_ref[pl.ds(0, HALO), :] = _layer_norm(xp_ref[...], g, b).astype(BF16)
    hext_ref[pl.ds(HALO + tm, HALO), :] = _layer_norm(xn_ref[...], g, b).astype(BF16)
    hm = hext_ref[pl.ds(HALO, tm), :]
    he = hext_ref[...]

    def proj(lhs, col, n):
        return _dot(lhs, win_ref[:, col:col + n])

    u = proj(he, COL_CONV_C, D_CONV) * proj(he, COL_CONV_H, D_CONV)
    rows = lax.broadcasted_iota(jnp.int32, (tm + 2 * HALO, 1), 0)
    lo = jnp.where(i == 0, HALO, 0)
    hi = jnp.where(i == nt - 1, HALO + tm, tm + 2 * HALO)
    u_ref[...] = jnp.where((rows >= lo) & (rows < hi), u, 0.0)
    cw = cw_ref[...]
    conv = (cw[0:1] * u_ref[pl.ds(HALO - 1, tm), :] + cw[1:2] * u_ref[pl.ds(HALO, tm), :]
            + cw[2:3] * u_ref[pl.ds(HALO + 1, tm), :])
    cb = proj(hm, COL_CONV_B, D_CONV)
    conv_branch = _dot((cb * conv).astype(BF16), wco_ref[...])
    gate_conv = jax.nn.sigmoid(proj(hm, COL_GATE_CONV, D_MODEL) + bg_ref[:, D_MODEL:])
    cg_ref[...] = (gate_conv * conv_branch).astype(BF16)
    ga_ref[...] = jax.nn.sigmoid(proj(hm, COL_GATE_ATTN, D_MODEL) + bg_ref[:, :D_MODEL]).astype(BF16)

    cosv = cos_ref[...]
    sav = sa_ref[...]
    sbv = sb_ref[...]
    half = ROPE_DIM // 2
    for gi, (out_ref, r) in enumerate(zip((qkv0_ref, qkv1_ref, qkv2_ref), DILATIONS)):
        for t in range(3):
            p = proj(hm, t * D_ATTN + gi * GROUP_W, GROUP_W)
            if t < 2:
                parts = []
                for c in range(GROUP_W // LANES):
                    pc = p[:, c * LANES:(c + 1) * LANES]
                    parts.append(pc * cosv + pltpu.roll(pc, LANES - half, 1) * sav
                                 + pltpu.roll(pc, half, 1) * sbv)
                p = jnp.concatenate(parts, axis=1)
                if t == 0:
                    p = p * ATTN_SCALE
            if r == 1:
                out_ref[t, 0] = p.astype(BF16)
            else:
                for c in range(GROUP_W // LANES):
                    stage_ref[c] = p[:, c * LANES:(c + 1) * LANES]
                for res in range(r):
                    for c in range(GROUP_W // LANES):
                        out_ref[t, res, :, c * LANES:(c + 1) * LANES] = (
                            stage_ref[c, pl.ds(res, tm // r, stride=r), :].astype(BF16))


def _rope_tables(s):
    half = ROPE_DIM // 2
    inv_freq = (np.float32(ROPE_THETA) ** (-np.arange(half, dtype=np.float32) * np.float32(2.0) / ROPE_DIM))
    ang = np.arange(s, dtype=np.float32)[:, None] * inv_freq.astype(np.float32)[None, :]
    cos = np.cos(ang).astype(np.float32)
    sin = np.sin(ang).astype(np.float32)
    pad = np.zeros((s, HEAD_DIM - ROPE_DIM), np.float32)
    zero = np.zeros((s, half), np.float32)
    cos_t = np.concatenate([cos, cos, pad + 1.0], axis=1)
    sa_t = np.concatenate([-sin, zero, pad], axis=1)
    sb_t = np.concatenate([zero, sin, pad], axis=1)
    rep = LANES // HEAD_DIM
    return tuple(jnp.asarray(np.tile(t, (1, rep))) for t in (cos_t, sa_t, sb_t))


def _inproj(x, ln_g, ln_b, w_in, b_gate, conv_w, w_conv_out):
    bsz, s, d = x.shape
    tm = ROW_TILE
    nt = s // tm
    hpt = tm // HALO
    cos_t, sa_t, sb_t = _rope_tables(s)
    row = lambda b, i: (b, i, 0)
    tab = pl.BlockSpec((tm, LANES), lambda b, i: (i, 0))
    qkv_shapes = [jax.ShapeDtypeStruct((3, bsz, r, s // r, GROUP_W), BF16) for r in DILATIONS]
    qkv_specs = [pl.BlockSpec((3, None, r, tm // r, GROUP_W), lambda b, i: (0, b, 0, i, 0))
                 for r in DILATIONS]
    return pl.pallas_call(
        _inproj_kernel,
        grid=(bsz, nt),
        in_specs=[
            pl.BlockSpec((None, tm, d), row),
            pl.BlockSpec((None, HALO, d), lambda b, i: (b, jnp.maximum(i * hpt - 1, 0), 0)),
            pl.BlockSpec((None, HALO, d), lambda b, i: (b, jnp.minimum((i + 1) * hpt, s // HALO - 1), 0)),
            _resident((1, d)), _resident((1, d)),
            _resident(w_in.shape), _resident((1, 2 * d)), _resident(conv_w.shape),
            _resident(w_conv_out.shape),
            tab, tab, tab,
        ],
        out_specs=qkv_specs + [pl.BlockSpec((None, tm, d), row)] * 3,
        out_shape=(qkv_shapes + [jax.ShapeDtypeStruct((bsz, s, d), BF16)] * 2
                   + [jax.ShapeDtypeStruct((bsz, s, d), F32)]),
        scratch_shapes=[
            pltpu.VMEM((tm + 2 * HALO, d), BF16),
            pltpu.VMEM((tm + 2 * HALO, D_CONV), F32),
            pltpu.VMEM((GROUP_W // LANES, tm, LANES), F32),
        ],
        compiler_params=pltpu.CompilerParams(
            dimension_semantics=("parallel", "parallel"), vmem_limit_bytes=VMEM_LIMIT),
        name="inproj",
    )(x, x, x, ln_g, ln_b, w_in, b_gate, conv_w, w_conv_out, cos_t, sa_t, sb_t)


def _attn_kernel(q_ref, k_ref, v_ref, o_ref, lse_ref, *, sub_len):
    s = k_ref.shape[0]
    qc = q_ref.shape[0]
    base = pl.program_id(1) * qc
    lane = lax.broadcasted_iota(jnp.int32, (1, GROUP_W), 1)
    head_masks = [(lane >= h * HEAD_DIM) & (lane < (h + 1) * HEAD_DIM) for h in range(HEADS_PER_GROUP)]
    qi = lax.broadcasted_iota(jnp.int32, (Q_BLOCK, 1), 0)
    kj = lax.broadcasted_iota(jnp.int32, (1, K_BLOCK), 1)
    low_half = lax.broadcasted_iota(jnp.int32, (1, LANES), 1) < HEAD_DIM

    def body(j, carry):
        r0 = pl.multiple_of(j * Q_BLOCK, Q_BLOCK)
        s0 = base + r0
        k0 = pl.multiple_of(jnp.clip(s0 - BAND, 0, s - K_BLOCK), BAND)
        q = q_ref[pl.ds(r0, Q_BLOCK), :]
        k = k_ref[pl.ds(k0, K_BLOCK), :]
        v = v_ref[pl.ds(k0, K_BLOCK), :]
        sub_lo = (s0 // sub_len) * sub_len
        qpos = s0 + qi
        kpos = k0 + kj
        valid = (jnp.abs(qpos - kpos) <= BAND) & (kpos >= sub_lo) & (kpos < sub_lo + sub_len)
        qm = jnp.concatenate([jnp.where(hm, q, jnp.zeros_like(q)) for hm in head_masks], axis=0)
        sc = lax.dot_general(qm, k, (((1,), (1,)), ((), ())), preferred_element_type=F32)
        sc = jnp.where(jnp.concatenate([valid] * HEADS_PER_GROUP, axis=0), sc, MASK_VALUE)
        m = jnp.max(sc, axis=-1, keepdims=True)
        p = jnp.exp(sc - m)
        den = jnp.sum(p, axis=-1, keepdims=True)
        o_all = _dot((p * (1.0 / den)).astype(BF16), v)
        lse_all = m + jnp.log(den)
        o_cols, lse_cols = [], []
        for c in range(GROUP_W // LANES):
            lanes = slice(c * LANES, (c + 1) * LANES)
            h_lo, h_hi = HEADS_PER_BLOCK * c, HEADS_PER_BLOCK * c + 1
            rows_lo = slice(h_lo * Q_BLOCK, (h_lo + 1) * Q_BLOCK)
            rows_hi = slice(h_hi * Q_BLOCK, (h_hi + 1) * Q_BLOCK)
            o_cols.append(jnp.where(low_half, o_all[rows_lo, lanes], o_all[rows_hi, lanes]))
            lse_cols.append(jnp.where(low_half, lse_all[rows_lo], lse_all[rows_hi]))
        o_ref[pl.ds(r0, Q_BLOCK), :] = jnp.concatenate(o_cols, axis=1).astype(BF16)
        lse_ref[pl.ds(r0, Q_BLOCK), :] = jnp.concatenate(lse_cols, axis=1)
        return carry

    lax.fori_loop(0, qc // Q_BLOCK, body, 0, unroll=True)


def _attention(qkv, sub_len):
    _, bsz, s, w = qkv.shape
    qc = min(Q_CHUNK, s)
    kv_spec = lambda t: pl.BlockSpec((None, None, s, w), lambda b, j: (t, b, 0, 0))
    return pl.pallas_call(
        functools.partial(_attn_kernel, sub_len=sub_len),
        grid=(bsz, s // qc),
        in_specs=[pl.BlockSpec((None, None, qc, w), lambda b, j: (0, b, j, 0)), kv_spec(1), kv_spec(2)],
        out_specs=[pl.BlockSpec((None, qc, w), lambda b, j: (b, j, 0))] * 2,
        out_shape=[jax.ShapeDtypeStruct((bsz, s, w), BF16), jax.ShapeDtypeStruct((bsz, s, w), F32)],
        compiler_params=pltpu.CompilerParams(
            dimension_semantics=("parallel", "arbitrary"), vmem_limit_bytes=VMEM_LIMIT),
        name="attn",
    )(qkv, qkv, qkv)


def _first_index_of_max(vals, lane_f):
    mx = jnp.max(vals, axis=-1, keepdims=True)
    idx = jnp.min(jnp.where(vals == mx, lane_f, float(LANES)), axis=-1, keepdims=True)
    return mx, idx


def _mix_kernel(h_ref, o0_ref, o1_ref, o2_ref, l0_ref, l1_ref, l2_ref, cg_ref, ga_ref,
                wao_ref, wo_ref, g1_ref, b1_ref, wrh_ref, wrl_ref, br_ref,
                h1_ref, h1p_ref, route_ref, plan_ref, counts_ref, so_ref, sl_ref, tri_ref):
    tm = h_ref.shape[0]
    halves = GROUP_W // LANES
    for gi, (o_ref, l_ref, r) in enumerate(zip((o0_ref, o1_ref, o2_ref), (l0_ref, l1_ref, l2_ref), DILATIONS)):
        for c in range(halves):
            cols = slice(c * LANES, (c + 1) * LANES)
            for res in range(r):
                rows = pl.ds(res, tm // r, stride=r) if r > 1 else slice(None)
                so_ref[gi * halves + c, rows, :] = o_ref[res, :, cols].astype(F32)
                sl_ref[gi * halves + c, rows, :] = l_ref[res, :, cols]

    def natural(ref, gi):
        return jnp.concatenate([ref[gi * halves + c] for c in range(halves)], axis=1)

    lses = [natural(sl_ref, gi) for gi in range(N_GROUPS)]
    mx = jnp.maximum(jnp.maximum(lses[0], lses[1]), lses[2])
    es = [jnp.exp(l - mx) for l in lses]
    inv_den = 1.0 / (es[0] + es[1] + es[2])
    attn = jnp.concatenate(
        [(natural(so_ref, gi) * (es[gi] * inv_den)).astype(BF16) for gi in range(N_GROUPS)], axis=1)
    attn_branch = _dot(attn, wao_ref[...])
    merged = ga_ref[...].astype(F32) * attn_branch + cg_ref[...].astype(F32)
    mix = _dot(merged.astype(BF16), wo_ref[...])
    h1 = _layer_norm(DEEPNORM_ALPHA * h_ref[...] + mix, g1_ref[...], b1_ref[...])
    _store_chunk_major(h1_ref, 0, h1)
    _store_chunk_major(h1p_ref, 0, _pack_bf16_pairs(h1))

    hi = h1.astype(BF16)
    lo = (h1 - hi.astype(F32)).astype(BF16)
    wrh = wrh_ref[...]
    logits = _dot(hi, wrh) + _dot(lo, wrh) + _dot(hi, wrl_ref[...]) + br_ref[...]
    lane = lax.broadcasted_iota(jnp.int32, (1, LANES), 1)
    lane_f = lane.astype(F32)
    is_grp = lane < N_EXPERT_GROUPS
    gl = jnp.where(is_grp, logits, MASK_VALUE)
    ge = jnp.exp(gl - jnp.max(gl, axis=-1, keepdims=True))
    gp = jnp.where(is_grp, ge / jnp.sum(ge, axis=-1, keepdims=True), -1.0)
    grp_w, grp_idx = _first_index_of_max(gp, lane_f)
    e_lo = ROUTE_LANE0 + grp_idx * EXPERTS_PER_GROUP
    sel = jnp.where((lane_f >= e_lo) & (lane_f < e_lo + EXPERTS_PER_GROUP), logits, MASK_VALUE)
    v1, i1 = _first_index_of_max(sel, lane_f)
    v2, i2 = _first_index_of_max(jnp.where(lane_f == i1, MASK_VALUE, sel), lane_f)
    e2 = jnp.exp(v2 - v1)
    w1 = grp_w / (1.0 + e2)
    w2 = grp_w * e2 / (1.0 + e2)

    first = (pl.program_id(0) == 0) & (pl.program_id(1) == 0)

    @pl.when(first)
    def _():
        counts_ref[...] = jnp.zeros_like(counts_ref)
        r = lax.broadcasted_iota(jnp.int32, (tm, tm), 0)
        c = lax.broadcasted_iota(jnp.int32, (tm, tm), 1)
        tri_ref[...] = (c < r).astype(BF16)

    oh0 = lane_f == i1 - ROUTE_LANE0
    oh1 = lane_f == i2 - ROUTE_LANE0
    oh = oh0.astype(F32) + oh1.astype(F32)
    before = counts_ref[...] + _dot(tri_ref[...], oh.astype(BF16))
    r0 = jnp.sum(jnp.where(oh0, before, 0.0), axis=-1, keepdims=True)
    r1 = jnp.sum(jnp.where(oh1, before + oh0.astype(F32), 0.0), axis=-1, keepdims=True)
    counts_ref[...] = counts_ref[...] + jnp.sum(oh, axis=0, keepdims=True)

    per_lane = (i1 - ROUTE_LANE0, i2 - ROUTE_LANE0, w1, w2, r0, r1)
    route = jnp.zeros((tm, LANES), F32)
    for k, v in enumerate(per_lane):
        route = jnp.where(lane == k, v, route)
    route_ref[...] = route
    plan_ref[...] = route.T[:PLAN_ROWS, :]


def _mix(h, o_list, lse_list, cg, ga, w_attn_out, w_o, ln1_g, ln1_b, wr_hi, wr_lo, b_route):
    bsz, s, d = h.shape
    tm = ROW_TILE
    row = lambda b, i: (b, i, 0)
    dil_specs = [pl.BlockSpec((None, r, tm // r, GROUP_W), lambda b, i: (b, 0, i, 0)) for r in DILATIONS]
    o_views = [o.reshape(bsz, r, s // r, GROUP_W) for o, r in zip(o_list, DILATIONS)]
    l_views = [l.reshape(bsz, r, s // r, GROUP_W) for l, r in zip(lse_list, DILATIONS)]
    return pl.pallas_call(
        _mix_kernel,
        grid=(bsz, s // tm),
        in_specs=[pl.BlockSpec((None, tm, d), row)] + dil_specs + dil_specs + [
            pl.BlockSpec((None, tm, d), row), pl.BlockSpec((None, tm, d), row),
            _resident(w_attn_out.shape), _resident(w_o.shape),
            _resident((1, d)), _resident((1, d)),
            _resident(wr_hi.shape), _resident(wr_lo.shape), _resident((1, LANES)),
        ],
        out_specs=[_chunk_major_spec(F32_CHUNKS, tm, lambda b, i: (0, b * (s // tm) + i, 0)),
                   _chunk_major_spec(PACKED_CHUNKS, tm, lambda b, i: (0, b * (s // tm) + i, 0)),
                   pl.BlockSpec((None, tm, LANES), row),
                   pl.BlockSpec((PLAN_ROWS, tm), lambda b, i: (0, b * (s // tm) + i)),
                   pl.BlockSpec((1, LANES), lambda b, i: (0, 0))],
        out_shape=[jax.ShapeDtypeStruct((F32_CHUNKS, bsz * s, SC_COLS), F32),
                   jax.ShapeDtypeStruct((PACKED_CHUNKS, bsz * s, SC_COLS), U32),
                   jax.ShapeDtypeStruct((bsz, s, LANES), F32),
                   jax.ShapeDtypeStruct((PLAN_ROWS, bsz * s), F32),
                   jax.ShapeDtypeStruct((1, LANES), F32)],
        scratch_shapes=[pltpu.VMEM((N_GROUPS * GROUP_W // LANES, tm, LANES), F32)] * 2
        + [pltpu.VMEM((tm, tm), BF16)],
        compiler_params=pltpu.CompilerParams(
            dimension_semantics=("arbitrary", "arbitrary"), vmem_limit_bytes=VMEM_LIMIT),
        name="mix",
    )(h, *o_views, *l_views, cg, ga, w_attn_out, w_o, ln1_g, ln1_b, wr_hi, wr_lo, b_route)


def _dispatch_plan(plan, counts, t):
    eid = plan[:TOP_K_INNER].astype(jnp.int32)
    rank = plan[ROUTE_LANE_RANK:ROUTE_LANE_RANK + TOP_K_INNER].astype(jnp.int32)
    counts = counts[0, :N_EXPERTS].astype(jnp.int32)
    padded = (counts + EXPERT_BLOCK - 1) // EXPERT_BLOCK * EXPERT_BLOCK
    pad_end = jnp.cumsum(padded)
    pad_start = pad_end - padded
    experts = jnp.arange(N_EXPERTS, dtype=jnp.int32)[:, None, None]
    start_of = jnp.sum(jnp.where(eid[None] == experts, pad_start[:, None, None], 0), axis=0)
    dest = start_of + rank
    n_blocks = -(-t * TOP_K_INNER // EXPERT_BLOCK) + N_EXPERTS
    n_slots = n_blocks * EXPERT_BLOCK
    block_start = jnp.arange(n_blocks, dtype=jnp.int32) * EXPERT_BLOCK
    block_e = jnp.minimum(jnp.sum((block_start[:, None] >= pad_end[None, :]).astype(jnp.int32), axis=1),
                          N_EXPERTS - 1)
    n_empty = n_slots - t * TOP_K_INNER
    gap_end = jnp.cumsum(padded - counts)
    gap_start = gap_end - (padded - counts)
    k = jnp.arange(n_empty, dtype=jnp.int32)
    in_gap = (k[None, :] >= gap_start[:, None]) & (k[None, :] < gap_end[:, None])
    first_of_gap = (pad_start + counts - gap_start)[:, None]
    empty = jnp.where(k < gap_end[-1],
                      jnp.sum(jnp.where(in_gap, first_of_gap + k[None, :], 0), axis=0),
                      pad_end[-1] + k - gap_end[-1])
    return dest.astype(jnp.int32), empty.astype(jnp.int32), block_e


def _sc_mesh():
    return plsc.VectorSubcoreMesh(core_axis_name="core", subcore_axis_name="subcore")


def _sc_rows_pipeline(body, n_rows, row_index_map, cols):
    return pltpu.emit_pipeline(
        body,
        grid=(n_rows // SC_WINDOW,),
        in_specs=[pl.BlockSpec((SC_WINDOW, cols), index_map=row_index_map),
                  pl.BlockSpec((1, SC_WINDOW), index_map=lambda i: (0, i))],
        out_specs=[],
        core_axis_name=("core", "subcore"),
        dimension_semantics=(pltpu.PARALLEL,),
    )


def _chunk_rows(rows, chunks, n_rows_per_chunk):
    base = jnp.arange(chunks, dtype=jnp.int32)[:, None] * n_rows_per_chunk
    return (base + rows[None, :]).reshape(1, -1)


def _dispatch(rows, dest, empty, n_slots):
    chunks, t, _ = rows.shape
    x = rows.reshape(chunks * t, SC_COLS)
    idx_first = _chunk_rows(dest[0], chunks, n_slots)
    idx_second = _chunk_rows(dest[1], chunks, n_slots)
    idx_empty = _chunk_rows(empty, chunks, n_slots)
    zeros = jnp.zeros((SC_WINDOW, SC_COLS), rows.dtype)

    @pl.kernel(out_type=jax.ShapeDtypeStruct((chunks * n_slots, SC_COLS), rows.dtype), mesh=_sc_mesh(),
               scratch_types=[])
    def scatter(x_hbm, i0_hbm, i1_hbm, z_hbm, ie_hbm, o_hbm):
        def body(x_vmem, i_vmem):
            pltpu.sync_copy(x_vmem, o_hbm.at[i_vmem.at[0]])

        for i_hbm in (i0_hbm, i1_hbm):
            _sc_rows_pipeline(body, x.shape[0], lambda i: (i, 0), SC_COLS)(x_hbm, i_hbm)
        _sc_rows_pipeline(body, idx_empty.shape[1], lambda i: (0, 0), SC_COLS)(z_hbm, ie_hbm)

    return scatter(x, idx_first, idx_second, zeros, idx_empty).reshape(chunks, n_slots, SC_COLS)


def _cast_kernel(*refs):
    n = len(refs) // 2
    for src, dst in zip(refs[:n], refs[n:]):
        dst[...] = src[...].astype(dst.dtype)


def _expert_weights_bf16(w_gate, w_up, w_down):
    ws = (w_gate, w_up, w_down)
    specs = [pl.BlockSpec((None,) + w.shape[1:], lambda e: (e, 0, 0)) for w in ws]
    return pl.pallas_call(
        _cast_kernel,
        grid=(N_EXPERTS,),
        in_specs=specs,
        out_specs=specs,
        out_shape=[jax.ShapeDtypeStruct(w.shape, BF16) for w in ws],
        compiler_params=pltpu.CompilerParams(dimension_semantics=("parallel",), vmem_limit_bytes=VMEM_LIMIT),
        name="cast_weights",
    )(*ws)


def _expert_mlp(xb, wg_ref, wu_ref, wd_ref):
    acc = None
    for c in range(D_EXPERT // EXPERT_CHUNK):
        cols = slice(c * EXPERT_CHUNK, (c + 1) * EXPERT_CHUNK)
        gate = _dot(xb, wg_ref[:, cols])
        up = _dot(xb, wu_ref[:, cols])
        hidden = (gate * jax.nn.sigmoid(gate) * up).astype(BF16)
        part = _dot(hidden, wd_ref[cols, :])
        acc = part if acc is None else acc + part
    return acc


def _expert_kernel(be_ref, x_ref, wg_ref, wu_ref, wd_ref, y_ref):
    xb = _unpack_bf16_pairs(_load_chunk_major(x_ref, 0, EXPERT_BLOCK)).astype(BF16)
    _store_chunk_major(y_ref, 0, _pack_bf16_pairs(_expert_mlp(xb, wg_ref, wu_ref, wd_ref)))


def _experts(xs, block_e, w_gate, w_up, w_down):
    n_blocks = block_e.shape[0]
    d = D_MODEL
    blk = _chunk_major_spec(PACKED_CHUNKS, EXPERT_BLOCK, lambda i, be: (0, i, 0))
    grid_spec = pltpu.PrefetchScalarGridSpec(
        num_scalar_prefetch=1,
        grid=(n_blocks,),
        in_specs=[
            blk,
            pl.BlockSpec((None, d, D_EXPERT), lambda i, be: (be[i], 0, 0)),
            pl.BlockSpec((None, d, D_EXPERT), lambda i, be: (be[i], 0, 0)),
            pl.BlockSpec((None, D_EXPERT, d), lambda i, be: (be[i], 0, 0)),
        ],
        out_specs=blk,
    )
    return pl.pallas_call(
        _expert_kernel,
        grid_spec=grid_spec,
        out_shape=jax.ShapeDtypeStruct(xs.shape, xs.dtype),
        compiler_params=pltpu.CompilerParams(
            dimension_semantics=("arbitrary",), vmem_limit_bytes=VMEM_LIMIT),
        name="experts",
    )(block_e, xs, w_gate, w_up, w_down)


def _gather(yb, dest):
    chunks, n_slots, _ = yb.shape
    x = yb.reshape(chunks * n_slots, SC_COLS)
    idx = _chunk_rows(dest.reshape(-1), chunks, n_slots)
    m = idx.shape[1]

    @pl.kernel(out_type=jax.ShapeDtypeStruct((m, SC_COLS), yb.dtype), mesh=_sc_mesh(), scratch_types=[])
    def gather(x_hbm, i_hbm, o_hbm):
        def body(i_vmem, o_vmem):
            pltpu.sync_copy(x_hbm.at[i_vmem.at[0]], o_vmem)

        pltpu.emit_pipeline(
            body,
            grid=(m // SC_WINDOW,),
            in_specs=[pl.BlockSpec((1, SC_WINDOW), index_map=lambda i: (0, i))],
            out_specs=[pl.BlockSpec((SC_WINDOW, SC_COLS), index_map=lambda i: (i, 0))],
            core_axis_name=("core", "subcore"),
            dimension_semantics=(pltpu.PARALLEL,),
        )(i_hbm, o_hbm)

    return gather(x, idx).reshape(chunks, m // chunks, SC_COLS)


def _combine_kernel(first_ref, second_ref, h1_ref, route_ref, g_ref, b_ref, y_ref):
    tm = y_ref.shape[0]
    route = route_ref[...]
    w_first = route[:, ROUTE_LANE_W:ROUTE_LANE_W + 1]
    w_second = route[:, ROUTE_LANE_W + 1:ROUTE_LANE_W + 2]
    ffn = (w_first * _unpack_bf16_pairs(_load_chunk_major(first_ref, 0, tm))
           + w_second * _unpack_bf16_pairs(_load_chunk_major(second_ref, 0, tm)))
    y_ref[...] = _layer_norm(DEEPNORM_ALPHA * _load_chunk_major(h1_ref, 0, tm) + ffn, g_ref[...], b_ref[...])


def _combine(h1, route, g, ln2_g, ln2_b):
    t = route.shape[0]
    d = D_MODEL
    tm = ROW_TILE
    nt = t // tm
    return pl.pallas_call(
        _combine_kernel,
        grid=(nt,),
        in_specs=[
            _chunk_major_spec(PACKED_CHUNKS, tm, lambda i: (0, i, 0)),
            _chunk_major_spec(PACKED_CHUNKS, tm, lambda i: (0, nt + i, 0)),
            _chunk_major_spec(F32_CHUNKS, tm, lambda i: (0, i, 0)),
            pl.BlockSpec((tm, LANES), lambda i: (i, 0)),
            _resident((1, d)), _resident((1, d)),
        ],
        out_specs=pl.BlockSpec((tm, d), lambda i: (i, 0)),
        out_shape=jax.ShapeDtypeStruct((t, d), F32),
        compiler_params=pltpu.CompilerParams(
            dimension_semantics=("parallel",), vmem_limit_bytes=VMEM_LIMIT),
        name="combine",
    )(g, g, h1, route, ln2_g, ln2_b)


def _token_mixing(x, p):
    bsz, s, d = x.shape
    qkv0, qkv1, qkv2, cg, ga, h = _inproj(x, p['ln_in_g'], p['ln_in_b'], p['w_in'], p['b_gate'], p['conv_w'],
                                          p['w_conv_out'])
    o_list, lse_list = [], []
    for qkv, r in zip((qkv0, qkv1, qkv2), DILATIONS):
        o, lse = _attention(qkv.reshape(3, bsz, s, GROUP_W), s // r)
        o_list.append(o)
        lse_list.append(lse)
    h1, h1_packed, route, plan, counts = _mix(h, o_list, lse_list, cg, ga, p['w_attn_out'], p['w_o'],
                             p['ln1_g'], p['ln1_b'], p['wr_hi'], p['wr_lo'], p['b_route'])
    t = bsz * s
    route = route.reshape(t, LANES)
    dest, empty, block_e = _dispatch_plan(plan, counts, t)
    return h1, route, dest, empty, block_e, h1_packed


def kernel(x_prompt, x_sample, ln_in_g, ln_in_b, w_in, b_gate, conv_w, w_attn_out, w_conv_out, w_o, ln1_g, ln1_b, w_route_group, b_route_group, w_route_expert, b_route_expert, w_gate, w_up, w_down, ln2_g, ln2_b):
    d = D_MODEL
    w_route = jnp.concatenate([w_route_group[0], w_route_expert[0]], axis=1)
    w_route = jnp.pad(w_route, ((0, 0), (0, LANES - w_route.shape[1])))
    wr_hi = w_route.astype(BF16)
    b_route = jnp.concatenate([b_route_group[0], b_route_expert[0]]).astype(F32)
    p = {
        'ln_in_g': ln_in_g.reshape(1, d), 'ln_in_b': ln_in_b.reshape(1, d),
        'w_in': w_in[0].astype(BF16), 'b_gate': b_gate[0].reshape(1, 2 * d), 'conv_w': conv_w[0],
        'w_conv_out': w_conv_out[0].astype(BF16), 'w_attn_out': w_attn_out[0].astype(BF16),
        'w_o': w_o[0].astype(BF16),
        'ln1_g': ln1_g[0].reshape(1, d), 'ln1_b': ln1_b[0].reshape(1, d),
        'wr_hi': wr_hi, 'wr_lo': (w_route - wr_hi.astype(F32)).astype(BF16),
        'b_route': jnp.pad(b_route, (0, LANES - b_route.shape[0])).reshape(1, LANES),
        'ln2_g': ln2_g[0].reshape(1, d), 'ln2_b': ln2_b[0].reshape(1, d),
    }
    p['w_gate'], p['w_up'], p['w_down'] = _expert_weights_bf16(w_gate[0], w_up[0], w_down[0])
    batches = [_token_mixing(x, p) for x in (x_prompt, x_sample)]
    sorted_rows = [_dispatch(h1_packed, dest, empty, block_e.shape[0] * EXPERT_BLOCK)
                   for _, _, dest, empty, block_e, h1_packed in batches]
    gathered = [_gather(_experts(xs, b[4], p['w_gate'], p['w_up'], p['w_down']), b[2])
                for xs, b in zip(sorted_rows, batches)]
    outs = [_combine(b[0], b[1], g, p['ln2_g'], p['ln2_b']).reshape(x.shape)
            for b, g, x in zip(batches, gathered, (x_prompt, x_sample))]
    return tuple(outs)
```

```python
import functools

import jax
import jax.numpy as jnp
import numpy as np
from jax import lax
from jax.experimental import pallas as pl
from jax.experimental.pallas import tpu as pltpu
from jax.experimental.pallas import tpu_sc as plsc

F32 = jnp.float32
BF16 = jnp.bfloat16

D_MODEL = 1024
HEAD_DIM = 64
HEADS_PER_GROUP = 4
GROUP_W = HEADS_PER_GROUP * HEAD_DIM
DILATIONS = (1, 4, 16)
BAND = 64
N_GROUPS = len(DILATIONS)
D_ATTN = N_GROUPS * GROUP_W
ATTN_SCALE = HEAD_DIM ** -0.5
ROPE_DIM = HEAD_DIM // 4
ROPE_THETA = 500000.0
MASK_VALUE = -1e30
D_CONV = D_MODEL
COL_CONV_B = 3 * D_ATTN
COL_CONV_C = COL_CONV_B + D_CONV
COL_CONV_H = COL_CONV_C + D_CONV
COL_GATE_ATTN = COL_CONV_H + D_CONV
COL_GATE_CONV = COL_GATE_ATTN + D_MODEL
N_EXPERT_GROUPS = 4
EXPERTS_PER_GROUP = 8
N_EXPERTS = N_EXPERT_GROUPS * EXPERTS_PER_GROUP
TOP_K_INNER = 2
D_EXPERT = 512
EXPERT_BLOCK = 512
EXPERT_CHUNK = 256
LN_EPS = 1e-5
DEPTH = 1
DEEPNORM_ALPHA = (2 * DEPTH) ** 0.25

LANES = 128
HEADS_PER_BLOCK = LANES // HEAD_DIM
ROW_TILE = 512
HALO = 16
Q_BLOCK = 128
K_BLOCK = Q_BLOCK + 2 * BAND
Q_CHUNK = 2048
SC_COLS = 256
SC_WINDOW = 128
F32_CHUNKS = D_MODEL // SC_COLS
PACKED_COLS = D_MODEL // 2
PACKED_CHUNKS = PACKED_COLS // SC_COLS
U32 = jnp.uint32
HIGH_HALF = np.uint32(0xFFFF0000)
ROUTE_LANE0 = N_EXPERT_GROUPS
ROUTE_LANE_W = TOP_K_INNER
ROUTE_LANE_RANK = 2 * TOP_K_INNER
PLAN_ROWS = 8
VMEM_LIMIT = 56 * 1024 * 1024


def _layer_norm(v, g, b):
    mu = jnp.mean(v, axis=-1, keepdims=True)
    d = v - mu
    var = jnp.mean(d * d, axis=-1, keepdims=True)
    return d * lax.rsqrt(var + LN_EPS) * g + b


def _dot(a, b):
    return jnp.dot(a, b, preferred_element_type=F32)


def _store_chunk_major(ref, first_row, val):
    for c in range(ref.shape[0]):
        ref[c, pl.ds(first_row, val.shape[0]), :] = val[:, c * SC_COLS:(c + 1) * SC_COLS]


def _load_chunk_major(ref, first_row, n):
    return jnp.concatenate([ref[c, pl.ds(first_row, n), :] for c in range(ref.shape[0])], axis=1)


def _chunk_major_spec(chunks, rows, index_map):
    return pl.BlockSpec((chunks, rows, SC_COLS), index_map)


def _pack_bf16_pairs(val):
    bits = lax.bitcast_convert_type(val.astype(BF16).astype(F32), U32)
    return (bits[:, PACKED_COLS:] & HIGH_HALF) | (bits[:, :PACKED_COLS] >> 16)


def _unpack_bf16_pairs(packed):
    low = lax.bitcast_convert_type(packed << 16, F32)
    high = lax.bitcast_convert_type(packed & HIGH_HALF, F32)
    return jnp.concatenate([low, high], axis=1)


def _resident(shape):
    nd = len(shape)
    return pl.BlockSpec(shape, lambda *_: (0,) * nd, pipeline_mode=pl.Buffered(1))


def _inproj_kernel(x_ref, xp_ref, xn_ref, g_ref, b_ref, win_ref, bg_ref, cw_ref, wco_ref,
                   cos_ref, sa_ref, sb_ref,
                   qkv0_ref, qkv1_ref, qkv2_ref, cg_ref, ga_ref, h_ref,
                   hext_ref, u_ref, stage_ref):
    i = pl.program_id(1)
    nt = pl.num_programs(1)
    tm = x_ref.shape[0]
    g = g_ref[...]
    b = b_ref[...]
    h = _layer_norm(x_ref[...], g, b)
    h_ref[...] = h
    hext_ref[pl.ds(HALO, tm), :] = h.astype(BF16)
    hext_ref[pl.ds(0, HALO), :] = _layer_norm(xp_ref[...], g, b).astype(BF16)
    hext_ref[pl.ds(HALO + tm, HALO), :] = _layer_norm(xn_ref[...], g, b).astype(BF16)
    hm = hext_ref[pl.ds(HALO, tm), :]
    he = hext_ref[...]

    def proj(lhs, col, n):
        return _dot(lhs, win_ref[:, col:col + n])

    u = proj(he, COL_CONV_C, D_CONV) * proj(he, COL_CONV_H, D_CONV)
    rows = lax.broadcasted_iota(jnp.int32, (tm + 2 * HALO, 1), 0)
    lo = jnp.where(i == 0, HALO, 0)
    hi = jnp.where(i == nt - 1, HALO + tm, tm + 2 * HALO)
    u_ref[...] = jnp.where((rows >= lo) & (rows < hi), u, 0.0)
    cw = cw_ref[...]
    conv = (cw[0:1] * u_ref[pl.ds(HALO - 1, tm), :] + cw[1:2] * u_ref[pl.ds(HALO, tm), :]
            + cw[2:3] * u_ref[pl.ds(HALO + 1, tm), :])
    cb = proj(hm, COL_CONV_B, D_CONV)
    conv_branch = _dot((cb * conv).astype(BF16), wco_ref[...])
    gate_conv = jax.nn.sigmoid(proj(hm, COL_GATE_CONV, D_MODEL) + bg_ref[:, D_MODEL:])
    cg_ref[...] = (gate_conv * conv_branch).astype(BF16)
    ga_ref[...] = jax.nn.sigmoid(proj(hm, COL_GATE_ATTN, D_MODEL) + bg_ref[:, :D_MODEL]).astype(BF16)

    cosv = cos_ref[...]
    sav = sa_ref[...]
    sbv = sb_ref[...]
    half = ROPE_DIM // 2
    for gi, (out_ref, r) in enumerate(zip((qkv0_ref, qkv1_ref, qkv2_ref), DILATIONS)):
        for t in range(3):
            p = proj(hm, t * D_ATTN + gi * GROUP_W, GROUP_W)
            if t < 2:
                parts = []
                for c in range(GROUP_W // LANES):
                    pc = p[:, c * LANES:(c + 1) * LANES]
                    parts.append(pc * cosv + pltpu.roll(pc, LANES - half, 1) * sav
                                 + pltpu.roll(pc, half, 1) * sbv)
                p = jnp.concatenate(parts, axis=1)
                if t == 0:
                    p = p * ATTN_SCALE
            if r == 1:
                out_ref[t, 0] = p.astype(BF16)
            else:
                for c in range(GROUP_W // LANES):
                    stage_ref[c] = p[:, c * LANES:(c + 1) * LANES]
                for res in range(r):
                    for c in range(GROUP_W // LANES):
                        out_ref[t, res, :, c * LANES:(c + 1) * LANES] = (
                            stage_ref[c, pl.ds(res, tm // r, stride=r), :].astype(BF16))


def _rope_tables(s):
    half = ROPE_DIM // 2
    inv_freq = (np.float32(ROPE_THETA) ** (-np.arange(half, dtype=np.float32) * np.float32(2.0) / ROPE_DIM))
    ang = np.arange(s, dtype=np.float32)[:, None] * inv_freq.astype(np.float32)[None, :]
    cos = np.cos(ang).astype(np.float32)
    sin = np.sin(ang).astype(np.float32)
    pad = np.zeros((s, HEAD_DIM - ROPE_DIM), np.float32)
    zero = np.zeros((s, half), np.float32)
    cos_t = np.concatenate([cos, cos, pad + 1.0], axis=1)
    sa_t = np.concatenate([-sin, zero, pad], axis=1)
    sb_t = np.concatenate([zero, sin, pad], axis=1)
    rep = LANES // HEAD_DIM
    return tuple(jnp.asarray(np.tile(t, (1, rep))) for t in (cos_t, sa_t, sb_t))


def _inproj(x, ln_g, ln_b, w_in, b_gate, conv_w, w_conv_out):
    bsz, s, d = x.shape
    tm = ROW_TILE
    nt = s // tm
    hpt = tm // HALO
    cos_t, sa_t, sb_t = _rope_tables(s)
    row = lambda b, i: (b, i, 0)
    tab = pl.BlockSpec((tm, LANES), lambda b, i: (i, 0))
    qkv_shapes = [jax.ShapeDtypeStruct((3, bsz, r, s // r, GROUP_W), BF16) for r in DILATIONS]
    qkv_specs = [pl.BlockSpec((3, None, r, tm // r, GROUP_W), lambda b, i: (0, b, 0, i, 0))
                 for r in DILATIONS]
    return pl.pallas_call(
        _inproj_kernel,
        grid=(bsz, nt),
        in_specs=[
            pl.BlockSpec((None, tm, d), row),
            pl.BlockSpec((None, HALO, d), lambda b, i: (b, jnp.maximum(i * hpt - 1, 0), 0)),
            pl.BlockSpec((None, HALO, d), lambda b, i: (b, jnp.minimum((i + 1) * hpt, s // HALO - 1), 0)),
            _resident((1, d)), _resident((1, d)),
            _resident(w_in.shape), _resident((1, 2 * d)), _resident(conv_w.shape),
            _resident(w_conv_out.shape),
            tab, tab, tab,
        ],
        out_specs=qkv_specs + [pl.BlockSpec((None, tm, d), row)] * 3,
        out_shape=(qkv_shapes + [jax.ShapeDtypeStruct((bsz, s, d), BF16)] * 2
                   + [jax.ShapeDtypeStruct((bsz, s, d), F32)]),
        scratch_shapes=[
            pltpu.VMEM((tm + 2 * HALO, d), BF16),
            pltpu.VMEM((tm + 2 * HALO, D_CONV), F32),
            pltpu.VMEM((GROUP_W // LANES, tm, LANES), F32),
        ],
        compiler_params=pltpu.CompilerParams(
            dimension_semantics=("parallel", "parallel"), vmem_limit_bytes=VMEM_LIMIT),
        name="inproj",
    )(x, x, x, ln_g, ln_b, w_in, b_gate, conv_w, w_conv_out, cos_t, sa_t, sb_t)


def _attn_kernel(q_ref, k_ref, v_ref, o_ref, lse_ref, *, sub_len):
    s = k_ref.shape[0]
    qc = q_ref.shape[0]
    base = pl.program_id(1) * qc
    lane = lax.broadcasted_iota(jnp.int32, (1, GROUP_W), 1)
    head_masks = [(lane >= h * HEAD_DIM) & (lane < (h + 1) * HEAD_DIM) for h in range(HEADS_PER_GROUP)]
    qi = lax.broadcasted_iota(jnp.int32, (Q_BLOCK, 1), 0)
    kj = lax.broadcasted_iota(jnp.int32, (1, K_BLOCK), 1)
    low_half = lax.broadcasted_iota(jnp.int32, (1, LANES), 1) < HEAD_DIM

    def body(j, carry):
        r0 = pl.multiple_of(j * Q_BLOCK, Q_BLOCK)
        s0 = base + r0
        k0 = pl.multiple_of(jnp.clip(s0 - BAND, 0, s - K_BLOCK), BAND)
        q = q_ref[pl.ds(r0, Q_BLOCK), :]
        k = k_ref[pl.ds(k0, K_BLOCK), :]
        v = v_ref[pl.ds(k0, K_BLOCK), :]
        sub_lo = (s0 // sub_len) * sub_len
        qpos = s0 + qi
        kpos = k0 + kj
        valid = (jnp.abs(qpos - kpos) <= BAND) & (kpos >= sub_lo) & (kpos < sub_lo + sub_len)
        qm = jnp.concatenate([jnp.where(hm, q, jnp.zeros_like(q)) for hm in head_masks], axis=0)
        sc = lax.dot_general(qm, k, (((1,), (1,)), ((), ())), preferred_element_type=F32)
        sc = jnp.where(jnp.concatenate([valid] * HEADS_PER_GROUP, axis=0), sc, MASK_VALUE)
        m = jnp.max(sc, axis=-1, keepdims=True)
        p = jnp.exp(sc - m)
        den = jnp.sum(p, axis=-1, keepdims=True)
        o_all = _dot((p * (1.0 / den)).astype(BF16), v)
        lse_all = m + jnp.log(den)
        o_cols, lse_cols = [], []
        for c in range(GROUP_W // LANES):
            lanes = slice(c * LANES, (c + 1) * LANES)
            h_lo, h_hi = HEADS_PER_BLOCK * c, HEADS_PER_BLOCK * c + 1
            rows_lo = slice(h_lo * Q_BLOCK, (h_lo + 1) * Q_BLOCK)
            rows_hi = slice(h_hi * Q_BLOCK, (h_hi + 1) * Q_BLOCK)
            o_cols.append(jnp.where(low_half, o_all[rows_lo, lanes], o_all[rows_hi, lanes]))
            lse_cols.append(jnp.where(low_half, lse_all[rows_lo], lse_all[rows_hi]))
        o_ref[pl.ds(r0, Q_BLOCK), :] = jnp.concatenate(o_cols, axis=1).astype(BF16)
        lse_ref[pl.ds(r0, Q_BLOCK), :] = jnp.concatenate(lse_cols, axis=1)
        return carry

    lax.fori_loop(0, qc // Q_BLOCK, body, 0, unroll=True)


def _attention(qkv, sub_len):
    _, bsz, s, w = qkv.shape
    qc = min(Q_CHUNK, s)
    kv_spec = lambda t: pl.BlockSpec((None, None, s, w), lambda b, j: (t, b, 0, 0))
    return pl.pallas_call(
        functools.partial(_attn_kernel, sub_len=sub_len),
        grid=(bsz, s // qc),
        in_specs=[pl.BlockSpec((None, None, qc, w), lambda b, j: (0, b, j, 0)), kv_spec(1), kv_spec(2)],
        out_specs=[pl.BlockSpec((None, qc, w), lambda b, j: (b, j, 0))] * 2,
        out_shape=[jax.ShapeDtypeStruct((bsz, s, w), BF16), jax.ShapeDtypeStruct((bsz, s, w), F32)],
        compiler_params=pltpu.CompilerParams(
            dimension_semantics=("parallel", "arbitrary"), vmem_limit_bytes=VMEM_LIMIT),
        name="attn",
    )(qkv, qkv, qkv)


def _first_index_of_max(vals, lane_f):
    mx = jnp.max(vals, axis=-1, keepdims=True)
    idx = jnp.min(jnp.where(vals == mx, lane_f, float(LANES)), axis=-1, keepdims=True)
    return mx, idx


def _mix_kernel(h_ref, o0_ref, o1_ref, o2_ref, l0_ref, l1_ref, l2_ref, cg_ref, ga_ref,
                wao_ref, wo_ref, g1_ref, b1_ref, wrh_ref, wrl_ref, br_ref,
                h1_ref, h1p_ref, route_ref, plan_ref, counts_ref, so_ref, sl_ref, tri_ref):
    tm = h_ref.shape[0]
    halves = GROUP_W // LANES
    for gi, (o_ref, l_ref, r) in enumerate(zip((o0_ref, o1_ref, o2_ref), (l0_ref, l1_ref, l2_ref), DILATIONS)):
        for c in range(halves):
            cols = slice(c * LANES, (c + 1) * LANES)
            for res in range(r):
                rows = pl.ds(res, tm // r, stride=r) if r > 1 else slice(None)
                so_ref[gi * halves + c, rows, :] = o_ref[res, :, cols].astype(F32)
                sl_ref[gi * halves + c, rows, :] = l_ref[res, :, cols]

    def natural(ref, gi):
        return jnp.concatenate([ref[gi * halves + c] for c in range(halves)], axis=1)

    lses = [natural(sl_ref, gi) for gi in range(N_GROUPS)]
    mx = jnp.maximum(jnp.maximum(lses[0], lses[1]), lses[2])
    es = [jnp.exp(l - mx) for l in lses]
    inv_den = 1.0 / (es[0] + es[1] + es[2])
    attn = jnp.concatenate(
        [(natural(so_ref, gi) * (es[gi] * inv_den)).astype(BF16) for gi in range(N_GROUPS)], axis=1)
    attn_branch = _dot(attn, wao_ref[...])
    merged = ga_ref[...].astype(F32) * attn_branch + cg_ref[...].astype(F32)
    mix = _dot(merged.astype(BF16), wo_ref[...])
    h1 = _layer_norm(DEEPNORM_ALPHA * h_ref[...] + mix, g1_ref[...], b1_ref[...])
    _store_chunk_major(h1_ref, 0, h1)
    _store_chunk_major(h1p_ref, 0, _pack_bf16_pairs(h1))

    hi = h1.astype(BF16)
    lo = (h1 - hi.astype(F32)).astype(BF16)
    wrh = wrh_ref[...]
    logits = _dot(hi, wrh) + _dot(lo, wrh) + _dot(hi, wrl_ref[...]) + br_ref[...]
    lane = lax.broadcasted_iota(jnp.int32, (1, LANES), 1)
    lane_f = lane.astype(F32)
    is_grp = lane < N_EXPERT_GROUPS
    gl = jnp.where(is_grp, logits, MASK_VALUE)
    ge = jnp.exp(gl - jnp.max(gl, axis=-1, keepdims=True))
    gp = jnp.where(is_grp, ge / jnp.sum(ge, axis=-1, keepdims=True), -1.0)
    grp_w, grp_idx = _first_index_of_max(gp, lane_f)
    e_lo = ROUTE_LANE0 + grp_idx * EXPERTS_PER_GROUP
    sel = jnp.where((lane_f >= e_lo) & (lane_f < e_lo + EXPERTS_PER_GROUP), logits, MASK_VALUE)
    v1, i1 = _first_index_of_max(sel, lane_f)
    v2, i2 = _first_index_of_max(jnp.where(lane_f == i1, MASK_VALUE, sel), lane_f)
    e2 = jnp.exp(v2 - v1)
    w1 = grp_w / (1.0 + e2)
    w2 = grp_w * e2 / (1.0 + e2)

    first = (pl.program_id(0) == 0) & (pl.program_id(1) == 0)

    @pl.when(first)
    def _():
        counts_ref[...] = jnp.zeros_like(counts_ref)
        r = lax.broadcasted_iota(jnp.int32, (tm, tm), 0)
        c = lax.broadcasted_iota(jnp.int32, (tm, tm), 1)
        tri_ref[...] = (c < r).astype(BF16)

    oh0 = lane_f == i1 - ROUTE_LANE0
    oh1 = lane_f == i2 - ROUTE_LANE0
    oh = oh0.astype(F32) + oh1.astype(F32)
    before = counts_ref[...] + _dot(tri_ref[...], oh.astype(BF16))
    r0 = jnp.sum(jnp.where(oh0, before, 0.0), axis=-1, keepdims=True)
    r1 = jnp.sum(jnp.where(oh1, before + oh0.astype(F32), 0.0), axis=-1, keepdims=True)
    counts_ref[...] = counts_ref[...] + jnp.sum(oh, axis=0, keepdims=True)

    per_lane = (i1 - ROUTE_LANE0, i2 - ROUTE_LANE0, w1, w2, r0, r1)
    route = jnp.zeros((tm, LANES), F32)
    for k, v in enumerate(per_lane):
        route = jnp.where(lane == k, v, route)
    route_ref[...] = route
    plan_ref[...] = route.T[:PLAN_ROWS, :]


def _mix(h, o_list, lse_list, cg, ga, w_attn_out, w_o, ln1_g, ln1_b, wr_hi, wr_lo, b_route):
    bsz, s, d = h.shape
    tm = ROW_TILE
    row = lambda b, i: (b, i, 0)
    dil_specs = [pl.BlockSpec((None, r, tm // r, GROUP_W), lambda b, i: (b, 0, i, 0)) for r in DILATIONS]
    o_views = [o.reshape(bsz, r, s // r, GROUP_W) for o, r in zip(o_list, DILATIONS)]
    l_views = [l.reshape(bsz, r, s // r, GROUP_W) for l, r in zip(lse_list, DILATIONS)]
    return pl.pallas_call(
        _mix_kernel,
        grid=(bsz, s // tm),
        in_specs=[pl.BlockSpec((None, tm, d), row)] + dil_specs + dil_specs + [
            pl.BlockSpec((None, tm, d), row), pl.BlockSpec((None, tm, d), row),
            _resident(w_attn_out.shape), _resident(w_o.shape),
            _resident((1, d)), _resident((1, d)),
            _resident(wr_hi.shape), _resident(wr_lo.shape), _resident((1, LANES)),
        ],
        out_specs=[_chunk_major_spec(F32_CHUNKS, tm, lambda b, i: (0, b * (s // tm) + i, 0)),
                   _chunk_major_spec(PACKED_CHUNKS, tm, lambda b, i: (0, b * (s // tm) + i, 0)),
                   pl.BlockSpec((None, tm, LANES), row),
                   pl.BlockSpec((PLAN_ROWS, tm), lambda b, i: (0, b * (s // tm) + i)),
                   pl.BlockSpec((1, LANES), lambda b, i: (0, 0))],
        out_shape=[jax.ShapeDtypeStruct((F32_CHUNKS, bsz * s, SC_COLS), F32),
                   jax.ShapeDtypeStruct((PACKED_CHUNKS, bsz * s, SC_COLS), U32),
                   jax.ShapeDtypeStruct((bsz, s, LANES), F32),
                   jax.ShapeDtypeStruct((PLAN_ROWS, bsz * s), F32),
                   jax.ShapeDtypeStruct((1, LANES), F32)],
        scratch_shapes=[pltpu.VMEM((N_GROUPS * GROUP_W // LANES, tm, LANES), F32)] * 2
        + [pltpu.VMEM((tm, tm), BF16)],
        compiler_params=pltpu.CompilerParams(
            dimension_semantics=("arbitrary", "arbitrary"), vmem_limit_bytes=VMEM_LIMIT),
        name="mix",
    )(h, *o_views, *l_views, cg, ga, w_attn_out, w_o, ln1_g, ln1_b, wr_hi, wr_lo, b_route)


def _dispatch_plan(plan, counts, t):
    eid = plan[:TOP_K_INNER].astype(jnp.int32)
    rank = plan[ROUTE_LANE_RANK:ROUTE_LANE_RANK + TOP_K_INNER].astype(jnp.int32)
    counts = counts[0, :N_EXPERTS].astype(jnp.int32)
    padded = (counts + EXPERT_BLOCK - 1) // EXPERT_BLOCK * EXPERT_BLOCK
    pad_end = jnp.cumsum(padded)
    pad_start = pad_end - padded
    experts = jnp.arange(N_EXPERTS, dtype=jnp.int32)[:, None, None]
    start_of = jnp.sum(jnp.where(eid[None] == experts, pad_start[:, None, None], 0), axis=0)
    dest = start_of + rank
    n_blocks = -(-t * TOP_K_INNER // EXPERT_BLOCK) + N_EXPERTS
    n_slots = n_blocks * EXPERT_BLOCK
    block_start = jnp.arange(n_blocks, dtype=jnp.int32) * EXPERT_BLOCK
    block_e = jnp.minimum(jnp.sum((block_start[:, None] >= pad_end[None, :]).astype(jnp.int32), axis=1),
                          N_EXPERTS - 1)
    n_empty = n_slots - t * TOP_K_INNER
    gap_end = jnp.cumsum(padded - counts)
    gap_start = gap_end - (padded - counts)
    k = jnp.arange(n_empty, dtype=jnp.int32)
    in_gap = (k[None, :] >= gap_start[:, None]) & (k[None, :] < gap_end[:, None])
    first_of_gap = (pad_start + counts - gap_start)[:, None]
    empty = jnp.where(k < gap_end[-1],
                      jnp.sum(jnp.where(in_gap, first_of_gap + k[None, :], 0), axis=0),
                      pad_end[-1] + k - gap_end[-1])
    return dest.astype(jnp.int32), empty.astype(jnp.int32), block_e


def _sc_mesh():
    return plsc.VectorSubcoreMesh(core_axis_name="core", subcore_axis_name="subcore")


def _sc_rows_pipeline(body, n_rows, row_index_map, cols):
    return pltpu.emit_pipeline(
        body,
        grid=(n_rows // SC_WINDOW,),
        in_specs=[pl.BlockSpec((SC_WINDOW, cols), index_map=row_index_map),
                  pl.BlockSpec((1, SC_WINDOW), index_map=lambda i: (0, i))],
        out_specs=[],
        core_axis_name=("core", "subcore"),
        dimension_semantics=(pltpu.PARALLEL,),
    )


def _chunk_rows(rows, chunks, n_rows_per_chunk):
    base = jnp.arange(chunks, dtype=jnp.int32)[:, None] * n_rows_per_chunk
    return (base + rows[None, :]).reshape(1, -1)


def _dispatch(rows, dest, empty, n_slots):
    chunks, t, _ = rows.shape
    x = rows.reshape(chunks * t, SC_COLS)
    idx_first = _chunk_rows(dest[0], chunks, n_slots)
    idx_second = _chunk_rows(dest[1], chunks, n_slots)
    idx_empty = _chunk_rows(empty, chunks, n_slots)
    zeros = jnp.zeros((SC_WINDOW, SC_COLS), rows.dtype)

    @pl.kernel(out_type=jax.ShapeDtypeStruct((chunks * n_slots, SC_COLS), rows.dtype), mesh=_sc_mesh(),
               scratch_types=[])
    def scatter(x_hbm, i0_hbm, i1_hbm, z_hbm, ie_hbm, o_hbm):
        def body(x_vmem, i_vmem):
            pltpu.sync_copy(x_vmem, o_hbm.at[i_vmem.at[0]])

        for i_hbm in (i0_hbm, i1_hbm):
            _sc_rows_pipeline(body, x.shape[0], lambda i: (i, 0), SC_COLS)(x_hbm, i_hbm)
        _sc_rows_pipeline(body, idx_empty.shape[1], lambda i: (0, 0), SC_COLS)(z_hbm, ie_hbm)

    return scatter(x, idx_first, idx_second, zeros, idx_empty).reshape(chunks, n_slots, SC_COLS)


def _cast_kernel(*refs):
    n = len(refs) // 2
    for src, dst in zip(refs[:n], refs[n:]):
        dst[...] = src[...].astype(dst.dtype)


def _expert_weights_bf16(w_gate, w_up, w_down):
    ws = (w_gate, w_up, w_down)
    specs = [pl.BlockSpec((None,) + w.shape[1:], lambda e: (e, 0, 0)) for w in ws]
    return pl.pallas_call(
        _cast_kernel,
        grid=(N_EXPERTS,),
        in_specs=specs,
        out_specs=specs,
        out_shape=[jax.ShapeDtypeStruct(w.shape, BF16) for w in ws],
        compiler_params=pltpu.CompilerParams(dimension_semantics=("parallel",), vmem_limit_bytes=VMEM_LIMIT),
        name="cast_weights",
    )(*ws)


def _expert_mlp(xb, wg_ref, wu_ref, wd_ref):
    acc = None
    for c in range(D_EXPERT // EXPERT_CHUNK):
        cols = slice(c * EXPERT_CHUNK, (c + 1) * EXPERT_CHUNK)
        gate = _dot(xb, wg_ref[:, cols].astype(BF16))
        up = _dot(xb, wu_ref[:, cols].astype(BF16))
        hidden = (gate * jax.nn.sigmoid(gate) * up).astype(BF16)
        part = _dot(hidden, wd_ref[cols, :].astype(BF16))
        acc = part if acc is None else acc + part
    return acc


def _expert_kernel(be_ref, x_ref, wg_ref, wu_ref, wd_ref, y_ref):
    xb = _unpack_bf16_pairs(_load_chunk_major(x_ref, 0, EXPERT_BLOCK)).astype(BF16)
    _store_chunk_major(y_ref, 0, _pack_bf16_pairs(_expert_mlp(xb, wg_ref, wu_ref, wd_ref)))


def _experts(xs, block_e, w_gate, w_up, w_down):
    n_blocks = block_e.shape[0]
    d = D_MODEL
    blk = _chunk_major_spec(PACKED_CHUNKS, EXPERT_BLOCK, lambda i, be: (0, i, 0))
    grid_spec = pltpu.PrefetchScalarGridSpec(
        num_scalar_prefetch=1,
        grid=(n_blocks,),
        in_specs=[
            blk,
            pl.BlockSpec((None, d, D_EXPERT), lambda i, be: (be[i], 0, 0)),
            pl.BlockSpec((None, d, D_EXPERT), lambda i, be: (be[i], 0, 0)),
            pl.BlockSpec((None, D_EXPERT, d), lambda i, be: (be[i], 0, 0)),
        ],
        out_specs=blk,
    )
    return pl.pallas_call(
        _expert_kernel,
        grid_spec=grid_spec,
        out_shape=jax.ShapeDtypeStruct(xs.shape, xs.dtype),
        compiler_params=pltpu.CompilerParams(
            dimension_semantics=("arbitrary",), vmem_limit_bytes=VMEM_LIMIT),
        name="experts",
    )(block_e, xs, w_gate, w_up, w_down)


def _gather(yb, dest):
    chunks, n_slots, _ = yb.shape
    x = yb.reshape(chunks * n_slots, SC_COLS)
    idx = _chunk_rows(dest.reshape(-1), chunks, n_slots)
    m = idx.shape[1]

    @pl.kernel(out_type=jax.ShapeDtypeStruct((m, SC_COLS), yb.dtype), mesh=_sc_mesh(), scratch_types=[])
    def gather(x_hbm, i_hbm, o_hbm):
        def body(i_vmem, o_vmem):
            pltpu.sync_copy(x_hbm.at[i_vmem.at[0]], o_vmem)

        pltpu.emit_pipeline(
            body,
            grid=(m // SC_WINDOW,),
            in_specs=[pl.BlockSpec((1, SC_WINDOW), index_map=lambda i: (0, i))],
            out_specs=[pl.BlockSpec((SC_WINDOW, SC_COLS), index_map=lambda i: (i, 0))],
            core_axis_name=("core", "subcore"),
            dimension_semantics=(pltpu.PARALLEL,),
        )(i_hbm, o_hbm)

    return gather(x, idx).reshape(chunks, m // chunks, SC_COLS)


def _combine_kernel(first_ref, second_ref, h1_ref, route_ref, g_ref, b_ref, y_ref):
    tm = y_ref.shape[0]
    route = route_ref[...]
    w_first = route[:, ROUTE_LANE_W:ROUTE_LANE_W + 1]
    w_second = route[:, ROUTE_LANE_W + 1:ROUTE_LANE_W + 2]
    ffn = (w_first * _unpack_bf16_pairs(_load_chunk_major(first_ref, 0, tm))
           + w_second * _unpack_bf16_pairs(_load_chunk_major(second_ref, 0, tm)))
    y_ref[...] = _layer_norm(DEEPNORM_ALPHA * _load_chunk_major(h1_ref, 0, tm) + ffn, g_ref[...], b_ref[...])


def _combine(h1, route, g, ln2_g, ln2_b):
    t = route.shape[0]
    d = D_MODEL
    tm = ROW_TILE
    nt = t // tm
    return pl.pallas_call(
        _combine_kernel,
        grid=(nt,),
        in_specs=[
            _chunk_major_spec(PACKED_CHUNKS, tm, lambda i: (0, i, 0)),
            _chunk_major_spec(PACKED_CHUNKS, tm, lambda i: (0, nt + i, 0)),
            _chunk_major_spec(F32_CHUNKS, tm, lambda i: (0, i, 0)),
            pl.BlockSpec((tm, LANES), lambda i: (i, 0)),
            _resident((1, d)), _resident((1, d)),
        ],
        out_specs=pl.BlockSpec((tm, d), lambda i: (i, 0)),
        out_shape=jax.ShapeDtypeStruct((t, d), F32),
        compiler_params=pltpu.CompilerParams(
            dimension_semantics=("parallel",), vmem_limit_bytes=VMEM_LIMIT),
        name="combine",
    )(g, g, h1, route, ln2_g, ln2_b)


def _token_mixing(x, p):
    bsz, s, d = x.shape
    qkv0, qkv1, qkv2, cg, ga, h = _inproj(x, p['ln_in_g'], p['ln_in_b'], p['w_in'], p['b_gate'], p['conv_w'],
                                          p['w_conv_out'])
    o_list, lse_list = [], []
    for qkv, r in zip((qkv0, qkv1, qkv2), DILATIONS):
        o, lse = _attention(qkv.reshape(3, bsz, s, GROUP_W), s // r)
        o_list.append(o)
        lse_list.append(lse)
    h1, h1_packed, route, plan, counts = _mix(h, o_list, lse_list, cg, ga, p['w_attn_out'], p['w_o'],
                             p['ln1_g'], p['ln1_b'], p['wr_hi'], p['wr_lo'], p['b_route'])
    t = bsz * s
    route = route.reshape(t, LANES)
    dest, empty, block_e = _dispatch_plan(plan, counts, t)
    return h1, route, dest, empty, block_e, h1_packed


def kernel(x_prompt, x_sample, ln_in_g, ln_in_b, w_in, b_gate, conv_w, w_attn_out, w_conv_out, w_o, ln1_g, ln1_b, w_route_group, b_route_group, w_route_expert, b_route_expert, w_gate, w_up, w_down, ln2_g, ln2_b):
    d = D_MODEL
    w_route = jnp.concatenate([w_route_group[0], w_route_expert[0]], axis=1)
    w_route = jnp.pad(w_route, ((0, 0), (0, LANES - w_route.shape[1])))
    wr_hi = w_route.astype(BF16)
    b_route = jnp.concatenate([b_route_group[0], b_route_expert[0]]).astype(F32)
    p = {
        'ln_in_g': ln_in_g.reshape(1, d), 'ln_in_b': ln_in_b.reshape(1, d),
        'w_in': w_in[0].astype(BF16), 'b_gate': b_gate[0].reshape(1, 2 * d), 'conv_w': conv_w[0],
        'w_conv_out': w_conv_out[0].astype(BF16), 'w_attn_out': w_attn_out[0].astype(BF16),
        'w_o': w_o[0].astype(BF16),
        'ln1_g': ln1_g[0].reshape(1, d), 'ln1_b': ln1_b[0].reshape(1, d),
        'wr_hi': wr_hi, 'wr_lo': (w_route - wr_hi.astype(F32)).astype(BF16),
        'b_route': jnp.pad(b_route, (0, LANES - b_route.shape[0])).reshape(1, LANES),
        'ln2_g': ln2_g[0].reshape(1, d), 'ln2_b': ln2_b[0].reshape(1, d),
    }
    p['w_gate'], p['w_up'], p['w_down'] = w_gate[0], w_up[0], w_down[0]
    batches = [_token_mixing(x, p) for x in (x_prompt, x_sample)]
    sorted_rows = [_dispatch(h1_packed, dest, empty, block_e.shape[0] * EXPERT_BLOCK)
                   for _, _, dest, empty, block_e, h1_packed in batches]
    gathered = [_gather(_experts(xs, b[4], p['w_gate'], p['w_up'], p['w_down']), b[2])
                for xs, b in zip(sorted_rows, batches)]
    outs = [_combine(b[0], b[1], g, p['ln2_g'], p['ln2_b']).reshape(x.shape)
            for b, g, x in zip(batches, gathered, (x_prompt, x_sample))]
    return tuple(outs)
```

```python
import functools

import jax
import jax.numpy as jnp
import numpy as np
from jax import lax
from jax.experimental import pallas as pl
from jax.experimental.pallas import tpu as pltpu
from jax.experimental.pallas import tpu_sc as plsc

F32 = jnp.float32
BF16 = jnp.bfloat16

D_MODEL = 1024
HEAD_DIM = 64
HEADS_PER_GROUP = 4
GROUP_W = HEADS_PER_GROUP * HEAD_DIM
DILATIONS = (1, 4, 16)
BAND = 64
N_GROUPS = len(DILATIONS)
D_ATTN = N_GROUPS * GROUP_W
ATTN_SCALE = HEAD_DIM ** -0.5
ROPE_DIM = HEAD_DIM // 4
ROPE_THETA = 500000.0
MASK_VALUE = -1e30
D_CONV = D_MODEL
COL_CONV_B = 3 * D_ATTN
COL_CONV_C = COL_CONV_B + D_CONV
COL_CONV_H = COL_CONV_C + D_CONV
COL_GATE_ATTN = COL_CONV_H + D_CONV
COL_GATE_CONV = COL_GATE_ATTN + D_MODEL
N_EXPERT_GROUPS = 4
EXPERTS_PER_GROUP = 8
N_EXPERTS = N_EXPERT_GROUPS * EXPERTS_PER_GROUP
TOP_K_INNER = 2
D_EXPERT = 512
EXPERT_BLOCK = 512
EXPERT_CHUNK = 256
LN_EPS = 1e-5
DEPTH = 1
DEEPNORM_ALPHA = (2 * DEPTH) ** 0.25

LANES = 128
HEADS_PER_BLOCK = LANES // HEAD_DIM
ROW_TILE = 512
HALO = 16
Q_BLOCK = 128
K_BLOCK = Q_BLOCK + 2 * BAND
Q_CHUNK = 2048
SC_COLS = 256
SC_WINDOW = 128
F32_CHUNKS = D_MODEL // SC_COLS
PACKED_COLS = D_MODEL // 2
PACKED_CHUNKS = PACKED_COLS // SC_COLS
U32 = jnp.uint32
HIGH_HALF = np.uint32(0xFFFF0000)
ROUTE_LANE0 = N_EXPERT_GROUPS
ROUTE_LANE_W = TOP_K_INNER
ROUTE_LANE_RANK = 2 * TOP_K_INNER
PLAN_ROWS = 8
VMEM_LIMIT = 56 * 1024 * 1024


def _layer_norm(v, g, b):
    mu = jnp.mean(v, axis=-1, keepdims=True)
    d = v - mu
    var = jnp.mean(d * d, axis=-1, keepdims=True)
    return d * lax.rsqrt(var + LN_EPS) * g + b


def _dot(a, b):
    return jnp.dot(a, b, preferred_element_type=F32)


def _store_chunk_major(ref, first_row, val):
    for c in range(ref.shape[0]):
        ref[c, pl.ds(first_row, val.shape[0]), :] = val[:, c * SC_COLS:(c + 1) * SC_COLS]


def _load_chunk_major(ref, first_row, n):
    return jnp.concatenate([ref[c, pl.ds(first_row, n), :] for c in range(ref.shape[0])], axis=1)


def _chunk_major_spec(chunks, rows, index_map):
    return pl.BlockSpec((chunks, rows, SC_COLS), index_map)


def _pack_bf16_pairs(val):
    bits = lax.bitcast_convert_type(val.astype(BF16).astype(F32), U32)
    return (bits[:, PACKED_COLS:] & HIGH_HALF) | (bits[:, :PACKED_COLS] >> 16)


def _unpack_bf16_pairs(packed):
    low = lax.bitcast_convert_type(packed << 16, F32)
    high = lax.bitcast_convert_type(packed & HIGH_HALF, F32)
    return jnp.concatenate([low, high], axis=1)


def _resident(shape):
    nd = len(shape)
    return pl.BlockSpec(shape, lambda *_: (0,) * nd, pipeline_mode=pl.Buffered(1))


def _inproj_kernel(x_ref, xp_ref, xn_ref, g_ref, b_ref, win_ref, bg_ref, cw_ref, wco_ref,
                   cos_ref, sa_ref, sb_ref,
                   qkv0_ref, qkv1_ref, qkv2_ref, cg_ref, ga_ref, h_ref,
                   hext_ref, u_ref, stage_ref):
    i = pl.program_id(1)
    nt = pl.num_programs(1)
    tm = x_ref.shape[0]
    g = g_ref[...]
    b = b_ref[...]
    h = _layer_norm(x_ref[...], g, b)
    h_ref[...] = h
    hext_ref[pl.ds(HALO, tm), :] = h.astype(BF16)
    hext_ref[pl.ds(0, HALO), :] = _layer_norm(xp_ref[...], g, b).astype(BF16)
    hext_ref[pl.ds(HALO + tm, HALO), :] = _layer_norm(xn_ref[...], g, b).astype(BF16)
    hm = hext_ref[pl.ds(HALO, tm), :]
    he = hext_ref[...]

    def proj(lhs, col, n):
        return _dot(lhs, win_ref[:, col:col + n])

    u = proj(he, COL_CONV_C, D_CONV) * proj(he, COL_CONV_H, D_CONV)
    rows = lax.broadcasted_iota(jnp.int32, (tm + 2 * HALO, 1), 0)
    lo = jnp.where(i == 0, HALO, 0)
    hi = jnp.where(i == nt - 1, HALO + tm, tm + 2 * HALO)
    u_ref[...] = jnp.where((rows >= lo) & (rows < hi), u, 0.0)
    cw = cw_ref[...]
    conv = (cw[0:1] * u_ref[pl.ds(HALO - 1, tm), :] + cw[1:2] * u_ref[pl.ds(HALO, tm), :]
            + cw[2:3] * u_ref[pl.ds(HALO + 1, tm), :])
    cb = proj(hm, COL_CONV_B, D_CONV)
    conv_branch = _dot((cb * conv).astype(BF16), wco_ref[...])
    gate_conv = jax.nn.sigmoid(proj(hm, COL_GATE_CONV, D_MODEL) + bg_ref[:, D_MODEL:])
    cg_ref[...] = (gate_conv * conv_branch).astype(BF16)
    ga_ref[...] = jax.nn.sigmoid(proj(hm, COL_GATE_ATTN, D_MODEL) + bg_ref[:, :D_MODEL]).astype(BF16)

    cosv = cos_ref[...]
    sav = sa_ref[...]
    sbv = sb_ref[...]
    half = ROPE_DIM // 2
    for gi, (out_ref, r) in enumerate(zip((qkv0_ref, qkv1_ref, qkv2_ref), DILATIONS)):
        for t in range(3):
            p = proj(hm, t * D_ATTN + gi * GROUP_W, GROUP_W)
            if t < 2:
                parts = []
                for c in range(GROUP_W // LANES):
                    pc = p[:, c * LANES:(c + 1) * LANES]
                    parts.append(pc * cosv + pltpu.roll(pc, LANES - half, 1) * sav
                                 + pltpu.roll(pc, half, 1) * sbv)
                p = jnp.concatenate(parts, axis=1)
                if t == 0:
                    p = p * ATTN_SCALE
            if r == 1:
                out_ref[t, 0] = p.astype(BF16)
            else:
                for c in range(GROUP_W // LANES):
                    stage_ref[c] = p[:, c * LANES:(c + 1) * LANES]
                for res in range(r):
                    for c in range(GROUP_W // LANES):
                        out_ref[t, res, :, c * LANES:(c + 1) * LANES] = (
                            stage_ref[c, pl.ds(res, tm // r, stride=r), :].astype(BF16))


def _rope_tables(s):
    half = ROPE_DIM // 2
    inv_freq = (np.float32(ROPE_THETA) ** (-np.arange(half, dtype=np.float32) * np.float32(2.0) / ROPE_DIM))
    ang = np.arange(s, dtype=np.float32)[:, None] * inv_freq.astype(np.float32)[None, :]
    cos = np.cos(ang).astype(np.float32)
    sin = np.sin(ang).astype(np.float32)
    pad = np.zeros((s, HEAD_DIM - ROPE_DIM), np.float32)
    zero = np.zeros((s, half), np.float32)
    cos_t = np.concatenate([cos, cos, pad + 1.0], axis=1)
    sa_t = np.concatenate([-sin, zero, pad], axis=1)
    sb_t = np.concatenate([zero, sin, pad], axis=1)
    rep = LANES // HEAD_DIM
    return tuple(jnp.asarray(np.tile(t, (1, rep))) for t in (cos_t, sa_t, sb_t))


def _inproj(x, ln_g, ln_b, w_in, b_gate, conv_w, w_conv_out):
    bsz, s, d = x.shape
    tm = ROW_TILE
    nt = s // tm
    hpt = tm // HALO
    cos_t, sa_t, sb_t = _rope_tables(s)
    row = lambda b, i: (b, i, 0)
    tab = pl.BlockSpec((tm, LANES), lambda b, i: (i, 0))
    qkv_shapes = [jax.ShapeDtypeStruct((3, bsz, r, s // r, GROUP_W), BF16) for r in DILATIONS]
    qkv_specs = [pl.BlockSpec((3, None, r, tm // r, GROUP_W), lambda b, i: (0, b, 0, i, 0))
                 for r in DILATIONS]
    return pl.pallas_call(
        _inproj_kernel,
        grid=(bsz, nt),
        in_specs=[
            pl.BlockSpec((None, tm, d), row),
            pl.BlockSpec((None, HALO, d), lambda b, i: (b, jnp.maximum(i * hpt - 1, 0), 0)),
            pl.BlockSpec((None, HALO, d), lambda b, i: (b, jnp.minimum((i + 1) * hpt, s // HALO - 1), 0)),
            _resident((1, d)), _resident((1, d)),
            _resident(w_in.shape), _resident((1, 2 * d)), _resident(conv_w.shape),
            _resident(w_conv_out.shape),
            tab, tab, tab,
        ],
        out_specs=qkv_specs + [pl.BlockSpec((None, tm, d), row)] * 3,
        out_shape=(qkv_shapes + [jax.ShapeDtypeStruct((bsz, s, d), BF16)] * 2
                   + [jax.ShapeDtypeStruct((bsz, s, d), F32)]),
        scratch_shapes=[
            pltpu.VMEM((tm + 2 * HALO, d), BF16),
            pltpu.VMEM((tm + 2 * HALO, D_CONV), F32),
            pltpu.VMEM((GROUP_W // LANES, tm, LANES), F32),
        ],
        compiler_params=pltpu.CompilerParams(
            dimension_semantics=("parallel", "parallel"), vmem_limit_bytes=VMEM_LIMIT),
        name="inproj",
    )(x, x, x, ln_g, ln_b, w_in, b_gate, conv_w, w_conv_out, cos_t, sa_t, sb_t)


def _attn_kernel(q_ref, k_ref, v_ref, o_ref, lse_ref, *, sub_len):
    s = k_ref.shape[0]
    qc = q_ref.shape[0]
    base = pl.program_id(1) * qc
    lane = lax.broadcasted_iota(jnp.int32, (1, GROUP_W), 1)
    head_masks = [(lane >= h * HEAD_DIM) & (lane < (h + 1) * HEAD_DIM) for h in range(HEADS_PER_GROUP)]
    qi = lax.broadcasted_iota(jnp.int32, (Q_BLOCK, 1), 0)
    kj = lax.broadcasted_iota(jnp.int32, (1, K_BLOCK), 1)
    low_half = lax.broadcasted_iota(jnp.int32, (1, LANES), 1) < HEAD_DIM

    def body(j, carry):
        r0 = pl.multiple_of(j * Q_BLOCK, Q_BLOCK)
        s0 = base + r0
        k0 = pl.multiple_of(jnp.clip(s0 - BAND, 0, s - K_BLOCK), BAND)
        q = q_ref[pl.ds(r0, Q_BLOCK), :]
        k = k_ref[pl.ds(k0, K_BLOCK), :]
        v = v_ref[pl.ds(k0, K_BLOCK), :]
        sub_lo = (s0 // sub_len) * sub_len
        qpos = s0 + qi
        kpos = k0 + kj
        valid = (jnp.abs(qpos - kpos) <= BAND) & (kpos >= sub_lo) & (kpos < sub_lo + sub_len)
        qm = jnp.concatenate([jnp.where(hm, q, jnp.zeros_like(q)) for hm in head_masks], axis=0)
        sc = lax.dot_general(qm, k, (((1,), (1,)), ((), ())), preferred_element_type=F32)
        sc = jnp.where(jnp.concatenate([valid] * HEADS_PER_GROUP, axis=0), sc, MASK_VALUE)
        m = jnp.max(sc, axis=-1, keepdims=True)
        p = jnp.exp(sc - m)
        den = jnp.sum(p, axis=-1, keepdims=True)
        o_all = _dot((p * (1.0 / den)).astype(BF16), v)
        lse_all = m + jnp.log(den)
        o_cols, lse_cols = [], []
        for c in range(GROUP_W // LANES):
            lanes = slice(c * LANES, (c + 1) * LANES)
            h_lo, h_hi = HEADS_PER_BLOCK * c, HEADS_PER_BLOCK * c + 1
            rows_lo = slice(h_lo * Q_BLOCK, (h_lo + 1) * Q_BLOCK)
            rows_hi = slice(h_hi * Q_BLOCK, (h_hi + 1) * Q_BLOCK)
            o_cols.append(jnp.where(low_half, o_all[rows_lo, lanes], o_all[rows_hi, lanes]))
            lse_cols.append(jnp.where(low_half, lse_all[rows_lo], lse_all[rows_hi]))
        o_ref[pl.ds(r0, Q_BLOCK), :] = jnp.concatenate(o_cols, axis=1).astype(BF16)
        lse_ref[pl.ds(r0, Q_BLOCK), :] = jnp.concatenate(lse_cols, axis=1)
        return carry

    lax.fori_loop(0, qc // Q_BLOCK, body, 0, unroll=True)


def _attention(qkv, sub_len):
    _, bsz, s, w = qkv.shape
    qc = min(Q_CHUNK, s)
    kv_spec = lambda t: pl.BlockSpec((None, None, s, w), lambda b, j: (t, b, 0, 0))
    return pl.pallas_call(
        functools.partial(_attn_kernel, sub_len=sub_len),
        grid=(bsz, s // qc),
        in_specs=[pl.BlockSpec((None, None, qc, w), lambda b, j: (0, b, j, 0)), kv_spec(1), kv_spec(2)],
        out_specs=[pl.BlockSpec((None, qc, w), lambda b, j: (b, j, 0))] * 2,
        out_shape=[jax.ShapeDtypeStruct((bsz, s, w), BF16), jax.ShapeDtypeStruct((bsz, s, w), F32)],
        compiler_params=pltpu.CompilerParams(
            dimension_semantics=("parallel", "arbitrary"), vmem_limit_bytes=VMEM_LIMIT),
        name="attn",
    )(qkv, qkv, qkv)


def _first_index_of_max(vals, lane_f):
    mx = jnp.max(vals, axis=-1, keepdims=True)
    idx = jnp.min(jnp.where(vals == mx, lane_f, float(LANES)), axis=-1, keepdims=True)
    return mx, idx


def _mix_kernel(h_ref, o0_ref, o1_ref, o2_ref, l0_ref, l1_ref, l2_ref, cg_ref, ga_ref,
                wao_ref, wo_ref, g1_ref, b1_ref, wrh_ref, wrl_ref, br_ref,
                h1_ref, h1p_ref, route_ref, plan_ref, counts_ref, so_ref, sl_ref, tri_ref):
    tm = h_ref.shape[0]
    halves = GROUP_W // LANES
    for gi, (o_ref, l_ref, r) in enumerate(zip((o0_ref, o1_ref, o2_ref), (l0_ref, l1_ref, l2_ref), DILATIONS)):
        for c in range(halves):
            cols = slice(c * LANES, (c + 1) * LANES)
            for res in range(r):
                rows = pl.ds(res, tm // r, stride=r) if r > 1 else slice(None)
                so_ref[gi * halves + c, rows, :] = o_ref[res, :, cols].astype(F32)
                sl_ref[gi * halves + c, rows, :] = l_ref[res, :, cols]

    def natural(ref, gi):
        return jnp.concatenate([ref[gi * halves + c] for c in range(halves)], axis=1)

    lses = [natural(sl_ref, gi) for gi in range(N_GROUPS)]
    mx = jnp.maximum(jnp.maximum(lses[0], lses[1]), lses[2])
    es = [jnp.exp(l - mx) for l in lses]
    inv_den = 1.0 / (es[0] + es[1] + es[2])
    attn = jnp.concatenate(
        [(natural(so_ref, gi) * (es[gi] * inv_den)).astype(BF16) for gi in range(N_GROUPS)], axis=1)
    attn_branch = _dot(attn, wao_ref[...])
    merged = ga_ref[...].astype(F32) * attn_branch + cg_ref[...].astype(F32)
    mix = _dot(merged.astype(BF16), wo_ref[...])
    h1 = _layer_norm(DEEPNORM_ALPHA * h_ref[...] + mix, g1_ref[...], b1_ref[...])
    _store_chunk_major(h1_ref, 0, h1)
    _store_chunk_major(h1p_ref, 0, _pack_bf16_pairs(h1))

    hi = h1.astype(BF16)
    lo = (h1 - hi.astype(F32)).astype(BF16)
    wrh = wrh_ref[...]
    logits = _dot(hi, wrh) + _dot(lo, wrh) + _dot(hi, wrl_ref[...]) + br_ref[...]
    lane = lax.broadcasted_iota(jnp.int32, (1, LANES), 1)
    lane_f = lane.astype(F32)
    is_grp = lane < N_EXPERT_GROUPS
    gl = jnp.where(is_grp, logits, MASK_VALUE)
    ge = jnp.exp(gl - jnp.max(gl, axis=-1, keepdims=True))
    gp = jnp.where(is_grp, ge / jnp.sum(ge, axis=-1, keepdims=True), -1.0)
    grp_w, grp_idx = _first_index_of_max(gp, lane_f)
    e_lo = ROUTE_LANE0 + grp_idx * EXPERTS_PER_GROUP
    sel = jnp.where((lane_f >= e_lo) & (lane_f < e_lo + EXPERTS_PER_GROUP), logits, MASK_VALUE)
    v1, i1 = _first_index_of_max(sel, lane_f)
    v2, i2 = _first_index_of_max(jnp.where(lane_f == i1, MASK_VALUE, sel), lane_f)
    e2 = jnp.exp(v2 - v1)
    w1 = grp_w / (1.0 + e2)
    w2 = grp_w * e2 / (1.0 + e2)

    first = (pl.program_id(0) == 0) & (pl.program_id(1) == 0)

    @pl.when(first)
    def _():
        counts_ref[...] = jnp.zeros_like(counts_ref)
        r = lax.broadcasted_iota(jnp.int32, (tm, tm), 0)
        c = lax.broadcasted_iota(jnp.int32, (tm, tm), 1)
        tri_ref[...] = (c < r).astype(BF16)

    oh0 = lane_f == i1 - ROUTE_LANE0
    oh1 = lane_f == i2 - ROUTE_LANE0
    oh = oh0.astype(F32) + oh1.astype(F32)
    before = counts_ref[...] + _dot(tri_ref[...], oh.astype(BF16))
    r0 = jnp.sum(jnp.where(oh0, before, 0.0), axis=-1, keepdims=True)
    r1 = jnp.sum(jnp.where(oh1, before + oh0.astype(F32), 0.0), axis=-1, keepdims=True)
    counts_ref[...] = counts_ref[...] + jnp.sum(oh, axis=0, keepdims=True)

    per_lane = (i1 - ROUTE_LANE0, i2 - ROUTE_LANE0, w1, w2, r0, r1)
    route = jnp.zeros((tm, LANES), F32)
    for k, v in enumerate(per_lane):
        route = jnp.where(lane == k, v, route)
    route_ref[...] = route
    plan_ref[...] = route.T[:PLAN_ROWS, :]


def _mix(h, o_list, lse_list, cg, ga, w_attn_out, w_o, ln1_g, ln1_b, wr_hi, wr_lo, b_route):
    bsz, s, d = h.shape
    tm = ROW_TILE
    row = lambda b, i: (b, i, 0)
    dil_specs = [pl.BlockSpec((None, r, tm // r, GROUP_W), lambda b, i: (b, 0, i, 0)) for r in DILATIONS]
    o_views = [o.reshape(bsz, r, s // r, GROUP_W) for o, r in zip(o_list, DILATIONS)]
    l_views = [l.reshape(bsz, r, s // r, GROUP_W) for l, r in zip(lse_list, DILATIONS)]
    return pl.pallas_call(
        _mix_kernel,
        grid=(bsz, s // tm),
        in_specs=[pl.BlockSpec((None, tm, d), row)] + dil_specs + dil_specs + [
            pl.BlockSpec((None, tm, d), row), pl.BlockSpec((None, tm, d), row),
            _resident(w_attn_out.shape), _resident(w_o.shape),
            _resident((1, d)), _resident((1, d)),
            _resident(wr_hi.shape), _resident(wr_lo.shape), _resident((1, LANES)),
        ],
        out_specs=[_chunk_major_spec(F32_CHUNKS, tm, lambda b, i: (0, b * (s // tm) + i, 0)),
                   _chunk_major_spec(PACKED_CHUNKS, tm, lambda b, i: (0, b * (s // tm) + i, 0)),
                   pl.BlockSpec((None, tm, LANES), row),
                   pl.BlockSpec((PLAN_ROWS, tm), lambda b, i: (0, b * (s // tm) + i)),
                   pl.BlockSpec((1, LANES), lambda b, i: (0, 0))],
        out_shape=[jax.ShapeDtypeStruct((F32_CHUNKS, bsz * s, SC_COLS), F32),
                   jax.ShapeDtypeStruct((PACKED_CHUNKS, bsz * s, SC_COLS), U32),
                   jax.ShapeDtypeStruct((bsz, s, LANES), F32),
                   jax.ShapeDtypeStruct((PLAN_ROWS, bsz * s), F32),
                   jax.ShapeDtypeStruct((1, LANES), F32)],
        scratch_shapes=[pltpu.VMEM((N_GROUPS * GROUP_W // LANES, tm, LANES), F32)] * 2
        + [pltpu.VMEM((tm, tm), BF16)],
        compiler_params=pltpu.CompilerParams(
            dimension_semantics=("arbitrary", "arbitrary"), vmem_limit_bytes=VMEM_LIMIT),
        name="mix",
    )(h, *o_views, *l_views, cg, ga, w_attn_out, w_o, ln1_g, ln1_b, wr_hi, wr_lo, b_route)


def _dispatch_plan(plan, counts, t):
    eid = plan[:TOP_K_INNER].astype(jnp.int32)
    rank = plan[ROUTE_LANE_RANK:ROUTE_LANE_RANK + TOP_K_INNER].astype(jnp.int32)
    counts = counts[0, :N_EXPERTS].astype(jnp.int32)
    padded = (counts + EXPERT_BLOCK - 1) // EXPERT_BLOCK * EXPERT_BLOCK
    pad_end = jnp.cumsum(padded)
    pad_start = pad_end - padded
    experts = jnp.arange(N_EXPERTS, dtype=jnp.int32)[:, None, None]
    start_of = jnp.sum(jnp.where(eid[None] == experts, pad_start[:, None, None], 0), axis=0)
    dest = start_of + rank
    n_blocks = -(-t * TOP_K_INNER // EXPERT_BLOCK) + N_EXPERTS
    n_slots = n_blocks * EXPERT_BLOCK
    block_start = jnp.arange(n_blocks, dtype=jnp.int32) * EXPERT_BLOCK
    block_e = jnp.minimum(jnp.sum((block_start[:, None] >= pad_end[None, :]).astype(jnp.int32), axis=1),
                          N_EXPERTS - 1)
    n_empty = n_slots - t * TOP_K_INNER
    gap_end = jnp.cumsum(padded - counts)
    gap_start = gap_end - (padded - counts)
    k = jnp.arange(n_empty, dtype=jnp.int32)
    in_gap = (k[None, :] >= gap_start[:, None]) & (k[None, :] < gap_end[:, None])
    first_of_gap = (pad_start + counts - gap_start)[:, None]
    empty = jnp.where(k < gap_end[-1],
                      jnp.sum(jnp.where(in_gap, first_of_gap + k[None, :], 0), axis=0),
                      pad_end[-1] + k - gap_end[-1])
    return dest.astype(jnp.int32), empty.astype(jnp.int32), block_e


def _sc_mesh():
    return plsc.VectorSubcoreMesh(core_axis_name="core", subcore_axis_name="subcore")


def _sc_rows_pipeline(body, n_rows, row_index_map, cols):
    return pltpu.emit_pipeline(
        body,
        grid=(n_rows // SC_WINDOW,),
        in_specs=[pl.BlockSpec((SC_WINDOW, cols), index_map=row_index_map),
                  pl.BlockSpec((1, SC_WINDOW), index_map=lambda i: (0, i))],
        out_specs=[],
        core_axis_name=("core", "subcore"),
        dimension_semantics=(pltpu.PARALLEL,),
    )


def _chunk_rows(rows, chunks, n_rows_per_chunk):
    base = jnp.arange(chunks, dtype=jnp.int32)[:, None] * n_rows_per_chunk
    return (base + rows[None, :]).reshape(1, -1)


def _dispatch(rows, dest, empty, n_slots):
    chunks, t, _ = rows.shape
    x = rows.reshape(chunks * t, SC_COLS)
    idx_first = _chunk_rows(dest[0], chunks, n_slots)
    idx_second = _chunk_rows(dest[1], chunks, n_slots)
    idx_empty = _chunk_rows(empty, chunks, n_slots)
    zeros = jnp.zeros((SC_WINDOW, SC_COLS), rows.dtype)

    @pl.kernel(out_type=jax.ShapeDtypeStruct((chunks * n_slots, SC_COLS), rows.dtype), mesh=_sc_mesh(),
               scratch_types=[])
    def scatter(x_hbm, i0_hbm, i1_hbm, z_hbm, ie_hbm, o_hbm):
        def body(x_vmem, i_vmem):
            pltpu.sync_copy(x_vmem, o_hbm.at[i_vmem.at[0]])

        for i_hbm in (i0_hbm, i1_hbm):
            _sc_rows_pipeline(body, x.shape[0], lambda i: (i, 0), SC_COLS)(x_hbm, i_hbm)
        _sc_rows_pipeline(body, idx_empty.shape[1], lambda i: (0, 0), SC_COLS)(z_hbm, ie_hbm)

    return scatter(x, idx_first, idx_second, zeros, idx_empty).reshape(chunks, n_slots, SC_COLS)


def _expert_mlp(xb, wg_ref, wu_ref, wd_ref):
    acc = None
    for c in range(D_EXPERT // EXPERT_CHUNK):
        cols = slice(c * EXPERT_CHUNK, (c + 1) * EXPERT_CHUNK)
        gate = _dot(xb, wg_ref[:, cols].astype(BF16))
        up = _dot(xb, wu_ref[:, cols].astype(BF16))
        hidden = (gate * jax.nn.sigmoid(gate) * up).astype(BF16)
        part = _dot(hidden, wd_ref[cols, :].astype(BF16))
        acc = part if acc is None else acc + part
    return acc


def _expert_kernel(be_ref, x_ref, wg_ref, wu_ref, wd_ref, y_ref):
    xb = _unpack_bf16_pairs(_load_chunk_major(x_ref, 0, EXPERT_BLOCK)).astype(BF16)
    _store_chunk_major(y_ref, 0, _pack_bf16_pairs(_expert_mlp(xb, wg_ref, wu_ref, wd_ref)))


def _experts(xs, block_e, w_gate, w_up, w_down):
    n_blocks = block_e.shape[0]
    d = D_MODEL
    blk = _chunk_major_spec(PACKED_CHUNKS, EXPERT_BLOCK, lambda i, be: (0, i, 0))
    grid_spec = pltpu.PrefetchScalarGridSpec(
        num_scalar_prefetch=1,
        grid=(n_blocks,),
        in_specs=[
            blk,
            pl.BlockSpec((None, d, D_EXPERT), lambda i, be: (be[i], 0, 0)),
            pl.BlockSpec((None, d, D_EXPERT), lambda i, be: (be[i], 0, 0)),
            pl.BlockSpec((None, D_EXPERT, d), lambda i, be: (be[i], 0, 0)),
        ],
        out_specs=blk,
    )
    return pl.pallas_call(
        _expert_kernel,
        grid_spec=grid_spec,
        out_shape=jax.ShapeDtypeStruct(xs.shape, xs.dtype),
        compiler_params=pltpu.CompilerParams(
            dimension_semantics=("arbitrary",), vmem_limit_bytes=VMEM_LIMIT),
        name="experts",
    )(block_e, xs, w_gate, w_up, w_down)


def _gather(yb, dest):
    chunks, n_slots, _ = yb.shape
    x = yb.reshape(chunks * n_slots, SC_COLS)
    idx = _chunk_rows(dest.reshape(-1), chunks, n_slots)
    m = idx.shape[1]

    @pl.kernel(out_type=jax.ShapeDtypeStruct((m, SC_COLS), yb.dtype), mesh=_sc_mesh(), scratch_types=[])
    def gather(x_hbm, i_hbm, o_hbm):
        def body(i_vmem, o_vmem):
            pltpu.sync_copy(x_hbm.at[i_vmem.at[0]], o_vmem)

        pltpu.emit_pipeline(
            body,
            grid=(m // SC_WINDOW,),
            in_specs=[pl.BlockSpec((1, SC_WINDOW), index_map=lambda i: (0, i))],
            out_specs=[pl.BlockSpec((SC_WINDOW, SC_COLS), index_map=lambda i: (i, 0))],
            core_axis_name=("core", "subcore"),
            dimension_semantics=(pltpu.PARALLEL,),
        )(i_hbm, o_hbm)

    return gather(x, idx).reshape(chunks, m // chunks, SC_COLS)


def _combine_kernel(first_ref, second_ref, h1_ref, route_ref, g_ref, b_ref, y_ref):
    tm = y_ref.shape[0]
    route = route_ref[...]
    w_first = route[:, ROUTE_LANE_W:ROUTE_LANE_W + 1]
    w_second = route[:, ROUTE_LANE_W + 1:ROUTE_LANE_W + 2]
    ffn = (w_first * _unpack_bf16_pairs(_load_chunk_major(first_ref, 0, tm))
           + w_second * _unpack_bf16_pairs(_load_chunk_major(second_ref, 0, tm)))
    y_ref[...] = _layer_norm(DEEPNORM_ALPHA * _load_chunk_major(h1_ref, 0, tm) + ffn, g_ref[...], b_ref[...])


def _combine(h1, route, g, ln2_g, ln2_b):
    t = route.shape[0]
    d = D_MODEL
    tm = ROW_TILE
    nt = t // tm
    return pl.pallas_call(
        _combine_kernel,
        grid=(nt,),
        in_specs=[
            _chunk_major_spec(PACKED_CHUNKS, tm, lambda i: (0, i, 0)),
            _chunk_major_spec(PACKED_CHUNKS, tm, lambda i: (0, nt + i, 0)),
            _chunk_major_spec(F32_CHUNKS, tm, lambda i: (0, i, 0)),
            pl.BlockSpec((tm, LANES), lambda i: (i, 0)),
            _resident((1, d)), _resident((1, d)),
        ],
        out_specs=pl.BlockSpec((tm, d), lambda i: (i, 0)),
        out_shape=jax.ShapeDtypeStruct((t, d), F32),
        compiler_params=pltpu.CompilerParams(
            dimension_semantics=("parallel",), vmem_limit_bytes=VMEM_LIMIT),
        name="combine",
    )(g, g, h1, route, ln2_g, ln2_b)


def _token_mixing(x, p):
    bsz, s, d = x.shape
    qkv0, qkv1, qkv2, cg, ga, h = _inproj(x, p['ln_in_g'], p['ln_in_b'], p['w_in'], p['b_gate'], p['conv_w'],
                                          p['w_conv_out'])
    o_list, lse_list = [], []
    for qkv, r in zip((qkv0, qkv1, qkv2), DILATIONS):
        o, lse = _attention(qkv.reshape(3, bsz, s, GROUP_W), s // r)
        o_list.append(o)
        lse_list.append(lse)
    h1, h1_packed, route, plan, counts = _mix(h, o_list, lse_list, cg, ga, p['w_attn_out'], p['w_o'],
                             p['ln1_g'], p['ln1_b'], p['wr_hi'], p['wr_lo'], p['b_route'])
    t = bsz * s
    route = route.reshape(t, LANES)
    dest, empty, block_e = _dispatch_plan(plan, counts, t)
    return h1, route, dest, empty, block_e, h1_packed


def kernel(x_prompt, x_sample, ln_in_g, ln_in_b, w_in, b_gate, conv_w, w_attn_out, w_conv_out, w_o, ln1_g, ln1_b, w_route_group, b_route_group, w_route_expert, b_route_expert, w_gate, w_up, w_down, ln2_g, ln2_b):
    d = D_MODEL
    w_route = jnp.concatenate([w_route_group[0], w_route_expert[0]], axis=1)
    w_route = jnp.pad(w_route, ((0, 0), (0, LANES - w_route.shape[1])))
    wr_hi = w_route.astype(BF16)
    b_route = jnp.concatenate([b_route_group[0], b_route_expert[0]]).astype(F32)
    p = {
        'ln_in_g': ln_in_g.reshape(1, d), 'ln_in_b': ln_in_b.reshape(1, d),
        'w_in': w_in[0].astype(BF16), 'b_gate': b_gate[0].reshape(1, 2 * d), 'conv_w': conv_w[0],
        'w_conv_out': w_conv_out[0].astype(BF16), 'w_attn_out': w_attn_out[0].astype(BF16),
        'w_o': w_o[0].astype(BF16),
        'ln1_g': ln1_g[0].reshape(1, d), 'ln1_b': ln1_b[0].reshape(1, d),
        'wr_hi': wr_hi, 'wr_lo': (w_route - wr_hi.astype(F32)).astype(BF16),
        'b_route': jnp.pad(b_route, (0, LANES - b_route.shape[0])).reshape(1, LANES),
        'ln2_g': ln2_g[0].reshape(1, d), 'ln2_b': ln2_b[0].reshape(1, d),
    }
    p['w_gate'], p['w_up'], p['w_down'] = w_gate[0], w_up[0], w_down[0]
    batches = [_token_mixing(x, p) for x in (x_prompt, x_sample)]
    sorted_rows = [_dispatch(h1_packed, dest, empty, block_e.shape[0] * EXPERT_BLOCK)
                   for _, _, dest, empty, block_e, h1_packed in batches]
    gathered = [_gather(_experts(xs, b[4], p['w_gate'], p['w_up'], p['w_down']), b[2])
                for xs, b in zip(sorted_rows, batches)]
    outs = [_combine(b[0], b[1], g, p['ln2_g'], p['ln2_b']).reshape(x.shape)
            for b, g, x in zip(batches, gathered, (x_prompt, x_sample))]
    return tuple(outs)
```

```python
import functools

import jax
import jax.numpy as jnp
import numpy as np
from jax import lax
from jax.experimental import pallas as pl
from jax.experimental.pallas import tpu as pltpu
from jax.experimental.pallas import tpu_sc as plsc

F32 = jnp.float32
BF16 = jnp.bfloat16

D_MODEL = 1024
HEAD_DIM = 64
HEADS_PER_GROUP = 4
GROUP_W = HEADS_PER_GROUP * HEAD_DIM
DILATIONS = (1, 4, 16)
BAND = 64
N_GROUPS = len(DILATIONS)
D_ATTN = N_GROUPS * GROUP_W
ATTN_SCALE = HEAD_DIM ** -0.5
ROPE_DIM = HEAD_DIM // 4
ROPE_THETA = 500000.0
MASK_VALUE = -1e30
D_CONV = D_MODEL
COL_CONV_B = 3 * D_ATTN
COL_CONV_C = COL_CONV_B + D_CONV
COL_CONV_H = COL_CONV_C + D_CONV
COL_GATE_ATTN = COL_CONV_H + D_CONV
COL_GATE_CONV = COL_GATE_ATTN + D_MODEL
N_EXPERT_GROUPS = 4
EXPERTS_PER_GROUP = 8
N_EXPERTS = N_EXPERT_GROUPS * EXPERTS_PER_GROUP
TOP_K_INNER = 2
D_EXPERT = 512
EXPERT_BLOCK = 512
EXPERT_CHUNK = 256
LN_EPS = 1e-5
DEPTH = 1
DEEPNORM_ALPHA = (2 * DEPTH) ** 0.25

LANES = 128
HEADS_PER_BLOCK = LANES // HEAD_DIM
ROW_TILE = 512
HALO = 16
Q_BLOCK = 128
K_BLOCK = Q_BLOCK + 2 * BAND
Q_CHUNK = 4096
SC_COLS = 256
SC_WINDOW = 128
F32_CHUNKS = D_MODEL // SC_COLS
PACKED_COLS = D_MODEL // 2
PACKED_CHUNKS = PACKED_COLS // SC_COLS
U32 = jnp.uint32
HIGH_HALF = np.uint32(0xFFFF0000)
ROUTE_LANE0 = N_EXPERT_GROUPS
ROUTE_LANE_W = TOP_K_INNER
ROUTE_LANE_RANK = 2 * TOP_K_INNER
PLAN_ROWS = 8
VMEM_LIMIT = 56 * 1024 * 1024


def _layer_norm(v, g, b):
    mu = jnp.mean(v, axis=-1, keepdims=True)
    d = v - mu
    var = jnp.mean(d * d, axis=-1, keepdims=True)
    return d * lax.rsqrt(var + LN_EPS) * g + b


def _dot(a, b):
    return jnp.dot(a, b, preferred_element_type=F32)


def _store_chunk_major(ref, first_row, val):
    for c in range(ref.shape[0]):
        ref[c, pl.ds(first_row, val.shape[0]), :] = val[:, c * SC_COLS:(c + 1) * SC_COLS]


def _load_chunk_major(ref, first_row, n):
    return jnp.concatenate([ref[c, pl.ds(first_row, n), :] for c in range(ref.shape[0])], axis=1)


def _chunk_major_spec(chunks, rows, index_map):
    return pl.BlockSpec((chunks, rows, SC_COLS), index_map)


def _pack_bf16_pairs(val):
    bits = lax.bitcast_convert_type(val.astype(BF16).astype(F32), U32)
    return (bits[:, PACKED_COLS:] & HIGH_HALF) | (bits[:, :PACKED_COLS] >> 16)


def _unpack_bf16_pairs(packed):
    low = lax.bitcast_convert_type(packed << 16, F32)
    high = lax.bitcast_convert_type(packed & HIGH_HALF, F32)
    return jnp.concatenate([low, high], axis=1)


def _resident(shape):
    nd = len(shape)
    return pl.BlockSpec(shape, lambda *_: (0,) * nd, pipeline_mode=pl.Buffered(1))


def _inproj_kernel(x_ref, xp_ref, xn_ref, g_ref, b_ref, win_ref, bg_ref, cw_ref, wco_ref,
                   cos_ref, sa_ref, sb_ref,
                   qkv0_ref, qkv1_ref, qkv2_ref, cg_ref, ga_ref, h_ref,
                   hext_ref, u_ref, stage_ref):
    i = pl.program_id(1)
    nt = pl.num_programs(1)
    tm = x_ref.shape[0]
    g = g_ref[...]
    b = b_ref[...]
    h = _layer_norm(x_ref[...], g, b)
    h_ref[...] = h
    hext_ref[pl.ds(HALO, tm), :] = h.astype(BF16)
    hext_ref[pl.ds(0, HALO), :] = _layer_norm(xp_ref[...], g, b).astype(BF16)
    hext_ref[pl.ds(HALO + tm, HALO), :] = _layer_norm(xn_ref[...], g, b).astype(BF16)
    hm = hext_ref[pl.ds(HALO, tm), :]
    he = hext_ref[...]

    def proj(lhs, col, n):
        return _dot(lhs, win_ref[:, col:col + n])

    u = proj(he, COL_CONV_C, D_CONV) * proj(he, COL_CONV_H, D_CONV)
    rows = lax.broadcasted_iota(jnp.int32, (tm + 2 * HALO, 1), 0)
    lo = jnp.where(i == 0, HALO, 0)
    hi = jnp.where(i == nt - 1, HALO + tm, tm + 2 * HALO)
    u_ref[...] = jnp.where((rows >= lo) & (rows < hi), u, 0.0)
    cw = cw_ref[...]
    conv = (cw[0:1] * u_ref[pl.ds(HALO - 1, tm), :] + cw[1:2] * u_ref[pl.ds(HALO, tm), :]
            + cw[2:3] * u_ref[pl.ds(HALO + 1, tm), :])
    cb = proj(hm, COL_CONV_B, D_CONV)
    conv_branch = _dot((cb * conv).astype(BF16), wco_ref[...])
    gate_conv = jax.nn.sigmoid(proj(hm, COL_GATE_CONV, D_MODEL) + bg_ref[:, D_MODEL:])
    cg_ref[...] = (gate_conv * conv_branch).astype(BF16)
    ga_ref[...] = jax.nn.sigmoid(proj(hm, COL_GATE_ATTN, D_MODEL) + bg_ref[:, :D_MODEL]).astype(BF16)

    cosv = cos_ref[...]
    sav = sa_ref[...]
    sbv = sb_ref[...]
    half = ROPE_DIM // 2
    for gi, (out_ref, r) in enumerate(zip((qkv0_ref, qkv1_ref, qkv2_ref), DILATIONS)):
        for t in range(3):
            p = proj(hm, t * D_ATTN + gi * GROUP_W, GROUP_W)
            if t < 2:
                parts = []
                for c in range(GROUP_W // LANES):
                    pc = p[:, c * LANES:(c + 1) * LANES]
                    parts.append(pc * cosv + pltpu.roll(pc, LANES - half, 1) * sav
                                 + pltpu.roll(pc, half, 1) * sbv)
                p = jnp.concatenate(parts, axis=1)
                if t == 0:
                    p = p * ATTN_SCALE
            if r == 1:
                out_ref[t, 0] = p.astype(BF16)
            else:
                for c in range(GROUP_W // LANES):
                    stage_ref[c] = p[:, c * LANES:(c + 1) * LANES]
                for res in range(r):
                    for c in range(GROUP_W // LANES):
                        out_ref[t, res, :, c * LANES:(c + 1) * LANES] = (
                            stage_ref[c, pl.ds(res, tm // r, stride=r), :].astype(BF16))


def _rope_tables(s):
    half = ROPE_DIM // 2
    inv_freq = (np.float32(ROPE_THETA) ** (-np.arange(half, dtype=np.float32) * np.float32(2.0) / ROPE_DIM))
    ang = np.arange(s, dtype=np.float32)[:, None] * inv_freq.astype(np.float32)[None, :]
    cos = np.cos(ang).astype(np.float32)
    sin = np.sin(ang).astype(np.float32)
    pad = np.zeros((s, HEAD_DIM - ROPE_DIM), np.float32)
    zero = np.zeros((s, half), np.float32)
    cos_t = np.concatenate([cos, cos, pad + 1.0], axis=1)
    sa_t = np.concatenate([-sin, zero, pad], axis=1)
    sb_t = np.concatenate([zero, sin, pad], axis=1)
    rep = LANES // HEAD_DIM
    return tuple(jnp.asarray(np.tile(t, (1, rep))) for t in (cos_t, sa_t, sb_t))


def _inproj(x, ln_g, ln_b, w_in, b_gate, conv_w, w_conv_out):
    bsz, s, d = x.shape
    tm = ROW_TILE
    nt = s // tm
    hpt = tm // HALO
    cos_t, sa_t, sb_t = _rope_tables(s)
    row = lambda b, i: (b, i, 0)
    tab = pl.BlockSpec((tm, LANES), lambda b, i: (i, 0))
    qkv_shapes = [jax.ShapeDtypeStruct((3, bsz, r, s // r, GROUP_W), BF16) for r in DILATIONS]
    qkv_specs = [pl.BlockSpec((3, None, r, tm // r, GROUP_W), lambda b, i: (0, b, 0, i, 0))
                 for r in DILATIONS]
    return pl.pallas_call(
        _inproj_kernel,
        grid=(bsz, nt),
        in_specs=[
            pl.BlockSpec((None, tm, d), row),
            pl.BlockSpec((None, HALO, d), lambda b, i: (b, jnp.maximum(i * hpt - 1, 0), 0)),
            pl.BlockSpec((None, HALO, d), lambda b, i: (b, jnp.minimum((i + 1) * hpt, s // HALO - 1), 0)),
            _resident((1, d)), _resident((1, d)),
            _resident(w_in.shape), _resident((1, 2 * d)), _resident(conv_w.shape),
            _resident(w_conv_out.shape),
            tab, tab, tab,
        ],
        out_specs=qkv_specs + [pl.BlockSpec((None, tm, d), row)] * 3,
        out_shape=(qkv_shapes + [jax.ShapeDtypeStruct((bsz, s, d), BF16)] * 2
                   + [jax.ShapeDtypeStruct((bsz, s, d), F32)]),
        scratch_shapes=[
            pltpu.VMEM((tm + 2 * HALO, d), BF16),
            pltpu.VMEM((tm + 2 * HALO, D_CONV), F32),
            pltpu.VMEM((GROUP_W // LANES, tm, LANES), F32),
        ],
        compiler_params=pltpu.CompilerParams(
            dimension_semantics=("parallel", "parallel"), vmem_limit_bytes=VMEM_LIMIT),
        name="inproj",
    )(x, x, x, ln_g, ln_b, w_in, b_gate, conv_w, w_conv_out, cos_t, sa_t, sb_t)


def _attn_kernel(q_ref, k_ref, v_ref, o_ref, lse_ref, *, sub_len):
    s = k_ref.shape[0]
    qc = q_ref.shape[0]
    base = pl.program_id(1) * qc
    lane = lax.broadcasted_iota(jnp.int32, (1, GROUP_W), 1)
    head_masks = [(lane >= h * HEAD_DIM) & (lane < (h + 1) * HEAD_DIM) for h in range(HEADS_PER_GROUP)]
    qi = lax.broadcasted_iota(jnp.int32, (Q_BLOCK, 1), 0)
    kj = lax.broadcasted_iota(jnp.int32, (1, K_BLOCK), 1)
    low_half = lax.broadcasted_iota(jnp.int32, (1, LANES), 1) < HEAD_DIM

    def body(j, carry):
        r0 = pl.multiple_of(j * Q_BLOCK, Q_BLOCK)
        s0 = base + r0
        k0 = pl.multiple_of(jnp.clip(s0 - BAND, 0, s - K_BLOCK), BAND)
        q = q_ref[pl.ds(r0, Q_BLOCK), :]
        k = k_ref[pl.ds(k0, K_BLOCK), :]
        v = v_ref[pl.ds(k0, K_BLOCK), :]
        sub_lo = (s0 // sub_len) * sub_len
        qpos = s0 + qi
        kpos = k0 + kj
        valid = (jnp.abs(qpos - kpos) <= BAND) & (kpos >= sub_lo) & (kpos < sub_lo + sub_len)
        qm = jnp.concatenate([jnp.where(hm, q, jnp.zeros_like(q)) for hm in head_masks], axis=0)
        sc = lax.dot_general(qm, k, (((1,), (1,)), ((), ())), preferred_element_type=F32)
        sc = jnp.where(jnp.concatenate([valid] * HEADS_PER_GROUP, axis=0), sc, MASK_VALUE)
        m = jnp.max(sc, axis=-1, keepdims=True)
        p = jnp.exp(sc - m)
        den = jnp.sum(p, axis=-1, keepdims=True)
        o_all = _dot((p * (1.0 / den)).astype(BF16), v)
        lse_all = m + jnp.log(den)
        o_cols, lse_cols = [], []
        for c in range(GROUP_W // LANES):
            lanes = slice(c * LANES, (c + 1) * LANES)
            h_lo, h_hi = HEADS_PER_BLOCK * c, HEADS_PER_BLOCK * c + 1
            rows_lo = slice(h_lo * Q_BLOCK, (h_lo + 1) * Q_BLOCK)
            rows_hi = slice(h_hi * Q_BLOCK, (h_hi + 1) * Q_BLOCK)
            o_cols.append(jnp.where(low_half, o_all[rows_lo, lanes], o_all[rows_hi, lanes]))
            lse_cols.append(jnp.where(low_half, lse_all[rows_lo], lse_all[rows_hi]))
        o_ref[pl.ds(r0, Q_BLOCK), :] = jnp.concatenate(o_cols, axis=1).astype(BF16)
        lse_ref[pl.ds(r0, Q_BLOCK), :] = jnp.concatenate(lse_cols, axis=1)
        return carry

    lax.fori_loop(0, qc // Q_BLOCK, body, 0, unroll=True)


def _attention(qkv, sub_len):
    _, bsz, s, w = qkv.shape
    qc = min(Q_CHUNK, s)
    kv_spec = lambda t: pl.BlockSpec((None, None, s, w), lambda b, j: (t, b, 0, 0))
    return pl.pallas_call(
        functools.partial(_attn_kernel, sub_len=sub_len),
        grid=(bsz, s // qc),
        in_specs=[pl.BlockSpec((None, None, qc, w), lambda b, j: (0, b, j, 0)), kv_spec(1), kv_spec(2)],
        out_specs=[pl.BlockSpec((None, qc, w), lambda b, j: (b, j, 0))] * 2,
        out_shape=[jax.ShapeDtypeStruct((bsz, s, w), BF16), jax.ShapeDtypeStruct((bsz, s, w), F32)],
        compiler_params=pltpu.CompilerParams(
            dimension_semantics=("parallel", "arbitrary"), vmem_limit_bytes=VMEM_LIMIT),
        name="attn",
    )(qkv, qkv, qkv)


def _first_index_of_max(vals, lane_f):
    mx = jnp.max(vals, axis=-1, keepdims=True)
    idx = jnp.min(jnp.where(vals == mx, lane_f, float(LANES)), axis=-1, keepdims=True)
    return mx, idx


def _mix_kernel(h_ref, o0_ref, o1_ref, o2_ref, l0_ref, l1_ref, l2_ref, cg_ref, ga_ref,
                wao_ref, wo_ref, g1_ref, b1_ref, wrh_ref, wrl_ref, br_ref,
                h1_ref, h1p_ref, route_ref, plan_ref, counts_ref, so_ref, sl_ref, tri_ref):
    tm = h_ref.shape[0]
    halves = GROUP_W // LANES
    for gi, (o_ref, l_ref, r) in enumerate(zip((o0_ref, o1_ref, o2_ref), (l0_ref, l1_ref, l2_ref), DILATIONS)):
        for c in range(halves):
            cols = slice(c * LANES, (c + 1) * LANES)
            for res in range(r):
                rows = pl.ds(res, tm // r, stride=r) if r > 1 else slice(None)
                so_ref[gi * halves + c, rows, :] = o_ref[res, :, cols].astype(F32)
                sl_ref[gi * halves + c, rows, :] = l_ref[res, :, cols]

    def natural(ref, gi):
        return jnp.concatenate([ref[gi * halves + c] for c in range(halves)], axis=1)

    lses = [natural(sl_ref, gi) for gi in range(N_GROUPS)]
    mx = jnp.maximum(jnp.maximum(lses[0], lses[1]), lses[2])
    es = [jnp.exp(l - mx) for l in lses]
    inv_den = 1.0 / (es[0] + es[1] + es[2])
    attn = jnp.concatenate(
        [(natural(so_ref, gi) * (es[gi] * inv_den)).astype(BF16) for gi in range(N_GROUPS)], axis=1)
    attn_branch = _dot(attn, wao_ref[...])
    merged = ga_ref[...].astype(F32) * attn_branch + cg_ref[...].astype(F32)
    mix = _dot(merged.astype(BF16), wo_ref[...])
    h1 = _layer_norm(DEEPNORM_ALPHA * h_ref[...] + mix, g1_ref[...], b1_ref[...])
    _store_chunk_major(h1_ref, 0, h1)
    _store_chunk_major(h1p_ref, 0, _pack_bf16_pairs(h1))

    hi = h1.astype(BF16)
    lo = (h1 - hi.astype(F32)).astype(BF16)
    wrh = wrh_ref[...]
    logits = _dot(hi, wrh) + _dot(lo, wrh) + _dot(hi, wrl_ref[...]) + br_ref[...]
    lane = lax.broadcasted_iota(jnp.int32, (1, LANES), 1)
    lane_f = lane.astype(F32)
    is_grp = lane < N_EXPERT_GROUPS
    gl = jnp.where(is_grp, logits, MASK_VALUE)
    ge = jnp.exp(gl - jnp.max(gl, axis=-1, keepdims=True))
    gp = jnp.where(is_grp, ge / jnp.sum(ge, axis=-1, keepdims=True), -1.0)
    grp_w, grp_idx = _first_index_of_max(gp, lane_f)
    e_lo = ROUTE_LANE0 + grp_idx * EXPERTS_PER_GROUP
    sel = jnp.where((lane_f >= e_lo) & (lane_f < e_lo + EXPERTS_PER_GROUP), logits, MASK_VALUE)
    v1, i1 = _first_index_of_max(sel, lane_f)
    v2, i2 = _first_index_of_max(jnp.where(lane_f == i1, MASK_VALUE, sel), lane_f)
    e2 = jnp.exp(v2 - v1)
    w1 = grp_w / (1.0 + e2)
    w2 = grp_w * e2 / (1.0 + e2)

    first = (pl.program_id(0) == 0) & (pl.program_id(1) == 0)

    @pl.when(first)
    def _():
        counts_ref[...] = jnp.zeros_like(counts_ref)
        r = lax.broadcasted_iota(jnp.int32, (tm, tm), 0)
        c = lax.broadcasted_iota(jnp.int32, (tm, tm), 1)
        tri_ref[...] = (c < r).astype(BF16)

    oh0 = lane_f == i1 - ROUTE_LANE0
    oh1 = lane_f == i2 - ROUTE_LANE0
    oh = oh0.astype(F32) + oh1.astype(F32)
    before = counts_ref[...] + _dot(tri_ref[...], oh.astype(BF16))
    r0 = jnp.sum(jnp.where(oh0, before, 0.0), axis=-1, keepdims=True)
    r1 = jnp.sum(jnp.where(oh1, before + oh0.astype(F32), 0.0), axis=-1, keepdims=True)
    counts_ref[...] = counts_ref[...] + jnp.sum(oh, axis=0, keepdims=True)

    per_lane = (i1 - ROUTE_LANE0, i2 - ROUTE_LANE0, w1, w2, r0, r1)
    route = jnp.zeros((tm, LANES), F32)
    for k, v in enumerate(per_lane):
        route = jnp.where(lane == k, v, route)
    route_ref[...] = route
    plan_ref[...] = route.T[:PLAN_ROWS, :]


def _mix(h, o_list, lse_list, cg, ga, w_attn_out, w_o, ln1_g, ln1_b, wr_hi, wr_lo, b_route):
    bsz, s, d = h.shape
    tm = ROW_TILE
    row = lambda b, i: (b, i, 0)
    dil_specs = [pl.BlockSpec((None, r, tm // r, GROUP_W), lambda b, i: (b, 0, i, 0)) for r in DILATIONS]
    o_views = [o.reshape(bsz, r, s // r, GROUP_W) for o, r in zip(o_list, DILATIONS)]
    l_views = [l.reshape(bsz, r, s // r, GROUP_W) for l, r in zip(lse_list, DILATIONS)]
    return pl.pallas_call(
        _mix_kernel,
        grid=(bsz, s // tm),
        in_specs=[pl.BlockSpec((None, tm, d), row)] + dil_specs + dil_specs + [
            pl.BlockSpec((None, tm, d), row), pl.BlockSpec((None, tm, d), row),
            _resident(w_attn_out.shape), _resident(w_o.shape),
            _resident((1, d)), _resident((1, d)),
            _resident(wr_hi.shape), _resident(wr_lo.shape), _resident((1, LANES)),
        ],
        out_specs=[_chunk_major_spec(F32_CHUNKS, tm, lambda b, i: (0, b * (s // tm) + i, 0)),
                   _chunk_major_spec(PACKED_CHUNKS, tm, lambda b, i: (0, b * (s // tm) + i, 0)),
                   pl.BlockSpec((None, tm, LANES), row),
                   pl.BlockSpec((PLAN_ROWS, tm), lambda b, i: (0, b * (s // tm) + i)),
                   pl.BlockSpec((1, LANES), lambda b, i: (0, 0))],
        out_shape=[jax.ShapeDtypeStruct((F32_CHUNKS, bsz * s, SC_COLS), F32),
                   jax.ShapeDtypeStruct((PACKED_CHUNKS, bsz * s, SC_COLS), U32),
                   jax.ShapeDtypeStruct((bsz, s, LANES), F32),
                   jax.ShapeDtypeStruct((PLAN_ROWS, bsz * s), F32),
                   jax.ShapeDtypeStruct((1, LANES), F32)],
        scratch_shapes=[pltpu.VMEM((N_GROUPS * GROUP_W // LANES, tm, LANES), F32)] * 2
        + [pltpu.VMEM((tm, tm), BF16)],
        compiler_params=pltpu.CompilerParams(
            dimension_semantics=("arbitrary", "arbitrary"), vmem_limit_bytes=VMEM_LIMIT),
        name="mix",
    )(h, *o_views, *l_views, cg, ga, w_attn_out, w_o, ln1_g, ln1_b, wr_hi, wr_lo, b_route)


def _dispatch_plan(plan, counts, t):
    eid = plan[:TOP_K_INNER].astype(jnp.int32)
    rank = plan[ROUTE_LANE_RANK:ROUTE_LANE_RANK + TOP_K_INNER].astype(jnp.int32)
    counts = counts[0, :N_EXPERTS].astype(jnp.int32)
    padded = (counts + EXPERT_BLOCK - 1) // EXPERT_BLOCK * EXPERT_BLOCK
    pad_end = jnp.cumsum(padded)
    pad_start = pad_end - padded
    experts = jnp.arange(N_EXPERTS, dtype=jnp.int32)[:, None, None]
    start_of = jnp.sum(jnp.where(eid[None] == experts, pad_start[:, None, None], 0), axis=0)
    dest = start_of + rank
    n_blocks = -(-t * TOP_K_INNER // EXPERT_BLOCK) + N_EXPERTS
    n_slots = n_blocks * EXPERT_BLOCK
    block_start = jnp.arange(n_blocks, dtype=jnp.int32) * EXPERT_BLOCK
    block_e = jnp.minimum(jnp.sum((block_start[:, None] >= pad_end[None, :]).astype(jnp.int32), axis=1),
                          N_EXPERTS - 1)
    n_empty = n_slots - t * TOP_K_INNER
    gap_end = jnp.cumsum(padded - counts)
    gap_start = gap_end - (padded - counts)
    k = jnp.arange(n_empty, dtype=jnp.int32)
    in_gap = (k[None, :] >= gap_start[:, None]) & (k[None, :] < gap_end[:, None])
    first_of_gap = (pad_start + counts - gap_start)[:, None]
    empty = jnp.where(k < gap_end[-1],
                      jnp.sum(jnp.where(in_gap, first_of_gap + k[None, :], 0), axis=0),
                      pad_end[-1] + k - gap_end[-1])
    return dest.astype(jnp.int32), empty.astype(jnp.int32), block_e


def _sc_mesh():
    return plsc.VectorSubcoreMesh(core_axis_name="core", subcore_axis_name="subcore")


def _sc_rows_pipeline(body, n_rows, row_index_map, cols):
    return pltpu.emit_pipeline(
        body,
        grid=(n_rows // SC_WINDOW,),
        in_specs=[pl.BlockSpec((SC_WINDOW, cols), index_map=row_index_map),
                  pl.BlockSpec((1, SC_WINDOW), index_map=lambda i: (0, i))],
        out_specs=[],
        core_axis_name=("core", "subcore"),
        dimension_semantics=(pltpu.PARALLEL,),
    )


def _chunk_rows(rows, chunks, n_rows_per_chunk):
    base = jnp.arange(chunks, dtype=jnp.int32)[:, None] * n_rows_per_chunk
    return (base + rows[None, :]).reshape(1, -1)


def _dispatch(rows, dest, empty, n_slots):
    chunks, t, _ = rows.shape
    x = rows.reshape(chunks * t, SC_COLS)
    idx_first = _chunk_rows(dest[0], chunks, n_slots)
    idx_second = _chunk_rows(dest[1], chunks, n_slots)
    idx_empty = _chunk_rows(empty, chunks, n_slots)
    zeros = jnp.zeros((SC_WINDOW, SC_COLS), rows.dtype)

    @pl.kernel(out_type=jax.ShapeDtypeStruct((chunks * n_slots, SC_COLS), rows.dtype), mesh=_sc_mesh(),
               scratch_types=[])
    def scatter(x_hbm, i0_hbm, i1_hbm, z_hbm, ie_hbm, o_hbm):
        def body(x_vmem, i_vmem):
            pltpu.sync_copy(x_vmem, o_hbm.at[i_vmem.at[0]])

        for i_hbm in (i0_hbm, i1_hbm):
            _sc_rows_pipeline(body, x.shape[0], lambda i: (i, 0), SC_COLS)(x_hbm, i_hbm)
        _sc_rows_pipeline(body, idx_empty.shape[1], lambda i: (0, 0), SC_COLS)(z_hbm, ie_hbm)

    return scatter(x, idx_first, idx_second, zeros, idx_empty).reshape(chunks, n_slots, SC_COLS)


def _expert_mlp(xb, wg_ref, wu_ref, wd_ref):
    acc = None
    for c in range(D_EXPERT // EXPERT_CHUNK):
        cols = slice(c * EXPERT_CHUNK, (c + 1) * EXPERT_CHUNK)
        gate = _dot(xb, wg_ref[:, cols].astype(BF16))
        up = _dot(xb, wu_ref[:, cols].astype(BF16))
        hidden = (gate * jax.nn.sigmoid(gate) * up).astype(BF16)
        part = _dot(hidden, wd_ref[cols, :].astype(BF16))
        acc = part if acc is None else acc + part
    return acc


def _expert_kernel(be_ref, x_ref, wg_ref, wu_ref, wd_ref, y_ref):
    xb = _unpack_bf16_pairs(_load_chunk_major(x_ref, 0, EXPERT_BLOCK)).astype(BF16)
    _store_chunk_major(y_ref, 0, _pack_bf16_pairs(_expert_mlp(xb, wg_ref, wu_ref, wd_ref)))


def _experts(xs, block_e, w_gate, w_up, w_down):
    n_blocks = block_e.shape[0]
    d = D_MODEL
    blk = _chunk_major_spec(PACKED_CHUNKS, EXPERT_BLOCK, lambda i, be: (0, i, 0))
    grid_spec = pltpu.PrefetchScalarGridSpec(
        num_scalar_prefetch=1,
        grid=(n_blocks,),
        in_specs=[
            blk,
            pl.BlockSpec((None, d, D_EXPERT), lambda i, be: (be[i], 0, 0)),
            pl.BlockSpec((None, d, D_EXPERT), lambda i, be: (be[i], 0, 0)),
            pl.BlockSpec((None, D_EXPERT, d), lambda i, be: (be[i], 0, 0)),
        ],
        out_specs=blk,
    )
    return pl.pallas_call(
        _expert_kernel,
        grid_spec=grid_spec,
        out_shape=jax.ShapeDtypeStruct(xs.shape, xs.dtype),
        compiler_params=pltpu.CompilerParams(
            dimension_semantics=("arbitrary",), vmem_limit_bytes=VMEM_LIMIT),
        name="experts",
    )(block_e, xs, w_gate, w_up, w_down)


def _gather(yb, dest):
    chunks, n_slots, _ = yb.shape
    x = yb.reshape(chunks * n_slots, SC_COLS)
    idx = _chunk_rows(dest.reshape(-1), chunks, n_slots)
    m = idx.shape[1]

    @pl.kernel(out_type=jax.ShapeDtypeStruct((m, SC_COLS), yb.dtype), mesh=_sc_mesh(), scratch_types=[])
    def gather(x_hbm, i_hbm, o_hbm):
        def body(i_vmem, o_vmem):
            pltpu.sync_copy(x_hbm.at[i_vmem.at[0]], o_vmem)

        pltpu.emit_pipeline(
            body,
            grid=(m // SC_WINDOW,),
            in_specs=[pl.BlockSpec((1, SC_WINDOW), index_map=lambda i: (0, i))],
            out_specs=[pl.BlockSpec((SC_WINDOW, SC_COLS), index_map=lambda i: (i, 0))],
            core_axis_name=("core", "subcore"),
            dimension_semantics=(pltpu.PARALLEL,),
        )(i_hbm, o_hbm)

    return gather(x, idx).reshape(chunks, m // chunks, SC_COLS)


def _combine_kernel(first_ref, second_ref, h1_ref, route_ref, g_ref, b_ref, y_ref):
    tm = y_ref.shape[0]
    route = route_ref[...]
    w_first = route[:, ROUTE_LANE_W:ROUTE_LANE_W + 1]
    w_second = route[:, ROUTE_LANE_W + 1:ROUTE_LANE_W + 2]
    ffn = (w_first * _unpack_bf16_pairs(_load_chunk_major(first_ref, 0, tm))
           + w_second * _unpack_bf16_pairs(_load_chunk_major(second_ref, 0, tm)))
    y_ref[...] = _layer_norm(DEEPNORM_ALPHA * _load_chunk_major(h1_ref, 0, tm) + ffn, g_ref[...], b_ref[...])


def _combine(h1, route, g, ln2_g, ln2_b):
    t = route.shape[0]
    d = D_MODEL
    tm = ROW_TILE
    nt = t // tm
    return pl.pallas_call(
        _combine_kernel,
        grid=(nt,),
        in_specs=[
            _chunk_major_spec(PACKED_CHUNKS, tm, lambda i: (0, i, 0)),
            _chunk_major_spec(PACKED_CHUNKS, tm, lambda i: (0, nt + i, 0)),
            _chunk_major_spec(F32_CHUNKS, tm, lambda i: (0, i, 0)),
            pl.BlockSpec((tm, LANES), lambda i: (i, 0)),
            _resident((1, d)), _resident((1, d)),
        ],
        out_specs=pl.BlockSpec((tm, d), lambda i: (i, 0)),
        out_shape=jax.ShapeDtypeStruct((t, d), F32),
        compiler_params=pltpu.CompilerParams(
            dimension_semantics=("parallel",), vmem_limit_bytes=VMEM_LIMIT),
        name="combine",
    )(g, g, h1, route, ln2_g, ln2_b)


def _token_mixing(x, p):
    bsz, s, d = x.shape
    qkv0, qkv1, qkv2, cg, ga, h = _inproj(x, p['ln_in_g'], p['ln_in_b'], p['w_in'], p['b_gate'], p['conv_w'],
                                          p['w_conv_out'])
    o_list, lse_list = [], []
    for qkv, r in zip((qkv0, qkv1, qkv2), DILATIONS):
        o, lse = _attention(qkv.reshape(3, bsz, s, GROUP_W), s // r)
        o_list.append(o)
        lse_list.append(lse)
    h1, h1_packed, route, plan, counts = _mix(h, o_list, lse_list, cg, ga, p['w_attn_out'], p['w_o'],
                             p['ln1_g'], p['ln1_b'], p['wr_hi'], p['wr_lo'], p['b_route'])
    t = bsz * s
    route = route.reshape(t, LANES)
    dest, empty, block_e = _dispatch_plan(plan, counts, t)
    return h1, route, dest, empty, block_e, h1_packed


def kernel(x_prompt, x_sample, ln_in_g, ln_in_b, w_in, b_gate, conv_w, w_attn_out, w_conv_out, w_o, ln1_g, ln1_b, w_route_group, b_route_group, w_route_expert, b_route_expert, w_gate, w_up, w_down, ln2_g, ln2_b):
    d = D_MODEL
    w_route = jnp.concatenate([w_route_group[0], w_route_expert[0]], axis=1)
    w_route = jnp.pad(w_route, ((0, 0), (0, LANES - w_route.shape[1])))
    wr_hi = w_route.astype(BF16)
    b_route = jnp.concatenate([b_route_group[0], b_route_expert[0]]).astype(F32)
    p = {
        'ln_in_g': ln_in_g.reshape(1, d), 'ln_in_b': ln_in_b.reshape(1, d),
        'w_in': w_in[0].astype(BF16), 'b_gate': b_gate[0].reshape(1, 2 * d), 'conv_w': conv_w[0],
        'w_conv_out': w_conv_out[0].astype(BF16), 'w_attn_out': w_attn_out[0].astype(BF16),
        'w_o': w_o[0].astype(BF16),
        'ln1_g': ln1_g[0].reshape(1, d), 'ln1_b': ln1_b[0].reshape(1, d),
        'wr_hi': wr_hi, 'wr_lo': (w_route - wr_hi.astype(F32)).astype(BF16),
        'b_route': jnp.pad(b_route, (0, LANES - b_route.shape[0])).reshape(1, LANES),
        'ln2_g': ln2_g[0].reshape(1, d), 'ln2_b': ln2_b[0].reshape(1, d),
    }
    p['w_gate'], p['w_up'], p['w_down'] = w_gate[0], w_up[0], w_down[0]
    batches = [_token_mixing(x, p) for x in (x_prompt, x_sample)]
    sorted_rows = [_dispatch(h1_packed, dest, empty, block_e.shape[0] * EXPERT_BLOCK)
                   for _, _, dest, empty, block_e, h1_packed in batches]
    gathered = [_gather(_experts(xs, b[4], p['w_gate'], p['w_up'], p['w_down']), b[2])
                for xs, b in zip(sorted_rows, batches)]
    outs = [_combine(b[0], b[1], g, p['ln2_g'], p['ln2_b']).reshape(x.shape)
            for b, g, x in zip(batches, gathered, (x_prompt, x_sample))]
    return tuple(outs)
```

```python
import functools

import jax
import jax.numpy as jnp
import numpy as np
from jax import lax
from jax.experimental import pallas as pl
from jax.experimental.pallas import tpu as pltpu
from jax.experimental.pallas import tpu_sc as plsc

F32 = jnp.float32
BF16 = jnp.bfloat16

D_MODEL = 1024
HEAD_DIM = 64
HEADS_PER_GROUP = 4
GROUP_W = HEADS_PER_GROUP * HEAD_DIM
DILATIONS = (1, 4, 16)
BAND = 64
N_GROUPS = len(DILATIONS)
D_ATTN = N_GROUPS * GROUP_W
ATTN_SCALE = HEAD_DIM ** -0.5
ROPE_DIM = HEAD_DIM // 4
ROPE_THETA = 500000.0
MASK_VALUE = -1e30
D_CONV = D_MODEL
COL_CONV_B = 3 * D_ATTN
COL_CONV_C = COL_CONV_B + D_CONV
COL_CONV_H = COL_CONV_C + D_CONV
COL_GATE_ATTN = COL_CONV_H + D_CONV
COL_GATE_CONV = COL_GATE_ATTN + D_MODEL
N_EXPERT_GROUPS = 4
EXPERTS_PER_GROUP = 8
N_EXPERTS = N_EXPERT_GROUPS * EXPERTS_PER_GROUP
TOP_K_INNER = 2
D_EXPERT = 512
EXPERT_BLOCK = 512
EXPERT_CHUNK = 256
LN_EPS = 1e-5
DEPTH = 1
DEEPNORM_ALPHA = (2 * DEPTH) ** 0.25

LANES = 128
HEADS_PER_BLOCK = LANES // HEAD_DIM
ROW_TILE = 512
HALO = 16
Q_BLOCK = 128
K_BLOCK = Q_BLOCK + 2 * BAND
Q_CHUNK = 4096
COMBINE_TILE = 1024
SC_COLS = 256
SC_WINDOW = 128
F32_CHUNKS = D_MODEL // SC_COLS
PACKED_COLS = D_MODEL // 2
PACKED_CHUNKS = PACKED_COLS // SC_COLS
U32 = jnp.uint32
HIGH_HALF = np.uint32(0xFFFF0000)
ROUTE_LANE0 = N_EXPERT_GROUPS
ROUTE_LANE_W = TOP_K_INNER
ROUTE_LANE_RANK = 2 * TOP_K_INNER
PLAN_ROWS = 8
VMEM_LIMIT = 56 * 1024 * 1024


def _layer_norm(v, g, b):
    mu = jnp.mean(v, axis=-1, keepdims=True)
    d = v - mu
    var = jnp.mean(d * d, axis=-1, keepdims=True)
    return d * lax.rsqrt(var + LN_EPS) * g + b


def _dot(a, b):
    return jnp.dot(a, b, preferred_element_type=F32)


def _store_chunk_major(ref, first_row, val):
    for c in range(ref.shape[0]):
        ref[c, pl.ds(first_row, val.shape[0]), :] = val[:, c * SC_COLS:(c + 1) * SC_COLS]


def _load_chunk_major(ref, first_row, n):
    return jnp.concatenate([ref[c, pl.ds(first_row, n), :] for c in range(ref.shape[0])], axis=1)


def _chunk_major_spec(chunks, rows, index_map):
    return pl.BlockSpec((chunks, rows, SC_COLS), index_map)


def _pack_bf16_pairs(val):
    bits = lax.bitcast_convert_type(val.astype(BF16).astype(F32), U32)
    return (bits[:, PACKED_COLS:] & HIGH_HALF) | (bits[:, :PACKED_COLS] >> 16)


def _unpack_bf16_pairs(packed):
    low = lax.bitcast_convert_type(packed << 16, F32)
    high = lax.bitcast_convert_type(packed & HIGH_HALF, F32)
    return jnp.concatenate([low, high], axis=1)


def _resident(shape):
    nd = len(shape)
    return pl.BlockSpec(shape, lambda *_: (0,) * nd, pipeline_mode=pl.Buffered(1))


def _inproj_kernel(x_ref, xp_ref, xn_ref, g_ref, b_ref, win_ref, bg_ref, cw_ref, wco_ref,
                   cos_ref, sa_ref, sb_ref,
                   qkv0_ref, qkv1_ref, qkv2_ref, cg_ref, ga_ref, h_ref,
                   hext_ref, u_ref, stage_ref):
    i = pl.program_id(1)
    nt = pl.num_programs(1)
    tm = x_ref.shape[0]
    g = g_ref[...]
    b = b_ref[...]
    h = _layer_norm(x_ref[...], g, b)
    h_ref[...] = h
    hext_ref[pl.ds(HALO, tm), :] = h.astype(BF16)
    hext_ref[pl.ds(0, HALO), :] = _layer_norm(xp_ref[...], g, b).astype(BF16)
    hext_ref[pl.ds(HALO + tm, HALO), :] = _layer_norm(xn_ref[...], g, b).astype(BF16)
    hm = hext_ref[pl.ds(HALO, tm), :]
    he = hext_ref[...]

    def proj(lhs, col, n):
        return _dot(lhs, win_ref[:, col:col + n])

    u = proj(he, COL_CONV_C, D_CONV) * proj(he, COL_CONV_H, D_CONV)
    rows = lax.broadcasted_iota(jnp.int32, (tm + 2 * HALO, 1), 0)
    lo = jnp.where(i == 0, HALO, 0)
    hi = jnp.where(i == nt - 1, HALO + tm, tm + 2 * HALO)
    u_ref[...] = jnp.where((rows >= lo) & (rows < hi), u, 0.0)
    cw = cw_ref[...]
    conv = (cw[0:1] * u_ref[pl.ds(HALO - 1, tm), :] + cw[1:2] * u_ref[pl.ds(HALO, tm), :]
            + cw[2:3] * u_ref[pl.ds(HALO + 1, tm), :])
    cb = proj(hm, COL_CONV_B, D_CONV)
    conv_branch = _dot((cb * conv).astype(BF16), wco_ref[...])
    gate_conv = jax.nn.sigmoid(proj(hm, COL_GATE_CONV, D_MODEL) + bg_ref[:, D_MODEL:])
    cg_ref[...] = (gate_conv * conv_branch).astype(BF16)
    ga_ref[...] = jax.nn.sigmoid(proj(hm, COL_GATE_ATTN, D_MODEL) + bg_ref[:, :D_MODEL]).astype(BF16)

    cosv = cos_ref[...]
    sav = sa_ref[...]
    sbv = sb_ref[...]
    half = ROPE_DIM // 2
    for gi, (out_ref, r) in enumerate(zip((qkv0_ref, qkv1_ref, qkv2_ref), DILATIONS)):
        for t in range(3):
            p = proj(hm, t * D_ATTN + gi * GROUP_W, GROUP_W)
            if t < 2:
                parts = []
                for c in range(GROUP_W // LANES):
                    pc = p[:, c * LANES:(c + 1) * LANES]
                    parts.append(pc * cosv + pltpu.roll(pc, LANES - half, 1) * sav
                                 + pltpu.roll(pc, half, 1) * sbv)
                p = jnp.concatenate(parts, axis=1)
                if t == 0:
                    p = p * ATTN_SCALE
            if r == 1:
                out_ref[t, 0] = p.astype(BF16)
            else:
                for c in range(GROUP_W // LANES):
                    stage_ref[c] = p[:, c * LANES:(c + 1) * LANES]
                for res in range(r):
                    for c in range(GROUP_W // LANES):
                        out_ref[t, res, :, c * LANES:(c + 1) * LANES] = (
                            stage_ref[c, pl.ds(res, tm // r, stride=r), :].astype(BF16))


def _rope_tables(s):
    half = ROPE_DIM // 2
    inv_freq = (np.float32(ROPE_THETA) ** (-np.arange(half, dtype=np.float32) * np.float32(2.0) / ROPE_DIM))
    ang = np.arange(s, dtype=np.float32)[:, None] * inv_freq.astype(np.float32)[None, :]
    cos = np.cos(ang).astype(np.float32)
    sin = np.sin(ang).astype(np.float32)
    pad = np.zeros((s, HEAD_DIM - ROPE_DIM), np.float32)
    zero = np.zeros((s, half), np.float32)
    cos_t = np.concatenate([cos, cos, pad + 1.0], axis=1)
    sa_t = np.concatenate([-sin, zero, pad], axis=1)
    sb_t = np.concatenate([zero, sin, pad], axis=1)
    rep = LANES // HEAD_DIM
    return tuple(jnp.asarray(np.tile(t, (1, rep))) for t in (cos_t, sa_t, sb_t))


def _inproj(x, ln_g, ln_b, w_in, b_gate, conv_w, w_conv_out):
    bsz, s, d = x.shape
    tm = ROW_TILE
    nt = s // tm
    hpt = tm // HALO
    cos_t, sa_t, sb_t = _rope_tables(s)
    row = lambda b, i: (b, i, 0)
    tab = pl.BlockSpec((tm, LANES), lambda b, i: (i, 0))
    qkv_shapes = [jax.ShapeDtypeStruct((3, bsz, r, s // r, GROUP_W), BF16) for r in DILATIONS]
    qkv_specs = [pl.BlockSpec((3, None, r, tm // r, GROUP_W), lambda b, i: (0, b, 0, i, 0))
                 for r in DILATIONS]
    return pl.pallas_call(
        _inproj_kernel,
        grid=(bsz, nt),
        in_specs=[
            pl.BlockSpec((None, tm, d), row),
            pl.BlockSpec((None, HALO, d), lambda b, i: (b, jnp.maximum(i * hpt - 1, 0), 0)),
            pl.BlockSpec((None, HALO, d), lambda b, i: (b, jnp.minimum((i + 1) * hpt, s // HALO - 1), 0)),
            _resident((1, d)), _resident((1, d)),
            _resident(w_in.shape), _resident((1, 2 * d)), _resident(conv_w.shape),
            _resident(w_conv_out.shape),
            tab, tab, tab,
        ],
        out_specs=qkv_specs + [pl.BlockSpec((None, tm, d), row)] * 3,
        out_shape=(qkv_shapes + [jax.ShapeDtypeStruct((bsz, s, d), BF16)] * 2
                   + [jax.ShapeDtypeStruct((bsz, s, d), F32)]),
        scratch_shapes=[
            pltpu.VMEM((tm + 2 * HALO, d), BF16),
            pltpu.VMEM((tm + 2 * HALO, D_CONV), F32),
            pltpu.VMEM((GROUP_W // LANES, tm, LANES), F32),
        ],
        compiler_params=pltpu.CompilerParams(
            dimension_semantics=("parallel", "parallel"), vmem_limit_bytes=VMEM_LIMIT),
        name="inproj",
    )(x, x, x, ln_g, ln_b, w_in, b_gate, conv_w, w_conv_out, cos_t, sa_t, sb_t)


def _attn_kernel(q_ref, k_ref, v_ref, o_ref, lse_ref, *, sub_len):
    s = k_ref.shape[0]
    qc = q_ref.shape[0]
    base = pl.program_id(1) * qc
    lane = lax.broadcasted_iota(jnp.int32, (1, GROUP_W), 1)
    head_masks = [(lane >= h * HEAD_DIM) & (lane < (h + 1) * HEAD_DIM) for h in range(HEADS_PER_GROUP)]
    qi = lax.broadcasted_iota(jnp.int32, (Q_BLOCK, 1), 0)
    kj = lax.broadcasted_iota(jnp.int32, (1, K_BLOCK), 1)
    low_half = lax.broadcasted_iota(jnp.int32, (1, LANES), 1) < HEAD_DIM

    def body(j, carry):
        r0 = pl.multiple_of(j * Q_BLOCK, Q_BLOCK)
        s0 = base + r0
        k0 = pl.multiple_of(jnp.clip(s0 - BAND, 0, s - K_BLOCK), BAND)
        q = q_ref[pl.ds(r0, Q_BLOCK), :]
        k = k_ref[pl.ds(k0, K_BLOCK), :]
        v = v_ref[pl.ds(k0, K_BLOCK), :]
        sub_lo = (s0 // sub_len) * sub_len
        qpos = s0 + qi
        kpos = k0 + kj
        valid = (jnp.abs(qpos - kpos) <= BAND) & (kpos >= sub_lo) & (kpos < sub_lo + sub_len)
        qm = jnp.concatenate([jnp.where(hm, q, jnp.zeros_like(q)) for hm in head_masks], axis=0)
        sc = lax.dot_general(qm, k, (((1,), (1,)), ((), ())), preferred_element_type=F32)
        sc = jnp.where(jnp.concatenate([valid] * HEADS_PER_GROUP, axis=0), sc, MASK_VALUE)
        m = jnp.max(sc, axis=-1, keepdims=True)
        p = jnp.exp(sc - m)
        den = jnp.sum(p, axis=-1, keepdims=True)
        o_all = _dot((p * (1.0 / den)).astype(BF16), v)
        lse_all = m + jnp.log(den)
        o_cols, lse_cols = [], []
        for c in range(GROUP_W // LANES):
            lanes = slice(c * LANES, (c + 1) * LANES)
            h_lo, h_hi = HEADS_PER_BLOCK * c, HEADS_PER_BLOCK * c + 1
            rows_lo = slice(h_lo * Q_BLOCK, (h_lo + 1) * Q_BLOCK)
            rows_hi = slice(h_hi * Q_BLOCK, (h_hi + 1) * Q_BLOCK)
            o_cols.append(jnp.where(low_half, o_all[rows_lo, lanes], o_all[rows_hi, lanes]))
            lse_cols.append(jnp.where(low_half, lse_all[rows_lo], lse_all[rows_hi]))
        o_ref[pl.ds(r0, Q_BLOCK), :] = jnp.concatenate(o_cols, axis=1).astype(BF16)
        lse_ref[pl.ds(r0, Q_BLOCK), :] = jnp.concatenate(lse_cols, axis=1)
        return carry

    lax.fori_loop(0, qc // Q_BLOCK, body, 0, unroll=True)


def _attention(qkv, sub_len):
    _, bsz, s, w = qkv.shape
    qc = min(Q_CHUNK, s)
    kv_spec = lambda t: pl.BlockSpec((None, None, s, w), lambda b, j: (t, b, 0, 0))
    return pl.pallas_call(
        functools.partial(_attn_kernel, sub_len=sub_len),
        grid=(bsz, s // qc),
        in_specs=[pl.BlockSpec((None, None, qc, w), lambda b, j: (0, b, j, 0)), kv_spec(1), kv_spec(2)],
        out_specs=[pl.BlockSpec((None, qc, w), lambda b, j: (b, j, 0))] * 2,
        out_shape=[jax.ShapeDtypeStruct((bsz, s, w), BF16), jax.ShapeDtypeStruct((bsz, s, w), F32)],
        compiler_params=pltpu.CompilerParams(
            dimension_semantics=("parallel", "arbitrary"), vmem_limit_bytes=VMEM_LIMIT),
        name="attn",
    )(qkv, qkv, qkv)


def _first_index_of_max(vals, lane_f):
    mx = jnp.max(vals, axis=-1, keepdims=True)
    idx = jnp.min(jnp.where(vals == mx, lane_f, float(LANES)), axis=-1, keepdims=True)
    return mx, idx


def _mix_kernel(h_ref, o0_ref, o1_ref, o2_ref, l0_ref, l1_ref, l2_ref, cg_ref, ga_ref,
                wao_ref, wo_ref, g1_ref, b1_ref, wrh_ref, wrl_ref, br_ref,
                h1_ref, h1p_ref, route_ref, plan_ref, counts_ref, so_ref, sl_ref, tri_ref):
    tm = h_ref.shape[0]
    halves = GROUP_W // LANES
    for gi, (o_ref, l_ref, r) in enumerate(zip((o0_ref, o1_ref, o2_ref), (l0_ref, l1_ref, l2_ref), DILATIONS)):
        for c in range(halves):
            cols = slice(c * LANES, (c + 1) * LANES)
            for res in range(r):
                rows = pl.ds(res, tm // r, stride=r) if r > 1 else slice(None)
                so_ref[gi * halves + c, rows, :] = o_ref[res, :, cols].astype(F32)
                sl_ref[gi * halves + c, rows, :] = l_ref[res, :, cols]

    def natural(ref, gi):
        return jnp.concatenate([ref[gi * halves + c] for c in range(halves)], axis=1)

    lses = [natural(sl_ref, gi) for gi in range(N_GROUPS)]
    mx = jnp.maximum(jnp.maximum(lses[0], lses[1]), lses[2])
    es = [jnp.exp(l - mx) for l in lses]
    inv_den = 1.0 / (es[0] + es[1] + es[2])
    attn = jnp.concatenate(
        [(natural(so_ref, gi) * (es[gi] * inv_den)).astype(BF16) for gi in range(N_GROUPS)], axis=1)
    attn_branch = _dot(attn, wao_ref[...])
    merged = ga_ref[...].astype(F32) * attn_branch + cg_ref[...].astype(F32)
    mix = _dot(merged.astype(BF16), wo_ref[...])
    h1 = _layer_norm(DEEPNORM_ALPHA * h_ref[...] + mix, g1_ref[...], b1_ref[...])
    _store_chunk_major(h1_ref, 0, h1)
    _store_chunk_major(h1p_ref, 0, _pack_bf16_pairs(h1))

    hi = h1.astype(BF16)
    lo = (h1 - hi.astype(F32)).astype(BF16)
    wrh = wrh_ref[...]
    logits = _dot(hi, wrh) + _dot(lo, wrh) + _dot(hi, wrl_ref[...]) + br_ref[...]
    lane = lax.broadcasted_iota(jnp.int32, (1, LANES), 1)
    lane_f = lane.astype(F32)
    is_grp = lane < N_EXPERT_GROUPS
    gl = jnp.where(is_grp, logits, MASK_VALUE)
    ge = jnp.exp(gl - jnp.max(gl, axis=-1, keepdims=True))
    gp = jnp.where(is_grp, ge / jnp.sum(ge, axis=-1, keepdims=True), -1.0)
    grp_w, grp_idx = _first_index_of_max(gp, lane_f)
    e_lo = ROUTE_LANE0 + grp_idx * EXPERTS_PER_GROUP
    sel = jnp.where((lane_f >= e_lo) & (lane_f < e_lo + EXPERTS_PER_GROUP), logits, MASK_VALUE)
    v1, i1 = _first_index_of_max(sel, lane_f)
    v2, i2 = _first_index_of_max(jnp.where(lane_f == i1, MASK_VALUE, sel), lane_f)
    e2 = jnp.exp(v2 - v1)
    w1 = grp_w / (1.0 + e2)
    w2 = grp_w * e2 / (1.0 + e2)

    first = (pl.program_id(0) == 0) & (pl.program_id(1) == 0)

    @pl.when(first)
    def _():
        counts_ref[...] = jnp.zeros_like(counts_ref)
        r = lax.broadcasted_iota(jnp.int32, (tm, tm), 0)
        c = lax.broadcasted_iota(jnp.int32, (tm, tm), 1)
        tri_ref[...] = (c < r).astype(BF16)

    oh0 = lane_f == i1 - ROUTE_LANE0
    oh1 = lane_f == i2 - ROUTE_LANE0
    oh = oh0.astype(F32) + oh1.astype(F32)
    before = counts_ref[...] + _dot(tri_ref[...], oh.astype(BF16))
    r0 = jnp.sum(jnp.where(oh0, before, 0.0), axis=-1, keepdims=True)
    r1 = jnp.sum(jnp.where(oh1, before + oh0.astype(F32), 0.0), axis=-1, keepdims=True)
    counts_ref[...] = counts_ref[...] + jnp.sum(oh, axis=0, keepdims=True)

    per_lane = (i1 - ROUTE_LANE0, i2 - ROUTE_LANE0, w1, w2, r0, r1)
    route = jnp.zeros((tm, LANES), F32)
    for k, v in enumerate(per_lane):
        route = jnp.where(lane == k, v, route)
    route_ref[...] = route
    plan_ref[...] = route.T[:PLAN_ROWS, :]


def _mix(h, o_list, lse_list, cg, ga, w_attn_out, w_o, ln1_g, ln1_b, wr_hi, wr_lo, b_route):
    bsz, s, d = h.shape
    tm = ROW_TILE
    row = lambda b, i: (b, i, 0)
    dil_specs = [pl.BlockSpec((None, r, tm // r, GROUP_W), lambda b, i: (b, 0, i, 0)) for r in DILATIONS]
    o_views = [o.reshape(bsz, r, s // r, GROUP_W) for o, r in zip(o_list, DILATIONS)]
    l_views = [l.reshape(bsz, r, s // r, GROUP_W) for l, r in zip(lse_list, DILATIONS)]
    return pl.pallas_call(
        _mix_kernel,
        grid=(bsz, s // tm),
        in_specs=[pl.BlockSpec((None, tm, d), row)] + dil_specs + dil_specs + [
            pl.BlockSpec((None, tm, d), row), pl.BlockSpec((None, tm, d), row),
            _resident(w_attn_out.shape), _resident(w_o.shape),
            _resident((1, d)), _resident((1, d)),
            _resident(wr_hi.shape), _resident(wr_lo.shape), _resident((1, LANES)),
        ],
        out_specs=[_chunk_major_spec(F32_CHUNKS, tm, lambda b, i: (0, b * (s // tm) + i, 0)),
                   _chunk_major_spec(PACKED_CHUNKS, tm, lambda b, i: (0, b * (s // tm) + i, 0)),
                   pl.BlockSpec((None, tm, LANES), row),
                   pl.BlockSpec((PLAN_ROWS, tm), lambda b, i: (0, b * (s // tm) + i)),
                   pl.BlockSpec((1, LANES), lambda b, i: (0, 0))],
        out_shape=[jax.ShapeDtypeStruct((F32_CHUNKS, bsz * s, SC_COLS), F32),
                   jax.ShapeDtypeStruct((PACKED_CHUNKS, bsz * s, SC_COLS), U32),
                   jax.ShapeDtypeStruct((bsz, s, LANES), F32),
                   jax.ShapeDtypeStruct((PLAN_ROWS, bsz * s), F32),
                   jax.ShapeDtypeStruct((1, LANES), F32)],
        scratch_shapes=[pltpu.VMEM((N_GROUPS * GROUP_W // LANES, tm, LANES), F32)] * 2
        + [pltpu.VMEM((tm, tm), BF16)],
        compiler_params=pltpu.CompilerParams(
            dimension_semantics=("arbitrary", "arbitrary"), vmem_limit_bytes=VMEM_LIMIT),
        name="mix",
    )(h, *o_views, *l_views, cg, ga, w_attn_out, w_o, ln1_g, ln1_b, wr_hi, wr_lo, b_route)


def _dispatch_plan(plan, counts, t):
    eid = plan[:TOP_K_INNER].astype(jnp.int32)
    rank = plan[ROUTE_LANE_RANK:ROUTE_LANE_RANK + TOP_K_INNER].astype(jnp.int32)
    counts = counts[0, :N_EXPERTS].astype(jnp.int32)
    padded = (counts + EXPERT_BLOCK - 1) // EXPERT_BLOCK * EXPERT_BLOCK
    pad_end = jnp.cumsum(padded)
    pad_start = pad_end - padded
    experts = jnp.arange(N_EXPERTS, dtype=jnp.int32)[:, None, None]
    start_of = jnp.sum(jnp.where(eid[None] == experts, pad_start[:, None, None], 0), axis=0)
    dest = start_of + rank
    n_blocks = -(-t * TOP_K_INNER // EXPERT_BLOCK) + N_EXPERTS
    n_slots = n_blocks * EXPERT_BLOCK
    block_start = jnp.arange(n_blocks, dtype=jnp.int32) * EXPERT_BLOCK
    block_e = jnp.minimum(jnp.sum((block_start[:, None] >= pad_end[None, :]).astype(jnp.int32), axis=1),
                          N_EXPERTS - 1)
    n_empty = n_slots - t * TOP_K_INNER
    gap_end = jnp.cumsum(padded - counts)
    gap_start = gap_end - (padded - counts)
    k = jnp.arange(n_empty, dtype=jnp.int32)
    in_gap = (k[None, :] >= gap_start[:, None]) & (k[None, :] < gap_end[:, None])
    first_of_gap = (pad_start + counts - gap_start)[:, None]
    empty = jnp.where(k < gap_end[-1],
                      jnp.sum(jnp.where(in_gap, first_of_gap + k[None, :], 0), axis=0),
                      pad_end[-1] + k - gap_end[-1])
    return dest.astype(jnp.int32), empty.astype(jnp.int32), block_e


def _sc_mesh():
    return plsc.VectorSubcoreMesh(core_axis_name="core", subcore_axis_name="subcore")


def _sc_rows_pipeline(body, n_rows, row_index_map, cols):
    return pltpu.emit_pipeline(
        body,
        grid=(n_rows // SC_WINDOW,),
        in_specs=[pl.BlockSpec((SC_WINDOW, cols), index_map=row_index_map),
                  pl.BlockSpec((1, SC_WINDOW), index_map=lambda i: (0, i))],
        out_specs=[],
        core_axis_name=("core", "subcore"),
        dimension_semantics=(pltpu.PARALLEL,),
    )


def _chunk_rows(rows, chunks, n_rows_per_chunk):
    base = jnp.arange(chunks, dtype=jnp.int32)[:, None] * n_rows_per_chunk
    return (base + rows[None, :]).reshape(1, -1)


def _dispatch(rows, dest, empty, n_slots):
    chunks, t, _ = rows.shape
    x = rows.reshape(chunks * t, SC_COLS)
    idx_first = _chunk_rows(dest[0], chunks, n_slots)
    idx_second = _chunk_rows(dest[1], chunks, n_slots)
    idx_empty = _chunk_rows(empty, chunks, n_slots)
    zeros = jnp.zeros((SC_WINDOW, SC_COLS), rows.dtype)

    @pl.kernel(out_type=jax.ShapeDtypeStruct((chunks * n_slots, SC_COLS), rows.dtype), mesh=_sc_mesh(),
               scratch_types=[])
    def scatter(x_hbm, i0_hbm, i1_hbm, z_hbm, ie_hbm, o_hbm):
        def body(x_vmem, i_vmem):
            pltpu.sync_copy(x_vmem, o_hbm.at[i_vmem.at[0]])

        for i_hbm in (i0_hbm, i1_hbm):
            _sc_rows_pipeline(body, x.shape[0], lambda i: (i, 0), SC_COLS)(x_hbm, i_hbm)
        _sc_rows_pipeline(body, idx_empty.shape[1], lambda i: (0, 0), SC_COLS)(z_hbm, ie_hbm)

    return scatter(x, idx_first, idx_second, zeros, idx_empty).reshape(chunks, n_slots, SC_COLS)


def _expert_mlp(xb, wg_ref, wu_ref, wd_ref):
    acc = None
    for c in range(D_EXPERT // EXPERT_CHUNK):
        cols = slice(c * EXPERT_CHUNK, (c + 1) * EXPERT_CHUNK)
        gate = _dot(xb, wg_ref[:, cols].astype(BF16))
        up = _dot(xb, wu_ref[:, cols].astype(BF16))
        hidden = (gate * jax.nn.sigmoid(gate) * up).astype(BF16)
        part = _dot(hidden, wd_ref[cols, :].astype(BF16))
        acc = part if acc is None else acc + part
    return acc


def _expert_kernel(be_ref, x_ref, wg_ref, wu_ref, wd_ref, y_ref):
    xb = _unpack_bf16_pairs(_load_chunk_major(x_ref, 0, EXPERT_BLOCK)).astype(BF16)
    _store_chunk_major(y_ref, 0, _pack_bf16_pairs(_expert_mlp(xb, wg_ref, wu_ref, wd_ref)))


def _experts(xs, block_e, w_gate, w_up, w_down):
    n_blocks = block_e.shape[0]
    d = D_MODEL
    blk = _chunk_major_spec(PACKED_CHUNKS, EXPERT_BLOCK, lambda i, be: (0, i, 0))
    grid_spec = pltpu.PrefetchScalarGridSpec(
        num_scalar_prefetch=1,
        grid=(n_blocks,),
        in_specs=[
            blk,
            pl.BlockSpec((None, d, D_EXPERT), lambda i, be: (be[i], 0, 0)),
            pl.BlockSpec((None, d, D_EXPERT), lambda i, be: (be[i], 0, 0)),
            pl.BlockSpec((None, D_EXPERT, d), lambda i, be: (be[i], 0, 0)),
        ],
        out_specs=blk,
    )
    return pl.pallas_call(
        _expert_kernel,
        grid_spec=grid_spec,
        out_shape=jax.ShapeDtypeStruct(xs.shape, xs.dtype),
        compiler_params=pltpu.CompilerParams(
            dimension_semantics=("arbitrary",), vmem_limit_bytes=VMEM_LIMIT),
        name="experts",
    )(block_e, xs, w_gate, w_up, w_down)


def _gather(yb, dest):
    chunks, n_slots, _ = yb.shape
    x = yb.reshape(chunks * n_slots, SC_COLS)
    idx = _chunk_rows(dest.reshape(-1), chunks, n_slots)
    m = idx.shape[1]

    @pl.kernel(out_type=jax.ShapeDtypeStruct((m, SC_COLS), yb.dtype), mesh=_sc_mesh(), scratch_types=[])
    def gather(x_hbm, i_hbm, o_hbm):
        def body(i_vmem, o_vmem):
            pltpu.sync_copy(x_hbm.at[i_vmem.at[0]], o_vmem)

        pltpu.emit_pipeline(
            body,
            grid=(m // SC_WINDOW,),
            in_specs=[pl.BlockSpec((1, SC_WINDOW), index_map=lambda i: (0, i))],
            out_specs=[pl.BlockSpec((SC_WINDOW, SC_COLS), index_map=lambda i: (i, 0))],
            core_axis_name=("core", "subcore"),
            dimension_semantics=(pltpu.PARALLEL,),
        )(i_hbm, o_hbm)

    return gather(x, idx).reshape(chunks, m // chunks, SC_COLS)


def _combine_kernel(first_ref, second_ref, h1_ref, route_ref, g_ref, b_ref, y_ref):
    tm = y_ref.shape[0]
    route = route_ref[...]
    w_first = route[:, ROUTE_LANE_W:ROUTE_LANE_W + 1]
    w_second = route[:, ROUTE_LANE_W + 1:ROUTE_LANE_W + 2]
    ffn = (w_first * _unpack_bf16_pairs(_load_chunk_major(first_ref, 0, tm))
           + w_second * _unpack_bf16_pairs(_load_chunk_major(second_ref, 0, tm)))
    y_ref[...] = _layer_norm(DEEPNORM_ALPHA * _load_chunk_major(h1_ref, 0, tm) + ffn, g_ref[...], b_ref[...])


def _combine(h1, route, g, ln2_g, ln2_b):
    t = route.shape[0]
    d = D_MODEL
    tm = COMBINE_TILE
    nt = t // tm
    return pl.pallas_call(
        _combine_kernel,
        grid=(nt,),
        in_specs=[
            _chunk_major_spec(PACKED_CHUNKS, tm, lambda i: (0, i, 0)),
            _chunk_major_spec(PACKED_CHUNKS, tm, lambda i: (0, nt + i, 0)),
            _chunk_major_spec(F32_CHUNKS, tm, lambda i: (0, i, 0)),
            pl.BlockSpec((tm, LANES), lambda i: (i, 0)),
            _resident((1, d)), _resident((1, d)),
        ],
        out_specs=pl.BlockSpec((tm, d), lambda i: (i, 0)),
        out_shape=jax.ShapeDtypeStruct((t, d), F32),
        compiler_params=pltpu.CompilerParams(
            dimension_semantics=("parallel",), vmem_limit_bytes=VMEM_LIMIT),
        name="combine",
    )(g, g, h1, route, ln2_g, ln2_b)


def _token_mixing(x, p):
    bsz, s, d = x.shape
    qkv0, qkv1, qkv2, cg, ga, h = _inproj(x, p['ln_in_g'], p['ln_in_b'], p['w_in'], p['b_gate'], p['conv_w'],
                                          p['w_conv_out'])
    o_list, lse_list = [], []
    for qkv, r in zip((qkv0, qkv1, qkv2), DILATIONS):
        o, lse = _attention(qkv.reshape(3, bsz, s, GROUP_W), s // r)
        o_list.append(o)
        lse_list.append(lse)
    h1, h1_packed, route, plan, counts = _mix(h, o_list, lse_list, cg, ga, p['w_attn_out'], p['w_o'],
                             p['ln1_g'], p['ln1_b'], p['wr_hi'], p['wr_lo'], p['b_route'])
    t = bsz * s
    route = route.reshape(t, LANES)
    dest, empty, block_e = _dispatch_plan(plan, counts, t)
    return h1, route, dest, empty, block_e, h1_packed


def kernel(x_prompt, x_sample, ln_in_g, ln_in_b, w_in, b_gate, conv_w, w_attn_out, w_conv_out, w_o, ln1_g, ln1_b, w_route_group, b_route_group, w_route_expert, b_route_expert, w_gate, w_up, w_down, ln2_g, ln2_b):
    d = D_MODEL
    w_route = jnp.concatenate([w_route_group[0], w_route_expert[0]], axis=1)
    w_route = jnp.pad(w_route, ((0, 0), (0, LANES - w_route.shape[1])))
    wr_hi = w_route.astype(BF16)
    b_route = jnp.concatenate([b_route_group[0], b_route_expert[0]]).astype(F32)
    p = {
        'ln_in_g': ln_in_g.reshape(1, d), 'ln_in_b': ln_in_b.reshape(1, d),
        'w_in': w_in[0].astype(BF16), 'b_gate': b_gate[0].reshape(1, 2 * d), 'conv_w': conv_w[0],
        'w_conv_out': w_conv_out[0].astype(BF16), 'w_attn_out': w_attn_out[0].astype(BF16),
        'w_o': w_o[0].astype(BF16),
        'ln1_g': ln1_g[0].reshape(1, d), 'ln1_b': ln1_b[0].reshape(1, d),
        'wr_hi': wr_hi, 'wr_lo': (w_route - wr_hi.astype(F32)).astype(BF16),
        'b_route': jnp.pad(b_route, (0, LANES - b_route.shape[0])).reshape(1, LANES),
        'ln2_g': ln2_g[0].reshape(1, d), 'ln2_b': ln2_b[0].reshape(1, d),
    }
    p['w_gate'], p['w_up'], p['w_down'] = w_gate[0], w_up[0], w_down[0]
    batches = [_token_mixing(x, p) for x in (x_prompt, x_sample)]
    sorted_rows = [_dispatch(h1_packed, dest, empty, block_e.shape[0] * EXPERT_BLOCK)
                   for _, _, dest, empty, block_e, h1_packed in batches]
    gathered = [_gather(_experts(xs, b[4], p['w_gate'], p['w_up'], p['w_down']), b[2])
                for xs, b in zip(sorted_rows, batches)]
    outs = [_combine(b[0], b[1], g, p['ln2_g'], p['ln2_b']).reshape(x.shape)
            for b, g, x in zip(batches, gathered, (x_prompt, x_sample))]
    return tuple(outs)
```

```python
import functools

import jax
import jax.numpy as jnp
import numpy as np
from jax import lax
from jax.experimental import pallas as pl
from jax.experimental.pallas import tpu as pltpu
from jax.experimental.pallas import tpu_sc as plsc

F32 = jnp.float32
BF16 = jnp.bfloat16

D_MODEL = 1024
HEAD_DIM = 64
HEADS_PER_GROUP = 4
GROUP_W = HEADS_PER_GROUP * HEAD_DIM
DILATIONS = (1, 4, 16)
BAND = 64
N_GROUPS = len(DILATIONS)
D_ATTN = N_GROUPS * GROUP_W
ATTN_SCALE = HEAD_DIM ** -0.5
ROPE_DIM = HEAD_DIM // 4
ROPE_THETA = 500000.0
MASK_VALUE = -1e30
D_CONV = D_MODEL
COL_CONV_B = 3 * D_ATTN
COL_CONV_C = COL_CONV_B + D_CONV
COL_CONV_H = COL_CONV_C + D_CONV
COL_GATE_ATTN = COL_CONV_H + D_CONV
COL_GATE_CONV = COL_GATE_ATTN + D_MODEL
N_EXPERT_GROUPS = 4
EXPERTS_PER_GROUP = 8
N_EXPERTS = N_EXPERT_GROUPS * EXPERTS_PER_GROUP
TOP_K_INNER = 2
D_EXPERT = 512
EXPERT_BLOCK = 512
EXPERT_CHUNK = 256
LN_EPS = 1e-5
DEPTH = 1
DEEPNORM_ALPHA = (2 * DEPTH) ** 0.25

LANES = 128
HEADS_PER_BLOCK = LANES // HEAD_DIM
ROW_TILE = 512
HALO = 16
Q_BLOCK = 128
K_BLOCK = Q_BLOCK + 2 * BAND
Q_CHUNK = 4096
COMBINE_TILE = 1024
SC_COLS = 256
SC_WINDOW = 128
F32_CHUNKS = D_MODEL // SC_COLS
PACKED_COLS = D_MODEL // 2
PACKED_CHUNKS = PACKED_COLS // SC_COLS
U32 = jnp.uint32
HIGH_HALF = np.uint32(0xFFFF0000)
ROUTE_LANE0 = N_EXPERT_GROUPS
ROUTE_LANE_W = TOP_K_INNER
ROUTE_LANE_RANK = 2 * TOP_K_INNER
PLAN_ROWS = 8
VMEM_LIMIT = 56 * 1024 * 1024


def _layer_norm(v, g, b):
    mu = jnp.mean(v, axis=-1, keepdims=True)
    d = v - mu
    var = jnp.mean(d * d, axis=-1, keepdims=True)
    return d * lax.rsqrt(var + LN_EPS) * g + b


def _dot(a, b):
    return jnp.dot(a, b, preferred_element_type=F32)


def _store_chunk_major(ref, first_row, val):
    for c in range(ref.shape[0]):
        ref[c, pl.ds(first_row, val.shape[0]), :] = val[:, c * SC_COLS:(c + 1) * SC_COLS]


def _load_chunk_major(ref, first_row, n):
    return jnp.concatenate([ref[c, pl.ds(first_row, n), :] for c in range(ref.shape[0])], axis=1)


def _chunk_major_spec(chunks, rows, index_map):
    return pl.BlockSpec((chunks, rows, SC_COLS), index_map)


def _pack_bf16_pairs(val):
    bits = lax.bitcast_convert_type(val.astype(BF16).astype(F32), U32)
    return (bits[:, PACKED_COLS:] & HIGH_HALF) | (bits[:, :PACKED_COLS] >> 16)


def _unpack_bf16_pairs(packed):
    low = lax.bitcast_convert_type(packed << 16, F32)
    high = lax.bitcast_convert_type(packed & HIGH_HALF, F32)
    return jnp.concatenate([low, high], axis=1)


def _resident(shape):
    nd = len(shape)
    return pl.BlockSpec(shape, lambda *_: (0,) * nd, pipeline_mode=pl.Buffered(1))


def _inproj_kernel(x_ref, xp_ref, xn_ref, g_ref, b_ref, win_ref, bg_ref, cw_ref, wco_ref,
                   cos_ref, sa_ref, sb_ref,
                   qkv0_ref, qkv1_ref, qkv2_ref, cg_ref, ga_ref, h_ref,
                   hext_ref, u_ref, stage_ref):
    i = pl.program_id(1)
    nt = pl.num_programs(1)
    tm = x_ref.shape[0]
    g = g_ref[...]
    b = b_ref[...]
    h = _layer_norm(x_ref[...], g, b)
    h_ref[...] = h
    hext_ref[pl.ds(HALO, tm), :] = h.astype(BF16)
    hext_ref[pl.ds(0, HALO), :] = _layer_norm(xp_ref[...], g, b).astype(BF16)
    hext_ref[pl.ds(HALO + tm, HALO), :] = _layer_norm(xn_ref[...], g, b).astype(BF16)
    hm = hext_ref[pl.ds(HALO, tm), :]
    he = hext_ref[...]

    def proj(lhs, col, n):
        return _dot(lhs, win_ref[:, col:col + n])

    u = proj(he, COL_CONV_C, D_CONV) * proj(he, COL_CONV_H, D_CONV)
    rows = lax.broadcasted_iota(jnp.int32, (tm + 2 * HALO, 1), 0)
    lo = jnp.where(i == 0, HALO, 0)
    hi = jnp.where(i == nt - 1, HALO + tm, tm + 2 * HALO)
    u_ref[...] = jnp.where((rows >= lo) & (rows < hi), u, 0.0)
    cw = cw_ref[...]
    conv = (cw[0:1] * u_ref[pl.ds(HALO - 1, tm), :] + cw[1:2] * u_ref[pl.ds(HALO, tm), :]
            + cw[2:3] * u_ref[pl.ds(HALO + 1, tm), :])
    cb = proj(hm, COL_CONV_B, D_CONV)
    conv_branch = _dot((cb * conv).astype(BF16), wco_ref[...])
    gate_conv = jax.nn.sigmoid(proj(hm, COL_GATE_CONV, D_MODEL) + bg_ref[:, D_MODEL:])
    cg_ref[...] = (gate_conv * conv_branch).astype(BF16)
    ga_ref[...] = jax.nn.sigmoid(proj(hm, COL_GATE_ATTN, D_MODEL) + bg_ref[:, :D_MODEL]).astype(BF16)

    cosv = cos_ref[...]
    sav = sa_ref[...]
    sbv = sb_ref[...]
    half = ROPE_DIM // 2
    for gi, (out_ref, r) in enumerate(zip((qkv0_ref, qkv1_ref, qkv2_ref), DILATIONS)):
        for t in range(3):
            p = proj(hm, t * D_ATTN + gi * GROUP_W, GROUP_W)
            if t < 2:
                parts = []
                for c in range(GROUP_W // LANES):
                    pc = p[:, c * LANES:(c + 1) * LANES]
                    parts.append(pc * cosv + pltpu.roll(pc, LANES - half, 1) * sav
                                 + pltpu.roll(pc, half, 1) * sbv)
                p = jnp.concatenate(parts, axis=1)
                if t == 0:
                    p = p * ATTN_SCALE
            if r == 1:
                out_ref[t, 0] = p.astype(BF16)
            else:
                for c in range(GROUP_W // LANES):
                    stage_ref[c] = p[:, c * LANES:(c + 1) * LANES]
                for res in range(r):
                    for c in range(GROUP_W // LANES):
                        out_ref[t, res, :, c * LANES:(c + 1) * LANES] = (
                            stage_ref[c, pl.ds(res, tm // r, stride=r), :].astype(BF16))


def _rope_tables(s):
    half = ROPE_DIM // 2
    inv_freq = (np.float32(ROPE_THETA) ** (-np.arange(half, dtype=np.float32) * np.float32(2.0) / ROPE_DIM))
    ang = np.arange(s, dtype=np.float32)[:, None] * inv_freq.astype(np.float32)[None, :]
    cos = np.cos(ang).astype(np.float32)
    sin = np.sin(ang).astype(np.float32)
    pad = np.zeros((s, HEAD_DIM - ROPE_DIM), np.float32)
    zero = np.zeros((s, half), np.float32)
    cos_t = np.concatenate([cos, cos, pad + 1.0], axis=1)
    sa_t = np.concatenate([-sin, zero, pad], axis=1)
    sb_t = np.concatenate([zero, sin, pad], axis=1)
    rep = LANES // HEAD_DIM
    return tuple(jnp.asarray(np.tile(t, (1, rep))) for t in (cos_t, sa_t, sb_t))


def _inproj(x, ln_g, ln_b, w_in, b_gate, conv_w, w_conv_out):
    bsz, s, d = x.shape
    tm = ROW_TILE
    nt = s // tm
    hpt = tm // HALO
    cos_t, sa_t, sb_t = _rope_tables(s)
    row = lambda b, i: (b, i, 0)
    tab = pl.BlockSpec((tm, LANES), lambda b, i: (i, 0))
    qkv_shapes = [jax.ShapeDtypeStruct((3, bsz, r, s // r, GROUP_W), BF16) for r in DILATIONS]
    qkv_specs = [pl.BlockSpec((3, None, r, tm // r, GROUP_W), lambda b, i: (0, b, 0, i, 0))
                 for r in DILATIONS]
    return pl.pallas_call(
        _inproj_kernel,
        grid=(bsz, nt),
        in_specs=[
            pl.BlockSpec((None, tm, d), row),
            pl.BlockSpec((None, HALO, d), lambda b, i: (b, jnp.maximum(i * hpt - 1, 0), 0)),
            pl.BlockSpec((None, HALO, d), lambda b, i: (b, jnp.minimum((i + 1) * hpt, s // HALO - 1), 0)),
            _resident((1, d)), _resident((1, d)),
            _resident(w_in.shape), _resident((1, 2 * d)), _resident(conv_w.shape),
            _resident(w_conv_out.shape),
            tab, tab, tab,
        ],
        out_specs=qkv_specs + [pl.BlockSpec((None, tm, d), row)] * 3,
        out_shape=(qkv_shapes + [jax.ShapeDtypeStruct((bsz, s, d), BF16)] * 2
                   + [jax.ShapeDtypeStruct((bsz, s, d), F32)]),
        scratch_shapes=[
            pltpu.VMEM((tm + 2 * HALO, d), BF16),
            pltpu.VMEM((tm + 2 * HALO, D_CONV), F32),
            pltpu.VMEM((GROUP_W // LANES, tm, LANES), F32),
        ],
        compiler_params=pltpu.CompilerParams(
            dimension_semantics=("parallel", "parallel"), vmem_limit_bytes=VMEM_LIMIT),
        name="inproj",
    )(x, x, x, ln_g, ln_b, w_in, b_gate, conv_w, w_conv_out, cos_t, sa_t, sb_t)


def _attn_kernel(q_ref, k_ref, v_ref, o_ref, lse_ref, *, sub_len):
    s = k_ref.shape[0]
    qc = q_ref.shape[0]
    base = pl.program_id(1) * qc
    lane = lax.broadcasted_iota(jnp.int32, (1, GROUP_W), 1)
    head_masks = [(lane >= h * HEAD_DIM) & (lane < (h + 1) * HEAD_DIM) for h in range(HEADS_PER_GROUP)]
    qi = lax.broadcasted_iota(jnp.int32, (Q_BLOCK, 1), 0)
    kj = lax.broadcasted_iota(jnp.int32, (1, K_BLOCK), 1)
    low_half = lax.broadcasted_iota(jnp.int32, (1, LANES), 1) < HEAD_DIM

    def body(j, carry):
        r0 = pl.multiple_of(j * Q_BLOCK, Q_BLOCK)
        s0 = base + r0
        k0 = pl.multiple_of(jnp.clip(s0 - BAND, 0, s - K_BLOCK), BAND)
        q = q_ref[pl.ds(r0, Q_BLOCK), :]
        k = k_ref[pl.ds(k0, K_BLOCK), :]
        v = v_ref[pl.ds(k0, K_BLOCK), :]
        sub_lo = (s0 // sub_len) * sub_len
        qpos = s0 + qi
        kpos = k0 + kj
        valid = (jnp.abs(qpos - kpos) <= BAND) & (kpos >= sub_lo) & (kpos < sub_lo + sub_len)
        qm = jnp.concatenate([jnp.where(hm, q, jnp.zeros_like(q)) for hm in head_masks], axis=0)
        sc = lax.dot_general(qm, k, (((1,), (1,)), ((), ())), preferred_element_type=F32)
        sc = jnp.where(jnp.concatenate([valid] * HEADS_PER_GROUP, axis=0), sc, MASK_VALUE)
        m = jnp.max(sc, axis=-1, keepdims=True)
        p = jnp.exp(sc - m)
        den = jnp.sum(p, axis=-1, keepdims=True)
        o_all = _dot((p * (1.0 / den)).astype(BF16), v)
        lse_all = m + jnp.log(den)
        o_cols, lse_cols = [], []
        for c in range(GROUP_W // LANES):
            lanes = slice(c * LANES, (c + 1) * LANES)
            h_lo, h_hi = HEADS_PER_BLOCK * c, HEADS_PER_BLOCK * c + 1
            rows_lo = slice(h_lo * Q_BLOCK, (h_lo + 1) * Q_BLOCK)
            rows_hi = slice(h_hi * Q_BLOCK, (h_hi + 1) * Q_BLOCK)
            o_cols.append(jnp.where(low_half, o_all[rows_lo, lanes], o_all[rows_hi, lanes]))
            lse_cols.append(jnp.where(low_half, lse_all[rows_lo], lse_all[rows_hi]))
        o_ref[pl.ds(r0, Q_BLOCK), :] = jnp.concatenate(o_cols, axis=1).astype(BF16)
        lse_ref[pl.ds(r0, Q_BLOCK), :] = jnp.concatenate(lse_cols, axis=1)
        return carry

    lax.fori_loop(0, qc // Q_BLOCK, body, 0, unroll=True)


def _attention(qkv, sub_len):
    _, bsz, s, w = qkv.shape
    qc = min(Q_CHUNK, s)
    kv_spec = lambda t: pl.BlockSpec((None, None, s, w), lambda b, j: (t, b, 0, 0))
    return pl.pallas_call(
        functools.partial(_attn_kernel, sub_len=sub_len),
        grid=(bsz, s // qc),
        in_specs=[pl.BlockSpec((None, None, qc, w), lambda b, j: (0, b, j, 0)), kv_spec(1), kv_spec(2)],
        out_specs=[pl.BlockSpec((None, qc, w), lambda b, j: (b, j, 0))] * 2,
        out_shape=[jax.ShapeDtypeStruct((bsz, s, w), BF16), jax.ShapeDtypeStruct((bsz, s, w), F32)],
        compiler_params=pltpu.CompilerParams(
            dimension_semantics=("parallel", "arbitrary"), vmem_limit_bytes=VMEM_LIMIT),
        name="attn",
    )(qkv, qkv, qkv)


def _first_index_of_max(vals, lane_f):
    mx = jnp.max(vals, axis=-1, keepdims=True)
    idx = jnp.min(jnp.where(vals == mx, lane_f, float(LANES)), axis=-1, keepdims=True)
    return mx, idx


def _mix_kernel(h_ref, o0_ref, o1_ref, o2_ref, l0_ref, l1_ref, l2_ref, cg_ref, ga_ref,
                wao_ref, wo_ref, g1_ref, b1_ref, wrh_ref, wrl_ref, br_ref,
                h1_ref, h1p_ref, route_ref, plan_ref, counts_ref, so_ref, sl_ref, tri_ref):
    tm = h_ref.shape[0]
    halves = GROUP_W // LANES
    for gi, (o_ref, l_ref, r) in enumerate(zip((o0_ref, o1_ref, o2_ref), (l0_ref, l1_ref, l2_ref), DILATIONS)):
        for c in range(halves):
            cols = slice(c * LANES, (c + 1) * LANES)
            for res in range(r):
                rows = pl.ds(res, tm // r, stride=r) if r > 1 else slice(None)
                so_ref[gi * halves + c, rows, :] = o_ref[res, :, cols].astype(F32)
                sl_ref[gi * halves + c, rows, :] = l_ref[res, :, cols]

    def natural(ref, gi):
        return jnp.concatenate([ref[gi * halves + c] for c in range(halves)], axis=1)

    lses = [natural(sl_ref, gi) for gi in range(N_GROUPS)]
    mx = jnp.maximum(jnp.maximum(lses[0], lses[1]), lses[2])
    es = [jnp.exp(l - mx) for l in lses]
    inv_den = 1.0 / (es[0] + es[1] + es[2])
    attn = jnp.concatenate(
        [(natural(so_ref, gi) * (es[gi] * inv_den)).astype(BF16) for gi in range(N_GROUPS)], axis=1)
    attn_branch = _dot(attn, wao_ref[...])
    merged = ga_ref[...].astype(F32) * attn_branch + cg_ref[...].astype(F32)
    mix = _dot(merged.astype(BF16), wo_ref[...])
    h1 = _layer_norm(DEEPNORM_ALPHA * h_ref[...] + mix, g1_ref[...], b1_ref[...])
    _store_chunk_major(h1_ref, 0, h1)
    _store_chunk_major(h1p_ref, 0, _pack_bf16_pairs(h1))

    hi = h1.astype(BF16)
    lo = (h1 - hi.astype(F32)).astype(BF16)
    wrh = wrh_ref[...]
    logits = _dot(hi, wrh) + _dot(lo, wrh) + _dot(hi, wrl_ref[...]) + br_ref[...]
    lane = lax.broadcasted_iota(jnp.int32, (1, LANES), 1)
    lane_f = lane.astype(F32)
    is_grp = lane < N_EXPERT_GROUPS
    gl = jnp.where(is_grp, logits, MASK_VALUE)
    ge = jnp.exp(gl - jnp.max(gl, axis=-1, keepdims=True))
    gp = jnp.where(is_grp, ge / jnp.sum(ge, axis=-1, keepdims=True), -1.0)
    grp_w, grp_idx = _first_index_of_max(gp, lane_f)
    e_lo = ROUTE_LANE0 + grp_idx * EXPERTS_PER_GROUP
    sel = jnp.where((lane_f >= e_lo) & (lane_f < e_lo + EXPERTS_PER_GROUP), logits, MASK_VALUE)
    v1, i1 = _first_index_of_max(sel, lane_f)
    v2, i2 = _first_index_of_max(jnp.where(lane_f == i1, MASK_VALUE, sel), lane_f)
    e2 = jnp.exp(v2 - v1)
    w1 = grp_w / (1.0 + e2)
    w2 = grp_w * e2 / (1.0 + e2)

    first = (pl.program_id(0) == 0) & (pl.program_id(1) == 0)

    @pl.when(first)
    def _():
        counts_ref[...] = jnp.zeros_like(counts_ref)
        r = lax.broadcasted_iota(jnp.int32, (tm, tm), 0)
        c = lax.broadcasted_iota(jnp.int32, (tm, tm), 1)
        tri_ref[...] = (c < r).astype(BF16)

    oh0 = lane_f == i1 - ROUTE_LANE0
    oh1 = lane_f == i2 - ROUTE_LANE0
    oh = oh0.astype(F32) + oh1.astype(F32)
    before = counts_ref[...] + _dot(tri_ref[...], oh.astype(BF16))
    r0 = jnp.sum(jnp.where(oh0, before, 0.0), axis=-1, keepdims=True)
    r1 = jnp.sum(jnp.where(oh1, before + oh0.astype(F32), 0.0), axis=-1, keepdims=True)
    counts_ref[...] = counts_ref[...] + jnp.sum(oh, axis=0, keepdims=True)

    per_lane = (i1 - ROUTE_LANE0, i2 - ROUTE_LANE0, w1, w2, r0, r1)
    route = jnp.zeros((tm, LANES), F32)
    for k, v in enumerate(per_lane):
        route = jnp.where(lane == k, v, route)
    route_ref[...] = route
    plan_ref[...] = route.T[:PLAN_ROWS, :]


def _mix(h, o_list, lse_list, cg, ga, w_attn_out, w_o, ln1_g, ln1_b, wr_hi, wr_lo, b_route):
    bsz, s, d = h.shape
    tm = ROW_TILE
    row = lambda b, i: (b, i, 0)
    dil_specs = [pl.BlockSpec((None, r, tm // r, GROUP_W), lambda b, i: (b, 0, i, 0)) for r in DILATIONS]
    o_views = [o.reshape(bsz, r, s // r, GROUP_W) for o, r in zip(o_list, DILATIONS)]
    l_views = [l.reshape(bsz, r, s // r, GROUP_W) for l, r in zip(lse_list, DILATIONS)]
    return pl.pallas_call(
        _mix_kernel,
        grid=(bsz, s // tm),
        in_specs=[pl.BlockSpec((None, tm, d), row)] + dil_specs + dil_specs + [
            pl.BlockSpec((None, tm, d), row), pl.BlockSpec((None, tm, d), row),
            _resident(w_attn_out.shape), _resident(w_o.shape),
            _resident((1, d)), _resident((1, d)),
            _resident(wr_hi.shape), _resident(wr_lo.shape), _resident((1, LANES)),
        ],
        out_specs=[_chunk_major_spec(F32_CHUNKS, tm, lambda b, i: (0, b * (s // tm) + i, 0)),
                   _chunk_major_spec(PACKED_CHUNKS, tm, lambda b, i: (0, b * (s // tm) + i, 0)),
                   pl.BlockSpec((None, tm, LANES), row),
                   pl.BlockSpec((PLAN_ROWS, tm), lambda b, i: (0, b * (s // tm) + i)),
                   pl.BlockSpec((1, LANES), lambda b, i: (0, 0))],
        out_shape=[jax.ShapeDtypeStruct((F32_CHUNKS, bsz * s, SC_COLS), F32),
                   jax.ShapeDtypeStruct((PACKED_CHUNKS, bsz * s, SC_COLS), U32),
                   jax.ShapeDtypeStruct((bsz, s, LANES), F32),
                   jax.ShapeDtypeStruct((PLAN_ROWS, bsz * s), F32),
                   jax.ShapeDtypeStruct((1, LANES), F32)],
        scratch_shapes=[pltpu.VMEM((N_GROUPS * GROUP_W // LANES, tm, LANES), F32)] * 2
        + [pltpu.VMEM((tm, tm), BF16)],
        compiler_params=pltpu.CompilerParams(
            dimension_semantics=("arbitrary", "arbitrary"), vmem_limit_bytes=VMEM_LIMIT),
        name="mix",
    )(h, *o_views, *l_views, cg, ga, w_attn_out, w_o, ln1_g, ln1_b, wr_hi, wr_lo, b_route)


def _dispatch_plan(plan, counts, t):
    eid = plan[:TOP_K_INNER].astype(jnp.int32)
    rank = plan[ROUTE_LANE_RANK:ROUTE_LANE_RANK + TOP_K_INNER].astype(jnp.int32)
    counts = counts[0, :N_EXPERTS].astype(jnp.int32)
    padded = (counts + EXPERT_BLOCK - 1) // EXPERT_BLOCK * EXPERT_BLOCK
    pad_end = jnp.cumsum(padded)
    pad_start = pad_end - padded
    experts = jnp.arange(N_EXPERTS, dtype=jnp.int32)[:, None, None]
    start_of = jnp.sum(jnp.where(eid[None] == experts, pad_start[:, None, None], 0), axis=0)
    dest = start_of + rank
    n_blocks = -(-t * TOP_K_INNER // EXPERT_BLOCK) + N_EXPERTS
    n_slots = n_blocks * EXPERT_BLOCK
    block_start = jnp.arange(n_blocks, dtype=jnp.int32) * EXPERT_BLOCK
    block_e = jnp.minimum(jnp.sum((block_start[:, None] >= pad_end[None, :]).astype(jnp.int32), axis=1),
                          N_EXPERTS - 1)
    n_empty = n_slots - t * TOP_K_INNER
    gap_end = jnp.cumsum(padded - counts)
    gap_start = gap_end - (padded - counts)
    k = jnp.arange(n_empty, dtype=jnp.int32)
    in_gap = (k[None, :] >= gap_start[:, None]) & (k[None, :] < gap_end[:, None])
    first_of_gap = (pad_start + counts - gap_start)[:, None]
    empty = jnp.where(k < gap_end[-1],
                      jnp.sum(jnp.where(in_gap, first_of_gap + k[None, :], 0), axis=0),
                      pad_end[-1] + k - gap_end[-1])
    return dest.astype(jnp.int32), empty.astype(jnp.int32), block_e


def _sc_mesh():
    return plsc.VectorSubcoreMesh(core_axis_name="core", subcore_axis_name="subcore")


def _sc_rows_pipeline(body, n_rows, row_index_map, cols):
    return pltpu.emit_pipeline(
        body,
        grid=(n_rows // SC_WINDOW,),
        in_specs=[pl.BlockSpec((SC_WINDOW, cols), index_map=row_index_map),
                  pl.BlockSpec((1, SC_WINDOW), index_map=lambda i: (0, i))],
        out_specs=[],
        core_axis_name=("core", "subcore"),
        dimension_semantics=(pltpu.PARALLEL,),
    )


def _chunk_rows(rows, chunks, n_rows_per_chunk):
    base = jnp.arange(chunks, dtype=jnp.int32)[:, None] * n_rows_per_chunk
    return (base + rows[None, :]).reshape(1, -1)


def _dispatch(rows, dest, empty, n_slots):
    chunks, t, _ = rows.shape
    x = rows.reshape(chunks * t, SC_COLS)
    idx_first = _chunk_rows(dest[0], chunks, n_slots)
    idx_second = _chunk_rows(dest[1], chunks, n_slots)
    idx_empty = _chunk_rows(empty, chunks, n_slots)
    zeros = jnp.zeros((SC_WINDOW, SC_COLS), rows.dtype)

    @pl.kernel(out_type=jax.ShapeDtypeStruct((chunks * n_slots, SC_COLS), rows.dtype), mesh=_sc_mesh(),
               scratch_types=[])
    def scatter(x_hbm, i0_hbm, i1_hbm, z_hbm, ie_hbm, o_hbm):
        def body(x_vmem, i_vmem):
            pltpu.sync_copy(x_vmem, o_hbm.at[i_vmem.at[0]])

        for i_hbm in (i0_hbm, i1_hbm):
            _sc_rows_pipeline(body, x.shape[0], lambda i: (i, 0), SC_COLS)(x_hbm, i_hbm)
        _sc_rows_pipeline(body, idx_empty.shape[1], lambda i: (0, 0), SC_COLS)(z_hbm, ie_hbm)

    return scatter(x, idx_first, idx_second, zeros, idx_empty).reshape(chunks, n_slots, SC_COLS)


def _expert_mlp(xb, wg_ref, wu_ref, wd_ref):
    acc = None
    for c in range(D_EXPERT // EXPERT_CHUNK):
        cols = slice(c * EXPERT_CHUNK, (c + 1) * EXPERT_CHUNK)
        gate = _dot(xb, wg_ref[:, cols].astype(BF16))
        up = _dot(xb, wu_ref[:, cols].astype(BF16))
        hidden = (gate * jax.nn.sigmoid(gate) * up).astype(BF16)
        part = _dot(hidden, wd_ref[cols, :].astype(BF16))
        acc = part if acc is None else acc + part
    return acc


def _expert_kernel(meta_ref, x_ref, wg_hbm, wu_hbm, wd_hbm, y_ref, wg_buf, wu_buf, wd_buf, sem):
    i = pl.program_id(0)
    expert, run, nxt = meta_ref[0, i], meta_ref[1, i], meta_ref[2, i]
    slot = run % 2
    first_of_run = (i == 0) | (meta_ref[1, jnp.maximum(i - 1, 0)] != run)

    def weight_copies(e, s):
        return [pltpu.make_async_copy(src.at[e], dst.at[s], sem.at[s])
                for src, dst in ((wg_hbm, wg_buf), (wu_hbm, wu_buf), (wd_hbm, wd_buf))]

    @pl.when(i == 0)
    def _():
        for c in weight_copies(expert, 0):
            c.start()

    @pl.when(first_of_run)
    def _():
        for c in weight_copies(expert, slot):
            c.wait()

        @pl.when(nxt >= 0)
        def _():
            for c in weight_copies(nxt, 1 - slot):
                c.start()

    xb = _unpack_bf16_pairs(_load_chunk_major(x_ref, 0, EXPERT_BLOCK)).astype(BF16)
    y = _expert_mlp(xb, wg_buf.at[slot], wu_buf.at[slot], wd_buf.at[slot])
    _store_chunk_major(y_ref, 0, _pack_bf16_pairs(y))


def _experts(xs, block_e, w_gate, w_up, w_down):
    n_blocks = block_e.shape[0]
    first = jnp.concatenate([jnp.ones((1,), jnp.bool_), block_e[1:] != block_e[:-1]])
    run = jnp.cumsum(first.astype(jnp.int32)) - 1
    later = jnp.where(block_e[None, :] > block_e[:, None], block_e[None, :], N_EXPERTS)
    nxt = jnp.min(later, axis=1)
    meta = jnp.stack([block_e, run, jnp.where(nxt < N_EXPERTS, nxt, -1)]).astype(jnp.int32)
    blk = _chunk_major_spec(PACKED_CHUNKS, EXPERT_BLOCK, lambda i, meta: (0, i, 0))
    grid_spec = pltpu.PrefetchScalarGridSpec(
        num_scalar_prefetch=1,
        grid=(n_blocks,),
        in_specs=[blk] + [pl.BlockSpec(memory_space=pl.ANY)] * 3,
        out_specs=blk,
        scratch_shapes=[pltpu.VMEM((2,) + w.shape[1:], w.dtype) for w in (w_gate, w_up, w_down)]
        + [pltpu.SemaphoreType.DMA((2,))],
    )
    return pl.pallas_call(
        _expert_kernel,
        grid_spec=grid_spec,
        out_shape=jax.ShapeDtypeStruct(xs.shape, xs.dtype),
        compiler_params=pltpu.CompilerParams(
            dimension_semantics=("arbitrary",), vmem_limit_bytes=VMEM_LIMIT),
        name="experts",
    )(meta, xs, w_gate, w_up, w_down)


def _gather(yb, dest):
    chunks, n_slots, _ = yb.shape
    x = yb.reshape(chunks * n_slots, SC_COLS)
    idx = _chunk_rows(dest.reshape(-1), chunks, n_slots)
    m = idx.shape[1]

    @pl.kernel(out_type=jax.ShapeDtypeStruct((m, SC_COLS), yb.dtype), mesh=_sc_mesh(), scratch_types=[])
    def gather(x_hbm, i_hbm, o_hbm):
        def body(i_vmem, o_vmem):
            pltpu.sync_copy(x_hbm.at[i_vmem.at[0]], o_vmem)

        pltpu.emit_pipeline(
            body,
            grid=(m // SC_WINDOW,),
            in_specs=[pl.BlockSpec((1, SC_WINDOW), index_map=lambda i: (0, i))],
            out_specs=[pl.BlockSpec((SC_WINDOW, SC_COLS), index_map=lambda i: (i, 0))],
            core_axis_name=("core", "subcore"),
            dimension_semantics=(pltpu.PARALLEL,),
        )(i_hbm, o_hbm)

    return gather(x, idx).reshape(chunks, m // chunks, SC_COLS)


def _combine_kernel(first_ref, second_ref, h1_ref, route_ref, g_ref, b_ref, y_ref):
    tm = y_ref.shape[0]
    route = route_ref[...]
    w_first = route[:, ROUTE_LANE_W:ROUTE_LANE_W + 1]
    w_second = route[:, ROUTE_LANE_W + 1:ROUTE_LANE_W + 2]
    ffn = (w_first * _unpack_bf16_pairs(_load_chunk_major(first_ref, 0, tm))
           + w_second * _unpack_bf16_pairs(_load_chunk_major(second_ref, 0, tm)))
    y_ref[...] = _layer_norm(DEEPNORM_ALPHA * _load_chunk_major(h1_ref, 0, tm) + ffn, g_ref[...], b_ref[...])


def _combine(h1, route, g, ln2_g, ln2_b):
    t = route.shape[0]
    d = D_MODEL
    tm = COMBINE_TILE
    nt = t // tm
    return pl.pallas_call(
        _combine_kernel,
        grid=(nt,),
        in_specs=[
            _chunk_major_spec(PACKED_CHUNKS, tm, lambda i: (0, i, 0)),
            _chunk_major_spec(PACKED_CHUNKS, tm, lambda i: (0, nt + i, 0)),
            _chunk_major_spec(F32_CHUNKS, tm, lambda i: (0, i, 0)),
            pl.BlockSpec((tm, LANES), lambda i: (i, 0)),
            _resident((1, d)), _resident((1, d)),
        ],
        out_specs=pl.BlockSpec((tm, d), lambda i: (i, 0)),
        out_shape=jax.ShapeDtypeStruct((t, d), F32),
        compiler_params=pltpu.CompilerParams(
            dimension_semantics=("parallel",), vmem_limit_bytes=VMEM_LIMIT),
        name="combine",
    )(g, g, h1, route, ln2_g, ln2_b)


def _token_mixing(x, p):
    bsz, s, d = x.shape
    qkv0, qkv1, qkv2, cg, ga, h = _inproj(x, p['ln_in_g'], p['ln_in_b'], p['w_in'], p['b_gate'], p['conv_w'],
                                          p['w_conv_out'])
    o_list, lse_list = [], []
    for qkv, r in zip((qkv0, qkv1, qkv2), DILATIONS):
        o, lse = _attention(qkv.reshape(3, bsz, s, GROUP_W), s // r)
        o_list.append(o)
        lse_list.append(lse)
    h1, h1_packed, route, plan, counts = _mix(h, o_list, lse_list, cg, ga, p['w_attn_out'], p['w_o'],
                             p['ln1_g'], p['ln1_b'], p['wr_hi'], p['wr_lo'], p['b_route'])
    t = bsz * s
    route = route.reshape(t, LANES)
    dest, empty, block_e = _dispatch_plan(plan, counts, t)
    return h1, route, dest, empty, block_e, h1_packed


def kernel(x_prompt, x_sample, ln_in_g, ln_in_b, w_in, b_gate, conv_w, w_attn_out, w_conv_out, w_o, ln1_g, ln1_b, w_route_group, b_route_group, w_route_expert, b_route_expert, w_gate, w_up, w_down, ln2_g, ln2_b):
    d = D_MODEL
    w_route = jnp.concatenate([w_route_group[0], w_route_expert[0]], axis=1)
    w_route = jnp.pad(w_route, ((0, 0), (0, LANES - w_route.shape[1])))
    wr_hi = w_route.astype(BF16)
    b_route = jnp.concatenate([b_route_group[0], b_route_expert[0]]).astype(F32)
    p = {
        'ln_in_g': ln_in_g.reshape(1, d), 'ln_in_b': ln_in_b.reshape(1, d),
        'w_in': w_in[0].astype(BF16), 'b_gate': b_gate[0].reshape(1, 2 * d), 'conv_w': conv_w[0],
        'w_conv_out': w_conv_out[0].astype(BF16), 'w_attn_out': w_attn_out[0].astype(BF16),
        'w_o': w_o[0].astype(BF16),
        'ln1_g': ln1_g[0].reshape(1, d), 'ln1_b': ln1_b[0].reshape(1, d),
        'wr_hi': wr_hi, 'wr_lo': (w_route - wr_hi.astype(F32)).astype(BF16),
        'b_route': jnp.pad(b_route, (0, LANES - b_route.shape[0])).reshape(1, LANES),
        'ln2_g': ln2_g[0].reshape(1, d), 'ln2_b': ln2_b[0].reshape(1, d),
    }
    p['w_gate'], p['w_up'], p['w_down'] = w_gate[0], w_up[0], w_down[0]
    batches = [_token_mixing(x, p) for x in (x_prompt, x_sample)]
    sorted_rows = [_dispatch(h1_packed, dest, empty, block_e.shape[0] * EXPERT_BLOCK)
                   for _, _, dest, empty, block_e, h1_packed in batches]
    gathered = [_gather(_experts(xs, b[4], p['w_gate'], p['w_up'], p['w_down']), b[2])
                for xs, b in zip(sorted_rows, batches)]
    outs = [_combine(b[0], b[1], g, p['ln2_g'], p['ln2_b']).reshape(x.shape)
            for b, g, x in zip(batches, gathered, (x_prompt, x_sample))]
    return tuple(outs)
```
